```python
import math
import jax, jax.numpy as jnp
from jax import lax
import numpy as np

D_MODEL = 1024
BATCH = 32
SEQ = 256
DEPTH = 1
DEC_BATCH = 2
DEC_SEQ = 4096
PAST_LEN = 256

GRID_W = 64
H_A = 8
D_HA = 64
W_A = H_A * D_HA
WIN_H = 8
WIN_W = 16
QB_W = 16
BAND_W = 2 * WIN_W
H_B = 4
D_K = 128
D_V = 128
W_B = H_B * D_V
CONV_K = 5
CHUNK = 64
ROPE_BASE = 10000.0
N_GROUPS = 4
EXP_PER_GROUP = 8
N_EXPERTS = N_GROUPS * EXP_PER_GROUP
TOP_K = 2
D_FF_E = 256
EPS = 1e-6
IN_WIDTH = 3 * W_A + 2 * H_B * D_K + 2 * W_B + 4 * H_B + 2 * D_MODEL

kernel_name = "hybrid_na_gdn_hmoe_diffusion_step"


def _in_split_points():
    sizes = (W_A, W_A, W_A, H_B * D_K, H_B * D_K, W_B, W_B, H_B, H_B, H_B, H_B, D_MODEL, D_MODEL)
    points, acc = [], 0
    for s in sizes[:-1]:
        acc += s
        points.append(acc)
    return points


def rms_norm(x, w):
    xf = x.astype(jnp.float32)
    y = xf * lax.rsqrt(jnp.mean(xf * xf, axis=-1, keepdims=True) + EPS)
    return (y * w.astype(jnp.float32)).astype(x.dtype)


def l2_norm(x):
    xf = x.astype(jnp.float32)
    return (xf * lax.rsqrt(jnp.sum(xf * xf, axis=-1, keepdims=True) + EPS)).astype(x.dtype)


def axial_rope_tables(L):
    pos = jnp.arange(L)
    rows = (pos // GRID_W).astype(jnp.float32)
    cols = (pos % GRID_W).astype(jnp.float32)
    n_pairs = D_K // 4
    inv = ROPE_BASE ** (-jnp.arange(n_pairs, dtype=jnp.float32) / n_pairs)
    ang = jnp.concatenate([rows[:, None] * inv, cols[:, None] * inv], axis=-1)
    return jnp.cos(ang), jnp.sin(ang)


def apply_rope(x, cos, sin):
    xf = x.astype(jnp.float32).reshape(x.shape[:-1] + (x.shape[-1] // 2, 2))
    x1, x2 = xf[..., 0], xf[..., 1]
    c = cos[None, :, None, :]
    s = sin[None, :, None, :]
    out = jnp.stack([x1 * c - x2 * s, x1 * s + x2 * c], axis=-1).reshape(x.shape)
    return out.astype(x.dtype)


def short_conv(x, w):
    C = x.shape[-1]
    y = lax.conv_general_dilated(x, w[:, None, :].astype(x.dtype), window_strides=(1,),
                                 padding=[(CONV_K // 2, CONV_K // 2)],
                                 dimension_numbers=('NWC', 'WIO', 'NWC'),
                                 feature_group_count=C)
    return jax.nn.silu(y)


def context_attention(q, k, v):
    s = jnp.einsum('bqhd,bkhd->bhqk', q, k).astype(jnp.float32) * (q.shape[-1] ** -0.5)
    p = jax.nn.softmax(s, axis=-1).astype(v.dtype)
    return jnp.einsum('bhqk,bkhd->bqhd', p, v)


def neighbourhood_attention(q, k, v, k_ctx, v_ctx, rpb):
    B, L, H, D = q.shape
    rows = L // GRID_W
    kh = min(WIN_H, rows)
    n_cb = GRID_W // QB_W
    n_keys = kh * BAND_W
    r = jnp.arange(rows)
    key_rows = jnp.clip(r - kh // 2, 0, rows - kh)[:, None] + jnp.arange(kh)[None, :]
    m = jnp.arange(n_cb)
    key_cols = (jnp.clip(m * QB_W - WIN_W // 2, 0, GRID_W - BAND_W)[:, None]
                + jnp.arange(BAND_W)[None, :])
    q_cols = m[:, None] * QB_W + jnp.arange(QB_W)[None, :]
    win_start = jnp.clip(q_cols - WIN_W // 2, 0, GRID_W - WIN_W)
    kg = k.reshape(B, rows, GRID_W, H, D)
    vg = v.reshape(B, rows, GRID_W, H, D)
    ridx = key_rows[:, None, :, None]
    cidx = key_cols[None, :, None, :]
    k_band = kg[:, ridx, cidx].reshape(B, rows, n_cb, n_keys, H, D)
    v_band = vg[:, ridx, cidx].reshape(B, rows, n_cb, n_keys, H, D)
    qg = q.reshape(B, rows, n_cb, QB_W, H, D)
    scale = D ** -0.5
    s_loc = jnp.einsum('brmqhd,brmkhd->bhrmqk', qg, k_band).astype(jnp.float32) * scale
    kc = key_cols[:, None, :]
    ws = win_start[:, :, None]
    in_win = (kc >= ws) & (kc < ws + WIN_W)
    mask = jnp.broadcast_to(in_win[:, :, None, :], (n_cb, QB_W, kh, BAND_W)).reshape(n_cb, QB_W, n_keys)
    dr = key_rows - r[:, None] + (WIN_H - 1)
    dc = jnp.clip(key_cols[:, None, :] - q_cols[:, :, None] + (WIN_W - 1), 0, 2 * WIN_W - 2)
    bias = rpb[:, dr[:, None, None, :, None], dc[None, :, :, None, :]]
    bias = bias.reshape(H, rows, n_cb, QB_W, n_keys).astype(jnp.float32)
    s_loc = jnp.where(mask, s_loc + bias, -1e30)
    s_ctx = jnp.einsum('brmqhd,bchd->bhrmqc', qg, k_ctx).astype(jnp.float32) * scale
    p = jax.nn.softmax(jnp.concatenate([s_loc, s_ctx], axis=-1), axis=-1).astype(v.dtype)
    o = (jnp.einsum('bhrmqk,brmkhd->brmqhd', p[..., :n_keys], v_band)
         + jnp.einsum('bhrmqc,bchd->brmqhd', p[..., n_keys:], v_ctx))
    return o.reshape(B, L, H, D)


def gated_delta_chunked(q, k, v, g, beta, s0):
    B, L, H, DK = q.shape
    DV = v.shape[-1]
    n = L // CHUNK
    f32 = jnp.float32

    def to_chunks(t):
        t = t.astype(f32).reshape((B, n, CHUNK, H) + t.shape[3:])
        return jnp.moveaxis(t, (1, 3), (0, 2))

    qc = to_chunks(q) * (DK ** -0.5)
    kc = to_chunks(k)
    vc = to_chunks(v)
    bc = to_chunks(beta)
    gc = jnp.cumsum(to_chunks(g), axis=-1)
    tril = jnp.tril(jnp.ones((CHUNK, CHUNK), dtype=bool))
    strict = jnp.tril(jnp.ones((CHUNK, CHUNK), dtype=bool), -1)
    diff = gc[..., :, None] - gc[..., None, :]
    decay = jnp.where(tril, jnp.exp(jnp.where(tril, diff, 0.0)), 0.0)
    kb = kc * bc[..., None]
    a_mat = jnp.where(strict, jnp.einsum('nbhik,nbhjk->nbhij', kb, kc) * decay, 0.0)
    eye = jnp.eye(CHUNK, dtype=f32)
    t_inv = lax.linalg.triangular_solve(eye + a_mat, jnp.broadcast_to(eye, a_mat.shape),
                                        left_side=True, lower=True)
    u = jnp.einsum('nbhij,nbhjv->nbhiv', t_inv, vc * bc[..., None])
    w = jnp.einsum('nbhij,nbhjk->nbhik', t_inv, kb * jnp.exp(gc)[..., None])

    def step(S, xs):
        q_i, k_i, u_i, w_i, g_i, dec_i = xs
        v_new = u_i - jnp.einsum('bhck,bhkv->bhcv', w_i, S)
        intra = jnp.einsum('bhik,bhjk->bhij', q_i, k_i) * dec_i
        o = (jnp.einsum('bhck,bhkv->bhcv', q_i * jnp.exp(g_i)[..., None], S)
             + jnp.einsum('bhij,bhjv->bhiv', intra, v_new))
        g_last = g_i[..., -1:]
        S = (S * jnp.exp(g_last)[..., None]
             + jnp.einsum('bhck,bhcv->bhkv', k_i * jnp.exp(g_last - g_i)[..., None], v_new))
        return S, o

    S, o = lax.scan(step, s0.astype(f32), (qc, kc, u, w, gc, decay))
    o = jnp.moveaxis(o, (0, 2), (1, 3)).reshape(B, L, H, DV)
    return o.astype(v.dtype), S.astype(s0.dtype)


def bidirectional_delta(q, k, v, g_f, g_b, b_f, b_b, s_f, s_b):
    o_f, S_f = gated_delta_chunked(q, k, v, g_f, b_f, s_f)
    o_b, S_b = gated_delta_chunked(q[:, ::-1], k[:, ::-1], v[:, ::-1], g_b[:, ::-1], b_b[:, ::-1], s_b)
    return o_f + o_b[:, ::-1], S_f, S_b


def log_decay(a, a_log, dt_bias):
    return -jnp.exp(a_log.astype(jnp.float32)) * jax.nn.softplus((a + dt_bias).astype(jnp.float32))


def token_mix(h, w_in, conv_w, a_log, dt_bias, q_norm_w, k_norm_w, rpb, o_norm_w,
              w_proj_a, w_proj_b, w_out, k_ctx, v_ctx, s_f, s_b):
    B, L, _ = h.shape
    qa, ka, va, qb, kb, vb, zb, a_f, a_b, b_f, b_b, ga, gb = jnp.split(h @ w_in, _in_split_points(), axis=-1)
    qa = rms_norm(qa.reshape(B, L, H_A, D_HA), q_norm_w)
    ka = rms_norm(ka.reshape(B, L, H_A, D_HA), k_norm_w)
    va = va.reshape(B, L, H_A, D_HA)
    if k_ctx is None:
        o_a = context_attention(qa, ka, va)
    else:
        o_a = neighbourhood_attention(qa, ka, va, k_ctx, v_ctx, rpb)
    qkv = short_conv(jnp.concatenate([qb, kb, vb], axis=-1), conv_w)
    qd, kd, vd = jnp.split(qkv, [H_B * D_K, 2 * H_B * D_K], axis=-1)
    qd = l2_norm(qd.reshape(B, L, H_B, D_K))
    kd = l2_norm(kd.reshape(B, L, H_B, D_K))
    if k_ctx is not None:
        cos, sin = axial_rope_tables(L)
        qd = apply_rope(qd, cos, sin)
        kd = apply_rope(kd, cos, sin)
    vd = vd.reshape(B, L, H_B, D_V)
    g_f = log_decay(a_f, a_log[0], dt_bias[0])
    g_b = log_decay(a_b, a_log[1], dt_bias[1])
    o_b, S_f, S_b = bidirectional_delta(qd, kd, vd, g_f, g_b, jax.nn.sigmoid(b_f), jax.nn.sigmoid(b_b), s_f, s_b)
    o_b = rms_norm(o_b, o_norm_w) * jax.nn.silu(zb.reshape(B, L, H_B, D_V))
    merged = (jax.nn.sigmoid(ga) * (o_a.reshape(B, L, W_A) @ w_proj_a)
              + jax.nn.sigmoid(gb) * (o_b.reshape(B, L, W_B) @ w_proj_b))
    return merged @ w_out, ka, va, S_f, S_b


def hier_moe(h, w_router_g, b_router_g, w_router_e, b_router_e, w_exp_gate, w_exp_up, w_exp_down):
    f32 = jnp.float32
    grp_logits = (h @ w_router_g).astype(f32) + b_router_g.astype(f32)
    grp = jnp.argmax(grp_logits, axis=-1)
    p_grp = jnp.max(jax.nn.softmax(grp_logits, axis=-1), axis=-1, keepdims=True)
    exp_logits = ((h @ w_router_e).astype(f32) + b_router_e.astype(f32)).reshape(
        h.shape[:-1] + (N_GROUPS, EXP_PER_GROUP))
    in_grp = jnp.take_along_axis(exp_logits, grp[..., None, None], axis=-2)[..., 0, :]
    top_p, top_i = lax.top_k(jax.nn.softmax(in_grp, axis=-1), TOP_K)
    wts = p_grp * top_p / jnp.sum(top_p, axis=-1, keepdims=True)
    ids = grp[..., None] * EXP_PER_GROUP + top_i
    combine = jnp.sum(jax.nn.one_hot(ids, N_EXPERTS, dtype=f32) * wts[..., None], axis=-2)
    hid = (jax.nn.silu(jnp.einsum('bld,edf->blef', h, w_exp_gate))
           * jnp.einsum('bld,edf->blef', h, w_exp_up))
    hid = hid * combine[..., None].astype(h.dtype)
    return jnp.einsum('blef,efd->bld', hid, w_exp_down)


def layer_forward(x, cond, norm1_w, norm2_w, w_mod, b_mod, w_in, conv_w, a_log, dt_bias,
                  q_norm_w, k_norm_w, rpb, o_norm_w, w_proj_a, w_proj_b, w_out,
                  w_router_g, b_router_g, w_router_e, b_router_e, w_exp_gate, w_exp_up, w_exp_down,
                  k_ctx, v_ctx, s_f, s_b):
    mod = (jax.nn.silu(cond) @ w_mod + b_mod)[:, None, :]
    sh1, sc1, g1, sh2, sc2, g2 = jnp.split(mod, 6, axis=-1)
    h = rms_norm(x, norm1_w) * (1.0 + sc1) + sh1
    mix, k_a, v_a, S_f, S_b = token_mix(h, w_in, conv_w, a_log, dt_bias, q_norm_w, k_norm_w, rpb,
                                        o_norm_w, w_proj_a, w_proj_b, w_out, k_ctx, v_ctx, s_f, s_b)
    x = x + g1 * mix
    h = rms_norm(x, norm2_w) * (1.0 + sc2) + sh2
    x = x + g2 * hier_moe(h, w_router_g, b_router_g, w_router_e, b_router_e, w_exp_gate, w_exp_up, w_exp_down)
    return x, k_a, v_a, S_f, S_b


def setup_inputs(seed: int = 0) -> dict:
    key = jax.random.key(seed)
    ks = jax.random.split(key, 32)
    f32 = jnp.float32

    def nrm(k, shape, scale):
        return jax.random.normal(k, shape, f32) * scale

    L = DEPTH
    dt = jnp.exp(jax.random.uniform(ks[12], (L, 2, H_B), f32, minval=math.log(1e-3), maxval=math.log(1e-1)))
    return {
        'x_prompt': nrm(ks[0], (BATCH, SEQ, D_MODEL), 1.0),
        'x_sample': nrm(ks[1], (DEC_BATCH, DEC_SEQ, D_MODEL), 1.0),
        'c': nrm(ks[2], (DEC_BATCH, D_MODEL), 1.0),
        'cache_attn_k': nrm(ks[3], (DEC_BATCH, DEPTH, PAST_LEN, H_A, D_HA), 1.0),
        'cache_attn_v': nrm(ks[4], (DEC_BATCH, DEPTH, PAST_LEN, H_A, D_HA), 1.0),
        'state_delta_fwd': nrm(ks[5], (DEC_BATCH, DEPTH, H_B, D_K, D_V), D_K ** -0.5),
        'state_delta_bwd': nrm(ks[6], (DEC_BATCH, DEPTH, H_B, D_K, D_V), D_K ** -0.5),
        'c_ctx': nrm(ks[7], (D_MODEL,), 1.0),
        'norm1_w': 1.0 + nrm(ks[8], (L, D_MODEL), 0.02),
        'norm2_w': 1.0 + nrm(ks[9], (L, D_MODEL), 0.02),
        'w_mod': nrm(ks[10], (L, D_MODEL, 6 * D_MODEL), 0.5 * D_MODEL ** -0.5),
        'b_mod': nrm(ks[11], (L, 6 * D_MODEL), 0.02),
        'w_in': nrm(ks[13], (L, D_MODEL, IN_WIDTH), D_MODEL ** -0.5),
        'conv_w': nrm(ks[14], (L, CONV_K, 2 * H_B * D_K + W_B), CONV_K ** -0.5),
        'a_log': jnp.log(jax.random.uniform(ks[15], (L, 2, H_B), f32, minval=1.0, maxval=16.0)),
        'dt_bias': dt + jnp.log(-jnp.expm1(-dt)),
        'q_norm_w': 1.0 + nrm(ks[16], (L, D_HA), 0.02),
        'k_norm_w': 1.0 + nrm(ks[17], (L, D_HA), 0.02),
        'rpb': nrm(ks[18], (L, H_A, 2 * WIN_H - 1, 2 * WIN_W - 1), 0.1),
        'o_norm_w': 1.0 + nrm(ks[19], (L, D_V), 0.02),
        'w_proj_a': nrm(ks[20], (L, W_A, D_MODEL), W_A ** -0.5),
        'w_proj_b': nrm(ks[21], (L, W_B, D_MODEL), W_B ** -0.5),
        'w_out': nrm(ks[22], (L, D_MODEL, D_MODEL), D_MODEL ** -0.5),
        'w_router_g': nrm(ks[23], (L, D_MODEL, N_GROUPS), D_MODEL ** -0.5),
        'b_router_g': nrm(ks[24], (L, N_GROUPS), 0.01),
        'w_router_e': nrm(ks[25], (L, D_MODEL, N_EXPERTS), D_MODEL ** -0.5),
        'b_router_e': nrm(ks[26], (L, N_EXPERTS), 0.01),
        'w_exp_gate': nrm(ks[27], (L, N_EXPERTS, D_MODEL, D_FF_E), D_MODEL ** -0.5),
        'w_exp_up': nrm(ks[28], (L, N_EXPERTS, D_MODEL, D_FF_E), D_MODEL ** -0.5),
        'w_exp_down': nrm(ks[29], (L, N_EXPERTS, D_FF_E, D_MODEL), D_FF_E ** -0.5),
    }


def reference(x_prompt, x_sample, c, cache_attn_k, cache_attn_v, state_delta_fwd, state_delta_bwd,
              c_ctx, norm1_w, norm2_w, w_mod, b_mod, w_in, conv_w, a_log, dt_bias,
              q_norm_w, k_norm_w, rpb, o_norm_w, w_proj_a, w_proj_b, w_out,
              w_router_g, b_router_g, w_router_e, b_router_e, w_exp_gate, w_exp_up, w_exp_down):
    y_p = x_prompt
    y_s = x_sample
    new_k, new_v, new_sf, new_sb = [], [], [], []
    for l in range(DEPTH):
        lw = dict(norm1_w=norm1_w[l], norm2_w=norm2_w[l], w_mod=w_mod[l], b_mod=b_mod[l],
                  w_in=w_in[l], conv_w=conv_w[l], a_log=a_log[l], dt_bias=dt_bias[l],
                  q_norm_w=q_norm_w[l], k_norm_w=k_norm_w[l], rpb=rpb[l], o_norm_w=o_norm_w[l],
                  w_proj_a=w_proj_a[l], w_proj_b=w_proj_b[l], w_out=w_out[l],
                  w_router_g=w_router_g[l], b_router_g=b_router_g[l],
                  w_router_e=w_router_e[l], b_router_e=b_router_e[l],
                  w_exp_gate=w_exp_gate[l], w_exp_up=w_exp_up[l], w_exp_down=w_exp_down[l])
        zeros = jnp.zeros((x_prompt.shape[0], H_B, D_K, D_V), x_prompt.dtype)
        y_p, k_a, v_a, s_f, s_b = layer_forward(y_p, c_ctx[None, :], k_ctx=None, v_ctx=None,
                                                s_f=zeros, s_b=zeros, **lw)
        new_k.append(k_a)
        new_v.append(v_a)
        new_sf.append(s_f)
        new_sb.append(s_b)
        y_s, _, _, _, _ = layer_forward(y_s, c, k_ctx=cache_attn_k[:, l], v_ctx=cache_attn_v[:, l],
                                        s_f=state_delta_fwd[:, l], s_b=state_delta_bwd[:, l], **lw)
    new_attn_k = jnp.stack(new_k, axis=1)
    new_attn_v = jnp.stack(new_v, axis=1)
    new_delta_fwd = jnp.stack(new_sf, axis=1)
    new_delta_bwd = jnp.stack(new_sb, axis=1)
    return (y_p, y_s, new_attn_k, new_attn_v, new_delta_fwd, new_delta_bwd)
```

```python
import functools
import math

import numpy as np
import jax
import jax.numpy as jnp
from jax import lax
from jax.experimental import pallas as pl
from jax.experimental.pallas import tpu as pltpu

F32 = jnp.float32
BF16 = jnp.bfloat16

D_MODEL = 1024
GRID_W = 64
H_A = 8
D_HA = 64
W_A = H_A * D_HA
WIN_H = 8
WIN_W = 16
H_B = 4
D_K = 128
D_V = 128
W_B = H_B * D_V
CONV_K = 5
CHUNK = 64
ROPE_BASE = 10000.0
N_GROUPS = 4
EXP_PER_GROUP = 8
N_EXPERTS = N_GROUPS * EXP_PER_GROUP
D_FF_E = 256
EPS = 1e-6
NEG_INF = -1e30

LANES = 128
HP = 2
N_HG = H_B // HP
ROW_TILE = 512
MOE_TILE = 256
VMEM_LIMIT = 56 * 1024 * 1024

HIGHEST = lax.Precision.HIGHEST


def _cparams(sem):
    return pltpu.CompilerParams(dimension_semantics=sem, vmem_limit_bytes=VMEM_LIMIT)


def _dot(a, b):
    return jnp.dot(a, b, preferred_element_type=F32)


def _dot_nt(a, b):
    return lax.dot_general(a, b, (((1,), (1,)), ((), ())), preferred_element_type=F32)


def _dot_tn(a, b):
    return lax.dot_general(a, b, (((0,), (0,)), ((), ())), preferred_element_type=F32)


def _const_spec(shape):
    nd = len(shape)
    return pl.BlockSpec(shape, lambda *_: (0,) * nd)


def _mod_kernel(c_ref, w_ref, b_ref, o_ref):
    c = c_ref[...]
    s = c * jax.nn.sigmoid(c)
    o_ref[...] = jnp.dot(s, w_ref[...], preferred_element_type=F32, precision=HIGHEST) + b_ref[...]


def _mod_call(cond8, w_mod, b_mod):
    n_out = w_mod.shape[1]
    tn = 1024
    return pl.pallas_call(
        _mod_kernel,
        grid=(n_out // tn,),
        in_specs=[pl.BlockSpec((8, D_MODEL), lambda j: (0, 0)),
                  pl.BlockSpec((D_MODEL, tn), lambda j: (0, j)),
                  pl.BlockSpec((1, tn), lambda j: (0, j))],
        out_specs=pl.BlockSpec((8, tn), lambda j: (0, j)),
        out_shape=jax.ShapeDtypeStruct((8, n_out), F32),
        compiler_params=_cparams(("arbitrary",)),
        name="mod",
    )(cond8, w_mod, b_mod.reshape(1, n_out))


def _softplus(x):
    return jnp.maximum(x, 0.0) + jnp.log1p(jnp.exp(-jnp.abs(x)))


def _inproj_kernel(x_ref, sh_ref, sc_ref, nw_ref, wa_ref, wb_ref, wz_ref, wg_ref, we_ref,
                   qnw_ref, knw_ref, bd_ref, alog_ref, dtb_ref, ltri_ref, utri_ref,
                   qa_ref, ka_ref, va_ref, qkvb_ref, z_ref, gate_ref, e_ref):
    x = x_ref[0]
    ms = jnp.mean(x * x, axis=-1, keepdims=True)
    h = x * lax.rsqrt(ms + EPS) * nw_ref[...]
    h = h * (1.0 + sc_ref[0]) + sh_ref[0]
    hb = h.astype(BF16)

    a = _dot(hb, wa_ref[...])
    bd = bd_ref[...]

    def head_rms(t, w):
        sq = (t * t).astype(BF16)
        ss = jnp.concatenate([_dot(sq[:, i * 256:(i + 1) * 256], bd) for i in range(W_A // 256)], axis=-1)
        return t * lax.rsqrt(ss * (1.0 / D_HA) + EPS) * w

    qa = head_rms(a[:, :W_A], qnw_ref[...]) * (D_HA ** -0.5)
    qa_ref[0] = qa.astype(qa_ref.dtype)
    ka_ref[0] = head_rms(a[:, W_A:2 * W_A], knw_ref[...]).astype(ka_ref.dtype)
    va_ref[0] = a[:, 2 * W_A:].astype(va_ref.dtype)

    qkvb_ref[0] = _dot(hb, wb_ref[...]).astype(qkvb_ref.dtype)
    z = _dot(hb, wz_ref[...])
    z_ref[0] = (z * jax.nn.sigmoid(z)).astype(z_ref.dtype)
    gate_ref[0] = jax.nn.sigmoid(_dot(hb, wg_ref[...])).astype(gate_ref.dtype)

    e = _dot(hb, we_ref[...])
    lane = lax.broadcasted_iota(jnp.int32, e.shape, 1) % LANES
    g = -jnp.exp(alog_ref[...]) * _softplus(e + dtb_ref[...])
    act = jnp.where(lane < 2 * HP, g, jax.nn.sigmoid(e))
    cf = jnp.dot(ltri_ref[...], act, preferred_element_type=F32, precision=HIGHEST)
    cb = jnp.dot(utri_ref[...], act, preferred_element_type=F32, precision=HIGHEST)
    e_ref[0] = jnp.where(lane < HP, cf, jnp.where(lane < 2 * HP, cb, act))


def _inproj_call(x, mod3, mod_stride, w, kv_dtype):
    B, L, _ = x.shape
    tm = min(ROW_TILE, L)
    grid = (B, L // tm)
    row = lambda n, dt: jax.ShapeDtypeStruct((B, L, n), dt)
    rspec = lambda n: pl.BlockSpec((1, tm, n), lambda b, i: (b, i, 0))
    mspec = lambda k: pl.BlockSpec((1, 1, D_MODEL), lambda b, i: (b * mod_stride + k, 0, 0))
    t = np.arange(tm)
    same = (t[:, None] // CHUNK) == (t[None, :] // CHUNK)
    ltri = jnp.asarray((same & (t[:, None] >= t[None, :])).astype(np.float32))
    utri = jnp.asarray((same & (t[:, None] <= t[None, :])).astype(np.float32))
    consts = [w["norm1_w"], w["wa"], w["wb"], w["wz"], w["wg"], w["we"], w["qnw"], w["knw"],
              w["bd256"], w["alog_l"], w["dtb_l"], ltri, utri]
    return pl.pallas_call(
        _inproj_kernel,
        grid=grid,
        in_specs=[rspec(D_MODEL), mspec(0), mspec(1)] + [_const_spec(c.shape) for c in consts],
        out_specs=[rspec(W_A), rspec(W_A), rspec(W_A), rspec(3 * W_B), rspec(W_B),
                   rspec(2 * D_MODEL), rspec(N_HG * LANES)],
        out_shape=[row(W_A, BF16), row(W_A, kv_dtype), row(W_A, kv_dtype), row(3 * W_B, BF16),
                   row(W_B, BF16), row(2 * D_MODEL, BF16), row(N_HG * LANES, F32)],
        compiler_params=_cparams(("parallel", "parallel")),
        name="inproj",
    )(x, mod3, mod3, *consts)


def _pair_softmax_pv(s_parts, v_parts):
    m = s_parts[0].max(axis=-1, keepdims=True)
    for s in s_parts[1:]:
        m = jnp.maximum(m, s.max(axis=-1, keepdims=True))
    den = 0.0
    acc = 0.0
    for s, v in zip(s_parts, v_parts):
        p = jnp.exp(s - m)
        den = den + p.sum(axis=-1, keepdims=True)
        acc = acc + _dot(p.astype(BF16), v)
    return acc / den


def _stack_pair(qp):
    lane = lax.broadcasted_iota(jnp.int32, qp.shape, 1)
    zero = jnp.zeros_like(qp)
    return jnp.concatenate([jnp.where(lane < D_HA, qp, zero), jnp.where(lane >= D_HA, qp, zero)], axis=0)


def _unstack_pair(o2):
    t = o2.shape[0] // 2
    lane = lax.broadcasted_iota(jnp.int32, (t, LANES), 1)
    return jnp.where(lane < D_HA, o2[:t], o2[t:])


def _ctx_attn_kernel(q_ref, k_ref, v_ref, o_ref):
    for p in range(H_A // 2):
        sl = slice(p * LANES, (p + 1) * LANES)
        qs = _stack_pair(q_ref[0, :, sl])
        kp = k_ref[0, :, sl].astype(BF16)
        vp = v_ref[0, :, sl].astype(BF16)
        o2 = _pair_softmax_pv([_dot_nt(qs, kp)], [vp])
        o_ref[0, :, sl] = _unstack_pair(o2).astype(o_ref.dtype)


def _ctx_attn_call(q, k, v):
    B, L, _ = q.shape
    spec = pl.BlockSpec((1, L, W_A), lambda b: (b, 0, 0))
    return pl.pallas_call(
        _ctx_attn_kernel,
        grid=(B,),
        in_specs=[spec, spec, spec],
        out_specs=spec,
        out_shape=jax.ShapeDtypeStruct((B, L, W_A), BF16),
        compiler_params=_cparams(("parallel",)),
        name="ctx_attn",
    )(q, k, v)


def _bias_table_kernel(rpb_ref, o_ref):
    s = pl.program_id(0)
    p = pl.program_id(1)
    c = lax.broadcasted_iota(jnp.int32, (GRID_W, GRID_W), 0)
    kc = lax.broadcasted_iota(jnp.int32, (GRID_W, GRID_W), 1)
    ws = jnp.clip(c - WIN_W // 2, 0, GRID_W - WIN_W)
    in_win = (kc >= ws) & (kc < ws + WIN_W)
    dc = kc - c + (WIN_W - 1)
    n_dc = 2 * WIN_W - 1
    n_dr = 2 * WIN_H - 1
    for hh in range(2):
        for j in range(WIN_H):
            base = ((2 * p + hh) * n_dr + (j - s + WIN_H - 1)) * n_dc
            val = jnp.zeros((GRID_W, GRID_W), F32)
            for d in range(n_dc):
                val = jnp.where(dc == d, rpb_ref[base + d], val)
            o_ref[0, 0, hh * GRID_W:(hh + 1) * GRID_W, j * GRID_W:(j + 1) * GRID_W] = (
                jnp.where(in_win, val, NEG_INF))


def _bias_table_call(rpb):
    return pl.pallas_call(
        _bias_table_kernel,
        grid=(WIN_H, H_A // 2),
        in_specs=[pl.BlockSpec(memory_space=pltpu.SMEM)],
        out_specs=pl.BlockSpec((1, 1, 2 * GRID_W, WIN_H * GRID_W), lambda s, p: (s, p, 0, 0)),
        out_shape=jax.ShapeDtypeStruct((WIN_H, H_A // 2, 2 * GRID_W, WIN_H * GRID_W), F32),
        compiler_params=_cparams(("arbitrary", "arbitrary")),
        name="bias_table",
    )(rpb.reshape(-1))


def _nbr_attn_kernel(q_ref, k_ref, v_ref, kc_ref, vc_ref, bias_ref, o_ref, *, rows):
    r = pl.program_id(1)
    start = pl.multiple_of(jnp.clip(r - WIN_H // 2, 0, rows - WIN_H) * GRID_W, GRID_W)
    for p in range(H_A // 2):
        sl = slice(p * LANES, (p + 1) * LANES)
        qs = _stack_pair(q_ref[0, :, sl])
        kp = k_ref[0, pl.ds(start, WIN_H * GRID_W), sl]
        vp = v_ref[0, pl.ds(start, WIN_H * GRID_W), sl]
        s_loc = _dot_nt(qs, kp) + bias_ref[0, p]
        s_ctx = _dot_nt(qs, kc_ref[0, :, sl])
        o2 = _pair_softmax_pv([s_loc, s_ctx], [vp, vc_ref[0, :, sl]])
        o_ref[0, :, sl] = _unstack_pair(o2).astype(o_ref.dtype)


def _nbr_attn_call(q, k, v, k_ctx, v_ctx, bias):
    B, L, _ = q.shape
    rows = L // GRID_W
    lc = k_ctx.shape[1]
    full = pl.BlockSpec((1, L, W_A), lambda b, r: (b, 0, 0))
    ctx = pl.BlockSpec((1, lc, W_A), lambda b, r: (b, 0, 0))
    rowspec = pl.BlockSpec((1, GRID_W, W_A), lambda b, r: (b, r, 0))
    bspec = pl.BlockSpec((1, H_A // 2, 2 * GRID_W, WIN_H * GRID_W),
                         lambda b, r: (r - jnp.clip(r - WIN_H // 2, 0, rows - WIN_H), 0, 0, 0))
    return pl.pallas_call(
        functools.partial(_nbr_attn_kernel, rows=rows),
        grid=(B, rows),
        in_specs=[rowspec, full, full, ctx, ctx, bspec],
        out_specs=rowspec,
        out_shape=jax.ShapeDtypeStruct((B, L, W_A), BF16),
        compiler_params=_cparams(("parallel", "arbitrary")),
        name="nbr_attn",
    )(q, k, v, k_ctx, v_ctx, bias)


HALO = 16


def _conv_kernel(*refs, rope):
    if rope:
        x_ref, prev_ref, next_ref, w_ref, cos_ref, sin_ref, q_ref, k_ref, v_ref, xe_ref = refs
    else:
        x_ref, prev_ref, next_ref, w_ref, q_ref, k_ref, v_ref, xe_ref = refs
    i = pl.program_id(1)
    n_i = pl.num_programs(1)
    tl = x_ref.shape[1]
    xe_ref[0:HALO] = jnp.where(i > 0, prev_ref[0].astype(F32), 0.0)
    xe_ref[HALO:HALO + tl] = x_ref[0].astype(F32)
    xe_ref[HALO + tl:] = jnp.where(i < n_i - 1, next_ref[0].astype(F32), 0.0)
    half = CONV_K // 2
    y = w_ref[0:1, :] * xe_ref[HALO - half:HALO - half + tl]
    for j in range(1, CONV_K):
        y = y + w_ref[j:j + 1, :] * xe_ref[HALO - half + j:HALO - half + j + tl]
    y = y * jax.nn.sigmoid(y)

    def l2n(t):
        return t * lax.rsqrt(jnp.sum(t * t, axis=-1, keepdims=True) + EPS)

    def rot(t):
        if not rope:
            return t
        lane = lax.broadcasted_iota(jnp.int32, t.shape, 1)
        swapped = jnp.where(lane % 2 == 0, pltpu.roll(t, LANES - 1, 1), pltpu.roll(t, 1, 1))
        return t * cos_ref[...] + swapped * sin_ref[...]

    for h in range(H_B):
        sl = slice(h * D_K, (h + 1) * D_K)
        q_ref[0, :, sl] = (rot(l2n(y[:, sl])) * (D_K ** -0.5)).astype(q_ref.dtype)
        k_ref[0, :, sl] = rot(l2n(y[:, W_B + h * D_K:W_B + (h + 1) * D_K])).astype(k_ref.dtype)
    v_ref[0] = y[:, 2 * W_B:].astype(v_ref.dtype)


def _rope_tables(L):
    pos = jnp.arange(L)
    rows = (pos // GRID_W).astype(F32)
    cols = (pos % GRID_W).astype(F32)
    n_pairs = D_K // 4
    inv = ROPE_BASE ** (-jnp.arange(n_pairs, dtype=F32) / n_pairs)
    ang = jnp.concatenate([rows[:, None] * inv, cols[:, None] * inv], axis=-1)
    cos = jnp.repeat(jnp.cos(ang), 2, axis=-1)
    sin = jnp.repeat(jnp.sin(ang), 2, axis=-1)
    sign = jnp.asarray(np.tile(np.array([-1.0, 1.0], np.float32), D_K // 2))
    return cos, sin * sign


def _conv_call(x, conv_w, rope):
    B, L, C = x.shape
    tl = min(ROW_TILE, L)
    nh = tl // HALO
    n_halo = L // HALO
    cur = pl.BlockSpec((1, tl, C), lambda b, i: (b, i, 0))
    prev = pl.BlockSpec((1, HALO, C), lambda b, i: (b, jnp.maximum(i * nh - 1, 0), 0))
    nxt = pl.BlockSpec((1, HALO, C), lambda b, i: (b, jnp.minimum((i + 1) * nh, n_halo - 1), 0))
    ins = [x, x, x, conv_w]
    specs = [cur, prev, nxt, _const_spec(conv_w.shape)]
    if rope:
        cos, sin = _rope_tables(L)
        ins += [cos, sin]
        specs += [pl.BlockSpec((tl, D_K), lambda b, i: (i, 0))] * 2
    out = pl.BlockSpec((1, tl, W_B), lambda b, i: (b, i, 0))
    return pl.pallas_call(
        functools.partial(_conv_kernel, rope=rope),
        grid=(B, L // tl),
        in_specs=specs,
        out_specs=[out, out, out],
        out_shape=[jax.ShapeDtypeStruct((B, L, W_B), BF16)] * 3,
        scratch_shapes=[pltpu.VMEM((tl + 2 * HALO, C), F32)],
        compiler_params=_cparams(("parallel", "parallel")),
        name="conv",
    )(*ins)


def _tri_inverse(a):
    c = a.shape[0]
    row = lax.broadcasted_iota(jnp.int32, (c, c), 0)
    col = lax.broadcasted_iota(jnp.int32, (c, c), 1)
    t = jnp.where(row == col, 1.0, 0.0) - jnp.where(row // 2 == col // 2, a, 0.0)
    s = 2
    while s < c:
        join = (row // (2 * s) == col // (2 * s)) & (row // s != col // s)
        tb = t.astype(BF16)
        tl = _dot(tb, jnp.where(join, a, 0.0).astype(BF16))
        t = t - _dot(tl.astype(BF16), tb)
        s *= 2
    return t


def _delta_step(q_ref, k_ref, v_ref, ec_ref, er_ref, s_ref, o_ref, t, hh, backward):
    c = CHUNK
    r0 = pl.multiple_of(t * c, c)
    sl = slice(hh * D_K, (hh + 1) * D_K)
    qc = q_ref[0, pl.ds(r0, c), sl]
    kc = k_ref[0, pl.ds(r0, c), sl]
    vc = v_ref[0, pl.ds(r0, c), sl]
    ec = ec_ref[0, pl.ds(r0, c), :]
    er = er_ref[0, t]
    gi = (HP if backward else 0) + hh
    bi = (3 * HP if backward else 2 * HP) + hh
    gc = ec[:, gi:gi + 1]
    gr = er[gi:gi + 1, :]
    beta = ec[:, bi:bi + 1]
    row = lax.broadcasted_iota(jnp.int32, (c, c), 0)
    col = lax.broadcasted_iota(jnp.int32, (c, c), 1)
    incl = (row <= col) if backward else (row >= col)
    strict = (row < col) if backward else (row > col)
    decay = jnp.where(incl, jnp.exp(jnp.where(incl, gc - gr, 0.0)), 0.0)
    kf = kc.astype(F32)
    kb = kf * beta
    eg = jnp.exp(gc)
    kk = _dot_nt(kb.astype(BF16), kc)
    t_inv = _tri_inverse(jnp.where(strict, kk * decay, 0.0))
    rhs = jnp.concatenate([(vc.astype(F32) * beta).astype(BF16), (kb * eg).astype(BF16)], axis=-1)
    uw = _dot(t_inv.astype(BF16), rhs)
    u = uw[:, :D_V]
    w = uw[:, D_V:]
    s = s_ref[hh]
    sb = s.astype(BF16)
    qg = qc.astype(F32) * eg
    wq = _dot(jnp.concatenate([w.astype(BF16), qg.astype(BF16)], axis=0), sb)
    v_new = u - wq[:c]
    vnb = v_new.astype(BF16)
    intra = _dot_nt(qc, kc) * decay
    o = wq[c:] + _dot(intra.astype(BF16), vnb)
    o_ref[0, pl.ds(r0, c), sl] = o.astype(o_ref.dtype)
    g_last = gc[0:1, :] if backward else gc[c - 1:c, :]
    kg = kf * jnp.exp(g_last - gc)
    s_ref[hh] = s * jnp.exp(g_last) + _dot_tn(kg.astype(BF16), vnb)


def _delta_kernel(q_ref, k_ref, v_ref, ec_ref, er_ref, s0f_ref, s0b_ref,
                  of_ref, ob_ref, sf_ref, sb_ref, sf_scr, sb_scr, *, n_chunks):
    sf_scr[...] = s0f_ref[0]
    sb_scr[...] = s0b_ref[0]

    def body(t, carry):
        for hh in range(HP):
            _delta_step(q_ref, k_ref, v_ref, ec_ref, er_ref, sf_scr, of_ref, t, hh, False)
            _delta_step(q_ref, k_ref, v_ref, ec_ref, er_ref, sb_scr, ob_ref, n_chunks - 1 - t, hh, True)
        return carry

    lax.fori_loop(0, n_chunks, body, 0)
    sf_ref[0] = sf_scr[...]
    sb_ref[0] = sb_scr[...]


def _delta_call(q, k, v, ecol, s0f, s0b):
    B, L, _ = q.shape
    n = L // CHUNK
    erow = ecol.reshape(B, n, CHUNK, N_HG, LANES)[..., :4 * HP].transpose(0, 3, 1, 4, 2)
    erow = erow.reshape(B * N_HG, n, 4 * HP, CHUNK)
    seq = pl.BlockSpec((1, L, HP * D_K), lambda b, g: (b, 0, g))
    ecs = pl.BlockSpec((1, L, LANES), lambda b, g: (b, 0, g))
    ers = pl.BlockSpec((1, n, 4 * HP, CHUNK), lambda b, g: (b * N_HG + g, 0, 0, 0))
    st = pl.BlockSpec((1, HP, D_K, D_V), lambda b, g: (b, g, 0, 0))
    return pl.pallas_call(
        functools.partial(_delta_kernel, n_chunks=n),
        grid=(B, N_HG),
        in_specs=[seq, seq, seq, ecs, ers, st, st],
        out_specs=[seq, seq, st, st],
        out_shape=[jax.ShapeDtypeStruct((B, L, W_B), BF16)] * 2
        + [jax.ShapeDtypeStruct((B, H_B, D_K, D_V), F32)] * 2,
        scratch_shapes=[pltpu.VMEM((HP, D_K, D_V), F32)] * 2,
        compiler_params=_cparams(("parallel", "parallel")),
        name="delta",
    )(q, k, v, ecol, erow, s0f, s0b)


def _merge_kernel(x_ref, oa_ref, of_ref, ob_ref, z_ref, gate_ref, g1_ref, sh2_ref, sc2_ref,
                  onw_ref, wpa_ref, wpb_ref, wo_ref, n2w_ref, wr_ref, br_ref,
                  x1_ref, h2_ref, route_ref):
    o = of_ref[0].astype(F32) + ob_ref[0].astype(F32)
    parts = []
    for h in range(H_B):
        t = o[:, h * D_V:(h + 1) * D_V]
        parts.append(t * lax.rsqrt(jnp.mean(t * t, axis=-1, keepdims=True) + EPS))
    ob = jnp.concatenate(parts, axis=-1) * onw_ref[...] * z_ref[0].astype(F32)
    pa = _dot(oa_ref[0], wpa_ref[...])
    pb = _dot(ob.astype(BF16), wpb_ref[...])
    gate = gate_ref[0]
    merged = gate[:, :D_MODEL].astype(F32) * pa + gate[:, D_MODEL:].astype(F32) * pb
    mix = _dot(merged.astype(BF16), wo_ref[...])
    x1 = x_ref[0] + g1_ref[0] * mix
    x1_ref[0] = x1
    ms = jnp.mean(x1 * x1, axis=-1, keepdims=True)
    h2 = x1 * lax.rsqrt(ms + EPS) * n2w_ref[...]
    h2 = h2 * (1.0 + sc2_ref[0]) + sh2_ref[0]
    hi = h2.astype(BF16)
    h2_ref[0] = hi
    lo = (h2 - hi.astype(F32)).astype(BF16)
    logits = _dot(jnp.concatenate([hi, lo, hi], axis=-1), wr_ref[...]) + br_ref[...]
    lane = lax.broadcasted_iota(jnp.int32, logits.shape, 1)
    big = jnp.int32(LANES)

    def first_max(vals, mask):
        mv = jnp.where(mask, vals, NEG_INF)
        m = mv.max(axis=-1, keepdims=True)
        idx = jnp.where(mask & (mv == m), lane, big).min(axis=-1, keepdims=True)
        return m, idx

    gmask = lane < N_GROUPS
    gm, grp = first_max(logits, gmask)
    p_grp = 1.0 / jnp.where(gmask, jnp.exp(logits - gm), 0.0).sum(axis=-1, keepdims=True)
    lo_lane = N_GROUPS + grp * EXP_PER_GROUP
    emask = (lane >= lo_lane) & (lane < lo_lane + EXP_PER_GROUP)
    em, _ = first_max(logits, emask)
    ex = jnp.where(emask, jnp.exp(logits - em), 0.0)
    probs = ex / ex.sum(axis=-1, keepdims=True)
    p1, i1 = first_max(probs, emask)
    p2, i2 = first_max(probs, emask & (lane != i1))
    tot = p1 + p2
    w1 = p_grp * p1 / tot
    w2 = p_grp * p2 / tot
    e1 = (i1 - N_GROUPS).astype(F32)
    e2 = (i2 - N_GROUPS).astype(F32)
    route_ref[0] = jnp.where(lane == 0, e1, jnp.where(lane == 1, e2,
                             jnp.where(lane == 2, w1, jnp.where(lane == 3, w2, 0.0))))


def _merge_call(x, oa, of, ob, z, gate, mod3, mod_stride, w):
    B, L, _ = x.shape
    tm = min(ROW_TILE, L)
    rspec = lambda n: pl.BlockSpec((1, tm, n), lambda b, i: (b, i, 0))
    mspec = lambda k: pl.BlockSpec((1, 1, D_MODEL), lambda b, i: (b * mod_stride + k, 0, 0))
    consts = [w["onw"], w["wpa"], w["wpb"], w["wo"], w["norm2_w"], w["wr3"], w["br"]]
    return pl.pallas_call(
        _merge_kernel,
        grid=(B, L // tm),
        in_specs=[rspec(D_MODEL), rspec(W_A), rspec(W_B), rspec(W_B), rspec(W_B), rspec(2 * D_MODEL),
                  mspec(2), mspec(3), mspec(4)] + [_const_spec(c.shape) for c in consts],
        out_specs=[rspec(D_MODEL), rspec(D_MODEL), rspec(LANES)],
        out_shape=[jax.ShapeDtypeStruct((B, L, D_MODEL), F32),
                   jax.ShapeDtypeStruct((B, L, D_MODEL), BF16),
                   jax.ShapeDtypeStruct((B, L, LANES), F32)],
        compiler_params=_cparams(("parallel", "parallel")),
        name="merge",
    )(x, oa, of, ob, z, gate, mod3, mod3, mod3, *consts)


def _moe_kernel(te_ref, nt_ref, xs_ref, rw_ref, wgu_ref, wd_ref, y_ref):
    i = pl.program_id(0)

    @pl.when(i < nt_ref[0])
    def _():
        gu = _dot(xs_ref[...], wgu_ref[0])
        g = gu[:, :D_FF_E]
        hid = g * jax.nn.sigmoid(g) * gu[:, D_FF_E:] * rw_ref[...]
        y_ref[...] = _dot(hid.astype(BF16), wd_ref[0]).astype(y_ref.dtype)

    @pl.when(i >= nt_ref[0])
    def _():
        y_ref[...] = jnp.zeros_like(y_ref)


def _moe_call(xs, row_w, tile_expert, n_tiles_used, wgu, wd):
    p_pad = xs.shape[0]
    n_tiles = p_pad // MOE_TILE
    grid_spec = pltpu.PrefetchScalarGridSpec(
        num_scalar_prefetch=2,
        grid=(n_tiles,),
        in_specs=[pl.BlockSpec((MOE_TILE, D_MODEL), lambda i, te, nt: (i, 0)),
                  pl.BlockSpec((MOE_TILE, 1), lambda i, te, nt: (i, 0)),
                  pl.BlockSpec((1, D_MODEL, 2 * D_FF_E), lambda i, te, nt: (te[i], 0, 0)),
                  pl.BlockSpec((1, D_FF_E, D_MODEL), lambda i, te, nt: (te[i], 0, 0))],
        out_specs=pl.BlockSpec((MOE_TILE, D_MODEL), lambda i, te, nt: (i, 0)),
    )
    return pl.pallas_call(
        _moe_kernel,
        grid_spec=grid_spec,
        out_shape=jax.ShapeDtypeStruct((p_pad, D_MODEL), BF16),
        compiler_params=_cparams(("arbitrary",)),
        name="moe",
    )(tile_expert, n_tiles_used, xs, row_w, wgu, wd)


def _combine_kernel(x1_ref, ya_ref, yb_ref, g2_ref, o_ref):
    o_ref[0] = x1_ref[0] + g2_ref[0] * (ya_ref[0].astype(F32) + yb_ref[0].astype(F32))


def _combine_call(x1, ya, yb, mod3, mod_stride):
    B, L, _ = x1.shape
    tm = min(ROW_TILE, L)
    rspec = pl.BlockSpec((1, tm, D_MODEL), lambda b, i: (b, i, 0))
    return pl.pallas_call(
        _combine_kernel,
        grid=(B, L // tm),
        in_specs=[rspec, rspec, rspec,
                  pl.BlockSpec((1, 1, D_MODEL), lambda b, i: (b * mod_stride + 5, 0, 0))],
        out_specs=rspec,
        out_shape=jax.ShapeDtypeStruct((B, L, D_MODEL), F32),
        compiler_params=_cparams(("parallel", "parallel")),
        name="combine",
    )(x1, ya, yb, mod3)


def _moe_forward(h2, route, wgu, wd):
    n = h2.shape[0]
    eid = route[:, :2].astype(jnp.int32).reshape(-1)
    wts = route[:, 2:4].reshape(-1)
    tok = jnp.arange(2 * n, dtype=jnp.int32) // 2
    onehot = (eid[:, None] == jnp.arange(N_EXPERTS, dtype=jnp.int32)[None, :]).astype(jnp.int32)
    csum = jnp.cumsum(onehot, axis=0)
    counts = csum[-1]
    rank = jnp.take_along_axis(csum, eid[:, None], axis=1)[:, 0] - 1
    padded = ((counts + MOE_TILE - 1) // MOE_TILE) * MOE_TILE
    seg_end = jnp.cumsum(padded)
    seg_start = seg_end - padded
    pos = seg_start[eid] + rank
    p_pad = 2 * n + N_EXPERTS * MOE_TILE
    src_tok = jnp.zeros((p_pad,), jnp.int32).at[pos].set(tok)
    row_w = jnp.zeros((p_pad,), F32).at[pos].set(wts)
    n_tiles = p_pad // MOE_TILE
    tile_start = jnp.arange(n_tiles, dtype=jnp.int32) * MOE_TILE
    tile_expert = jnp.minimum(jnp.sum((tile_start[:, None] >= seg_end[None, :]).astype(jnp.int32), axis=1),
                              N_EXPERTS - 1)
    n_used = (seg_end[-1] // MOE_TILE).astype(jnp.int32).reshape(1)
    xs = jnp.take(h2, src_tok, axis=0)
    y = _moe_call(xs, row_w[:, None], tile_expert, n_used, wgu, wd)
    pos2 = pos.reshape(n, 2)
    return jnp.take(y, pos2[:, 0], axis=0), jnp.take(y, pos2[:, 1], axis=0)


def _prep_weights(norm1_w, norm2_w, w_in, conv_w, a_log, dt_bias, q_norm_w, k_norm_w, o_norm_w,
                  w_proj_a, w_proj_b, w_out, w_router_g, b_router_g, w_router_e, b_router_e,
                  w_exp_gate, w_exp_up, w_exp_down):
    w = {}
    w["norm1_w"] = norm1_w.reshape(1, D_MODEL)
    w["norm2_w"] = norm2_w.reshape(1, D_MODEL)
    o = 0
    w["wa"] = w_in[:, o:o + 3 * W_A].astype(BF16); o += 3 * W_A
    w["wb"] = w_in[:, o:o + 3 * W_B].astype(BF16); o += 3 * W_B
    w["wz"] = w_in[:, o:o + W_B].astype(BF16); o += W_B
    small = w_in[:, o:o + 4 * H_B]; o += 4 * H_B
    w["wg"] = w_in[:, o:o + 2 * D_MODEL].astype(BF16)
    we = jnp.zeros((D_MODEL, N_HG, LANES), F32)
    al = jnp.zeros((N_HG, LANES), F32)
    db = jnp.zeros((N_HG, LANES), F32)
    for g in range(N_HG):
        for kind in range(4):
            for hh in range(HP):
                we = we.at[:, g, kind * HP + hh].set(small[:, kind * H_B + g * HP + hh])
                if kind < 2:
                    al = al.at[g, kind * HP + hh].set(a_log[kind, g * HP + hh])
                    db = db.at[g, kind * HP + hh].set(dt_bias[kind, g * HP + hh])
    w["we"] = we.reshape(D_MODEL, N_HG * LANES).astype(BF16)
    w["alog_l"] = al.reshape(1, N_HG * LANES)
    w["dtb_l"] = db.reshape(1, N_HG * LANES)
    w["qnw"] = jnp.tile(q_norm_w, H_A).reshape(1, W_A)
    w["knw"] = jnp.tile(k_norm_w, H_A).reshape(1, W_A)
    w["onw"] = jnp.tile(o_norm_w, H_B).reshape(1, W_B)
    blk = np.arange(256) // D_HA
    w["bd256"] = jnp.asarray((blk[:, None] == blk[None, :]).astype(np.float32)).astype(BF16)
    w["conv_w"] = conv_w
    w["wpa"] = w_proj_a.astype(BF16)
    w["wpb"] = w_proj_b.astype(BF16)
    w["wo"] = w_out.astype(BF16)
    wr = jnp.zeros((D_MODEL, LANES), F32)
    wr = wr.at[:, :N_GROUPS].set(w_router_g).at[:, N_GROUPS:N_GROUPS + N_EXPERTS].set(w_router_e)
    wr_hi = wr.astype(BF16)
    wr_lo = (wr - wr_hi.astype(F32)).astype(BF16)
    w["wr3"] = jnp.concatenate([wr_hi, wr_hi, wr_lo], axis=0)
    br = jnp.zeros((1, LANES), F32)
    w["br"] = br.at[0, :N_GROUPS].set(b_router_g).at[0, N_GROUPS:N_GROUPS + N_EXPERTS].set(b_router_e)
    w["wgu"] = jnp.concatenate([w_exp_gate, w_exp_up], axis=-1).astype(BF16)
    w["wd"] = w_exp_down.astype(BF16)
    return w


def _layer(x, mod3, mod_stride, w, ctx):
    B, L, _ = x.shape
    latent = ctx is not None
    kv_dtype = BF16 if latent else F32
    qa, ka, va, qkvb, z, gate, ecol = _inproj_call(x, mod3, mod_stride, w, kv_dtype)
    if latent:
        k_ctx, v_ctx, s0f, s0b, bias = ctx
        oa = _nbr_attn_call(qa, ka, va, k_ctx, v_ctx, bias)
    else:
        s0f = s0b = jnp.zeros((B, H_B, D_K, D_V), F32)
        oa = _ctx_attn_call(qa, ka, va)
    qd, kd, vd = _conv_call(qkvb, w["conv_w"], rope=latent)
    of, ob, s_f, s_b = _delta_call(qd, kd, vd, ecol, s0f, s0b)
    x1, h2, route = _merge_call(x, oa, of, ob, z, gate, mod3, mod_stride, w)
    ya, yb = _moe_forward(h2.reshape(B * L, D_MODEL), route.reshape(B * L, LANES), w["wgu"], w["wd"])
    y = _combine_call(x1, ya.reshape(B, L, D_MODEL), yb.reshape(B, L, D_MODEL), mod3, mod_stride)
    return y, ka, va, s_f, s_b


def kernel(x_prompt, x_sample, c, cache_attn_k, cache_attn_v, state_delta_fwd, state_delta_bwd, c_ctx,
           norm1_w, norm2_w, w_mod, b_mod, w_in, conv_w, a_log, dt_bias, q_norm_w, k_norm_w, rpb,
           o_norm_w, w_proj_a, w_proj_b, w_out, w_router_g, b_router_g, w_router_e, b_router_e,
           w_exp_gate, w_exp_up, w_exp_down):
    assert norm1_w.shape[0] == 1, "single-layer trunk"
    Bp, Lp, _ = x_prompt.shape
    Bs, Ls, _ = x_sample.shape
    l = 0
    w = _prep_weights(norm1_w[l], norm2_w[l], w_in[l], conv_w[l], a_log[l], dt_bias[l], q_norm_w[l],
                      k_norm_w[l], o_norm_w[l], w_proj_a[l], w_proj_b[l], w_out[l], w_router_g[l],
                      b_router_g[l], w_router_e[l], b_router_e[l], w_exp_gate[l], w_exp_up[l],
                      w_exp_down[l])
    cond = jnp.zeros((8, D_MODEL), F32).at[0].set(c_ctx).at[1:1 + Bs].set(c)
    mod = _mod_call(cond, w_mod[l], b_mod[l])
    mod3 = mod.reshape(8 * 6, 1, D_MODEL)
    bias = _bias_table_call(rpb[l])
    y_p, k_a, v_a, s_f, s_b = _layer(x_prompt, mod3[:6], 0, w, None)
    lc = cache_attn_k.shape[2]
    ctx = (cache_attn_k[:, l].reshape(Bs, lc, W_A).astype(BF16),
           cache_attn_v[:, l].reshape(Bs, lc, W_A).astype(BF16),
           state_delta_fwd[:, l], state_delta_bwd[:, l], bias)
    y_s, _, _, _, _ = _layer(x_sample, mod3[6:6 + 6 * Bs], 6, w, ctx)
    new_k = k_a.reshape(Bp, 1, Lp, H_A, D_HA)
    new_v = v_a.reshape(Bp, 1, Lp, H_A, D_HA)
    return (y_p, y_s, new_k, new_v, s_f[:, None], s_b[:, None])
```

```python
import functools
import math

import numpy as np
import jax
import jax.numpy as jnp
from jax import lax
from jax.experimental import pallas as pl
from jax.experimental.pallas import tpu as pltpu

F32 = jnp.float32
BF16 = jnp.bfloat16

D_MODEL = 1024
GRID_W = 64
H_A = 8
D_HA = 64
W_A = H_A * D_HA
WIN_H = 8
WIN_W = 16
H_B = 4
D_K = 128
D_V = 128
W_B = H_B * D_V
CONV_K = 5
CHUNK = 64
ROPE_BASE = 10000.0
N_GROUPS = 4
EXP_PER_GROUP = 8
N_EXPERTS = N_GROUPS * EXP_PER_GROUP
D_FF_E = 256
EPS = 1e-6
NEG_INF = -1e30

LANES = 128
HP = H_B
N_HG = H_B // HP
ROW_TILE = 512
MOE_TILE = 256
VMEM_LIMIT = 56 * 1024 * 1024

HIGHEST = lax.Precision.HIGHEST


def _cparams(sem):
    return pltpu.CompilerParams(dimension_semantics=sem, vmem_limit_bytes=VMEM_LIMIT)


def _dot(a, b):
    return jnp.dot(a, b, preferred_element_type=F32)


def _dot_nt(a, b):
    return lax.dot_general(a, b, (((1,), (1,)), ((), ())), preferred_element_type=F32)


def _dot_tn(a, b):
    return lax.dot_general(a, b, (((0,), (0,)), ((), ())), preferred_element_type=F32)


def _const_spec(shape):
    nd = len(shape)
    return pl.BlockSpec(shape, lambda *_: (0,) * nd)


def _mod_kernel(c_ref, w_ref, b_ref, o_ref):
    c = c_ref[...]
    s = c * jax.nn.sigmoid(c)
    o_ref[...] = jnp.dot(s, w_ref[...], preferred_element_type=F32, precision=HIGHEST) + b_ref[...]


def _mod_call(cond8, w_mod, b_mod):
    n_out = w_mod.shape[1]
    tn = 1024
    return pl.pallas_call(
        _mod_kernel,
        grid=(n_out // tn,),
        in_specs=[pl.BlockSpec((8, D_MODEL), lambda j: (0, 0)),
                  pl.BlockSpec((D_MODEL, tn), lambda j: (0, j)),
                  pl.BlockSpec((1, tn), lambda j: (0, j))],
        out_specs=pl.BlockSpec((8, tn), lambda j: (0, j)),
        out_shape=jax.ShapeDtypeStruct((8, n_out), F32),
        compiler_params=_cparams(("arbitrary",)),
        name="mod",
    )(cond8, w_mod, b_mod.reshape(1, n_out))


def _softplus(x):
    return jnp.maximum(x, 0.0) + jnp.log1p(jnp.exp(-jnp.abs(x)))


def _inproj_kernel(x_ref, sh_ref, sc_ref, nw_ref, wa_ref, wb_ref, wz_ref, wg_ref, we_ref,
                   qnw_ref, knw_ref, bd_ref, alog_ref, dtb_ref, ltri_ref, utri_ref,
                   qa_ref, ka_ref, va_ref, qkvb_ref, z_ref, gate_ref, e_ref):
    x = x_ref[0]
    ms = jnp.mean(x * x, axis=-1, keepdims=True)
    h = x * lax.rsqrt(ms + EPS) * nw_ref[...]
    h = h * (1.0 + sc_ref[0]) + sh_ref[0]
    hb = h.astype(BF16)

    a = _dot(hb, wa_ref[...])
    bd = bd_ref[...]

    def head_rms(t, w):
        sq = (t * t).astype(BF16)
        ss = jnp.concatenate([_dot(sq[:, i * 256:(i + 1) * 256], bd) for i in range(W_A // 256)], axis=-1)
        return t * lax.rsqrt(ss * (1.0 / D_HA) + EPS) * w

    qa = head_rms(a[:, :W_A], qnw_ref[...]) * (D_HA ** -0.5)
    qa_ref[0] = qa.astype(qa_ref.dtype)
    ka_ref[0] = head_rms(a[:, W_A:2 * W_A], knw_ref[...]).astype(ka_ref.dtype)
    va_ref[0] = a[:, 2 * W_A:].astype(va_ref.dtype)

    qkvb_ref[0] = _dot(hb, wb_ref[...]).astype(qkvb_ref.dtype)
    z = _dot(hb, wz_ref[...])
    z_ref[0] = (z * jax.nn.sigmoid(z)).astype(z_ref.dtype)
    gate_ref[0] = jax.nn.sigmoid(_dot(hb, wg_ref[...])).astype(gate_ref.dtype)

    e = _dot(hb, we_ref[...])
    lane = lax.broadcasted_iota(jnp.int32, e.shape, 1) % LANES
    g = -jnp.exp(alog_ref[...]) * _softplus(e + dtb_ref[...])
    act = jnp.where(lane < 2 * HP, g, jax.nn.sigmoid(e))
    cf = jnp.dot(ltri_ref[...], act, preferred_element_type=F32, precision=HIGHEST)
    cb = jnp.dot(utri_ref[...], act, preferred_element_type=F32, precision=HIGHEST)
    e_ref[0] = jnp.where(lane < HP, cf, jnp.where(lane < 2 * HP, cb, act))


def _inproj_call(x, mod3, mod_stride, w, kv_dtype):
    B, L, _ = x.shape
    tm = min(ROW_TILE, L)
    grid = (B, L // tm)
    row = lambda n, dt: jax.ShapeDtypeStruct((B, L, n), dt)
    rspec = lambda n: pl.BlockSpec((1, tm, n), lambda b, i: (b, i, 0))
    mspec = lambda k: pl.BlockSpec((1, 1, D_MODEL), lambda b, i: (b * mod_stride + k, 0, 0))
    t = np.arange(tm)
    same = (t[:, None] // CHUNK) == (t[None, :] // CHUNK)
    ltri = jnp.asarray((same & (t[:, None] >= t[None, :])).astype(np.float32))
    utri = jnp.asarray((same & (t[:, None] <= t[None, :])).astype(np.float32))
    consts = [w["norm1_w"], w["wa"], w["wb"], w["wz"], w["wg"], w["we"], w["qnw"], w["knw"],
              w["bd256"], w["alog_l"], w["dtb_l"], ltri, utri]
    return pl.pallas_call(
        _inproj_kernel,
        grid=grid,
        in_specs=[rspec(D_MODEL), mspec(0), mspec(1)] + [_const_spec(c.shape) for c in consts],
        out_specs=[rspec(W_A), rspec(W_A), rspec(W_A), rspec(3 * W_B), rspec(W_B),
                   rspec(2 * D_MODEL), rspec(N_HG * LANES)],
        out_shape=[row(W_A, BF16), row(W_A, kv_dtype), row(W_A, kv_dtype), row(3 * W_B, BF16),
                   row(W_B, BF16), row(2 * D_MODEL, BF16), row(N_HG * LANES, F32)],
        compiler_params=_cparams(("parallel", "parallel")),
        name="inproj",
    )(x, mod3, mod3, *consts)


def _pair_softmax_pv(s_parts, v_parts):
    m = s_parts[0].max(axis=-1, keepdims=True)
    for s in s_parts[1:]:
        m = jnp.maximum(m, s.max(axis=-1, keepdims=True))
    den = 0.0
    acc = 0.0
    for s, v in zip(s_parts, v_parts):
        p = jnp.exp(s - m)
        den = den + p.sum(axis=-1, keepdims=True)
        acc = acc + _dot(p.astype(BF16), v)
    return acc / den


def _stack_pair(qp):
    lane = lax.broadcasted_iota(jnp.int32, qp.shape, 1)
    zero = jnp.zeros_like(qp)
    return jnp.concatenate([jnp.where(lane < D_HA, qp, zero), jnp.where(lane >= D_HA, qp, zero)], axis=0)


def _unstack_pair(o2):
    t = o2.shape[0] // 2
    lane = lax.broadcasted_iota(jnp.int32, (t, LANES), 1)
    return jnp.where(lane < D_HA, o2[:t], o2[t:])


def _ctx_attn_kernel(q_ref, k_ref, v_ref, o_ref):
    for p in range(H_A // 2):
        sl = slice(p * LANES, (p + 1) * LANES)
        qs = _stack_pair(q_ref[0, :, sl])
        kp = k_ref[0, :, sl].astype(BF16)
        vp = v_ref[0, :, sl].astype(BF16)
        o2 = _pair_softmax_pv([_dot_nt(qs, kp)], [vp])
        o_ref[0, :, sl] = _unstack_pair(o2).astype(o_ref.dtype)


def _ctx_attn_call(q, k, v):
    B, L, _ = q.shape
    spec = pl.BlockSpec((1, L, W_A), lambda b: (b, 0, 0))
    return pl.pallas_call(
        _ctx_attn_kernel,
        grid=(B,),
        in_specs=[spec, spec, spec],
        out_specs=spec,
        out_shape=jax.ShapeDtypeStruct((B, L, W_A), BF16),
        compiler_params=_cparams(("parallel",)),
        name="ctx_attn",
    )(q, k, v)


def _bias_table_kernel(rpb_ref, o_ref):
    s = pl.program_id(0)
    p = pl.program_id(1)
    c = lax.broadcasted_iota(jnp.int32, (GRID_W, GRID_W), 0)
    kc = lax.broadcasted_iota(jnp.int32, (GRID_W, GRID_W), 1)
    ws = jnp.clip(c - WIN_W // 2, 0, GRID_W - WIN_W)
    in_win = (kc >= ws) & (kc < ws + WIN_W)
    dc = kc - c + (WIN_W - 1)
    n_dc = 2 * WIN_W - 1
    n_dr = 2 * WIN_H - 1
    for hh in range(2):
        for j in range(WIN_H):
            base = ((2 * p + hh) * n_dr + (j - s + WIN_H - 1)) * n_dc
            val = jnp.zeros((GRID_W, GRID_W), F32)
            for d in range(n_dc):
                val = jnp.where(dc == d, rpb_ref[base + d], val)
            o_ref[0, 0, hh * GRID_W:(hh + 1) * GRID_W, j * GRID_W:(j + 1) * GRID_W] = (
                jnp.where(in_win, val, NEG_INF))


def _bias_table_call(rpb):
    return pl.pallas_call(
        _bias_table_kernel,
        grid=(WIN_H, H_A // 2),
        in_specs=[pl.BlockSpec(memory_space=pltpu.SMEM)],
        out_specs=pl.BlockSpec((1, 1, 2 * GRID_W, WIN_H * GRID_W), lambda s, p: (s, p, 0, 0)),
        out_shape=jax.ShapeDtypeStruct((WIN_H, H_A // 2, 2 * GRID_W, WIN_H * GRID_W), F32),
        compiler_params=_cparams(("arbitrary", "arbitrary")),
        name="bias_table",
    )(rpb.reshape(-1))


def _nbr_attn_kernel(q_ref, k_ref, v_ref, kc_ref, vc_ref, bias_ref, o_ref, *, rows):
    r = pl.program_id(1)
    start = pl.multiple_of(jnp.clip(r - WIN_H // 2, 0, rows - WIN_H) * GRID_W, GRID_W)
    for p in range(H_A // 2):
        sl = slice(p * LANES, (p + 1) * LANES)
        qs = _stack_pair(q_ref[0, :, sl])
        kp = k_ref[0, pl.ds(start, WIN_H * GRID_W), sl]
        vp = v_ref[0, pl.ds(start, WIN_H * GRID_W), sl]
        s_loc = _dot_nt(qs, kp) + bias_ref[0, p]
        s_ctx = _dot_nt(qs, kc_ref[0, :, sl])
        o2 = _pair_softmax_pv([s_loc, s_ctx], [vp, vc_ref[0, :, sl]])
        o_ref[0, :, sl] = _unstack_pair(o2).astype(o_ref.dtype)


def _nbr_attn_call(q, k, v, k_ctx, v_ctx, bias):
    B, L, _ = q.shape
    rows = L // GRID_W
    lc = k_ctx.shape[1]
    full = pl.BlockSpec((1, L, W_A), lambda b, r: (b, 0, 0))
    ctx = pl.BlockSpec((1, lc, W_A), lambda b, r: (b, 0, 0))
    rowspec = pl.BlockSpec((1, GRID_W, W_A), lambda b, r: (b, r, 0))
    bspec = pl.BlockSpec((1, H_A // 2, 2 * GRID_W, WIN_H * GRID_W),
                         lambda b, r: (r - jnp.clip(r - WIN_H // 2, 0, rows - WIN_H), 0, 0, 0))
    return pl.pallas_call(
        functools.partial(_nbr_attn_kernel, rows=rows),
        grid=(B, rows),
        in_specs=[rowspec, full, full, ctx, ctx, bspec],
        out_specs=rowspec,
        out_shape=jax.ShapeDtypeStruct((B, L, W_A), BF16),
        compiler_params=_cparams(("parallel", "arbitrary")),
        name="nbr_attn",
    )(q, k, v, k_ctx, v_ctx, bias)


HALO = 16


def _conv_kernel(*refs, rope):
    if rope:
        x_ref, prev_ref, next_ref, w_ref, cos_ref, sin_ref, q_ref, k_ref, v_ref, xe_ref = refs
    else:
        x_ref, prev_ref, next_ref, w_ref, q_ref, k_ref, v_ref, xe_ref = refs
    i = pl.program_id(1)
    n_i = pl.num_programs(1)
    tl = x_ref.shape[1]
    xe_ref[0:HALO] = jnp.where(i > 0, prev_ref[0].astype(F32), 0.0)
    xe_ref[HALO:HALO + tl] = x_ref[0].astype(F32)
    xe_ref[HALO + tl:] = jnp.where(i < n_i - 1, next_ref[0].astype(F32), 0.0)
    half = CONV_K // 2
    y = w_ref[0:1, :] * xe_ref[HALO - half:HALO - half + tl]
    for j in range(1, CONV_K):
        y = y + w_ref[j:j + 1, :] * xe_ref[HALO - half + j:HALO - half + j + tl]
    y = y * jax.nn.sigmoid(y)

    def l2n(t):
        return t * lax.rsqrt(jnp.sum(t * t, axis=-1, keepdims=True) + EPS)

    def rot(t):
        if not rope:
            return t
        lane = lax.broadcasted_iota(jnp.int32, t.shape, 1)
        swapped = jnp.where(lane % 2 == 0, pltpu.roll(t, LANES - 1, 1), pltpu.roll(t, 1, 1))
        return t * cos_ref[...] + swapped * sin_ref[...]

    for h in range(H_B):
        sl = slice(h * D_K, (h + 1) * D_K)
        q_ref[0, :, sl] = (rot(l2n(y[:, sl])) * (D_K ** -0.5)).astype(q_ref.dtype)
        k_ref[0, :, sl] = rot(l2n(y[:, W_B + h * D_K:W_B + (h + 1) * D_K])).astype(k_ref.dtype)
    v_ref[0] = y[:, 2 * W_B:].astype(v_ref.dtype)


def _rope_tables(L):
    pos = jnp.arange(L)
    rows = (pos // GRID_W).astype(F32)
    cols = (pos % GRID_W).astype(F32)
    n_pairs = D_K // 4
    inv = ROPE_BASE ** (-jnp.arange(n_pairs, dtype=F32) / n_pairs)
    ang = jnp.concatenate([rows[:, None] * inv, cols[:, None] * inv], axis=-1)
    cos = jnp.repeat(jnp.cos(ang), 2, axis=-1)
    sin = jnp.repeat(jnp.sin(ang), 2, axis=-1)
    sign = jnp.asarray(np.tile(np.array([-1.0, 1.0], np.float32), D_K // 2))
    return cos, sin * sign


def _conv_call(x, conv_w, rope):
    B, L, C = x.shape
    tl = min(ROW_TILE, L)
    nh = tl // HALO
    n_halo = L // HALO
    cur = pl.BlockSpec((1, tl, C), lambda b, i: (b, i, 0))
    prev = pl.BlockSpec((1, HALO, C), lambda b, i: (b, jnp.maximum(i * nh - 1, 0), 0))
    nxt = pl.BlockSpec((1, HALO, C), lambda b, i: (b, jnp.minimum((i + 1) * nh, n_halo - 1), 0))
    ins = [x, x, x, conv_w]
    specs = [cur, prev, nxt, _const_spec(conv_w.shape)]
    if rope:
        cos, sin = _rope_tables(L)
        ins += [cos, sin]
        specs += [pl.BlockSpec((tl, D_K), lambda b, i: (i, 0))] * 2
    out = pl.BlockSpec((1, tl, W_B), lambda b, i: (b, i, 0))
    return pl.pallas_call(
        functools.partial(_conv_kernel, rope=rope),
        grid=(B, L // tl),
        in_specs=specs,
        out_specs=[out, out, out],
        out_shape=[jax.ShapeDtypeStruct((B, L, W_B), BF16)] * 3,
        scratch_shapes=[pltpu.VMEM((tl + 2 * HALO, C), F32)],
        compiler_params=_cparams(("parallel", "parallel")),
        name="conv",
    )(*ins)


DELTA_CT = 4
C4 = H_B * CHUNK


def _blockdiag_rows(x, n_blk):
    w = x.shape[1] // n_blk
    lane_blk = lax.broadcasted_iota(jnp.int32, x.shape, 1) // w
    return jnp.concatenate([jnp.where(lane_blk == b, x, 0.0) for b in range(n_blk)], axis=0).astype(BF16)


def _widen(ec, col0, width):
    blk = lax.broadcasted_iota(jnp.int32, (ec.shape[0], H_B * width), 1) // width
    out = ec[:, col0 + H_B - 1:col0 + H_B]
    for h in range(H_B - 2, -1, -1):
        out = jnp.where(blk == h, ec[:, col0 + h:col0 + h + 1], out)
    return out


def _delta_prep(groups):
    c = CHUNK
    row = lax.broadcasted_iota(jnp.int32, (c, C4), 0)
    coll = lax.broadcasted_iota(jnp.int32, (c, C4), 1) % c
    eye = jnp.where(row == coll, 1.0, 0.0)
    st = []
    for q, k, v, ec, gr, backward in groups:
        gi = H_B if backward else 0
        bi = 3 * H_B if backward else 2 * H_B
        gc64, gc128 = _widen(ec, gi, c), _widen(ec, gi, D_K)
        beta64, beta128 = _widen(ec, bi, c), _widen(ec, bi, D_K)
        incl = (row <= coll) if backward else (row >= coll)
        strict = (row < coll) if backward else (row > coll)
        decay = jnp.where(incl, jnp.exp(jnp.where(incl, gc64 - gr, 0.0)), 0.0)
        kf = k.astype(F32)
        qf = q.astype(F32)
        eg = jnp.exp(gc128)
        g_last = gc128[0:1, :] if backward else gc128[c - 1:c, :]
        st.append(dict(
            kf=kf, qf=qf, decay=decay, strict=strict, beta64=beta64,
            qg=(qf * eg).astype(BF16),
            kg=(kf * jnp.exp(g_last - gc128)).astype(BF16),
            rhs_v=(v.astype(F32) * beta128).astype(BF16),
            rhs_k=(kf * (beta128 * eg)).astype(BF16),
            egl=jnp.exp(g_last)))
    lane2 = lax.broadcasted_iota(jnp.int32, (c, 2 * D_K), 1)
    for s in st:
        outs = []
        for p in range(H_B // 2):
            kp = s["kf"][:, p * 2 * D_K:(p + 1) * 2 * D_K]
            qp = s["qf"][:, p * 2 * D_K:(p + 1) * 2 * D_K]
            rmat = jnp.concatenate([jnp.where(lane2 < D_K, kp, 0.0), jnp.where(lane2 >= D_K, kp, 0.0)], axis=0)
            outs.append(_dot_nt(jnp.concatenate([kp, qp], axis=0).astype(BF16), rmat.astype(BF16)))
        kk = jnp.concatenate([o[:c] for o in outs], axis=1)
        qk = jnp.concatenate([o[c:] for o in outs], axis=1)
        s["a"] = jnp.where(s["strict"], kk * s["decay"], 0.0) * s["beta64"]
        s["intra"] = (qk * s["decay"]).astype(BF16)
        s["t"] = eye - jnp.where(row // 2 == coll // 2, s["a"], 0.0)
    b = 2
    while b < c:
        join = (row // (2 * b) == coll // (2 * b)) & (row // b != coll // b)
        for s in st:
            s["tb"] = s["t"].astype(BF16)
            s["tl"] = _dot(s["tb"], _blockdiag_rows(jnp.where(join, s["a"], 0.0), H_B))
        for s in st:
            s["t"] = s["t"] - _dot(s["tl"].astype(BF16), _blockdiag_rows(s["t"], H_B))
        b *= 2
    zero = jnp.zeros((c, 2 * D_K), BF16)
    for s in st:
        tb = s["t"].astype(BF16)
        us, ws = [], []
        for h in range(H_B):
            rhs = jnp.concatenate([s["rhs_v"][:, h * D_V:(h + 1) * D_V], s["rhs_k"][:, h * D_K:(h + 1) * D_K]], axis=1)
            uw = _dot(tb, jnp.concatenate([zero] * h + [rhs] + [zero] * (H_B - 1 - h), axis=0))
            us.append(uw[:, :D_V])
            ws.append(uw[:, D_V:])
        s["u"] = jnp.concatenate(us, axis=1)
        s["w"] = jnp.concatenate(ws, axis=1).astype(BF16)
    return st


def _delta_scan_step(steps, states):
    c = CHUNK
    zero = jnp.zeros((D_K, D_V), BF16)
    mids = []
    for s, S in zip(steps, states):
        w_s, q_s = [], []
        for p in range(H_B // 2):
            sl = slice(p * 2 * D_K, (p + 1) * 2 * D_K)
            lhs = jnp.concatenate([s["w"][:, sl], s["qg"][:, sl]], axis=0)
            sa, sb = S[2 * p].astype(BF16), S[2 * p + 1].astype(BF16)
            sbd = jnp.concatenate([jnp.concatenate([sa, zero], axis=1), jnp.concatenate([zero, sb], axis=1)], axis=0)
            out = _dot(lhs, sbd)
            w_s.append(out[:c])
            q_s.append(out[c:])
        v_new = s["u"] - jnp.concatenate(w_s, axis=1)
        mids.append((v_new, jnp.concatenate(q_s, axis=1)))
    outs, new_states = [], []
    for s, S, (v_new, q_s) in zip(steps, states, mids):
        vnb = v_new.astype(BF16)
        outs.append(q_s + _dot(s["intra"], _blockdiag_rows(v_new, H_B)))
        new_states.append([S[h] * s["egl"][:, h * D_V:(h + 1) * D_V]
                           + _dot_tn(s["kg"][:, h * D_K:(h + 1) * D_K], vnb[:, h * D_V:(h + 1) * D_V])
                           for h in range(H_B)])
    return outs, new_states


def _delta_kernel(qf_ref, kf_ref, vf_ref, ecf_ref, erf_ref, qb_ref, kb_ref, vb_ref, ecb_ref, erb_ref,
                  s0f_ref, s0b_ref, of_ref, ob_ref, sf_ref, sb_ref, s_scr):
    j = pl.program_id(1)
    c = CHUNK

    @pl.when(j == 0)
    def _():
        s_scr[0] = s0f_ref[0]
        s_scr[1] = s0b_ref[0]

    def group(refs, ci, backward):
        q_ref, k_ref, v_ref, ec_ref, er_ref = refs
        rows = slice(ci * c, (ci + 1) * c)
        d = 1 if backward else 0
        return (q_ref[0, rows, :], k_ref[0, rows, :], v_ref[0, rows, :], ec_ref[0, rows, :],
                er_ref[0, ci, d:d + 1, :], backward)

    fwd = (qf_ref, kf_ref, vf_ref, ecf_ref, erf_ref)
    bwd = (qb_ref, kb_ref, vb_ref, ecb_ref, erb_ref)
    prep = _delta_prep([group(fwd, ci, False) for ci in range(DELTA_CT)]
                       + [group(bwd, ci, True) for ci in range(DELTA_CT)])
    states = [[s_scr[d, h] for h in range(H_B)] for d in range(2)]
    for step in range(DELTA_CT):
        cf, cb = step, DELTA_CT - 1 - step
        outs, states = _delta_scan_step([prep[cf], prep[DELTA_CT + cb]], states)
        of_ref[0, cf * c:(cf + 1) * c, :] = outs[0].astype(of_ref.dtype)
        ob_ref[0, cb * c:(cb + 1) * c, :] = outs[1].astype(ob_ref.dtype)
    for d in range(2):
        for h in range(H_B):
            s_scr[d, h] = states[d][h]

    @pl.when(j == pl.num_programs(1) - 1)
    def _():
        sf_ref[0] = s_scr[0]
        sb_ref[0] = s_scr[1]


def _delta_call(q, k, v, ecol, s0f, s0b):
    B, L, _ = q.shape
    n = L // CHUNK
    tl = DELTA_CT * CHUNK
    nb = L // tl
    erow = ecol[..., :2 * H_B].reshape(B, n, CHUNK, 2, H_B).transpose(0, 1, 3, 4, 2).reshape(B, n, 2, C4)
    fmap = lambda b, j: (b, j, 0)
    bmap = lambda b, j: (b, nb - 1 - j, 0)
    seq = lambda m: pl.BlockSpec((1, tl, W_B), m)
    ecs = lambda m: pl.BlockSpec((1, tl, LANES), m)
    ers = lambda m: pl.BlockSpec((1, DELTA_CT, 2, C4), lambda b, j: m(b, j) + (0,))
    st = pl.BlockSpec((1, H_B, D_K, D_V), lambda b, j: (b, 0, 0, 0))
    return pl.pallas_call(
        _delta_kernel,
        grid=(B, nb),
        in_specs=[seq(fmap), seq(fmap), seq(fmap), ecs(fmap), ers(fmap),
                  seq(bmap), seq(bmap), seq(bmap), ecs(bmap), ers(bmap), st, st],
        out_specs=[seq(fmap), seq(bmap), st, st],
        out_shape=[jax.ShapeDtypeStruct((B, L, W_B), BF16)] * 2
        + [jax.ShapeDtypeStruct((B, H_B, D_K, D_V), F32)] * 2,
        scratch_shapes=[pltpu.VMEM((2, H_B, D_K, D_V), F32)],
        compiler_params=_cparams(("parallel", "arbitrary")),
        name="delta",
    )(q, k, v, ecol, erow, q, k, v, ecol, erow, s0f, s0b)


def _merge_kernel(x_ref, oa_ref, of_ref, ob_ref, z_ref, gate_ref, g1_ref, sh2_ref, sc2_ref,
                  onw_ref, wpa_ref, wpb_ref, wo_ref, n2w_ref, wr_ref, br_ref,
                  x1_ref, h2_ref, route_ref):
    o = of_ref[0].astype(F32) + ob_ref[0].astype(F32)
    parts = []
    for h in range(H_B):
        t = o[:, h * D_V:(h + 1) * D_V]
        parts.append(t * lax.rsqrt(jnp.mean(t * t, axis=-1, keepdims=True) + EPS))
    ob = jnp.concatenate(parts, axis=-1) * onw_ref[...] * z_ref[0].astype(F32)
    pa = _dot(oa_ref[0], wpa_ref[...])
    pb = _dot(ob.astype(BF16), wpb_ref[...])
    gate = gate_ref[0]
    merged = gate[:, :D_MODEL].astype(F32) * pa + gate[:, D_MODEL:].astype(F32) * pb
    mix = _dot(merged.astype(BF16), wo_ref[...])
    x1 = x_ref[0] + g1_ref[0] * mix
    x1_ref[0] = x1
    ms = jnp.mean(x1 * x1, axis=-1, keepdims=True)
    h2 = x1 * lax.rsqrt(ms + EPS) * n2w_ref[...]
    h2 = h2 * (1.0 + sc2_ref[0]) + sh2_ref[0]
    hi = h2.astype(BF16)
    h2_ref[0] = hi
    lo = (h2 - hi.astype(F32)).astype(BF16)
    logits = _dot(jnp.concatenate([hi, lo, hi], axis=-1), wr_ref[...]) + br_ref[...]
    lane = lax.broadcasted_iota(jnp.int32, logits.shape, 1)
    big = jnp.int32(LANES)

    def first_max(vals, mask):
        mv = jnp.where(mask, vals, NEG_INF)
        m = mv.max(axis=-1, keepdims=True)
        idx = jnp.where(mask & (mv == m), lane, big).min(axis=-1, keepdims=True)
        return m, idx

    gmask = lane < N_GROUPS
    gm, grp = first_max(logits, gmask)
    p_grp = 1.0 / jnp.where(gmask, jnp.exp(logits - gm), 0.0).sum(axis=-1, keepdims=True)
    lo_lane = N_GROUPS + grp * EXP_PER_GROUP
    emask = (lane >= lo_lane) & (lane < lo_lane + EXP_PER_GROUP)
    em, _ = first_max(logits, emask)
    ex = jnp.where(emask, jnp.exp(logits - em), 0.0)
    probs = ex / ex.sum(axis=-1, keepdims=True)
    p1, i1 = first_max(probs, emask)
    p2, i2 = first_max(probs, emask & (lane != i1))
    tot = p1 + p2
    w1 = p_grp * p1 / tot
    w2 = p_grp * p2 / tot
    e1 = (i1 - N_GROUPS).astype(F32)
    e2 = (i2 - N_GROUPS).astype(F32)
    route_ref[0] = jnp.where(lane == 0, e1, jnp.where(lane == 1, e2,
                             jnp.where(lane == 2, w1, jnp.where(lane == 3, w2, 0.0))))


def _merge_call(x, oa, of, ob, z, gate, mod3, mod_stride, w):
    B, L, _ = x.shape
    tm = min(ROW_TILE, L)
    rspec = lambda n: pl.BlockSpec((1, tm, n), lambda b, i: (b, i, 0))
    mspec = lambda k: pl.BlockSpec((1, 1, D_MODEL), lambda b, i: (b * mod_stride + k, 0, 0))
    consts = [w["onw"], w["wpa"], w["wpb"], w["wo"], w["norm2_w"], w["wr3"], w["br"]]
    return pl.pallas_call(
        _merge_kernel,
        grid=(B, L // tm),
        in_specs=[rspec(D_MODEL), rspec(W_A), rspec(W_B), rspec(W_B), rspec(W_B), rspec(2 * D_MODEL),
                  mspec(2), mspec(3), mspec(4)] + [_const_spec(c.shape) for c in consts],
        out_specs=[rspec(D_MODEL), rspec(D_MODEL), rspec(LANES)],
        out_shape=[jax.ShapeDtypeStruct((B, L, D_MODEL), F32),
                   jax.ShapeDtypeStruct((B, L, D_MODEL), BF16),
                   jax.ShapeDtypeStruct((B, L, LANES), F32)],
        compiler_params=_cparams(("parallel", "parallel")),
        name="merge",
    )(x, oa, of, ob, z, gate, mod3, mod3, mod3, *consts)


def _moe_kernel(te_ref, nt_ref, xs_ref, rw_ref, wgu_ref, wd_ref, y_ref):
    i = pl.program_id(0)

    @pl.when(i < nt_ref[0])
    def _():
        gu = _dot(xs_ref[...], wgu_ref[0])
        g = gu[:, :D_FF_E]
        hid = g * jax.nn.sigmoid(g) * gu[:, D_FF_E:] * rw_ref[...]
        y_ref[...] = _dot(hid.astype(BF16), wd_ref[0]).astype(y_ref.dtype)

    @pl.when(i >= nt_ref[0])
    def _():
        y_ref[...] = jnp.zeros_like(y_ref)


def _moe_call(xs, row_w, tile_expert, n_tiles_used, wgu, wd):
    p_pad = xs.shape[0]
    n_tiles = p_pad // MOE_TILE
    grid_spec = pltpu.PrefetchScalarGridSpec(
        num_scalar_prefetch=2,
        grid=(n_tiles,),
        in_specs=[pl.BlockSpec((MOE_TILE, D_MODEL), lambda i, te, nt: (i, 0)),
                  pl.BlockSpec((MOE_TILE, 1), lambda i, te, nt: (i, 0)),
                  pl.BlockSpec((1, D_MODEL, 2 * D_FF_E), lambda i, te, nt: (te[i], 0, 0)),
                  pl.BlockSpec((1, D_FF_E, D_MODEL), lambda i, te, nt: (te[i], 0, 0))],
        out_specs=pl.BlockSpec((MOE_TILE, D_MODEL), lambda i, te, nt: (i, 0)),
    )
    return pl.pallas_call(
        _moe_kernel,
        grid_spec=grid_spec,
        out_shape=jax.ShapeDtypeStruct((p_pad, D_MODEL), BF16),
        compiler_params=_cparams(("arbitrary",)),
        name="moe",
    )(tile_expert, n_tiles_used, xs, row_w, wgu, wd)


def _combine_kernel(x1_ref, ya_ref, yb_ref, g2_ref, o_ref):
    o_ref[0] = x1_ref[0] + g2_ref[0] * (ya_ref[0].astype(F32) + yb_ref[0].astype(F32))


def _combine_call(x1, ya, yb, mod3, mod_stride):
    B, L, _ = x1.shape
    tm = min(ROW_TILE, L)
    rspec = pl.BlockSpec((1, tm, D_MODEL), lambda b, i: (b, i, 0))
    return pl.pallas_call(
        _combine_kernel,
        grid=(B, L // tm),
        in_specs=[rspec, rspec, rspec,
                  pl.BlockSpec((1, 1, D_MODEL), lambda b, i: (b * mod_stride + 5, 0, 0))],
        out_specs=rspec,
        out_shape=jax.ShapeDtypeStruct((B, L, D_MODEL), F32),
        compiler_params=_cparams(("parallel", "parallel")),
        name="combine",
    )(x1, ya, yb, mod3)


def _moe_forward(h2, route, wgu, wd):
    n = h2.shape[0]
    eid = route[:, :2].astype(jnp.int32).reshape(-1)
    wts = route[:, 2:4].reshape(-1)
    tok = jnp.arange(2 * n, dtype=jnp.int32) // 2
    onehot = (eid[:, None] == jnp.arange(N_EXPERTS, dtype=jnp.int32)[None, :]).astype(jnp.int32)
    csum = jnp.cumsum(onehot, axis=0)
    counts = csum[-1]
    rank = jnp.take_along_axis(csum, eid[:, None], axis=1)[:, 0] - 1
    padded = ((counts + MOE_TILE - 1) // MOE_TILE) * MOE_TILE
    seg_end = jnp.cumsum(padded)
    seg_start = seg_end - padded
    pos = seg_start[eid] + rank
    p_pad = 2 * n + N_EXPERTS * MOE_TILE
    src_tok = jnp.zeros((p_pad,), jnp.int32).at[pos].set(tok)
    row_w = jnp.zeros((p_pad,), F32).at[pos].set(wts)
    n_tiles = p_pad // MOE_TILE
    tile_start = jnp.arange(n_tiles, dtype=jnp.int32) * MOE_TILE
    tile_expert = jnp.minimum(jnp.sum((tile_start[:, None] >= seg_end[None, :]).astype(jnp.int32), axis=1),
                              N_EXPERTS - 1)
    n_used = (seg_end[-1] // MOE_TILE).astype(jnp.int32).reshape(1)
    xs = jnp.take(h2, src_tok, axis=0)
    y = _moe_call(xs, row_w[:, None], tile_expert, n_used, wgu, wd)
    pos2 = pos.reshape(n, 2)
    return jnp.take(y, pos2[:, 0], axis=0), jnp.take(y, pos2[:, 1], axis=0)


def _prep_weights(norm1_w, norm2_w, w_in, conv_w, a_log, dt_bias, q_norm_w, k_norm_w, o_norm_w,
                  w_proj_a, w_proj_b, w_out, w_router_g, b_router_g, w_router_e, b_router_e,
                  w_exp_gate, w_exp_up, w_exp_down):
    w = {}
    w["norm1_w"] = norm1_w.reshape(1, D_MODEL)
    w["norm2_w"] = norm2_w.reshape(1, D_MODEL)
    o = 0
    w["wa"] = w_in[:, o:o + 3 * W_A].astype(BF16); o += 3 * W_A
    w["wb"] = w_in[:, o:o + 3 * W_B].astype(BF16); o += 3 * W_B
    w["wz"] = w_in[:, o:o + W_B].astype(BF16); o += W_B
    small = w_in[:, o:o + 4 * H_B]; o += 4 * H_B
    w["wg"] = w_in[:, o:o + 2 * D_MODEL].astype(BF16)
    def group_lanes(t):
        r = t.shape[0]
        t = t.reshape(r, -1, N_HG, HP).transpose(0, 2, 1, 3).reshape(r, N_HG, -1)
        return jnp.pad(t, ((0, 0), (0, 0), (0, LANES - t.shape[-1]))).reshape(r, N_HG * LANES)

    w["we"] = group_lanes(small).astype(BF16)
    w["alog_l"] = group_lanes(a_log.reshape(1, 2 * H_B))
    w["dtb_l"] = group_lanes(dt_bias.reshape(1, 2 * H_B))
    w["qnw"] = jnp.tile(q_norm_w, H_A).reshape(1, W_A)
    w["knw"] = jnp.tile(k_norm_w, H_A).reshape(1, W_A)
    w["onw"] = jnp.tile(o_norm_w, H_B).reshape(1, W_B)
    blk = np.arange(256) // D_HA
    w["bd256"] = jnp.asarray((blk[:, None] == blk[None, :]).astype(np.float32)).astype(BF16)
    w["conv_w"] = conv_w
    w["wpa"] = w_proj_a.astype(BF16)
    w["wpb"] = w_proj_b.astype(BF16)
    w["wo"] = w_out.astype(BF16)
    wr = jnp.concatenate([w_router_g, w_router_e], axis=1)
    wr = jnp.pad(wr, ((0, 0), (0, LANES - wr.shape[1])))
    wr_hi = wr.astype(BF16)
    wr_lo = (wr - wr_hi.astype(F32)).astype(BF16)
    w["wr3"] = jnp.concatenate([wr_hi, wr_hi, wr_lo], axis=0)
    br = jnp.concatenate([b_router_g, b_router_e])
    w["br"] = jnp.pad(br, (0, LANES - br.shape[0])).reshape(1, LANES)
    w["wgu"] = jnp.concatenate([w_exp_gate, w_exp_up], axis=-1).astype(BF16)
    w["wd"] = w_exp_down.astype(BF16)
    return w


def _layer(x, mod3, mod_stride, w, ctx):
    B, L, _ = x.shape
    latent = ctx is not None
    kv_dtype = BF16 if latent else F32
    qa, ka, va, qkvb, z, gate, ecol = _inproj_call(x, mod3, mod_stride, w, kv_dtype)
    if latent:
        k_ctx, v_ctx, s0f, s0b, bias = ctx
        oa = _nbr_attn_call(qa, ka, va, k_ctx, v_ctx, bias)
    else:
        s0f = s0b = jnp.zeros((B, H_B, D_K, D_V), F32)
        oa = _ctx_attn_call(qa, ka, va)
    qd, kd, vd = _conv_call(qkvb, w["conv_w"], rope=latent)
    of, ob, s_f, s_b = _delta_call(qd, kd, vd, ecol, s0f, s0b)
    x1, h2, route = _merge_call(x, oa, of, ob, z, gate, mod3, mod_stride, w)
    ya, yb = _moe_forward(h2.reshape(B * L, D_MODEL), route.reshape(B * L, LANES), w["wgu"], w["wd"])
    y = _combine_call(x1, ya.reshape(B, L, D_MODEL), yb.reshape(B, L, D_MODEL), mod3, mod_stride)
    return y, ka, va, s_f, s_b


def kernel(x_prompt, x_sample, c, cache_attn_k, cache_attn_v, state_delta_fwd, state_delta_bwd, c_ctx,
           norm1_w, norm2_w, w_mod, b_mod, w_in, conv_w, a_log, dt_bias, q_norm_w, k_norm_w, rpb,
           o_norm_w, w_proj_a, w_proj_b, w_out, w_router_g, b_router_g, w_router_e, b_router_e,
           w_exp_gate, w_exp_up, w_exp_down):
    assert norm1_w.shape[0] == 1, "single-layer trunk"
    Bp, Lp, _ = x_prompt.shape
    Bs, Ls, _ = x_sample.shape
    l = 0
    w = _prep_weights(norm1_w[l], norm2_w[l], w_in[l], conv_w[l], a_log[l], dt_bias[l], q_norm_w[l],
                      k_norm_w[l], o_norm_w[l], w_proj_a[l], w_proj_b[l], w_out[l], w_router_g[l],
                      b_router_g[l], w_router_e[l], b_router_e[l], w_exp_gate[l], w_exp_up[l],
                      w_exp_down[l])
    cond = jnp.concatenate([c_ctx[None, :], c, jnp.zeros((8 - 1 - Bs, D_MODEL), F32)], axis=0)
    mod = _mod_call(cond, w_mod[l], b_mod[l])
    mod3 = mod.reshape(8 * 6, 1, D_MODEL)
    bias = _bias_table_call(rpb[l])
    y_p, k_a, v_a, s_f, s_b = _layer(x_prompt, mod3[:6], 0, w, None)
    lc = cache_attn_k.shape[2]
    ctx = (cache_attn_k[:, l].reshape(Bs, lc, W_A).astype(BF16),
           cache_attn_v[:, l].reshape(Bs, lc, W_A).astype(BF16),
           state_delta_fwd[:, l], state_delta_bwd[:, l], bias)
    y_s, _, _, _, _ = _layer(x_sample, mod3[6:6 + 6 * Bs], 6, w, ctx)
    new_k = k_a.reshape(Bp, 1, Lp, H_A, D_HA)
    new_v = v_a.reshape(Bp, 1, Lp, H_A, D_HA)
    return (y_p, y_s, new_k, new_v, s_f[:, None], s_b[:, None])
```

```python
import functools
import math

import numpy as np
import jax
import jax.numpy as jnp
from jax import lax
from jax.experimental import pallas as pl
from jax.experimental.pallas import tpu as pltpu

F32 = jnp.float32
BF16 = jnp.bfloat16

D_MODEL = 1024
GRID_W = 64
H_A = 8
D_HA = 64
W_A = H_A * D_HA
WIN_H = 8
WIN_W = 16
H_B = 4
D_K = 128
D_V = 128
W_B = H_B * D_V
CONV_K = 5
CHUNK = 64
ROPE_BASE = 10000.0
N_GROUPS = 4
EXP_PER_GROUP = 8
N_EXPERTS = N_GROUPS * EXP_PER_GROUP
D_FF_E = 256
EPS = 1e-6
NEG_INF = -1e30

LANES = 128
HP = H_B
N_HG = H_B // HP
ROW_TILE = 512
MOE_TILE = 256
VMEM_LIMIT = 56 * 1024 * 1024

HIGHEST = lax.Precision.HIGHEST


def _cparams(sem):
    return pltpu.CompilerParams(dimension_semantics=sem, vmem_limit_bytes=VMEM_LIMIT)


def _dot(a, b):
    return jnp.dot(a, b, preferred_element_type=F32)


def _dot_nt(a, b):
    return lax.dot_general(a, b, (((1,), (1,)), ((), ())), preferred_element_type=F32)


def _dot_tn(a, b):
    return lax.dot_general(a, b, (((0,), (0,)), ((), ())), preferred_element_type=F32)


def _const_spec(shape):
    nd = len(shape)
    return pl.BlockSpec(shape, lambda *_: (0,) * nd)


def _mod_kernel(c_ref, w_ref, b_ref, o_ref):
    c = c_ref[...]
    s = c * jax.nn.sigmoid(c)
    o_ref[...] = jnp.dot(s, w_ref[...], preferred_element_type=F32, precision=HIGHEST) + b_ref[...]


def _mod_call(cond8, w_mod, b_mod):
    n_out = w_mod.shape[1]
    tn = 1024
    return pl.pallas_call(
        _mod_kernel,
        grid=(n_out // tn,),
        in_specs=[pl.BlockSpec((8, D_MODEL), lambda j: (0, 0)),
                  pl.BlockSpec((D_MODEL, tn), lambda j: (0, j)),
                  pl.BlockSpec((1, tn), lambda j: (0, j))],
        out_specs=pl.BlockSpec((8, tn), lambda j: (0, j)),
        out_shape=jax.ShapeDtypeStruct((8, n_out), F32),
        compiler_params=_cparams(("arbitrary",)),
        name="mod",
    )(cond8, w_mod, b_mod.reshape(1, n_out))


def _softplus(x):
    return jnp.maximum(x, 0.0) + jnp.log1p(jnp.exp(-jnp.abs(x)))


def _inproj_kernel(x_ref, sh_ref, sc_ref, nw_ref, wa_ref, wb_ref, wz_ref, wg_ref, we_ref,
                   qnw_ref, knw_ref, bd_ref, alog_ref, dtb_ref, ltri_ref, utri_ref,
                   qa_ref, ka_ref, va_ref, qkvb_ref, z_ref, gate_ref, e_ref):
    x = x_ref[0]
    ms = jnp.mean(x * x, axis=-1, keepdims=True)
    h = x * lax.rsqrt(ms + EPS) * nw_ref[...]
    h = h * (1.0 + sc_ref[0]) + sh_ref[0]
    hb = h.astype(BF16)

    a = _dot(hb, wa_ref[...])
    bd = bd_ref[...]

    def head_rms(t, w):
        sq = (t * t).astype(BF16)
        ss = jnp.concatenate([_dot(sq[:, i * 256:(i + 1) * 256], bd) for i in range(W_A // 256)], axis=-1)
        return t * lax.rsqrt(ss * (1.0 / D_HA) + EPS) * w

    qa = head_rms(a[:, :W_A], qnw_ref[...]) * (D_HA ** -0.5)
    qa_ref[0] = qa.astype(qa_ref.dtype)
    ka_ref[0] = head_rms(a[:, W_A:2 * W_A], knw_ref[...]).astype(ka_ref.dtype)
    va_ref[0] = a[:, 2 * W_A:].astype(va_ref.dtype)

    qkvb_ref[0] = _dot(hb, wb_ref[...]).astype(qkvb_ref.dtype)
    z = _dot(hb, wz_ref[...])
    z_ref[0] = (z * jax.nn.sigmoid(z)).astype(z_ref.dtype)
    gate_ref[0] = jax.nn.sigmoid(_dot(hb, wg_ref[...])).astype(gate_ref.dtype)

    e = _dot(hb, we_ref[...])
    lane = lax.broadcasted_iota(jnp.int32, e.shape, 1) % LANES
    g = -jnp.exp(alog_ref[...]) * _softplus(e + dtb_ref[...])
    act = jnp.where(lane < 2 * HP, g, jax.nn.sigmoid(e))
    cf = jnp.dot(ltri_ref[...], act, preferred_element_type=F32, precision=HIGHEST)
    cb = jnp.dot(utri_ref[...], act, preferred_element_type=F32, precision=HIGHEST)
    e_ref[0] = jnp.where(lane < HP, cf, jnp.where(lane < 2 * HP, cb, act))


def _inproj_call(x, mod3, mod_stride, w, kv_dtype):
    B, L, _ = x.shape
    tm = min(ROW_TILE, L)
    grid = (B, L // tm)
    row = lambda n, dt: jax.ShapeDtypeStruct((B, L, n), dt)
    rspec = lambda n: pl.BlockSpec((1, tm, n), lambda b, i: (b, i, 0))
    mspec = lambda k: pl.BlockSpec((1, 1, D_MODEL), lambda b, i: (b * mod_stride + k, 0, 0))
    t = np.arange(tm)
    same = (t[:, None] // CHUNK) == (t[None, :] // CHUNK)
    ltri = jnp.asarray((same & (t[:, None] >= t[None, :])).astype(np.float32))
    utri = jnp.asarray((same & (t[:, None] <= t[None, :])).astype(np.float32))
    consts = [w["norm1_w"], w["wa"], w["wb"], w["wz"], w["wg"], w["we"], w["qnw"], w["knw"],
              w["bd256"], w["alog_l"], w["dtb_l"], ltri, utri]
    return pl.pallas_call(
        _inproj_kernel,
        grid=grid,
        in_specs=[rspec(D_MODEL), mspec(0), mspec(1)] + [_const_spec(c.shape) for c in consts],
        out_specs=[rspec(W_A), rspec(W_A), rspec(W_A), rspec(3 * W_B), rspec(W_B),
                   rspec(2 * D_MODEL), rspec(N_HG * LANES)],
        out_shape=[row(W_A, BF16), row(W_A, kv_dtype), row(W_A, kv_dtype), row(3 * W_B, BF16),
                   row(W_B, BF16), row(2 * D_MODEL, BF16), row(N_HG * LANES, F32)],
        compiler_params=_cparams(("parallel", "parallel")),
        name="inproj",
    )(x, mod3, mod3, *consts)


def _pair_softmax_pv(s_parts, v_parts):
    m = s_parts[0].max(axis=-1, keepdims=True)
    for s in s_parts[1:]:
        m = jnp.maximum(m, s.max(axis=-1, keepdims=True))
    den = 0.0
    acc = 0.0
    for s, v in zip(s_parts, v_parts):
        p = jnp.exp(s - m)
        den = den + p.sum(axis=-1, keepdims=True)
        acc = acc + _dot(p.astype(BF16), v)
    return acc / den


def _stack_pair(qp):
    lane = lax.broadcasted_iota(jnp.int32, qp.shape, 1)
    zero = jnp.zeros_like(qp)
    return jnp.concatenate([jnp.where(lane < D_HA, qp, zero), jnp.where(lane >= D_HA, qp, zero)], axis=0)


def _unstack_pair(o2):
    t = o2.shape[0] // 2
    lane = lax.broadcasted_iota(jnp.int32, (t, LANES), 1)
    return jnp.where(lane < D_HA, o2[:t], o2[t:])


def _ctx_attn_kernel(q_ref, k_ref, v_ref, o_ref):
    for p in range(H_A // 2):
        sl = slice(p * LANES, (p + 1) * LANES)
        qs = _stack_pair(q_ref[0, :, sl])
        kp = k_ref[0, :, sl].astype(BF16)
        vp = v_ref[0, :, sl].astype(BF16)
        o2 = _pair_softmax_pv([_dot_nt(qs, kp)], [vp])
        o_ref[0, :, sl] = _unstack_pair(o2).astype(o_ref.dtype)


def _ctx_attn_call(q, k, v):
    B, L, _ = q.shape
    spec = pl.BlockSpec((1, L, W_A), lambda b: (b, 0, 0))
    return pl.pallas_call(
        _ctx_attn_kernel,
        grid=(B,),
        in_specs=[spec, spec, spec],
        out_specs=spec,
        out_shape=jax.ShapeDtypeStruct((B, L, W_A), BF16),
        compiler_params=_cparams(("parallel",)),
        name="ctx_attn",
    )(q, k, v)


def _bias_table_kernel(rpb_ref, o_ref):
    s = pl.program_id(0)
    p = pl.program_id(1)
    c = lax.broadcasted_iota(jnp.int32, (GRID_W, GRID_W), 0)
    kc = lax.broadcasted_iota(jnp.int32, (GRID_W, GRID_W), 1)
    ws = jnp.clip(c - WIN_W // 2, 0, GRID_W - WIN_W)
    in_win = (kc >= ws) & (kc < ws + WIN_W)
    dc = kc - c + (WIN_W - 1)
    n_dc = 2 * WIN_W - 1
    n_dr = 2 * WIN_H - 1
    for hh in range(2):
        for j in range(WIN_H):
            base = ((2 * p + hh) * n_dr + (j - s + WIN_H - 1)) * n_dc
            val = jnp.zeros((GRID_W, GRID_W), F32)
            for d in range(n_dc):
                val = jnp.where(dc == d, rpb_ref[base + d], val)
            o_ref[0, 0, hh * GRID_W:(hh + 1) * GRID_W, j * GRID_W:(j + 1) * GRID_W] = (
                jnp.where(in_win, val, NEG_INF))


def _bias_table_call(rpb):
    return pl.pallas_call(
        _bias_table_kernel,
        grid=(WIN_H, H_A // 2),
        in_specs=[pl.BlockSpec(memory_space=pltpu.SMEM)],
        out_specs=pl.BlockSpec((1, 1, 2 * GRID_W, WIN_H * GRID_W), lambda s, p: (s, p, 0, 0)),
        out_shape=jax.ShapeDtypeStruct((WIN_H, H_A // 2, 2 * GRID_W, WIN_H * GRID_W), F32),
        compiler_params=_cparams(("arbitrary", "arbitrary")),
        name="bias_table",
    )(rpb.reshape(-1))


def _nbr_attn_kernel(q_ref, k_ref, v_ref, kc_ref, vc_ref, bias_ref, o_ref, *, rows):
    r = pl.program_id(1)
    start = pl.multiple_of(jnp.clip(r - WIN_H // 2, 0, rows - WIN_H) * GRID_W, GRID_W)
    for p in range(H_A // 2):
        sl = slice(p * LANES, (p + 1) * LANES)
        qs = _stack_pair(q_ref[0, :, sl])
        kp = k_ref[0, pl.ds(start, WIN_H * GRID_W), sl]
        vp = v_ref[0, pl.ds(start, WIN_H * GRID_W), sl]
        s_loc = _dot_nt(qs, kp) + bias_ref[0, p]
        s_ctx = _dot_nt(qs, kc_ref[0, :, sl])
        o2 = _pair_softmax_pv([s_loc, s_ctx], [vp, vc_ref[0, :, sl]])
        o_ref[0, :, sl] = _unstack_pair(o2).astype(o_ref.dtype)


def _nbr_attn_call(q, k, v, k_ctx, v_ctx, bias):
    B, L, _ = q.shape
    rows = L // GRID_W
    lc = k_ctx.shape[1]
    full = pl.BlockSpec((1, L, W_A), lambda b, r: (b, 0, 0))
    ctx = pl.BlockSpec((1, lc, W_A), lambda b, r: (b, 0, 0))
    rowspec = pl.BlockSpec((1, GRID_W, W_A), lambda b, r: (b, r, 0))
    bspec = pl.BlockSpec((1, H_A // 2, 2 * GRID_W, WIN_H * GRID_W),
                         lambda b, r: (r - jnp.clip(r - WIN_H // 2, 0, rows - WIN_H), 0, 0, 0))
    return pl.pallas_call(
        functools.partial(_nbr_attn_kernel, rows=rows),
        grid=(B, rows),
        in_specs=[rowspec, full, full, ctx, ctx, bspec],
        out_specs=rowspec,
        out_shape=jax.ShapeDtypeStruct((B, L, W_A), BF16),
        compiler_params=_cparams(("parallel", "arbitrary")),
        name="nbr_attn",
    )(q, k, v, k_ctx, v_ctx, bias)


HALO = 16


def _conv_kernel(*refs, rope):
    if rope:
        x_ref, prev_ref, next_ref, w_ref, cos_ref, sin_ref, q_ref, k_ref, v_ref, xe_ref = refs
    else:
        x_ref, prev_ref, next_ref, w_ref, q_ref, k_ref, v_ref, xe_ref = refs
    i = pl.program_id(1)
    n_i = pl.num_programs(1)
    tl = x_ref.shape[1]
    xe_ref[0:HALO] = jnp.where(i > 0, prev_ref[0].astype(F32), 0.0)
    xe_ref[HALO:HALO + tl] = x_ref[0].astype(F32)
    xe_ref[HALO + tl:] = jnp.where(i < n_i - 1, next_ref[0].astype(F32), 0.0)
    half = CONV_K // 2
    y = w_ref[0:1, :] * xe_ref[HALO - half:HALO - half + tl]
    for j in range(1, CONV_K):
        y = y + w_ref[j:j + 1, :] * xe_ref[HALO - half + j:HALO - half + j + tl]
    y = y * jax.nn.sigmoid(y)

    def l2n(t):
        return t * lax.rsqrt(jnp.sum(t * t, axis=-1, keepdims=True) + EPS)

    def rot(t):
        if not rope:
            return t
        lane = lax.broadcasted_iota(jnp.int32, t.shape, 1)
        swapped = jnp.where(lane % 2 == 0, pltpu.roll(t, LANES - 1, 1), pltpu.roll(t, 1, 1))
        return t * cos_ref[...] + swapped * sin_ref[...]

    for h in range(H_B):
        sl = slice(h * D_K, (h + 1) * D_K)
        q_ref[0, :, sl] = (rot(l2n(y[:, sl])) * (D_K ** -0.5)).astype(q_ref.dtype)
        k_ref[0, :, sl] = rot(l2n(y[:, W_B + h * D_K:W_B + (h + 1) * D_K])).astype(k_ref.dtype)
    v_ref[0] = y[:, 2 * W_B:].astype(v_ref.dtype)


def _rope_tables(L):
    pos = jnp.arange(L)
    rows = (pos // GRID_W).astype(F32)
    cols = (pos % GRID_W).astype(F32)
    n_pairs = D_K // 4
    inv = ROPE_BASE ** (-jnp.arange(n_pairs, dtype=F32) / n_pairs)
    ang = jnp.concatenate([rows[:, None] * inv, cols[:, None] * inv], axis=-1)
    cos = jnp.repeat(jnp.cos(ang), 2, axis=-1)
    sin = jnp.repeat(jnp.sin(ang), 2, axis=-1)
    sign = jnp.asarray(np.tile(np.array([-1.0, 1.0], np.float32), D_K // 2))
    return cos, sin * sign


def _conv_call(x, conv_w, rope):
    B, L, C = x.shape
    tl = min(ROW_TILE, L)
    nh = tl // HALO
    n_halo = L // HALO
    cur = pl.BlockSpec((1, tl, C), lambda b, i: (b, i, 0))
    prev = pl.BlockSpec((1, HALO, C), lambda b, i: (b, jnp.maximum(i * nh - 1, 0), 0))
    nxt = pl.BlockSpec((1, HALO, C), lambda b, i: (b, jnp.minimum((i + 1) * nh, n_halo - 1), 0))
    ins = [x, x, x, conv_w]
    specs = [cur, prev, nxt, _const_spec(conv_w.shape)]
    if rope:
        cos, sin = _rope_tables(L)
        ins += [cos, sin]
        specs += [pl.BlockSpec((tl, D_K), lambda b, i: (i, 0))] * 2
    out = pl.BlockSpec((1, tl, W_B), lambda b, i: (b, i, 0))
    return pl.pallas_call(
        functools.partial(_conv_kernel, rope=rope),
        grid=(B, L // tl),
        in_specs=specs,
        out_specs=[out, out, out],
        out_shape=[jax.ShapeDtypeStruct((B, L, W_B), BF16)] * 3,
        scratch_shapes=[pltpu.VMEM((tl + 2 * HALO, C), F32)],
        compiler_params=_cparams(("parallel", "parallel")),
        name="conv",
    )(*ins)


DELTA_CT = 4
C4 = H_B * CHUNK


def _blockdiag_rows(x, n_blk):
    w = x.shape[1] // n_blk
    lane_blk = lax.broadcasted_iota(jnp.int32, x.shape, 1) // w
    return jnp.concatenate([jnp.where(lane_blk == b, x, 0.0) for b in range(n_blk)], axis=0).astype(BF16)


def _widen(ec, col0, width):
    blk = lax.broadcasted_iota(jnp.int32, (ec.shape[0], H_B * width), 1) // width
    out = ec[:, col0 + H_B - 1:col0 + H_B]
    for h in range(H_B - 2, -1, -1):
        out = jnp.where(blk == h, ec[:, col0 + h:col0 + h + 1], out)
    return out


def _delta_prep(groups):
    c = CHUNK
    row = lax.broadcasted_iota(jnp.int32, (c, C4), 0)
    coll = lax.broadcasted_iota(jnp.int32, (c, C4), 1) % c
    eye = jnp.where(row == coll, 1.0, 0.0)
    st = []
    for q, k, v, ec, gr, backward in groups:
        gi = H_B if backward else 0
        bi = 3 * H_B if backward else 2 * H_B
        gc64, gc128 = _widen(ec, gi, c), _widen(ec, gi, D_K)
        beta64, beta128 = _widen(ec, bi, c), _widen(ec, bi, D_K)
        incl = (row <= coll) if backward else (row >= coll)
        strict = (row < coll) if backward else (row > coll)
        decay = jnp.where(incl, jnp.exp(jnp.where(incl, gc64 - gr, 0.0)), 0.0)
        kf = k.astype(F32)
        qf = q.astype(F32)
        eg = jnp.exp(gc128)
        g_last = gc128[0:1, :] if backward else gc128[c - 1:c, :]
        st.append(dict(
            kf=kf, qf=qf, decay=decay, strict=strict, beta64=beta64,
            qg=(qf * eg).astype(BF16),
            kg=(kf * jnp.exp(g_last - gc128)).astype(BF16),
            rhs_v=(v.astype(F32) * beta128).astype(BF16),
            rhs_k=(kf * (beta128 * eg)).astype(BF16),
            egl=jnp.exp(g_last)))
    lane2 = lax.broadcasted_iota(jnp.int32, (c, 2 * D_K), 1)
    for s in st:
        outs = []
        for p in range(H_B // 2):
            kp = s["kf"][:, p * 2 * D_K:(p + 1) * 2 * D_K]
            qp = s["qf"][:, p * 2 * D_K:(p + 1) * 2 * D_K]
            rmat = jnp.concatenate([jnp.where(lane2 < D_K, kp, 0.0), jnp.where(lane2 >= D_K, kp, 0.0)], axis=0)
            outs.append(_dot_nt(jnp.concatenate([kp, qp], axis=0).astype(BF16), rmat.astype(BF16)))
        kk = jnp.concatenate([o[:c] for o in outs], axis=1)
        qk = jnp.concatenate([o[c:] for o in outs], axis=1)
        s["a"] = jnp.where(s["strict"], kk * s["decay"], 0.0) * s["beta64"]
        s["intra"] = (qk * s["decay"]).astype(BF16)
        s["t"] = eye - jnp.where(row // 2 == coll // 2, s["a"], 0.0)
    b = 2
    while b < c:
        join = (row // (2 * b) == coll // (2 * b)) & (row // b != coll // b)
        for s in st:
            s["tb"] = s["t"].astype(BF16)
            s["tl"] = _dot(s["tb"], _blockdiag_rows(jnp.where(join, s["a"], 0.0), H_B))
        for s in st:
            s["t"] = s["t"] - _dot(s["tl"].astype(BF16), _blockdiag_rows(s["t"], H_B))
        b *= 2
    zero = jnp.zeros((c, 2 * D_K), BF16)
    for s in st:
        tb = s["t"].astype(BF16)
        us, ws = [], []
        for h in range(H_B):
            rhs = jnp.concatenate([s["rhs_v"][:, h * D_V:(h + 1) * D_V], s["rhs_k"][:, h * D_K:(h + 1) * D_K]], axis=1)
            uw = _dot(tb, jnp.concatenate([zero] * h + [rhs] + [zero] * (H_B - 1 - h), axis=0))
            us.append(uw[:, :D_V])
            ws.append(uw[:, D_V:])
        s["u"] = jnp.concatenate(us, axis=1)
        s["w"] = jnp.concatenate(ws, axis=1).astype(BF16)
    return st


def _delta_scan_step(steps, states):
    c = CHUNK
    zero = jnp.zeros((D_K, D_V), BF16)
    mids = []
    for s, S in zip(steps, states):
        w_s, q_s = [], []
        for p in range(H_B // 2):
            sl = slice(p * 2 * D_K, (p + 1) * 2 * D_K)
            lhs = jnp.concatenate([s["w"][:, sl], s["qg"][:, sl]], axis=0)
            sa, sb = S[2 * p].astype(BF16), S[2 * p + 1].astype(BF16)
            sbd = jnp.concatenate([jnp.concatenate([sa, zero], axis=1), jnp.concatenate([zero, sb], axis=1)], axis=0)
            out = _dot(lhs, sbd)
            w_s.append(out[:c])
            q_s.append(out[c:])
        v_new = s["u"] - jnp.concatenate(w_s, axis=1)
        mids.append((v_new, jnp.concatenate(q_s, axis=1)))
    outs, new_states = [], []
    for s, S, (v_new, q_s) in zip(steps, states, mids):
        vnb = v_new.astype(BF16)
        outs.append(q_s + _dot(s["intra"], _blockdiag_rows(v_new, H_B)))
        new_states.append([S[h] * s["egl"][:, h * D_V:(h + 1) * D_V]
                           + _dot_tn(s["kg"][:, h * D_K:(h + 1) * D_K], vnb[:, h * D_V:(h + 1) * D_V])
                           for h in range(H_B)])
    return outs, new_states


def _delta_kernel(qf_ref, kf_ref, vf_ref, ecf_ref, erf_ref, qb_ref, kb_ref, vb_ref, ecb_ref, erb_ref,
                  s0f_ref, s0b_ref, of_ref, ob_ref, sf_ref, sb_ref, s_scr):
    j = pl.program_id(1)
    c = CHUNK

    @pl.when(j == 0)
    def _():
        s_scr[0] = s0f_ref[0]
        s_scr[1] = s0b_ref[0]

    def group(refs, ci, backward):
        q_ref, k_ref, v_ref, ec_ref, er_ref = refs
        rows = slice(ci * c, (ci + 1) * c)
        d = 1 if backward else 0
        return (q_ref[0, rows, :], k_ref[0, rows, :], v_ref[0, rows, :], ec_ref[0, rows, :],
                er_ref[0, ci, d:d + 1, :], backward)

    fwd = (qf_ref, kf_ref, vf_ref, ecf_ref, erf_ref)
    bwd = (qb_ref, kb_ref, vb_ref, ecb_ref, erb_ref)
    prep = _delta_prep([group(fwd, ci, False) for ci in range(DELTA_CT)]
                       + [group(bwd, ci, True) for ci in range(DELTA_CT)])
    states = [[s_scr[d, h] for h in range(H_B)] for d in range(2)]
    for step in range(DELTA_CT):
        cf, cb = step, DELTA_CT - 1 - step
        outs, states = _delta_scan_step([prep[cf], prep[DELTA_CT + cb]], states)
        of_ref[0, cf * c:(cf + 1) * c, :] = outs[0].astype(of_ref.dtype)
        ob_ref[0, cb * c:(cb + 1) * c, :] = outs[1].astype(ob_ref.dtype)
    for d in range(2):
        for h in range(H_B):
            s_scr[d, h] = states[d][h]

    @pl.when(j == pl.num_programs(1) - 1)
    def _():
        sf_ref[0] = s_scr[0]
        sb_ref[0] = s_scr[1]


def _delta_call(q, k, v, ecol, s0f, s0b):
    B, L, _ = q.shape
    n = L // CHUNK
    tl = DELTA_CT * CHUNK
    nb = L // tl
    erow = ecol[..., :2 * H_B].reshape(B, n, CHUNK, 2, H_B).transpose(0, 1, 3, 4, 2).reshape(B, n, 2, C4)
    fmap = lambda b, j: (b, j, 0)
    bmap = lambda b, j: (b, nb - 1 - j, 0)
    seq = lambda m: pl.BlockSpec((1, tl, W_B), m)
    ecs = lambda m: pl.BlockSpec((1, tl, LANES), m)
    ers = lambda m: pl.BlockSpec((1, DELTA_CT, 2, C4), lambda b, j: m(b, j) + (0,))
    st = pl.BlockSpec((1, H_B, D_K, D_V), lambda b, j: (b, 0, 0, 0))
    return pl.pallas_call(
        _delta_kernel,
        grid=(B, nb),
        in_specs=[seq(fmap), seq(fmap), seq(fmap), ecs(fmap), ers(fmap),
                  seq(bmap), seq(bmap), seq(bmap), ecs(bmap), ers(bmap), st, st],
        out_specs=[seq(fmap), seq(bmap), st, st],
        out_shape=[jax.ShapeDtypeStruct((B, L, W_B), BF16)] * 2
        + [jax.ShapeDtypeStruct((B, H_B, D_K, D_V), F32)] * 2,
        scratch_shapes=[pltpu.VMEM((2, H_B, D_K, D_V), F32)],
        compiler_params=_cparams(("parallel", "arbitrary")),
        name="delta",
    )(q, k, v, ecol, erow, q, k, v, ecol, erow, s0f, s0b)


def _merge_kernel(x_ref, oa_ref, of_ref, ob_ref, z_ref, gate_ref, g1_ref, sh2_ref, sc2_ref,
                  onw_ref, wpa_ref, wpb_ref, wo_ref, n2w_ref, wr_ref, br_ref,
                  x1_ref, h2_ref, route_ref):
    o = of_ref[0].astype(F32) + ob_ref[0].astype(F32)
    parts = []
    for h in range(H_B):
        t = o[:, h * D_V:(h + 1) * D_V]
        parts.append(t * lax.rsqrt(jnp.mean(t * t, axis=-1, keepdims=True) + EPS))
    ob = jnp.concatenate(parts, axis=-1) * onw_ref[...] * z_ref[0].astype(F32)
    pa = _dot(oa_ref[0], wpa_ref[...])
    pb = _dot(ob.astype(BF16), wpb_ref[...])
    gate = gate_ref[0]
    merged = gate[:, :D_MODEL].astype(F32) * pa + gate[:, D_MODEL:].astype(F32) * pb
    mix = _dot(merged.astype(BF16), wo_ref[...])
    x1 = x_ref[0] + g1_ref[0] * mix
    x1_ref[0] = x1
    ms = jnp.mean(x1 * x1, axis=-1, keepdims=True)
    h2 = x1 * lax.rsqrt(ms + EPS) * n2w_ref[...]
    h2 = h2 * (1.0 + sc2_ref[0]) + sh2_ref[0]
    hi = h2.astype(BF16)
    h2_ref[0] = h2
    lo = (h2 - hi.astype(F32)).astype(BF16)
    logits = _dot(jnp.concatenate([hi, lo, hi], axis=-1), wr_ref[...]) + br_ref[...]
    lane = lax.broadcasted_iota(jnp.int32, logits.shape, 1)
    big = jnp.int32(LANES)

    def first_max(vals, mask):
        mv = jnp.where(mask, vals, NEG_INF)
        m = mv.max(axis=-1, keepdims=True)
        idx = jnp.where(mask & (mv == m), lane, big).min(axis=-1, keepdims=True)
        return m, idx

    gmask = lane < N_GROUPS
    gm, grp = first_max(logits, gmask)
    p_grp = 1.0 / jnp.where(gmask, jnp.exp(logits - gm), 0.0).sum(axis=-1, keepdims=True)
    lo_lane = N_GROUPS + grp * EXP_PER_GROUP
    emask = (lane >= lo_lane) & (lane < lo_lane + EXP_PER_GROUP)
    em, _ = first_max(logits, emask)
    ex = jnp.where(emask, jnp.exp(logits - em), 0.0)
    probs = ex / ex.sum(axis=-1, keepdims=True)
    p1, i1 = first_max(probs, emask)
    p2, i2 = first_max(probs, emask & (lane != i1))
    tot = p1 + p2
    w1 = p_grp * p1 / tot
    w2 = p_grp * p2 / tot
    e1 = (i1 - N_GROUPS).astype(F32)
    e2 = (i2 - N_GROUPS).astype(F32)
    route_ref[0] = jnp.where(lane == 0, e1, jnp.where(lane == 1, e2,
                             jnp.where(lane == 2, w1, jnp.where(lane == 3, w2, 0.0))))


def _merge_call(x, oa, of, ob, z, gate, mod3, mod_stride, w):
    B, L, _ = x.shape
    tm = min(ROW_TILE, L)
    rspec = lambda n: pl.BlockSpec((1, tm, n), lambda b, i: (b, i, 0))
    mspec = lambda k: pl.BlockSpec((1, 1, D_MODEL), lambda b, i: (b * mod_stride + k, 0, 0))
    consts = [w["onw"], w["wpa"], w["wpb"], w["wo"], w["norm2_w"], w["wr3"], w["br"]]
    return pl.pallas_call(
        _merge_kernel,
        grid=(B, L // tm),
        in_specs=[rspec(D_MODEL), rspec(W_A), rspec(W_B), rspec(W_B), rspec(W_B), rspec(2 * D_MODEL),
                  mspec(2), mspec(3), mspec(4)] + [_const_spec(c.shape) for c in consts],
        out_specs=[rspec(D_MODEL), rspec(D_MODEL), rspec(LANES)],
        out_shape=[jax.ShapeDtypeStruct((B, L, D_MODEL), F32),
                   jax.ShapeDtypeStruct((B, L, D_MODEL), F32),
                   jax.ShapeDtypeStruct((B, L, LANES), F32)],
        compiler_params=_cparams(("parallel", "parallel")),
        name="merge",
    )(x, oa, of, ob, z, gate, mod3, mod3, mod3, *consts)


def _row_gather_start(idx_ref, n_rows, src_hbm, buf, sem, slot):
    for r in range(n_rows):
        pltpu.make_async_copy(src_hbm.at[pl.ds(idx_ref[0, 0, r], 1), :], buf.at[slot, pl.ds(r, 1), :],
                              sem.at[slot]).start()


def _row_gather_wait(n_rows, src_hbm, buf, sem, slot):
    for r in range(n_rows):
        pltpu.make_async_copy(src_hbm.at[pl.ds(0, 1), :], buf.at[slot, pl.ds(r, 1), :], sem.at[slot]).wait()


def _gather_pipeline(i, n_steps, idx_ref, idx_next_ref, n_rows, src_hbm, buf, sem):
    slot = lax.rem(i, 2)

    @pl.when(i == 0)
    def _():
        _row_gather_start(idx_ref, n_rows, src_hbm, buf, sem, 0)

    @pl.when(i + 1 < n_steps)
    def _():
        _row_gather_start(idx_next_ref, n_rows, src_hbm, buf, sem, 1 - slot)

    @pl.when(i < n_steps)
    def _():
        _row_gather_wait(n_rows, src_hbm, buf, sem, slot)

    return slot


def _moe_kernel(te_ref, nt_ref, idx_ref, idxn_ref, h2_hbm, wgu_ref, wd_ref, y_ref, xbuf, sem):
    i = pl.program_id(0)
    n_used = nt_ref[0]
    slot = _gather_pipeline(i, n_used, idx_ref, idxn_ref, MOE_TILE, h2_hbm, xbuf, sem)

    @pl.when(i < n_used)
    def _():
        gu = _dot(xbuf[slot].astype(BF16), wgu_ref[0])
        g = gu[:, :D_FF_E]
        hid = g * jax.nn.sigmoid(g) * gu[:, D_FF_E:]
        y_ref[...] = _dot(hid.astype(BF16), wd_ref[0])

    @pl.when(i >= n_used)
    def _():
        y_ref[...] = jnp.zeros_like(y_ref)


def _moe_call(h2p, src_tok, tile_expert, n_tiles_used, wgu, wd):
    p_pad = src_tok.shape[0]
    n_tiles = p_pad // MOE_TILE
    idx = src_tok.reshape(n_tiles, 1, MOE_TILE)
    smem_tile = lambda m: pl.BlockSpec((1, 1, MOE_TILE), m, memory_space=pltpu.SMEM)
    grid_spec = pltpu.PrefetchScalarGridSpec(
        num_scalar_prefetch=2,
        grid=(n_tiles,),
        in_specs=[smem_tile(lambda i, te, nt: (i, 0, 0)),
                  smem_tile(lambda i, te, nt: (jnp.minimum(i + 1, n_tiles - 1), 0, 0)),
                  pl.BlockSpec(memory_space=pl.ANY),
                  pl.BlockSpec((1, D_MODEL, 2 * D_FF_E), lambda i, te, nt: (te[i], 0, 0)),
                  pl.BlockSpec((1, D_FF_E, D_MODEL), lambda i, te, nt: (te[i], 0, 0))],
        out_specs=pl.BlockSpec((MOE_TILE, D_MODEL), lambda i, te, nt: (i, 0)),
        scratch_shapes=[pltpu.VMEM((2, MOE_TILE, D_MODEL), F32), pltpu.SemaphoreType.DMA((2,))],
    )
    return pl.pallas_call(
        _moe_kernel,
        grid_spec=grid_spec,
        out_shape=jax.ShapeDtypeStruct((p_pad, D_MODEL), F32),
        compiler_params=_cparams(("arbitrary",)),
        name="moe",
    )(tile_expert, n_tiles_used, idx, idx, h2p, wgu, wd)


COMBINE_TILE = 256


def _combine_kernel(pos_ref, posn_ref, x1_ref, route_ref, g2_ref, y_hbm, o_ref, ybuf, sem):
    i = pl.program_id(0)
    t = COMBINE_TILE
    slot = _gather_pipeline(i, pl.num_programs(0), pos_ref, posn_ref, 2 * t, y_hbm, ybuf, sem)
    route = route_ref[...]
    w1 = route[:, 2:3]
    w2 = route[:, 3:4]
    o_ref[...] = x1_ref[...] + g2_ref[0] * (w1 * ybuf[slot, :t] + w2 * ybuf[slot, t:])


def _combine_call(x1, route, pos2, y, mod3, mod_stride, seq_len):
    n = x1.shape[0]
    t = COMBINE_TILE
    n_tiles = n // t
    pos_tiles = pos2.reshape(n_tiles, t, 2).transpose(0, 2, 1).reshape(n_tiles, 1, 2 * t)
    smem_tile = lambda m: pl.BlockSpec((1, 1, 2 * t), m, memory_space=pltpu.SMEM)
    return pl.pallas_call(
        _combine_kernel,
        grid=(n_tiles,),
        in_specs=[smem_tile(lambda i: (i, 0, 0)),
                  smem_tile(lambda i: (jnp.minimum(i + 1, n_tiles - 1), 0, 0)),
                  pl.BlockSpec((t, D_MODEL), lambda i: (i, 0)),
                  pl.BlockSpec((t, LANES), lambda i: (i, 0)),
                  pl.BlockSpec((1, 1, D_MODEL), lambda i: ((i * t) // seq_len * mod_stride + 5, 0, 0)),
                  pl.BlockSpec(memory_space=pl.ANY)],
        out_specs=pl.BlockSpec((t, D_MODEL), lambda i: (i, 0)),
        out_shape=jax.ShapeDtypeStruct((n, D_MODEL), F32),
        scratch_shapes=[pltpu.VMEM((2, 2 * t, D_MODEL), F32), pltpu.SemaphoreType.DMA((2,))],
        compiler_params=_cparams(("arbitrary",)),
        name="combine",
    )(pos_tiles, pos_tiles, x1, route, mod3, y)


def _moe_forward(h2p, route, wgu, wd):
    n = h2p.shape[0]
    eid = route[:, :2].astype(jnp.int32).reshape(-1)
    tok = jnp.arange(2 * n, dtype=jnp.int32) // 2
    onehot = (eid[:, None] == jnp.arange(N_EXPERTS, dtype=jnp.int32)[None, :]).astype(jnp.int32)
    csum = jnp.cumsum(onehot, axis=0)
    counts = csum[-1]
    rank = jnp.take_along_axis(csum, eid[:, None], axis=1)[:, 0] - 1
    padded = ((counts + MOE_TILE - 1) // MOE_TILE) * MOE_TILE
    seg_end = jnp.cumsum(padded)
    seg_start = seg_end - padded
    pos = seg_start[eid] + rank
    p_pad = 2 * n + N_EXPERTS * MOE_TILE
    src_tok = jnp.zeros((p_pad,), jnp.int32).at[pos].set(tok)
    n_tiles = p_pad // MOE_TILE
    tile_start = jnp.arange(n_tiles, dtype=jnp.int32) * MOE_TILE
    tile_expert = jnp.minimum(jnp.sum((tile_start[:, None] >= seg_end[None, :]).astype(jnp.int32), axis=1),
                              N_EXPERTS - 1)
    n_used = (seg_end[-1] // MOE_TILE).astype(jnp.int32).reshape(1)
    y = _moe_call(h2p, src_tok, tile_expert, n_used, wgu, wd)
    return y, pos.reshape(n, 2)


def _prep_weights(norm1_w, norm2_w, w_in, conv_w, a_log, dt_bias, q_norm_w, k_norm_w, o_norm_w,
                  w_proj_a, w_proj_b, w_out, w_router_g, b_router_g, w_router_e, b_router_e,
                  w_exp_gate, w_exp_up, w_exp_down):
    w = {}
    w["norm1_w"] = norm1_w.reshape(1, D_MODEL)
    w["norm2_w"] = norm2_w.reshape(1, D_MODEL)
    o = 0
    w["wa"] = w_in[:, o:o + 3 * W_A].astype(BF16); o += 3 * W_A
    w["wb"] = w_in[:, o:o + 3 * W_B].astype(BF16); o += 3 * W_B
    w["wz"] = w_in[:, o:o + W_B].astype(BF16); o += W_B
    small = w_in[:, o:o + 4 * H_B]; o += 4 * H_B
    w["wg"] = w_in[:, o:o + 2 * D_MODEL].astype(BF16)
    def group_lanes(t):
        r = t.shape[0]
        t = t.reshape(r, -1, N_HG, HP).transpose(0, 2, 1, 3).reshape(r, N_HG, -1)
        return jnp.pad(t, ((0, 0), (0, 0), (0, LANES - t.shape[-1]))).reshape(r, N_HG * LANES)

    w["we"] = group_lanes(small).astype(BF16)
    w["alog_l"] = group_lanes(a_log.reshape(1, 2 * H_B))
    w["dtb_l"] = group_lanes(dt_bias.reshape(1, 2 * H_B))
    w["qnw"] = jnp.tile(q_norm_w, H_A).reshape(1, W_A)
    w["knw"] = jnp.tile(k_norm_w, H_A).reshape(1, W_A)
    w["onw"] = jnp.tile(o_norm_w, H_B).reshape(1, W_B)
    blk = np.arange(256) // D_HA
    w["bd256"] = jnp.asarray((blk[:, None] == blk[None, :]).astype(np.float32)).astype(BF16)
    w["conv_w"] = conv_w
    w["wpa"] = w_proj_a.astype(BF16)
    w["wpb"] = w_proj_b.astype(BF16)
    w["wo"] = w_out.astype(BF16)
    wr = jnp.concatenate([w_router_g, w_router_e], axis=1)
    wr = jnp.pad(wr, ((0, 0), (0, LANES - wr.shape[1])))
    wr_hi = wr.astype(BF16)
    wr_lo = (wr - wr_hi.astype(F32)).astype(BF16)
    w["wr3"] = jnp.concatenate([wr_hi, wr_hi, wr_lo], axis=0)
    br = jnp.concatenate([b_router_g, b_router_e])
    w["br"] = jnp.pad(br, (0, LANES - br.shape[0])).reshape(1, LANES)
    w["wgu"] = jnp.concatenate([w_exp_gate, w_exp_up], axis=-1).astype(BF16)
    w["wd"] = w_exp_down.astype(BF16)
    return w


def _layer(x, mod3, mod_stride, w, ctx):
    B, L, _ = x.shape
    latent = ctx is not None
    kv_dtype = BF16 if latent else F32
    qa, ka, va, qkvb, z, gate, ecol = _inproj_call(x, mod3, mod_stride, w, kv_dtype)
    if latent:
        k_ctx, v_ctx, s0f, s0b, bias = ctx
        oa = _nbr_attn_call(qa, ka, va, k_ctx, v_ctx, bias)
    else:
        s0f = s0b = jnp.zeros((B, H_B, D_K, D_V), F32)
        oa = _ctx_attn_call(qa, ka, va)
    qd, kd, vd = _conv_call(qkvb, w["conv_w"], rope=latent)
    of, ob, s_f, s_b = _delta_call(qd, kd, vd, ecol, s0f, s0b)
    x1, h2, route = _merge_call(x, oa, of, ob, z, gate, mod3, mod_stride, w)
    route = route.reshape(B * L, LANES)
    ys, pos2 = _moe_forward(h2.reshape(B * L, D_MODEL), route, w["wgu"], w["wd"])
    y = _combine_call(x1.reshape(B * L, D_MODEL), route, pos2, ys, mod3, mod_stride, L)
    return y.reshape(B, L, D_MODEL), ka, va, s_f, s_b


def kernel(x_prompt, x_sample, c, cache_attn_k, cache_attn_v, state_delta_fwd, state_delta_bwd, c_ctx,
           norm1_w, norm2_w, w_mod, b_mod, w_in, conv_w, a_log, dt_bias, q_norm_w, k_norm_w, rpb,
           o_norm_w, w_proj_a, w_proj_b, w_out, w_router_g, b_router_g, w_router_e, b_router_e,
           w_exp_gate, w_exp_up, w_exp_down):
    assert norm1_w.shape[0] == 1, "single-layer trunk"
    Bp, Lp, _ = x_prompt.shape
    Bs, Ls, _ = x_sample.shape
    l = 0
    w = _prep_weights(norm1_w[l], norm2_w[l], w_in[l], conv_w[l], a_log[l], dt_bias[l], q_norm_w[l],
                      k_norm_w[l], o_norm_w[l], w_proj_a[l], w_proj_b[l], w_out[l], w_router_g[l],
                      b_router_g[l], w_router_e[l], b_router_e[l], w_exp_gate[l], w_exp_up[l],
                      w_exp_down[l])
    cond = jnp.concatenate([c_ctx[None, :], c, jnp.zeros((8 - 1 - Bs, D_MODEL), F32)], axis=0)
    mod = _mod_call(cond, w_mod[l], b_mod[l])
    mod3 = mod.reshape(8 * 6, 1, D_MODEL)
    bias = _bias_table_call(rpb[l])
    y_p, k_a, v_a, s_f, s_b = _layer(x_prompt, mod3[:6], 0, w, None)
    lc = cache_attn_k.shape[2]
    ctx = (cache_attn_k[:, l].reshape(Bs, lc, W_A).astype(BF16),
           cache_attn_v[:, l].reshape(Bs, lc, W_A).astype(BF16),
           state_delta_fwd[:, l], state_delta_bwd[:, l], bias)
    y_s, _, _, _, _ = _layer(x_sample, mod3[6:6 + 6 * Bs], 6, w, ctx)
    new_k = k_a.reshape(Bp, 1, Lp, H_A, D_HA)
    new_v = v_a.reshape(Bp, 1, Lp, H_A, D_HA)
    return (y_p, y_s, new_k, new_v, s_f[:, None], s_b[:, None])
```

```python
import functools
import math

import numpy as np
import jax
import jax.numpy as jnp
from jax import lax
from jax.experimental import pallas as pl
from jax.experimental.pallas import tpu as pltpu

F32 = jnp.float32
BF16 = jnp.bfloat16

D_MODEL = 1024
GRID_W = 64
H_A = 8
D_HA = 64
W_A = H_A * D_HA
WIN_H = 8
WIN_W = 16
H_B = 4
D_K = 128
D_V = 128
W_B = H_B * D_V
CONV_K = 5
CHUNK = 64
ROPE_BASE = 10000.0
N_GROUPS = 4
EXP_PER_GROUP = 8
N_EXPERTS = N_GROUPS * EXP_PER_GROUP
D_FF_E = 256
EPS = 1e-6
NEG_INF = -1e30

LANES = 128
HP = H_B
N_HG = H_B // HP
ROW_TILE = 512
MOE_TILE = 256
VMEM_LIMIT = 56 * 1024 * 1024

HIGHEST = lax.Precision.HIGHEST


def _cparams(sem):
    return pltpu.CompilerParams(dimension_semantics=sem, vmem_limit_bytes=VMEM_LIMIT)


def _dot(a, b):
    return jnp.dot(a, b, preferred_element_type=F32)


def _dot_nt(a, b):
    return lax.dot_general(a, b, (((1,), (1,)), ((), ())), preferred_element_type=F32)


def _dot_tn(a, b):
    return lax.dot_general(a, b, (((0,), (0,)), ((), ())), preferred_element_type=F32)


def _const_spec(shape):
    nd = len(shape)
    return pl.BlockSpec(shape, lambda *_: (0,) * nd)


def _mod_kernel(c_ref, w_ref, b_ref, o_ref):
    c = c_ref[...]
    s = c * jax.nn.sigmoid(c)
    o_ref[...] = jnp.dot(s, w_ref[...], preferred_element_type=F32, precision=HIGHEST) + b_ref[...]


def _mod_call(cond8, w_mod, b_mod):
    n_out = w_mod.shape[1]
    tn = 1024
    return pl.pallas_call(
        _mod_kernel,
        grid=(n_out // tn,),
        in_specs=[pl.BlockSpec((8, D_MODEL), lambda j: (0, 0)),
                  pl.BlockSpec((D_MODEL, tn), lambda j: (0, j)),
                  pl.BlockSpec((1, tn), lambda j: (0, j))],
        out_specs=pl.BlockSpec((8, tn), lambda j: (0, j)),
        out_shape=jax.ShapeDtypeStruct((8, n_out), F32),
        compiler_params=_cparams(("arbitrary",)),
        name="mod",
    )(cond8, w_mod, b_mod.reshape(1, n_out))


def _softplus(x):
    return jnp.maximum(x, 0.0) + jnp.log1p(jnp.exp(-jnp.abs(x)))


def _inproj_kernel(x_ref, sh_ref, sc_ref, nw_ref, wa_ref, wb_ref, wz_ref, wg_ref, we_ref,
                   qnw_ref, knw_ref, bd_ref, alog_ref, dtb_ref, ltri_ref, utri_ref,
                   qa_ref, ka_ref, va_ref, qkvb_ref, z_ref, gate_ref, e_ref):
    x = x_ref[0]
    ms = jnp.mean(x * x, axis=-1, keepdims=True)
    h = x * lax.rsqrt(ms + EPS) * nw_ref[...]
    h = h * (1.0 + sc_ref[0]) + sh_ref[0]
    hb = h.astype(BF16)

    a = _dot(hb, wa_ref[...])
    bd = bd_ref[...]

    def head_rms(t, w):
        sq = (t * t).astype(BF16)
        ss = jnp.concatenate([_dot(sq[:, i * 256:(i + 1) * 256], bd) for i in range(W_A // 256)], axis=-1)
        return t * lax.rsqrt(ss * (1.0 / D_HA) + EPS) * w

    qa = head_rms(a[:, :W_A], qnw_ref[...]) * (D_HA ** -0.5)
    qa_ref[0] = qa.astype(qa_ref.dtype)
    ka_ref[0] = head_rms(a[:, W_A:2 * W_A], knw_ref[...]).astype(ka_ref.dtype)
    va_ref[0] = a[:, 2 * W_A:].astype(va_ref.dtype)

    qkvb_ref[0] = _dot(hb, wb_ref[...]).astype(qkvb_ref.dtype)
    z = _dot(hb, wz_ref[...])
    z_ref[0] = (z * jax.nn.sigmoid(z)).astype(z_ref.dtype)
    gate_ref[0] = jax.nn.sigmoid(_dot(hb, wg_ref[...])).astype(gate_ref.dtype)

    e = _dot(hb, we_ref[...])
    lane = lax.broadcasted_iota(jnp.int32, e.shape, 1) % LANES
    g = -jnp.exp(alog_ref[...]) * _softplus(e + dtb_ref[...])
    act = jnp.where(lane < 2 * HP, g, jax.nn.sigmoid(e))
    cf = jnp.dot(ltri_ref[...], act, preferred_element_type=F32, precision=HIGHEST)
    cb = jnp.dot(utri_ref[...], act, preferred_element_type=F32, precision=HIGHEST)
    e_ref[0] = jnp.where(lane < HP, cf, jnp.where(lane < 2 * HP, cb, act))


def _inproj_call(x, mod3, mod_stride, w, kv_dtype):
    B, L, _ = x.shape
    tm = min(ROW_TILE, L)
    grid = (B, L // tm)
    row = lambda n, dt: jax.ShapeDtypeStruct((B, L, n), dt)
    rspec = lambda n: pl.BlockSpec((1, tm, n), lambda b, i: (b, i, 0))
    mspec = lambda k: pl.BlockSpec((1, 1, D_MODEL), lambda b, i: (b * mod_stride + k, 0, 0))
    t = np.arange(tm)
    same = (t[:, None] // CHUNK) == (t[None, :] // CHUNK)
    ltri = jnp.asarray((same & (t[:, None] >= t[None, :])).astype(np.float32))
    utri = jnp.asarray((same & (t[:, None] <= t[None, :])).astype(np.float32))
    consts = [w["norm1_w"], w["wa"], w["wb"], w["wz"], w["wg"], w["we"], w["qnw"], w["knw"],
              w["bd256"], w["alog_l"], w["dtb_l"], ltri, utri]
    return pl.pallas_call(
        _inproj_kernel,
        grid=grid,
        in_specs=[rspec(D_MODEL), mspec(0), mspec(1)] + [_const_spec(c.shape) for c in consts],
        out_specs=[rspec(W_A), rspec(W_A), rspec(W_A), rspec(3 * W_B), rspec(W_B),
                   rspec(2 * D_MODEL), rspec(N_HG * LANES)],
        out_shape=[row(W_A, BF16), row(W_A, kv_dtype), row(W_A, kv_dtype), row(3 * W_B, BF16),
                   row(W_B, BF16), row(2 * D_MODEL, BF16), row(N_HG * LANES, F32)],
        compiler_params=_cparams(("parallel", "parallel")),
        name="inproj",
    )(x, mod3, mod3, *consts)


def _pair_softmax_pv(s_parts, v_parts):
    m = s_parts[0].max(axis=-1, keepdims=True)
    for s in s_parts[1:]:
        m = jnp.maximum(m, s.max(axis=-1, keepdims=True))
    den = 0.0
    acc = 0.0
    for s, v in zip(s_parts, v_parts):
        p = jnp.exp(s - m)
        den = den + p.sum(axis=-1, keepdims=True)
        acc = acc + _dot(p.astype(BF16), v)
    return acc / den


def _stack_pair(qp):
    lane = lax.broadcasted_iota(jnp.int32, qp.shape, 1)
    zero = jnp.zeros_like(qp)
    return jnp.concatenate([jnp.where(lane < D_HA, qp, zero), jnp.where(lane >= D_HA, qp, zero)], axis=0)


def _unstack_pair(o2):
    t = o2.shape[0] // 2
    lane = lax.broadcasted_iota(jnp.int32, (t, LANES), 1)
    return jnp.where(lane < D_HA, o2[:t], o2[t:])


def _ctx_attn_kernel(q_ref, k_ref, v_ref, o_ref):
    for p in range(H_A // 2):
        sl = slice(p * LANES, (p + 1) * LANES)
        qs = _stack_pair(q_ref[0, :, sl])
        kp = k_ref[0, :, sl].astype(BF16)
        vp = v_ref[0, :, sl].astype(BF16)
        o2 = _pair_softmax_pv([_dot_nt(qs, kp)], [vp])
        o_ref[0, :, sl] = _unstack_pair(o2).astype(o_ref.dtype)


def _ctx_attn_call(q, k, v):
    B, L, _ = q.shape
    spec = pl.BlockSpec((1, L, W_A), lambda b: (b, 0, 0))
    return pl.pallas_call(
        _ctx_attn_kernel,
        grid=(B,),
        in_specs=[spec, spec, spec],
        out_specs=spec,
        out_shape=jax.ShapeDtypeStruct((B, L, W_A), BF16),
        compiler_params=_cparams(("parallel",)),
        name="ctx_attn",
    )(q, k, v)


def _bias_table_kernel(rpb_ref, o_ref):
    s = pl.program_id(0)
    p = pl.program_id(1)
    c = lax.broadcasted_iota(jnp.int32, (GRID_W, GRID_W), 0)
    kc = lax.broadcasted_iota(jnp.int32, (GRID_W, GRID_W), 1)
    ws = jnp.clip(c - WIN_W // 2, 0, GRID_W - WIN_W)
    in_win = (kc >= ws) & (kc < ws + WIN_W)
    dc = kc - c + (WIN_W - 1)
    n_dc = 2 * WIN_W - 1
    n_dr = 2 * WIN_H - 1
    for hh in range(2):
        for j in range(WIN_H):
            base = ((2 * p + hh) * n_dr + (j - s + WIN_H - 1)) * n_dc
            val = jnp.zeros((GRID_W, GRID_W), F32)
            for d in range(n_dc):
                val = jnp.where(dc == d, rpb_ref[base + d], val)
            o_ref[0, 0, hh * GRID_W:(hh + 1) * GRID_W, j * GRID_W:(j + 1) * GRID_W] = (
                jnp.where(in_win, val, NEG_INF))


def _bias_table_call(rpb):
    return pl.pallas_call(
        _bias_table_kernel,
        grid=(WIN_H, H_A // 2),
        in_specs=[pl.BlockSpec(memory_space=pltpu.SMEM)],
        out_specs=pl.BlockSpec((1, 1, 2 * GRID_W, WIN_H * GRID_W), lambda s, p: (s, p, 0, 0)),
        out_shape=jax.ShapeDtypeStruct((WIN_H, H_A // 2, 2 * GRID_W, WIN_H * GRID_W), F32),
        compiler_params=_cparams(("arbitrary", "arbitrary")),
        name="bias_table",
    )(rpb.reshape(-1))


def _nbr_attn_kernel(q_ref, k_ref, v_ref, kc_ref, vc_ref, bias_ref, o_ref, *, rows):
    r = pl.program_id(1)
    start = pl.multiple_of(jnp.clip(r - WIN_H // 2, 0, rows - WIN_H) * GRID_W, GRID_W)
    for p in range(H_A // 2):
        sl = slice(p * LANES, (p + 1) * LANES)
        qs = _stack_pair(q_ref[0, :, sl])
        kp = k_ref[0, pl.ds(start, WIN_H * GRID_W), sl]
        vp = v_ref[0, pl.ds(start, WIN_H * GRID_W), sl]
        s_loc = _dot_nt(qs, kp) + bias_ref[0, p]
        s_ctx = _dot_nt(qs, kc_ref[0, :, sl])
        o2 = _pair_softmax_pv([s_loc, s_ctx], [vp, vc_ref[0, :, sl]])
        o_ref[0, :, sl] = _unstack_pair(o2).astype(o_ref.dtype)


def _nbr_attn_call(q, k, v, k_ctx, v_ctx, bias):
    B, L, _ = q.shape
    rows = L // GRID_W
    lc = k_ctx.shape[1]
    full = pl.BlockSpec((1, L, W_A), lambda b, r: (b, 0, 0))
    ctx = pl.BlockSpec((1, lc, W_A), lambda b, r: (b, 0, 0))
    rowspec = pl.BlockSpec((1, GRID_W, W_A), lambda b, r: (b, r, 0))
    bspec = pl.BlockSpec((1, H_A // 2, 2 * GRID_W, WIN_H * GRID_W),
                         lambda b, r: (r - jnp.clip(r - WIN_H // 2, 0, rows - WIN_H), 0, 0, 0))
    return pl.pallas_call(
        functools.partial(_nbr_attn_kernel, rows=rows),
        grid=(B, rows),
        in_specs=[rowspec, full, full, ctx, ctx, bspec],
        out_specs=rowspec,
        out_shape=jax.ShapeDtypeStruct((B, L, W_A), BF16),
        compiler_params=_cparams(("parallel", "arbitrary")),
        name="nbr_attn",
    )(q, k, v, k_ctx, v_ctx, bias)


HALO = 16


def _conv_kernel(*refs, rope):
    if rope:
        x_ref, prev_ref, next_ref, w_ref, cos_ref, sin_ref, q_ref, k_ref, v_ref, xe_ref = refs
    else:
        x_ref, prev_ref, next_ref, w_ref, q_ref, k_ref, v_ref, xe_ref = refs
    i = pl.program_id(1)
    n_i = pl.num_programs(1)
    tl = x_ref.shape[1]
    xe_ref[0:HALO] = jnp.where(i > 0, prev_ref[0].astype(F32), 0.0)
    xe_ref[HALO:HALO + tl] = x_ref[0].astype(F32)
    xe_ref[HALO + tl:] = jnp.where(i < n_i - 1, next_ref[0].astype(F32), 0.0)
    half = CONV_K // 2
    y = w_ref[0:1, :] * xe_ref[HALO - half:HALO - half + tl]
    for j in range(1, CONV_K):
        y = y + w_ref[j:j + 1, :] * xe_ref[HALO - half + j:HALO - half + j + tl]
    y = y * jax.nn.sigmoid(y)

    def l2n(t):
        return t * lax.rsqrt(jnp.sum(t * t, axis=-1, keepdims=True) + EPS)

    def rot(t):
        if not rope:
            return t
        lane = lax.broadcasted_iota(jnp.int32, t.shape, 1)
        swapped = jnp.where(lane % 2 == 0, pltpu.roll(t, LANES - 1, 1), pltpu.roll(t, 1, 1))
        return t * cos_ref[...] + swapped * sin_ref[...]

    for h in range(H_B):
        sl = slice(h * D_K, (h + 1) * D_K)
        q_ref[0, :, sl] = (rot(l2n(y[:, sl])) * (D_K ** -0.5)).astype(q_ref.dtype)
        k_ref[0, :, sl] = rot(l2n(y[:, W_B + h * D_K:W_B + (h + 1) * D_K])).astype(k_ref.dtype)
    v_ref[0] = y[:, 2 * W_B:].astype(v_ref.dtype)


def _rope_tables(L):
    pos = jnp.arange(L)
    rows = (pos // GRID_W).astype(F32)
    cols = (pos % GRID_W).astype(F32)
    n_pairs = D_K // 4
    inv = ROPE_BASE ** (-jnp.arange(n_pairs, dtype=F32) / n_pairs)
    ang = jnp.concatenate([rows[:, None] * inv, cols[:, None] * inv], axis=-1)
    cos = jnp.repeat(jnp.cos(ang), 2, axis=-1)
    sin = jnp.repeat(jnp.sin(ang), 2, axis=-1)
    sign = jnp.asarray(np.tile(np.array([-1.0, 1.0], np.float32), D_K // 2))
    return cos, sin * sign


def _conv_call(x, conv_w, rope):
    B, L, C = x.shape
    tl = min(ROW_TILE, L)
    nh = tl // HALO
    n_halo = L // HALO
    cur = pl.BlockSpec((1, tl, C), lambda b, i: (b, i, 0))
    prev = pl.BlockSpec((1, HALO, C), lambda b, i: (b, jnp.maximum(i * nh - 1, 0), 0))
    nxt = pl.BlockSpec((1, HALO, C), lambda b, i: (b, jnp.minimum((i + 1) * nh, n_halo - 1), 0))
    ins = [x, x, x, conv_w]
    specs = [cur, prev, nxt, _const_spec(conv_w.shape)]
    if rope:
        cos, sin = _rope_tables(L)
        ins += [cos, sin]
        specs += [pl.BlockSpec((tl, D_K), lambda b, i: (i, 0))] * 2
    out = pl.BlockSpec((1, tl, W_B), lambda b, i: (b, i, 0))
    return pl.pallas_call(
        functools.partial(_conv_kernel, rope=rope),
        grid=(B, L // tl),
        in_specs=specs,
        out_specs=[out, out, out],
        out_shape=[jax.ShapeDtypeStruct((B, L, W_B), BF16)] * 3,
        scratch_shapes=[pltpu.VMEM((tl + 2 * HALO, C), F32)],
        compiler_params=_cparams(("parallel", "parallel")),
        name="conv",
    )(*ins)


DELTA_CT = 4
C4 = H_B * CHUNK


def _blockdiag_rows(x, n_blk):
    w = x.shape[1] // n_blk
    lane_blk = lax.broadcasted_iota(jnp.int32, x.shape, 1) // w
    return jnp.concatenate([jnp.where(lane_blk == b, x, 0.0) for b in range(n_blk)], axis=0).astype(BF16)


def _widen(ec, col0, width):
    blk = lax.broadcasted_iota(jnp.int32, (ec.shape[0], H_B * width), 1) // width
    out = ec[:, col0 + H_B - 1:col0 + H_B]
    for h in range(H_B - 2, -1, -1):
        out = jnp.where(blk == h, ec[:, col0 + h:col0 + h + 1], out)
    return out


def _delta_prep(groups):
    c = CHUNK
    row = lax.broadcasted_iota(jnp.int32, (c, C4), 0)
    coll = lax.broadcasted_iota(jnp.int32, (c, C4), 1) % c
    eye = jnp.where(row == coll, 1.0, 0.0)
    st = []
    for q, k, v, ec, gr, backward in groups:
        gi = H_B if backward else 0
        bi = 3 * H_B if backward else 2 * H_B
        gc64, gc128 = _widen(ec, gi, c), _widen(ec, gi, D_K)
        beta64, beta128 = _widen(ec, bi, c), _widen(ec, bi, D_K)
        incl = (row <= coll) if backward else (row >= coll)
        strict = (row < coll) if backward else (row > coll)
        decay = jnp.where(incl, jnp.exp(jnp.where(incl, gc64 - gr, 0.0)), 0.0)
        kf = k.astype(F32)
        qf = q.astype(F32)
        eg = jnp.exp(gc128)
        g_last = gc128[0:1, :] if backward else gc128[c - 1:c, :]
        st.append(dict(
            kf=kf, qf=qf, decay=decay, strict=strict, beta64=beta64,
            qg=(qf * eg).astype(BF16),
            kg=(kf * jnp.exp(g_last - gc128)).astype(BF16),
            rhs_v=(v.astype(F32) * beta128).astype(BF16),
            rhs_k=(kf * (beta128 * eg)).astype(BF16),
            egl=jnp.exp(g_last)))
    lane2 = lax.broadcasted_iota(jnp.int32, (c, 2 * D_K), 1)
    for s in st:
        outs = []
        for p in range(H_B // 2):
            kp = s["kf"][:, p * 2 * D_K:(p + 1) * 2 * D_K]
            qp = s["qf"][:, p * 2 * D_K:(p + 1) * 2 * D_K]
            rmat = jnp.concatenate([jnp.where(lane2 < D_K, kp, 0.0), jnp.where(lane2 >= D_K, kp, 0.0)], axis=0)
            outs.append(_dot_nt(jnp.concatenate([kp, qp], axis=0).astype(BF16), rmat.astype(BF16)))
        kk = jnp.concatenate([o[:c] for o in outs], axis=1)
        qk = jnp.concatenate([o[c:] for o in outs], axis=1)
        s["a"] = jnp.where(s["strict"], kk * s["decay"], 0.0) * s["beta64"]
        s["intra"] = (qk * s["decay"]).astype(BF16)
        s["t"] = eye - jnp.where(row // 2 == coll // 2, s["a"], 0.0)
    b = 2
    while b < c:
        join = (row // (2 * b) == coll // (2 * b)) & (row // b != coll // b)
        for s in st:
            s["tb"] = s["t"].astype(BF16)
            s["tl"] = _dot(s["tb"], _blockdiag_rows(jnp.where(join, s["a"], 0.0), H_B))
        for s in st:
            s["t"] = s["t"] - _dot(s["tl"].astype(BF16), _blockdiag_rows(s["t"], H_B))
        b *= 2
    zero = jnp.zeros((c, 2 * D_K), BF16)
    for s in st:
        tb = s["t"].astype(BF16)
        us, ws = [], []
        for h in range(H_B):
            rhs = jnp.concatenate([s["rhs_v"][:, h * D_V:(h + 1) * D_V], s["rhs_k"][:, h * D_K:(h + 1) * D_K]], axis=1)
            uw = _dot(tb, jnp.concatenate([zero] * h + [rhs] + [zero] * (H_B - 1 - h), axis=0))
            us.append(uw[:, :D_V])
            ws.append(uw[:, D_V:])
        s["u"] = jnp.concatenate(us, axis=1)
        s["w"] = jnp.concatenate(ws, axis=1).astype(BF16)
    return st


def _delta_scan_step(steps, states):
    c = CHUNK
    zero = jnp.zeros((D_K, D_V), BF16)
    mids = []
    for s, S in zip(steps, states):
        w_s, q_s = [], []
        for p in range(H_B // 2):
            sl = slice(p * 2 * D_K, (p + 1) * 2 * D_K)
            lhs = jnp.concatenate([s["w"][:, sl], s["qg"][:, sl]], axis=0)
            sa, sb = S[2 * p].astype(BF16), S[2 * p + 1].astype(BF16)
            sbd = jnp.concatenate([jnp.concatenate([sa, zero], axis=1), jnp.concatenate([zero, sb], axis=1)], axis=0)
            out = _dot(lhs, sbd)
            w_s.append(out[:c])
            q_s.append(out[c:])
        v_new = s["u"] - jnp.concatenate(w_s, axis=1)
        mids.append((v_new, jnp.concatenate(q_s, axis=1)))
    outs, new_states = [], []
    for s, S, (v_new, q_s) in zip(steps, states, mids):
        vnb = v_new.astype(BF16)
        outs.append(q_s + _dot(s["intra"], _blockdiag_rows(v_new, H_B)))
        new_states.append([S[h] * s["egl"][:, h * D_V:(h + 1) * D_V]
                           + _dot_tn(s["kg"][:, h * D_K:(h + 1) * D_K], vnb[:, h * D_V:(h + 1) * D_V])
                           for h in range(H_B)])
    return outs, new_states


def _delta_kernel(qf_ref, kf_ref, vf_ref, ecf_ref, erf_ref, qb_ref, kb_ref, vb_ref, ecb_ref, erb_ref,
                  s0f_ref, s0b_ref, of_ref, ob_ref, sf_ref, sb_ref, s_scr):
    j = pl.program_id(1)
    c = CHUNK

    @pl.when(j == 0)
    def _():
        s_scr[0] = s0f_ref[0]
        s_scr[1] = s0b_ref[0]

    def group(refs, ci, backward):
        q_ref, k_ref, v_ref, ec_ref, er_ref = refs
        rows = slice(ci * c, (ci + 1) * c)
        d = 1 if backward else 0
        return (q_ref[0, rows, :], k_ref[0, rows, :], v_ref[0, rows, :], ec_ref[0, rows, :],
                er_ref[0, ci, d:d + 1, :], backward)

    fwd = (qf_ref, kf_ref, vf_ref, ecf_ref, erf_ref)
    bwd = (qb_ref, kb_ref, vb_ref, ecb_ref, erb_ref)
    prep = _delta_prep([group(fwd, ci, False) for ci in range(DELTA_CT)]
                       + [group(bwd, ci, True) for ci in range(DELTA_CT)])
    states = [[s_scr[d, h] for h in range(H_B)] for d in range(2)]
    for step in range(DELTA_CT):
        cf, cb = step, DELTA_CT - 1 - step
        outs, states = _delta_scan_step([prep[cf], prep[DELTA_CT + cb]], states)
        of_ref[0, cf * c:(cf + 1) * c, :] = outs[0].astype(of_ref.dtype)
        ob_ref[0, cb * c:(cb + 1) * c, :] = outs[1].astype(ob_ref.dtype)
    for d in range(2):
        for h in range(H_B):
            s_scr[d, h] = states[d][h]

    @pl.when(j == pl.num_programs(1) - 1)
    def _():
        sf_ref[0] = s_scr[0]
        sb_ref[0] = s_scr[1]


def _delta_call(q, k, v, ecol, s0f, s0b):
    B, L, _ = q.shape
    n = L // CHUNK
    tl = DELTA_CT * CHUNK
    nb = L // tl
    erow = ecol[..., :2 * H_B].reshape(B, n, CHUNK, 2, H_B).transpose(0, 1, 3, 4, 2).reshape(B, n, 2, C4)
    fmap = lambda b, j: (b, j, 0)
    bmap = lambda b, j: (b, nb - 1 - j, 0)
    seq = lambda m: pl.BlockSpec((1, tl, W_B), m)
    ecs = lambda m: pl.BlockSpec((1, tl, LANES), m)
    ers = lambda m: pl.BlockSpec((1, DELTA_CT, 2, C4), lambda b, j: m(b, j) + (0,))
    st = pl.BlockSpec((1, H_B, D_K, D_V), lambda b, j: (b, 0, 0, 0))
    return pl.pallas_call(
        _delta_kernel,
        grid=(B, nb),
        in_specs=[seq(fmap), seq(fmap), seq(fmap), ecs(fmap), ers(fmap),
                  seq(bmap), seq(bmap), seq(bmap), ecs(bmap), ers(bmap), st, st],
        out_specs=[seq(fmap), seq(bmap), st, st],
        out_shape=[jax.ShapeDtypeStruct((B, L, W_B), BF16)] * 2
        + [jax.ShapeDtypeStruct((B, H_B, D_K, D_V), F32)] * 2,
        scratch_shapes=[pltpu.VMEM((2, H_B, D_K, D_V), F32)],
        compiler_params=_cparams(("parallel", "arbitrary")),
        name="delta",
    )(q, k, v, ecol, erow, q, k, v, ecol, erow, s0f, s0b)


def _merge_kernel(x_ref, oa_ref, of_ref, ob_ref, z_ref, gate_ref, g1_ref, sh2_ref, sc2_ref,
                  onw_ref, wpa_ref, wpb_ref, wo_ref, n2w_ref, wr_ref, br_ref,
                  x1_ref, h2_ref, route_ref):
    o = of_ref[0].astype(F32) + ob_ref[0].astype(F32)
    parts = []
    for h in range(H_B):
        t = o[:, h * D_V:(h + 1) * D_V]
        parts.append(t * lax.rsqrt(jnp.mean(t * t, axis=-1, keepdims=True) + EPS))
    ob = jnp.concatenate(parts, axis=-1) * onw_ref[...] * z_ref[0].astype(F32)
    pa = _dot(oa_ref[0], wpa_ref[...])
    pb = _dot(ob.astype(BF16), wpb_ref[...])
    gate = gate_ref[0]
    merged = gate[:, :D_MODEL].astype(F32) * pa + gate[:, D_MODEL:].astype(F32) * pb
    mix = _dot(merged.astype(BF16), wo_ref[...])
    x1 = x_ref[0] + g1_ref[0] * mix
    x1_ref[0] = x1
    ms = jnp.mean(x1 * x1, axis=-1, keepdims=True)
    h2 = x1 * lax.rsqrt(ms + EPS) * n2w_ref[...]
    h2 = h2 * (1.0 + sc2_ref[0]) + sh2_ref[0]
    hi = h2.astype(BF16)
    h2_ref[0] = h2
    lo = (h2 - hi.astype(F32)).astype(BF16)
    logits = _dot(jnp.concatenate([hi, lo, hi], axis=-1), wr_ref[...]) + br_ref[...]
    lane = lax.broadcasted_iota(jnp.int32, logits.shape, 1)
    big = jnp.int32(LANES)

    def first_max(vals, mask):
        mv = jnp.where(mask, vals, NEG_INF)
        m = mv.max(axis=-1, keepdims=True)
        idx = jnp.where(mask & (mv == m), lane, big).min(axis=-1, keepdims=True)
        return m, idx

    gmask = lane < N_GROUPS
    gm, grp = first_max(logits, gmask)
    p_grp = 1.0 / jnp.where(gmask, jnp.exp(logits - gm), 0.0).sum(axis=-1, keepdims=True)
    lo_lane = N_GROUPS + grp * EXP_PER_GROUP
    emask = (lane >= lo_lane) & (lane < lo_lane + EXP_PER_GROUP)
    em, _ = first_max(logits, emask)
    ex = jnp.where(emask, jnp.exp(logits - em), 0.0)
    probs = ex / ex.sum(axis=-1, keepdims=True)
    p1, i1 = first_max(probs, emask)
    p2, i2 = first_max(probs, emask & (lane != i1))
    tot = p1 + p2
    w1 = p_grp * p1 / tot
    w2 = p_grp * p2 / tot
    e1 = (i1 - N_GROUPS).astype(F32)
    e2 = (i2 - N_GROUPS).astype(F32)
    route_ref[0] = jnp.where(lane == 0, e1, jnp.where(lane == 1, e2,
                             jnp.where(lane == 2, w1, jnp.where(lane == 3, w2, 0.0))))


def _merge_call(x, oa, of, ob, z, gate, mod3, mod_stride, w):
    B, L, _ = x.shape
    tm = min(ROW_TILE, L)
    rspec = lambda n: pl.BlockSpec((1, tm, n), lambda b, i: (b, i, 0))
    mspec = lambda k: pl.BlockSpec((1, 1, D_MODEL), lambda b, i: (b * mod_stride + k, 0, 0))
    consts = [w["onw"], w["wpa"], w["wpb"], w["wo"], w["norm2_w"], w["wr3"], w["br"]]
    return pl.pallas_call(
        _merge_kernel,
        grid=(B, L // tm),
        in_specs=[rspec(D_MODEL), rspec(W_A), rspec(W_B), rspec(W_B), rspec(W_B), rspec(2 * D_MODEL),
                  mspec(2), mspec(3), mspec(4)] + [_const_spec(c.shape) for c in consts],
        out_specs=[rspec(D_MODEL), rspec(D_MODEL), rspec(LANES)],
        out_shape=[jax.ShapeDtypeStruct((B, L, D_MODEL), F32),
                   jax.ShapeDtypeStruct((B, L, D_MODEL), F32),
                   jax.ShapeDtypeStruct((B, L, LANES), F32)],
        compiler_params=_cparams(("parallel", "parallel")),
        name="merge",
    )(x, oa, of, ob, z, gate, mod3, mod3, mod3, *consts)


def _row_gather_start(idx_ref, n_rows, src_hbm, buf, sem, slot):
    for r in range(n_rows):
        pltpu.make_async_copy(src_hbm.at[pl.ds(idx_ref[0, 0, r], 1), :], buf.at[slot, pl.ds(r, 1), :],
                              sem.at[slot]).start(priority=r % 2)


def _row_gather_wait(n_rows, src_hbm, buf, sem, slot):
    for r in range(n_rows):
        pltpu.make_async_copy(src_hbm.at[pl.ds(0, 1), :], buf.at[slot, pl.ds(r, 1), :], sem.at[slot]).wait()


def _gather_pipeline(i, n_steps, idx_ref, idx_next_ref, n_rows, src_hbm, buf, sem):
    slot = lax.rem(i, 2)

    @pl.when(i == 0)
    def _():
        _row_gather_start(idx_ref, n_rows, src_hbm, buf, sem, 0)

    @pl.when(i + 1 < n_steps)
    def _():
        _row_gather_start(idx_next_ref, n_rows, src_hbm, buf, sem, 1 - slot)

    @pl.when(i < n_steps)
    def _():
        _row_gather_wait(n_rows, src_hbm, buf, sem, slot)

    return slot


def _moe_kernel(te_ref, nt_ref, idx_ref, idxn_ref, h2_hbm, wgu_ref, wd_ref, y_ref, xbuf, sem):
    i = pl.program_id(0)
    n_used = nt_ref[0]
    slot = _gather_pipeline(i, n_used, idx_ref, idxn_ref, MOE_TILE, h2_hbm, xbuf, sem)

    @pl.when(i < n_used)
    def _():
        gu = _dot(xbuf[slot].astype(BF16), wgu_ref[0])
        g = gu[:, :D_FF_E]
        hid = g * jax.nn.sigmoid(g) * gu[:, D_FF_E:]
        y_ref[...] = _dot(hid.astype(BF16), wd_ref[0])

    @pl.when(i >= n_used)
    def _():
        y_ref[...] = jnp.zeros_like(y_ref)


def _moe_call(h2p, src_tok, tile_expert, n_tiles_used, wgu, wd):
    p_pad = src_tok.shape[0]
    n_tiles = p_pad // MOE_TILE
    idx = src_tok.reshape(n_tiles, 1, MOE_TILE)
    smem_tile = lambda m: pl.BlockSpec((1, 1, MOE_TILE), m, memory_space=pltpu.SMEM)
    grid_spec = pltpu.PrefetchScalarGridSpec(
        num_scalar_prefetch=2,
        grid=(n_tiles,),
        in_specs=[smem_tile(lambda i, te, nt: (i, 0, 0)),
                  smem_tile(lambda i, te, nt: (jnp.minimum(i + 1, n_tiles - 1), 0, 0)),
                  pl.BlockSpec(memory_space=pl.ANY),
                  pl.BlockSpec((1, D_MODEL, 2 * D_FF_E), lambda i, te, nt: (te[i], 0, 0)),
                  pl.BlockSpec((1, D_FF_E, D_MODEL), lambda i, te, nt: (te[i], 0, 0))],
        out_specs=pl.BlockSpec((MOE_TILE, D_MODEL), lambda i, te, nt: (i, 0)),
        scratch_shapes=[pltpu.VMEM((2, MOE_TILE, D_MODEL), F32), pltpu.SemaphoreType.DMA((2,))],
    )
    return pl.pallas_call(
        _moe_kernel,
        grid_spec=grid_spec,
        out_shape=jax.ShapeDtypeStruct((p_pad, D_MODEL), F32),
        compiler_params=_cparams(("arbitrary",)),
        name="moe",
    )(tile_expert, n_tiles_used, idx, idx, h2p, wgu, wd)


COMBINE_TILE = 256


def _combine_kernel(pos_ref, posn_ref, x1_ref, route_ref, g2_ref, y_hbm, o_ref, ybuf, sem):
    i = pl.program_id(0)
    t = COMBINE_TILE
    slot = _gather_pipeline(i, pl.num_programs(0), pos_ref, posn_ref, 2 * t, y_hbm, ybuf, sem)
    route = route_ref[...]
    w1 = route[:, 2:3]
    w2 = route[:, 3:4]
    o_ref[...] = x1_ref[...] + g2_ref[0] * (w1 * ybuf[slot, :t] + w2 * ybuf[slot, t:])


def _combine_call(x1, route, pos2, y, mod3, mod_stride, seq_len):
    n = x1.shape[0]
    t = COMBINE_TILE
    n_tiles = n // t
    pos_tiles = pos2.reshape(n_tiles, t, 2).transpose(0, 2, 1).reshape(n_tiles, 1, 2 * t)
    smem_tile = lambda m: pl.BlockSpec((1, 1, 2 * t), m, memory_space=pltpu.SMEM)
    return pl.pallas_call(
        _combine_kernel,
        grid=(n_tiles,),
        in_specs=[smem_tile(lambda i: (i, 0, 0)),
                  smem_tile(lambda i: (jnp.minimum(i + 1, n_tiles - 1), 0, 0)),
                  pl.BlockSpec((t, D_MODEL), lambda i: (i, 0)),
                  pl.BlockSpec((t, LANES), lambda i: (i, 0)),
                  pl.BlockSpec((1, 1, D_MODEL), lambda i: ((i * t) // seq_len * mod_stride + 5, 0, 0)),
                  pl.BlockSpec(memory_space=pl.ANY)],
        out_specs=pl.BlockSpec((t, D_MODEL), lambda i: (i, 0)),
        out_shape=jax.ShapeDtypeStruct((n, D_MODEL), F32),
        scratch_shapes=[pltpu.VMEM((2, 2 * t, D_MODEL), F32), pltpu.SemaphoreType.DMA((2,))],
        compiler_params=_cparams(("arbitrary",)),
        name="combine",
    )(pos_tiles, pos_tiles, x1, route, mod3, y)


def _moe_forward(h2p, route, wgu, wd):
    n = h2p.shape[0]
    eid = route[:, :2].astype(jnp.int32).reshape(-1)
    tok = jnp.arange(2 * n, dtype=jnp.int32) // 2
    onehot = (eid[:, None] == jnp.arange(N_EXPERTS, dtype=jnp.int32)[None, :]).astype(jnp.int32)
    csum = jnp.cumsum(onehot, axis=0)
    counts = csum[-1]
    rank = jnp.take_along_axis(csum, eid[:, None], axis=1)[:, 0] - 1
    padded = ((counts + MOE_TILE - 1) // MOE_TILE) * MOE_TILE
    seg_end = jnp.cumsum(padded)
    seg_start = seg_end - padded
    pos = seg_start[eid] + rank
    p_pad = 2 * n + N_EXPERTS * MOE_TILE
    src_tok = (jnp.arange(p_pad, dtype=jnp.int32) % n).at[pos].set(tok)
    n_tiles = p_pad // MOE_TILE
    tile_start = jnp.arange(n_tiles, dtype=jnp.int32) * MOE_TILE
    tile_expert = jnp.minimum(jnp.sum((tile_start[:, None] >= seg_end[None, :]).astype(jnp.int32), axis=1),
                              N_EXPERTS - 1)
    n_used = (seg_end[-1] // MOE_TILE).astype(jnp.int32).reshape(1)
    y = _moe_call(h2p, src_tok, tile_expert, n_used, wgu, wd)
    return y, pos.reshape(n, 2)


def _prep_weights(norm1_w, norm2_w, w_in, conv_w, a_log, dt_bias, q_norm_w, k_norm_w, o_norm_w,
                  w_proj_a, w_proj_b, w_out, w_router_g, b_router_g, w_router_e, b_router_e,
                  w_exp_gate, w_exp_up, w_exp_down):
    w = {}
    w["norm1_w"] = norm1_w.reshape(1, D_MODEL)
    w["norm2_w"] = norm2_w.reshape(1, D_MODEL)
    o = 0
    w["wa"] = w_in[:, o:o + 3 * W_A].astype(BF16); o += 3 * W_A
    w["wb"] = w_in[:, o:o + 3 * W_B].astype(BF16); o += 3 * W_B
    w["wz"] = w_in[:, o:o + W_B].astype(BF16); o += W_B
    small = w_in[:, o:o + 4 * H_B]; o += 4 * H_B
    w["wg"] = w_in[:, o:o + 2 * D_MODEL].astype(BF16)
    def group_lanes(t):
        r = t.shape[0]
        t = t.reshape(r, -1, N_HG, HP).transpose(0, 2, 1, 3).reshape(r, N_HG, -1)
        return jnp.pad(t, ((0, 0), (0, 0), (0, LANES - t.shape[-1]))).reshape(r, N_HG * LANES)

    w["we"] = group_lanes(small).astype(BF16)
    w["alog_l"] = group_lanes(a_log.reshape(1, 2 * H_B))
    w["dtb_l"] = group_lanes(dt_bias.reshape(1, 2 * H_B))
    w["qnw"] = jnp.tile(q_norm_w, H_A).reshape(1, W_A)
    w["knw"] = jnp.tile(k_norm_w, H_A).reshape(1, W_A)
    w["onw"] = jnp.tile(o_norm_w, H_B).reshape(1, W_B)
    blk = np.arange(256) // D_HA
    w["bd256"] = jnp.asarray((blk[:, None] == blk[None, :]).astype(np.float32)).astype(BF16)
    w["conv_w"] = conv_w
    w["wpa"] = w_proj_a.astype(BF16)
    w["wpb"] = w_proj_b.astype(BF16)
    w["wo"] = w_out.astype(BF16)
    wr = jnp.concatenate([w_router_g, w_router_e], axis=1)
    wr = jnp.pad(wr, ((0, 0), (0, LANES - wr.shape[1])))
    wr_hi = wr.astype(BF16)
    wr_lo = (wr - wr_hi.astype(F32)).astype(BF16)
    w["wr3"] = jnp.concatenate([wr_hi, wr_hi, wr_lo], axis=0)
    br = jnp.concatenate([b_router_g, b_router_e])
    w["br"] = jnp.pad(br, (0, LANES - br.shape[0])).reshape(1, LANES)
    w["wgu"] = jnp.concatenate([w_exp_gate, w_exp_up], axis=-1).astype(BF16)
    w["wd"] = w_exp_down.astype(BF16)
    return w


def _layer(x, mod3, mod_stride, w, ctx):
    B, L, _ = x.shape
    latent = ctx is not None
    kv_dtype = BF16 if latent else F32
    qa, ka, va, qkvb, z, gate, ecol = _inproj_call(x, mod3, mod_stride, w, kv_dtype)
    if latent:
        k_ctx, v_ctx, s0f, s0b, bias = ctx
        oa = _nbr_attn_call(qa, ka, va, k_ctx, v_ctx, bias)
    else:
        s0f = s0b = jnp.zeros((B, H_B, D_K, D_V), F32)
        oa = _ctx_attn_call(qa, ka, va)
    qd, kd, vd = _conv_call(qkvb, w["conv_w"], rope=latent)
    of, ob, s_f, s_b = _delta_call(qd, kd, vd, ecol, s0f, s0b)
    x1, h2, route = _merge_call(x, oa, of, ob, z, gate, mod3, mod_stride, w)
    route = route.reshape(B * L, LANES)
    ys, pos2 = _moe_forward(h2.reshape(B * L, D_MODEL), route, w["wgu"], w["wd"])
    y = _combine_call(x1.reshape(B * L, D_MODEL), route, pos2, ys, mod3, mod_stride, L)
    return y.reshape(B, L, D_MODEL), ka, va, s_f, s_b


def kernel(x_prompt, x_sample, c, cache_attn_k, cache_attn_v, state_delta_fwd, state_delta_bwd, c_ctx,
           norm1_w, norm2_w, w_mod, b_mod, w_in, conv_w, a_log, dt_bias, q_norm_w, k_norm_w, rpb,
           o_norm_w, w_proj_a, w_proj_b, w_out, w_router_g, b_router_g, w_router_e, b_router_e,
           w_exp_gate, w_exp_up, w_exp_down):
    assert norm1_w.shape[0] == 1, "single-layer trunk"
    Bp, Lp, _ = x_prompt.shape
    Bs, Ls, _ = x_sample.shape
    l = 0
    w = _prep_weights(norm1_w[l], norm2_w[l], w_in[l], conv_w[l], a_log[l], dt_bias[l], q_norm_w[l],
                      k_norm_w[l], o_norm_w[l], w_proj_a[l], w_proj_b[l], w_out[l], w_router_g[l],
                      b_router_g[l], w_router_e[l], b_router_e[l], w_exp_gate[l], w_exp_up[l],
                      w_exp_down[l])
    cond = jnp.concatenate([c_ctx[None, :], c, jnp.zeros((8 - 1 - Bs, D_MODEL), F32)], axis=0)
    mod = _mod_call(cond, w_mod[l], b_mod[l])
    mod3 = mod.reshape(8 * 6, 1, D_MODEL)
    bias = _bias_table_call(rpb[l])
    y_p, k_a, v_a, s_f, s_b = _layer(x_prompt, mod3[:6], 0, w, None)
    lc = cache_attn_k.shape[2]
    ctx = (cache_attn_k[:, l].reshape(Bs, lc, W_A).astype(BF16),
           cache_attn_v[:, l].reshape(Bs, lc, W_A).astype(BF16),
           state_delta_fwd[:, l], state_delta_bwd[:, l], bias)
    y_s, _, _, _, _ = _layer(x_sample, mod3[6:6 + 6 * Bs], 6, w, ctx)
    new_k = k_a.reshape(Bp, 1, Lp, H_A, D_HA)
    new_v = v_a.reshape(Bp, 1, Lp, H_A, D_HA)
    return (y_p, y_s, new_k, new_v, s_f[:, None], s_b[:, None])
```

```python
import functools
import math

import numpy as np
import jax
import jax.numpy as jnp
from jax import lax
from jax.experimental import pallas as pl
from jax.experimental.pallas import tpu as pltpu

F32 = jnp.float32
BF16 = jnp.bfloat16

D_MODEL = 1024
GRID_W = 64
H_A = 8
D_HA = 64
W_A = H_A * D_HA
WIN_H = 8
WIN_W = 16
H_B = 4
D_K = 128
D_V = 128
W_B = H_B * D_V
CONV_K = 5
CHUNK = 64
ROPE_BASE = 10000.0
N_GROUPS = 4
EXP_PER_GROUP = 8
N_EXPERTS = N_GROUPS * EXP_PER_GROUP
D_FF_E = 256
EPS = 1e-6
NEG_INF = -1e30

LANES = 128
HP = H_B
N_HG = H_B // HP
ROW_TILE = 512
INPROJ_TILE = 256
MOE_TILE = 256
VMEM_LIMIT = 56 * 1024 * 1024

HIGHEST = lax.Precision.HIGHEST


def _cparams(sem):
    return pltpu.CompilerParams(dimension_semantics=sem, vmem_limit_bytes=VMEM_LIMIT)


def _dot(a, b):
    return jnp.dot(a, b, preferred_element_type=F32)


def _dot_nt(a, b):
    return lax.dot_general(a, b, (((1,), (1,)), ((), ())), preferred_element_type=F32)


def _dot_tn(a, b):
    return lax.dot_general(a, b, (((0,), (0,)), ((), ())), preferred_element_type=F32)


def _const_spec(shape):
    nd = len(shape)
    return pl.BlockSpec(shape, lambda *_: (0,) * nd)


def _mod_kernel(c_ref, w_ref, b_ref, o_ref):
    c = c_ref[...]
    s = c * jax.nn.sigmoid(c)
    o_ref[...] = jnp.dot(s, w_ref[...], preferred_element_type=F32, precision=HIGHEST) + b_ref[...]


def _mod_call(cond8, w_mod, b_mod):
    n_out = w_mod.shape[1]
    tn = 1024
    return pl.pallas_call(
        _mod_kernel,
        grid=(n_out // tn,),
        in_specs=[pl.BlockSpec((8, D_MODEL), lambda j: (0, 0)),
                  pl.BlockSpec((D_MODEL, tn), lambda j: (0, j)),
                  pl.BlockSpec((1, tn), lambda j: (0, j))],
        out_specs=pl.BlockSpec((8, tn), lambda j: (0, j)),
        out_shape=jax.ShapeDtypeStruct((8, n_out), F32),
        compiler_params=_cparams(("arbitrary",)),
        name="mod",
    )(cond8, w_mod, b_mod.reshape(1, n_out))


def _softplus(x):
    return jnp.maximum(x, 0.0) + jnp.log1p(jnp.exp(-jnp.abs(x)))


def _inproj_kernel(x_ref, sh_ref, sc_ref, nw_ref, wa_ref, wb_ref, wz_ref, wg_ref, we_ref,
                   qnw_ref, knw_ref, bd_ref, alog_ref, dtb_ref, ltri_ref, utri_ref,
                   qa_ref, ka_ref, va_ref, qkvb_ref, z_ref, gate_ref, e_ref):
    x = x_ref[...]
    ms = jnp.mean(x * x, axis=-1, keepdims=True)
    h = x * lax.rsqrt(ms + EPS) * nw_ref[...]
    h = h * (1.0 + sc_ref[0]) + sh_ref[0]
    hb = h.astype(BF16)

    a = _dot(hb, wa_ref[...])
    bd = bd_ref[...]

    def head_rms(t, w):
        sq = (t * t).astype(BF16)
        ss = jnp.concatenate([_dot(sq[:, i * 256:(i + 1) * 256], bd) for i in range(W_A // 256)], axis=-1)
        return t * lax.rsqrt(ss * (1.0 / D_HA) + EPS) * w

    qa = head_rms(a[:, :W_A], qnw_ref[...]) * (D_HA ** -0.5)
    qa_ref[...] = qa.astype(qa_ref.dtype)
    ka_ref[...] = head_rms(a[:, W_A:2 * W_A], knw_ref[...]).astype(ka_ref.dtype)
    va_ref[...] = a[:, 2 * W_A:].astype(va_ref.dtype)

    qkvb_ref[...] = _dot(hb, wb_ref[...]).astype(qkvb_ref.dtype)
    z = _dot(hb, wz_ref[...])
    z_ref[...] = (z * jax.nn.sigmoid(z)).astype(z_ref.dtype)
    gate_ref[...] = jax.nn.sigmoid(_dot(hb, wg_ref[...])).astype(gate_ref.dtype)

    e = _dot(hb, we_ref[...])
    lane = lax.broadcasted_iota(jnp.int32, e.shape, 1) % LANES
    g = -jnp.exp(alog_ref[...]) * _softplus(e + dtb_ref[...])
    act = jnp.where(lane < 2 * HP, g, jax.nn.sigmoid(e))
    cf = jnp.dot(ltri_ref[...], act, preferred_element_type=F32, precision=HIGHEST)
    cb = jnp.dot(utri_ref[...], act, preferred_element_type=F32, precision=HIGHEST)
    e_ref[...] = jnp.where(lane < HP, cf, jnp.where(lane < 2 * HP, cb, act))


def _inproj_call(x, mod3, mod_stride, w, kv_dtype):
    B, L, _ = x.shape
    n_tok = B * L
    tm = INPROJ_TILE
    assert n_tok % tm == 0 and (L % tm == 0 or tm % L == 0)
    row = lambda n, dt: jax.ShapeDtypeStruct((n_tok, n), dt)
    rspec = lambda n: pl.BlockSpec((tm, n), lambda i: (i, 0))
    mspec = lambda k: pl.BlockSpec((1, 1, D_MODEL), lambda i: ((i * tm) // L * mod_stride + k, 0, 0))
    t = np.arange(tm)
    same = (t[:, None] // CHUNK) == (t[None, :] // CHUNK)
    ltri = jnp.asarray((same & (t[:, None] >= t[None, :])).astype(np.float32))
    utri = jnp.asarray((same & (t[:, None] <= t[None, :])).astype(np.float32))
    consts = [w["norm1_w"], w["wa"], w["wb"], w["wz"], w["wg"], w["we"], w["qnw"], w["knw"],
              w["bd256"], w["alog_l"], w["dtb_l"], ltri, utri]
    outs = pl.pallas_call(
        _inproj_kernel,
        grid=(n_tok // tm,),
        in_specs=[rspec(D_MODEL), mspec(0), mspec(1)] + [_const_spec(c.shape) for c in consts],
        out_specs=[rspec(W_A), rspec(W_A), rspec(W_A), rspec(3 * W_B), rspec(W_B),
                   rspec(2 * D_MODEL), rspec(N_HG * LANES)],
        out_shape=[row(W_A, BF16), row(W_A, kv_dtype), row(W_A, kv_dtype), row(3 * W_B, BF16),
                   row(W_B, BF16), row(2 * D_MODEL, BF16), row(N_HG * LANES, F32)],
        compiler_params=_cparams(("parallel",)),
        name="inproj",
    )(x.reshape(n_tok, D_MODEL), mod3, mod3, *consts)
    return [o.reshape(B, L, o.shape[-1]) for o in outs]


def _pair_softmax_pv(s_parts, v_parts):
    m = s_parts[0].max(axis=-1, keepdims=True)
    for s in s_parts[1:]:
        m = jnp.maximum(m, s.max(axis=-1, keepdims=True))
    den = 0.0
    acc = 0.0
    for s, v in zip(s_parts, v_parts):
        p = jnp.exp(s - m)
        den = den + p.sum(axis=-1, keepdims=True)
        acc = acc + _dot(p.astype(BF16), v)
    return acc / den


def _stack_pair(qp):
    lane = lax.broadcasted_iota(jnp.int32, qp.shape, 1)
    zero = jnp.zeros_like(qp)
    return jnp.concatenate([jnp.where(lane < D_HA, qp, zero), jnp.where(lane >= D_HA, qp, zero)], axis=0)


def _unstack_pair(o2):
    t = o2.shape[0] // 2
    lane = lax.broadcasted_iota(jnp.int32, (t, LANES), 1)
    return jnp.where(lane < D_HA, o2[:t], o2[t:])


def _ctx_attn_kernel(q_ref, k_ref, v_ref, o_ref):
    for p in range(H_A // 2):
        sl = slice(p * LANES, (p + 1) * LANES)
        qs = _stack_pair(q_ref[0, :, sl])
        kp = k_ref[0, :, sl].astype(BF16)
        vp = v_ref[0, :, sl].astype(BF16)
        o2 = _pair_softmax_pv([_dot_nt(qs, kp)], [vp])
        o_ref[0, :, sl] = _unstack_pair(o2).astype(o_ref.dtype)


def _ctx_attn_call(q, k, v):
    B, L, _ = q.shape
    spec = pl.BlockSpec((1, L, W_A), lambda b: (b, 0, 0))
    return pl.pallas_call(
        _ctx_attn_kernel,
        grid=(B,),
        in_specs=[spec, spec, spec],
        out_specs=spec,
        out_shape=jax.ShapeDtypeStruct((B, L, W_A), BF16),
        compiler_params=_cparams(("parallel",)),
        name="ctx_attn",
    )(q, k, v)


N_DR = 2 * WIN_H - 1
N_DC = 2 * WIN_W - 1
NBR_ROWS = 4


def _bias_table_kernel(rpb_ref, o_ref):
    h = pl.program_id(0)
    c = lax.broadcasted_iota(jnp.int32, (GRID_W, 2 * GRID_W), 0)
    lane = lax.broadcasted_iota(jnp.int32, (GRID_W, 2 * GRID_W), 1)
    kc = lane % GRID_W
    ws = jnp.clip(c - WIN_W // 2, 0, GRID_W - WIN_W)
    in_win = (kc >= ws) & (kc < ws + WIN_W)
    dc = kc - c + (WIN_W - 1)
    for dr in range(N_DR - 1):
        base = (h * N_DR + dr) * N_DC
        val = jnp.zeros((GRID_W, 2 * GRID_W), F32)
        for d in range(N_DC):
            val = jnp.where(dc == d, jnp.where(lane < GRID_W, rpb_ref[base + d], rpb_ref[base + N_DC + d]), val)
        o_ref[0, dr] = jnp.where(in_win, val, NEG_INF)


def _bias_table_call(rpb):
    return pl.pallas_call(
        _bias_table_kernel,
        grid=(H_A,),
        in_specs=[pl.BlockSpec(memory_space=pltpu.SMEM)],
        out_specs=pl.BlockSpec((1, N_DR - 1, GRID_W, 2 * GRID_W), lambda h: (h, 0, 0, 0)),
        out_shape=jax.ShapeDtypeStruct((H_A, N_DR - 1, GRID_W, 2 * GRID_W), F32),
        compiler_params=_cparams(("arbitrary",)),
        name="bias_table",
    )(rpb.reshape(-1))


def _nbr_attn_kernel(q_ref, k_ref, v_ref, kc_ref, vc_ref, bias_ref, o_ref, *, rows):
    inst = []
    for rr in range(NBR_ROWS):
        r = pl.program_id(1) * NBR_ROWS + rr
        first = jnp.clip(r - WIN_H // 2, 0, rows - WIN_H)
        start = pl.multiple_of(first * GRID_W, GRID_W)
        shift = r - first
        for p in range(H_A // 2):
            inst.append((rr, p, start, shift))
    scores = []
    for rr, p, start, shift in inst:
        sl = slice(p * LANES, (p + 1) * LANES)
        qs = _stack_pair(q_ref[0, rr * GRID_W:(rr + 1) * GRID_W, sl])
        bias = jnp.concatenate(
            [jnp.concatenate([bias_ref[2 * p + hh, 2 * u - shift + WIN_H - 1] for u in range(WIN_H // 2)], axis=1)
             for hh in range(2)], axis=0)
        s_loc = _dot_nt(qs, k_ref[0, pl.ds(start, WIN_H * GRID_W), sl]) + bias
        s_ctx = _dot_nt(qs, kc_ref[0, :, sl])
        scores.append((s_loc, s_ctx))
    for (rr, p, start, shift), (s_loc, s_ctx) in zip(inst, scores):
        sl = slice(p * LANES, (p + 1) * LANES)
        o2 = _pair_softmax_pv([s_loc, s_ctx], [v_ref[0, pl.ds(start, WIN_H * GRID_W), sl], vc_ref[0, :, sl]])
        o_ref[0, rr * GRID_W:(rr + 1) * GRID_W, sl] = _unstack_pair(o2).astype(o_ref.dtype)


def _nbr_attn_call(q, k, v, k_ctx, v_ctx, bias):
    B, L, _ = q.shape
    rows = L // GRID_W
    lc = k_ctx.shape[1]
    full = pl.BlockSpec((1, L, W_A), lambda b, r: (b, 0, 0))
    ctx = pl.BlockSpec((1, lc, W_A), lambda b, r: (b, 0, 0))
    rowspec = pl.BlockSpec((1, NBR_ROWS * GRID_W, W_A), lambda b, r: (b, r, 0))
    return pl.pallas_call(
        functools.partial(_nbr_attn_kernel, rows=rows),
        grid=(B, rows // NBR_ROWS),
        in_specs=[rowspec, full, full, ctx, ctx, _const_spec(bias.shape)],
        out_specs=rowspec,
        out_shape=jax.ShapeDtypeStruct((B, L, W_A), BF16),
        compiler_params=_cparams(("parallel", "arbitrary")),
        name="nbr_attn",
    )(q, k, v, k_ctx, v_ctx, bias)


HALO = 16


def _conv_kernel(*refs, rope):
    if rope:
        x_ref, prev_ref, next_ref, w_ref, cos_ref, sin_ref, q_ref, k_ref, v_ref, xe_ref = refs
    else:
        x_ref, prev_ref, next_ref, w_ref, q_ref, k_ref, v_ref, xe_ref = refs
    i = pl.program_id(1)
    n_i = pl.num_programs(1)
    tl = x_ref.shape[1]
    xe_ref[0:HALO] = jnp.where(i > 0, prev_ref[0].astype(F32), 0.0)
    xe_ref[HALO:HALO + tl] = x_ref[0].astype(F32)
    xe_ref[HALO + tl:] = jnp.where(i < n_i - 1, next_ref[0].astype(F32), 0.0)
    half = CONV_K // 2
    y = w_ref[0:1, :] * xe_ref[HALO - half:HALO - half + tl]
    for j in range(1, CONV_K):
        y = y + w_ref[j:j + 1, :] * xe_ref[HALO - half + j:HALO - half + j + tl]
    y = y * jax.nn.sigmoid(y)

    def l2n(t):
        return t * lax.rsqrt(jnp.sum(t * t, axis=-1, keepdims=True) + EPS)

    def rot(t):
        if not rope:
            return t
        lane = lax.broadcasted_iota(jnp.int32, t.shape, 1)
        swapped = jnp.where(lane % 2 == 0, pltpu.roll(t, LANES - 1, 1), pltpu.roll(t, 1, 1))
        return t * cos_ref[...] + swapped * sin_ref[...]

    for h in range(H_B):
        sl = slice(h * D_K, (h + 1) * D_K)
        q_ref[0, :, sl] = (rot(l2n(y[:, sl])) * (D_K ** -0.5)).astype(q_ref.dtype)
        k_ref[0, :, sl] = rot(l2n(y[:, W_B + h * D_K:W_B + (h + 1) * D_K])).astype(k_ref.dtype)
    v_ref[0] = y[:, 2 * W_B:].astype(v_ref.dtype)


def _rope_tables(L):
    pos = jnp.arange(L)
    rows = (pos // GRID_W).astype(F32)
    cols = (pos % GRID_W).astype(F32)
    n_pairs = D_K // 4
    inv = ROPE_BASE ** (-jnp.arange(n_pairs, dtype=F32) / n_pairs)
    ang = jnp.concatenate([rows[:, None] * inv, cols[:, None] * inv], axis=-1)
    cos = jnp.repeat(jnp.cos(ang), 2, axis=-1)
    sin = jnp.repeat(jnp.sin(ang), 2, axis=-1)
    sign = jnp.asarray(np.tile(np.array([-1.0, 1.0], np.float32), D_K // 2))
    return cos, sin * sign


def _conv_call(x, conv_w, rope):
    B, L, C = x.shape
    tl = min(ROW_TILE, L)
    nh = tl // HALO
    n_halo = L // HALO
    cur = pl.BlockSpec((1, tl, C), lambda b, i: (b, i, 0))
    prev = pl.BlockSpec((1, HALO, C), lambda b, i: (b, jnp.maximum(i * nh - 1, 0), 0))
    nxt = pl.BlockSpec((1, HALO, C), lambda b, i: (b, jnp.minimum((i + 1) * nh, n_halo - 1), 0))
    ins = [x, x, x, conv_w]
    specs = [cur, prev, nxt, _const_spec(conv_w.shape)]
    if rope:
        cos, sin = _rope_tables(L)
        ins += [cos, sin]
        specs += [pl.BlockSpec((tl, D_K), lambda b, i: (i, 0))] * 2
    out = pl.BlockSpec((1, tl, W_B), lambda b, i: (b, i, 0))
    return pl.pallas_call(
        functools.partial(_conv_kernel, rope=rope),
        grid=(B, L // tl),
        in_specs=specs,
        out_specs=[out, out, out],
        out_shape=[jax.ShapeDtypeStruct((B, L, W_B), BF16)] * 3,
        scratch_shapes=[pltpu.VMEM((tl + 2 * HALO, C), F32)],
        compiler_params=_cparams(("parallel", "parallel")),
        name="conv",
    )(*ins)


DELTA_CT = 4
C4 = H_B * CHUNK


def _blockdiag_rows(x, n_blk):
    w = x.shape[1] // n_blk
    lane_blk = lax.broadcasted_iota(jnp.int32, x.shape, 1) // w
    return jnp.concatenate([jnp.where(lane_blk == b, x, 0.0) for b in range(n_blk)], axis=0).astype(BF16)


def _widen(ec, col0, width):
    blk = lax.broadcasted_iota(jnp.int32, (ec.shape[0], H_B * width), 1) // width
    out = ec[:, col0 + H_B - 1:col0 + H_B]
    for h in range(H_B - 2, -1, -1):
        out = jnp.where(blk == h, ec[:, col0 + h:col0 + h + 1], out)
    return out


def _delta_prep(groups):
    c = CHUNK
    row = lax.broadcasted_iota(jnp.int32, (c, C4), 0)
    coll = lax.broadcasted_iota(jnp.int32, (c, C4), 1) % c
    eye = jnp.where(row == coll, 1.0, 0.0)
    st = []
    for q, k, v, ec, gr, backward in groups:
        gi = H_B if backward else 0
        bi = 3 * H_B if backward else 2 * H_B
        gc64, gc128 = _widen(ec, gi, c), _widen(ec, gi, D_K)
        beta64, beta128 = _widen(ec, bi, c), _widen(ec, bi, D_K)
        incl = (row <= coll) if backward else (row >= coll)
        strict = (row < coll) if backward else (row > coll)
        decay = jnp.where(incl, jnp.exp(jnp.where(incl, gc64 - gr, 0.0)), 0.0)
        kf = k.astype(F32)
        qf = q.astype(F32)
        eg = jnp.exp(gc128)
        g_last = gc128[0:1, :] if backward else gc128[c - 1:c, :]
        st.append(dict(
            kf=kf, qf=qf, decay=decay, strict=strict, beta64=beta64,
            qg=(qf * eg).astype(BF16),
            kg=(kf * jnp.exp(g_last - gc128)).astype(BF16),
            rhs_v=(v.astype(F32) * beta128).astype(BF16),
            rhs_k=(kf * (beta128 * eg)).astype(BF16),
            egl=jnp.exp(g_last)))
    lane2 = lax.broadcasted_iota(jnp.int32, (c, 2 * D_K), 1)
    for s in st:
        outs = []
        for p in range(H_B // 2):
            kp = s["kf"][:, p * 2 * D_K:(p + 1) * 2 * D_K]
            qp = s["qf"][:, p * 2 * D_K:(p + 1) * 2 * D_K]
            rmat = jnp.concatenate([jnp.where(lane2 < D_K, kp, 0.0), jnp.where(lane2 >= D_K, kp, 0.0)], axis=0)
            outs.append(_dot_nt(jnp.concatenate([kp, qp], axis=0).astype(BF16), rmat.astype(BF16)))
        kk = jnp.concatenate([o[:c] for o in outs], axis=1)
        qk = jnp.concatenate([o[c:] for o in outs], axis=1)
        s["a"] = jnp.where(s["strict"], kk * s["decay"], 0.0) * s["beta64"]
        s["intra"] = (qk * s["decay"]).astype(BF16)
        s["t"] = eye - jnp.where(row // 2 == coll // 2, s["a"], 0.0)
    b = 2
    while b < c:
        join = (row // (2 * b) == coll // (2 * b)) & (row // b != coll // b)
        for s in st:
            s["tb"] = s["t"].astype(BF16)
            s["tl"] = _dot(s["tb"], _blockdiag_rows(jnp.where(join, s["a"], 0.0), H_B))
        for s in st:
            s["t"] = s["t"] - _dot(s["tl"].astype(BF16), _blockdiag_rows(s["t"], H_B))
        b *= 2
    zero = jnp.zeros((c, 2 * D_K), BF16)
    for s in st:
        tb = s["t"].astype(BF16)
        us, ws = [], []
        for h in range(H_B):
            rhs = jnp.concatenate([s["rhs_v"][:, h * D_V:(h + 1) * D_V], s["rhs_k"][:, h * D_K:(h + 1) * D_K]], axis=1)
            uw = _dot(tb, jnp.concatenate([zero] * h + [rhs] + [zero] * (H_B - 1 - h), axis=0))
            us.append(uw[:, :D_V])
            ws.append(uw[:, D_V:])
        s["u"] = jnp.concatenate(us, axis=1)
        s["w"] = jnp.concatenate(ws, axis=1).astype(BF16)
    return st


def _delta_scan_step(steps, states):
    c = CHUNK
    zero = jnp.zeros((D_K, D_V), BF16)
    mids = []
    for s, S in zip(steps, states):
        w_s, q_s = [], []
        for p in range(H_B // 2):
            sl = slice(p * 2 * D_K, (p + 1) * 2 * D_K)
            lhs = jnp.concatenate([s["w"][:, sl], s["qg"][:, sl]], axis=0)
            sa, sb = S[2 * p].astype(BF16), S[2 * p + 1].astype(BF16)
            sbd = jnp.concatenate([jnp.concatenate([sa, zero], axis=1), jnp.concatenate([zero, sb], axis=1)], axis=0)
            out = _dot(lhs, sbd)
            w_s.append(out[:c])
            q_s.append(out[c:])
        v_new = s["u"] - jnp.concatenate(w_s, axis=1)
        mids.append((v_new, jnp.concatenate(q_s, axis=1)))
    outs, new_states = [], []
    for s, S, (v_new, q_s) in zip(steps, states, mids):
        vnb = v_new.astype(BF16)
        outs.append(q_s + _dot(s["intra"], _blockdiag_rows(v_new, H_B)))
        new_states.append([S[h] * s["egl"][:, h * D_V:(h + 1) * D_V]
                           + _dot_tn(s["kg"][:, h * D_K:(h + 1) * D_K], vnb[:, h * D_V:(h + 1) * D_V])
                           for h in range(H_B)])
    return outs, new_states


def _delta_kernel(qf_ref, kf_ref, vf_ref, ecf_ref, erf_ref, qb_ref, kb_ref, vb_ref, ecb_ref, erb_ref,
                  s0f_ref, s0b_ref, of_ref, ob_ref, sf_ref, sb_ref, s_scr):
    j = pl.program_id(1)
    c = CHUNK

    @pl.when(j == 0)
    def _():
        s_scr[0] = s0f_ref[0]
        s_scr[1] = s0b_ref[0]

    def group(refs, ci, backward):
        q_ref, k_ref, v_ref, ec_ref, er_ref = refs
        rows = slice(ci * c, (ci + 1) * c)
        d = 1 if backward else 0
        return (q_ref[0, rows, :], k_ref[0, rows, :], v_ref[0, rows, :], ec_ref[0, rows, :],
                er_ref[0, ci, d:d + 1, :], backward)

    fwd = (qf_ref, kf_ref, vf_ref, ecf_ref, erf_ref)
    bwd = (qb_ref, kb_ref, vb_ref, ecb_ref, erb_ref)
    prep = _delta_prep([group(fwd, ci, False) for ci in range(DELTA_CT)]
                       + [group(bwd, ci, True) for ci in range(DELTA_CT)])
    states = [[s_scr[d, h] for h in range(H_B)] for d in range(2)]
    for step in range(DELTA_CT):
        cf, cb = step, DELTA_CT - 1 - step
        outs, states = _delta_scan_step([prep[cf], prep[DELTA_CT + cb]], states)
        of_ref[0, cf * c:(cf + 1) * c, :] = outs[0].astype(of_ref.dtype)
        ob_ref[0, cb * c:(cb + 1) * c, :] = outs[1].astype(ob_ref.dtype)
    for d in range(2):
        for h in range(H_B):
            s_scr[d, h] = states[d][h]

    @pl.when(j == pl.num_programs(1) - 1)
    def _():
        sf_ref[0] = s_scr[0]
        sb_ref[0] = s_scr[1]


def _delta_call(q, k, v, ecol, s0f, s0b):
    B, L, _ = q.shape
    n = L // CHUNK
    tl = DELTA_CT * CHUNK
    nb = L // tl
    erow = ecol[..., :2 * H_B].reshape(B, n, CHUNK, 2, H_B).transpose(0, 1, 3, 4, 2).reshape(B, n, 2, C4)
    fmap = lambda b, j: (b, j, 0)
    bmap = lambda b, j: (b, nb - 1 - j, 0)
    seq = lambda m: pl.BlockSpec((1, tl, W_B), m)
    ecs = lambda m: pl.BlockSpec((1, tl, LANES), m)
    ers = lambda m: pl.BlockSpec((1, DELTA_CT, 2, C4), lambda b, j: m(b, j) + (0,))
    st = pl.BlockSpec((1, H_B, D_K, D_V), lambda b, j: (b, 0, 0, 0))
    return pl.pallas_call(
        _delta_kernel,
        grid=(B, nb),
        in_specs=[seq(fmap), seq(fmap), seq(fmap), ecs(fmap), ers(fmap),
                  seq(bmap), seq(bmap), seq(bmap), ecs(bmap), ers(bmap), st, st],
        out_specs=[seq(fmap), seq(bmap), st, st],
        out_shape=[jax.ShapeDtypeStruct((B, L, W_B), BF16)] * 2
        + [jax.ShapeDtypeStruct((B, H_B, D_K, D_V), F32)] * 2,
        scratch_shapes=[pltpu.VMEM((2, H_B, D_K, D_V), F32)],
        compiler_params=_cparams(("parallel", "arbitrary")),
        name="delta",
    )(q, k, v, ecol, erow, q, k, v, ecol, erow, s0f, s0b)


def _merge_kernel(x_ref, oa_ref, of_ref, ob_ref, z_ref, gate_ref, g1_ref, sh2_ref, sc2_ref,
                  onw_ref, wpa_ref, wpb_ref, wo_ref, n2w_ref, wr_ref, br_ref, ltri_ref,
                  x1_ref, h2_ref, route_ref, cnt_ref, cnt_scr):
    o = of_ref[...].astype(F32) + ob_ref[...].astype(F32)
    parts = []
    for h in range(H_B):
        t = o[:, h * D_V:(h + 1) * D_V]
        parts.append(t * lax.rsqrt(jnp.mean(t * t, axis=-1, keepdims=True) + EPS))
    ob = jnp.concatenate(parts, axis=-1) * onw_ref[...] * z_ref[...].astype(F32)
    pa = _dot(oa_ref[...], wpa_ref[...])
    pb = _dot(ob.astype(BF16), wpb_ref[...])
    gate = gate_ref[...]
    merged = gate[:, :D_MODEL].astype(F32) * pa + gate[:, D_MODEL:].astype(F32) * pb
    mix = _dot(merged.astype(BF16), wo_ref[...])
    x1 = x_ref[...] + g1_ref[0] * mix
    x1_ref[...] = x1
    ms = jnp.mean(x1 * x1, axis=-1, keepdims=True)
    h2 = x1 * lax.rsqrt(ms + EPS) * n2w_ref[...]
    h2 = h2 * (1.0 + sc2_ref[0]) + sh2_ref[0]
    hi = h2.astype(BF16)
    h2_ref[...] = h2
    lo = (h2 - hi.astype(F32)).astype(BF16)
    logits = _dot(jnp.concatenate([hi, lo, hi], axis=-1), wr_ref[...]) + br_ref[...]
    lane = lax.broadcasted_iota(jnp.int32, logits.shape, 1)
    big = jnp.int32(LANES)

    def first_max(vals, mask):
        mv = jnp.where(mask, vals, NEG_INF)
        m = mv.max(axis=-1, keepdims=True)
        idx = jnp.where(mask & (mv == m), lane, big).min(axis=-1, keepdims=True)
        return m, idx

    gmask = lane < N_GROUPS
    gm, grp = first_max(logits, gmask)
    p_grp = 1.0 / jnp.where(gmask, jnp.exp(logits - gm), 0.0).sum(axis=-1, keepdims=True)
    lo_lane = N_GROUPS + grp * EXP_PER_GROUP
    emask = (lane >= lo_lane) & (lane < lo_lane + EXP_PER_GROUP)
    em, _ = first_max(logits, emask)
    ex = jnp.where(emask, jnp.exp(logits - em), 0.0)
    probs = ex / ex.sum(axis=-1, keepdims=True)
    p1, i1 = first_max(probs, emask)
    p2, i2 = first_max(probs, emask & (lane != i1))
    tot = p1 + p2
    w1 = p_grp * p1 / tot
    w2 = p_grp * p2 / tot
    e1 = i1 - N_GROUPS
    e2 = i2 - N_GROUPS
    @pl.when(pl.program_id(0) == 0)
    def _():
        cnt_scr[...] = jnp.zeros_like(cnt_scr)

    hit = jnp.where(lane == e1, 1.0, 0.0) + jnp.where(lane == e2, 1.0, 0.0)
    before = _dot(ltri_ref[...], hit.astype(BF16)) + cnt_scr[...]
    r1 = jnp.where(lane == e1, before, 0.0).sum(axis=-1, keepdims=True)
    r2 = jnp.where(lane == e2, before, 0.0).sum(axis=-1, keepdims=True)
    cnt_scr[...] = cnt_scr[...] + hit.sum(axis=0, keepdims=True)
    cnt_ref[...] = cnt_scr[...]
    cols = [e1.astype(F32), e2.astype(F32), w1, w2, r1, r2]
    route = jnp.zeros(logits.shape, F32)
    for i, col in enumerate(cols):
        route = jnp.where(lane == i, col, route)
    route_ref[...] = route


def _merge_call(x, oa, of, ob, z, gate, mod3, mod_stride, w):
    B, L, _ = x.shape
    n_tok = B * L
    tm = ROW_TILE
    assert n_tok % tm == 0 and (L % tm == 0 or tm % L == 0)
    flat = lambda a: a.reshape(n_tok, a.shape[-1])
    rspec = lambda n: pl.BlockSpec((tm, n), lambda i: (i, 0))
    mspec = lambda k: pl.BlockSpec((1, 1, D_MODEL), lambda i: ((i * tm) // L * mod_stride + k, 0, 0))
    t = np.arange(tm)
    ltri = jnp.asarray((t[:, None] > t[None, :]).astype(np.float32)).astype(BF16)
    consts = [w["onw"], w["wpa"], w["wpb"], w["wo"], w["norm2_w"], w["wr3"], w["br"], ltri]
    return pl.pallas_call(
        _merge_kernel,
        grid=(n_tok // tm,),
        in_specs=[rspec(D_MODEL), rspec(W_A), rspec(W_B), rspec(W_B), rspec(W_B), rspec(2 * D_MODEL),
                  mspec(2), mspec(3), mspec(4)] + [_const_spec(c.shape) for c in consts],
        out_specs=[rspec(D_MODEL), rspec(D_MODEL), rspec(LANES), _const_spec((1, LANES))],
        out_shape=[jax.ShapeDtypeStruct((n_tok, D_MODEL), F32),
                   jax.ShapeDtypeStruct((n_tok, D_MODEL), F32),
                   jax.ShapeDtypeStruct((n_tok, LANES), F32),
                   jax.ShapeDtypeStruct((1, LANES), F32)],
        scratch_shapes=[pltpu.VMEM((1, LANES), F32)],
        compiler_params=_cparams(("arbitrary",)),
        name="merge",
    )(flat(x), flat(oa), flat(of), flat(ob), flat(z), flat(gate), mod3, mod3, mod3, *consts)


def _row_gather_start(idx_ref, n_rows, src_hbm, buf, sem, slot):
    for r in range(n_rows):
        pltpu.make_async_copy(src_hbm.at[pl.ds(idx_ref[0, 0, r], 1), :], buf.at[slot, pl.ds(r, 1), :],
                              sem.at[slot]).start(priority=r % 2)


def _row_gather_wait(n_rows, src_hbm, buf, sem, slot):
    for r in range(n_rows):
        pltpu.make_async_copy(src_hbm.at[pl.ds(0, 1), :], buf.at[slot, pl.ds(r, 1), :], sem.at[slot]).wait()


def _gather_pipeline(i, n_steps, idx_ref, idx_next_ref, n_rows, src_hbm, buf, sem):
    slot = lax.rem(i, 2)

    @pl.when(i == 0)
    def _():
        _row_gather_start(idx_ref, n_rows, src_hbm, buf, sem, 0)

    @pl.when(i + 1 < n_steps)
    def _():
        _row_gather_start(idx_next_ref, n_rows, src_hbm, buf, sem, 1 - slot)

    @pl.when(i < n_steps)
    def _():
        _row_gather_wait(n_rows, src_hbm, buf, sem, slot)

    return slot


def _moe_kernel(te_ref, nt_ref, idx_ref, idxn_ref, h2_hbm, wg_ref, wu_ref, wd_ref, y_ref,
                xbuf, sem, wg_bf, wu_bf, wd_bf):
    i = pl.program_id(0)
    n_used = nt_ref[0]
    slot = _gather_pipeline(i, n_used, idx_ref, idxn_ref, MOE_TILE, h2_hbm, xbuf, sem)

    @pl.when((i == 0) | (te_ref[i] != te_ref[jnp.maximum(i - 1, 0)]))
    def _():
        wg_bf[...] = wg_ref[0].astype(BF16)
        wu_bf[...] = wu_ref[0].astype(BF16)
        wd_bf[...] = wd_ref[0].astype(BF16)

    @pl.when(i < n_used)
    def _():
        x = xbuf[slot].astype(BF16)
        g = _dot(x, wg_bf[...])
        hid = g * jax.nn.sigmoid(g) * _dot(x, wu_bf[...])
        y_ref[...] = _dot(hid.astype(BF16), wd_bf[...])

    @pl.when(i >= n_used)
    def _():
        y_ref[...] = jnp.zeros_like(y_ref)


def _moe_call(h2, src_tok, tile_expert, n_tiles_used, w_gate, w_up, w_down):
    p_pad = src_tok.shape[0]
    n_tiles = p_pad // MOE_TILE
    idx = src_tok.reshape(n_tiles, 1, MOE_TILE)
    smem_tile = lambda m: pl.BlockSpec((1, 1, MOE_TILE), m, memory_space=pltpu.SMEM)
    wspec = lambda r, c: pl.BlockSpec((1, r, c), lambda i, te, nt: (te[i], 0, 0))
    grid_spec = pltpu.PrefetchScalarGridSpec(
        num_scalar_prefetch=2,
        grid=(n_tiles,),
        in_specs=[smem_tile(lambda i, te, nt: (i, 0, 0)),
                  smem_tile(lambda i, te, nt: (jnp.minimum(i + 1, n_tiles - 1), 0, 0)),
                  pl.BlockSpec(memory_space=pl.ANY),
                  wspec(D_MODEL, D_FF_E), wspec(D_MODEL, D_FF_E), wspec(D_FF_E, D_MODEL)],
        out_specs=pl.BlockSpec((MOE_TILE, D_MODEL), lambda i, te, nt: (i, 0)),
        scratch_shapes=[pltpu.VMEM((2, MOE_TILE, D_MODEL), F32), pltpu.SemaphoreType.DMA((2,)),
                        pltpu.VMEM((D_MODEL, D_FF_E), BF16), pltpu.VMEM((D_MODEL, D_FF_E), BF16),
                        pltpu.VMEM((D_FF_E, D_MODEL), BF16)],
    )
    return pl.pallas_call(
        _moe_kernel,
        grid_spec=grid_spec,
        out_shape=jax.ShapeDtypeStruct((p_pad, D_MODEL), F32),
        compiler_params=_cparams(("arbitrary",)),
        name="moe",
    )(tile_expert, n_tiles_used, idx, idx, h2, w_gate, w_up, w_down)


COMBINE_TILE = 256


def _combine_kernel(pos_ref, posn_ref, x1_ref, route_ref, g2_ref, y_hbm, o_ref, ybuf, sem):
    i = pl.program_id(0)
    t = COMBINE_TILE
    slot = _gather_pipeline(i, pl.num_programs(0), pos_ref, posn_ref, 2 * t, y_hbm, ybuf, sem)
    route = route_ref[...]
    w1 = route[:, 2:3]
    w2 = route[:, 3:4]
    o_ref[...] = x1_ref[...] + g2_ref[0] * (w1 * ybuf[slot, :t] + w2 * ybuf[slot, t:])


def _combine_call(x1, route, pos2, y, mod3, mod_stride, seq_len):
    n = x1.shape[0]
    t = COMBINE_TILE
    n_tiles = n // t
    pos_tiles = pos2.reshape(n_tiles, t, 2).transpose(0, 2, 1).reshape(n_tiles, 1, 2 * t)
    smem_tile = lambda m: pl.BlockSpec((1, 1, 2 * t), m, memory_space=pltpu.SMEM)
    return pl.pallas_call(
        _combine_kernel,
        grid=(n_tiles,),
        in_specs=[smem_tile(lambda i: (i, 0, 0)),
                  smem_tile(lambda i: (jnp.minimum(i + 1, n_tiles - 1), 0, 0)),
                  pl.BlockSpec((t, D_MODEL), lambda i: (i, 0)),
                  pl.BlockSpec((t, LANES), lambda i: (i, 0)),
                  pl.BlockSpec((1, 1, D_MODEL), lambda i: ((i * t) // seq_len * mod_stride + 5, 0, 0)),
                  pl.BlockSpec(memory_space=pl.ANY)],
        out_specs=pl.BlockSpec((t, D_MODEL), lambda i: (i, 0)),
        out_shape=jax.ShapeDtypeStruct((n, D_MODEL), F32),
        scratch_shapes=[pltpu.VMEM((2, 2 * t, D_MODEL), F32), pltpu.SemaphoreType.DMA((2,))],
        compiler_params=_cparams(("arbitrary",)),
        name="combine",
    )(pos_tiles, pos_tiles, x1, route, mod3, y)


def _moe_forward(h2, route, counts, w_gate, w_up, w_down):
    n = h2.shape[0]
    eid = route[:, :2].astype(jnp.int32).reshape(-1)
    rank = route[:, 4:6].astype(jnp.int32).reshape(-1)
    tok = jnp.arange(2 * n, dtype=jnp.int32) // 2
    counts = counts[0, :N_EXPERTS].astype(jnp.int32)
    padded = ((counts + MOE_TILE - 1) // MOE_TILE) * MOE_TILE
    seg_end = jnp.cumsum(padded)
    seg_start = seg_end - padded
    pos = seg_start[eid] + rank
    p_pad = 2 * n + N_EXPERTS * MOE_TILE
    src_tok = (jnp.arange(p_pad, dtype=jnp.int32) % n).at[pos].set(tok)
    n_tiles = p_pad // MOE_TILE
    tile_start = jnp.arange(n_tiles, dtype=jnp.int32) * MOE_TILE
    tile_expert = jnp.minimum(jnp.sum((tile_start[:, None] >= seg_end[None, :]).astype(jnp.int32), axis=1),
                              N_EXPERTS - 1)
    n_used = (seg_end[-1] // MOE_TILE).astype(jnp.int32).reshape(1)
    y = _moe_call(h2, src_tok, tile_expert, n_used, w_gate, w_up, w_down)
    return y, pos.reshape(n, 2)


def _prep_weights(norm1_w, norm2_w, w_in, conv_w, a_log, dt_bias, q_norm_w, k_norm_w, o_norm_w,
                  w_proj_a, w_proj_b, w_out, w_router_g, b_router_g, w_router_e, b_router_e,
                  w_exp_gate, w_exp_up, w_exp_down):
    w = {}
    w["norm1_w"] = norm1_w.reshape(1, D_MODEL)
    w["norm2_w"] = norm2_w.reshape(1, D_MODEL)
    o = 0
    w["wa"] = w_in[:, o:o + 3 * W_A].astype(BF16); o += 3 * W_A
    w["wb"] = w_in[:, o:o + 3 * W_B].astype(BF16); o += 3 * W_B
    w["wz"] = w_in[:, o:o + W_B].astype(BF16); o += W_B
    small = w_in[:, o:o + 4 * H_B]; o += 4 * H_B
    w["wg"] = w_in[:, o:o + 2 * D_MODEL].astype(BF16)
    def group_lanes(t):
        r = t.shape[0]
        t = t.reshape(r, -1, N_HG, HP).transpose(0, 2, 1, 3).reshape(r, N_HG, -1)
        return jnp.pad(t, ((0, 0), (0, 0), (0, LANES - t.shape[-1]))).reshape(r, N_HG * LANES)

    w["we"] = group_lanes(small).astype(BF16)
    w["alog_l"] = group_lanes(a_log.reshape(1, 2 * H_B))
    w["dtb_l"] = group_lanes(dt_bias.reshape(1, 2 * H_B))
    w["qnw"] = jnp.tile(q_norm_w, H_A).reshape(1, W_A)
    w["knw"] = jnp.tile(k_norm_w, H_A).reshape(1, W_A)
    w["onw"] = jnp.tile(o_norm_w, H_B).reshape(1, W_B)
    blk = np.arange(256) // D_HA
    w["bd256"] = jnp.asarray((blk[:, None] == blk[None, :]).astype(np.float32)).astype(BF16)
    w["conv_w"] = conv_w
    w["wpa"] = w_proj_a.astype(BF16)
    w["wpb"] = w_proj_b.astype(BF16)
    w["wo"] = w_out.astype(BF16)
    wr = jnp.concatenate([w_router_g, w_router_e], axis=1)
    wr = jnp.pad(wr, ((0, 0), (0, LANES - wr.shape[1])))
    wr_hi = wr.astype(BF16)
    wr_lo = (wr - wr_hi.astype(F32)).astype(BF16)
    w["wr3"] = jnp.concatenate([wr_hi, wr_hi, wr_lo], axis=0)
    br = jnp.concatenate([b_router_g, b_router_e])
    w["br"] = jnp.pad(br, (0, LANES - br.shape[0])).reshape(1, LANES)
    w["w_gate"], w["w_up"], w["w_down"] = w_exp_gate, w_exp_up, w_exp_down
    return w


def _layer(x, mod3, mod_stride, w, ctx):
    B, L, _ = x.shape
    latent = ctx is not None
    kv_dtype = BF16 if latent else F32
    qa, ka, va, qkvb, z, gate, ecol = _inproj_call(x, mod3, mod_stride, w, kv_dtype)
    if latent:
        k_ctx, v_ctx, s0f, s0b, bias = ctx
        oa = _nbr_attn_call(qa, ka, va, k_ctx, v_ctx, bias)
    else:
        s0f = s0b = jnp.zeros((B, H_B, D_K, D_V), F32)
        oa = _ctx_attn_call(qa, ka, va)
    qd, kd, vd = _conv_call(qkvb, w["conv_w"], rope=latent)
    of, ob, s_f, s_b = _delta_call(qd, kd, vd, ecol, s0f, s0b)
    x1, h2, route, counts = _merge_call(x, oa, of, ob, z, gate, mod3, mod_stride, w)
    ys, pos2 = _moe_forward(h2, route, counts, w["w_gate"], w["w_up"], w["w_down"])
    y = _combine_call(x1, route, pos2, ys, mod3, mod_stride, L)
    return y.reshape(B, L, D_MODEL), ka, va, s_f, s_b


def kernel(x_prompt, x_sample, c, cache_attn_k, cache_attn_v, state_delta_fwd, state_delta_bwd, c_ctx,
           norm1_w, norm2_w, w_mod, b_mod, w_in, conv_w, a_log, dt_bias, q_norm_w, k_norm_w, rpb,
           o_norm_w, w_proj_a, w_proj_b, w_out, w_router_g, b_router_g, w_router_e, b_router_e,
           w_exp_gate, w_exp_up, w_exp_down):
    assert norm1_w.shape[0] == 1, "single-layer trunk"
    Bp, Lp, _ = x_prompt.shape
    Bs, Ls, _ = x_sample.shape
    l = 0
    w = _prep_weights(norm1_w[l], norm2_w[l], w_in[l], conv_w[l], a_log[l], dt_bias[l], q_norm_w[l],
                      k_norm_w[l], o_norm_w[l], w_proj_a[l], w_proj_b[l], w_out[l], w_router_g[l],
                      b_router_g[l], w_router_e[l], b_router_e[l], w_exp_gate[l], w_exp_up[l],
                      w_exp_down[l])
    cond = jnp.concatenate([c_ctx[None, :], c, jnp.zeros((8 - 1 - Bs, D_MODEL), F32)], axis=0)
    mod = _mod_call(cond, w_mod[l], b_mod[l])
    mod3 = mod.reshape(8 * 6, 1, D_MODEL)
    bias = _bias_table_call(rpb[l])
    y_p, k_a, v_a, s_f, s_b = _layer(x_prompt, mod3[:6], 0, w, None)
    lc = cache_attn_k.shape[2]
    ctx = (cache_attn_k[:, l].reshape(Bs, lc, W_A).astype(BF16),
           cache_attn_v[:, l].reshape(Bs, lc, W_A).astype(BF16),
           state_delta_fwd[:, l], state_delta_bwd[:, l], bias)
    y_s, _, _, _, _ = _layer(x_sample, mod3[6:6 + 6 * Bs], 6, w, ctx)
    new_k = k_a.reshape(Bp, 1, Lp, H_A, D_HA)
    new_v = v_a.reshape(Bp, 1, Lp, H_A, D_HA)
    return (y_p, y_s, new_k, new_v, s_f[:, None], s_b[:, None])
```

```python
import functools
import math

import numpy as np
import jax
import jax.numpy as jnp
from jax import lax
from jax.experimental import pallas as pl
from jax.experimental.pallas import tpu as pltpu

F32 = jnp.float32
BF16 = jnp.bfloat16

D_MODEL = 1024
GRID_W = 64
H_A = 8
D_HA = 64
W_A = H_A * D_HA
WIN_H = 8
WIN_W = 16
H_B = 4
D_K = 128
D_V = 128
W_B = H_B * D_V
CONV_K = 5
CHUNK = 64
ROPE_BASE = 10000.0
N_GROUPS = 4
EXP_PER_GROUP = 8
N_EXPERTS = N_GROUPS * EXP_PER_GROUP
D_FF_E = 256
EPS = 1e-6
NEG_INF = -1e30

LANES = 128
HP = H_B
N_HG = H_B // HP
ROW_TILE = 512
INPROJ_TILE = 256
MOE_TILE = 256
VMEM_LIMIT = 56 * 1024 * 1024

HIGHEST = lax.Precision.HIGHEST


def _cparams(sem):
    return pltpu.CompilerParams(dimension_semantics=sem, vmem_limit_bytes=VMEM_LIMIT)


def _dot(a, b):
    return jnp.dot(a, b, preferred_element_type=F32)


def _dot_nt(a, b):
    return lax.dot_general(a, b, (((1,), (1,)), ((), ())), preferred_element_type=F32)


def _dot_tn(a, b):
    return lax.dot_general(a, b, (((0,), (0,)), ((), ())), preferred_element_type=F32)


def _const_spec(shape):
    nd = len(shape)
    return pl.BlockSpec(shape, lambda *_: (0,) * nd)


def _mod_kernel(c_ref, w_ref, b_ref, o_ref):
    c = c_ref[...]
    s = c * jax.nn.sigmoid(c)
    o_ref[...] = jnp.dot(s, w_ref[...], preferred_element_type=F32, precision=HIGHEST) + b_ref[...]


def _mod_call(cond8, w_mod, b_mod):
    n_out = w_mod.shape[1]
    tn = 1024
    return pl.pallas_call(
        _mod_kernel,
        grid=(n_out // tn,),
        in_specs=[pl.BlockSpec((8, D_MODEL), lambda j: (0, 0)),
                  pl.BlockSpec((D_MODEL, tn), lambda j: (0, j)),
                  pl.BlockSpec((1, tn), lambda j: (0, j))],
        out_specs=pl.BlockSpec((8, tn), lambda j: (0, j)),
        out_shape=jax.ShapeDtypeStruct((8, n_out), F32),
        compiler_params=_cparams(("arbitrary",)),
        name="mod",
    )(cond8, w_mod, b_mod.reshape(1, n_out))


def _softplus(x):
    return jnp.maximum(x, 0.0) + jnp.log1p(jnp.exp(-jnp.abs(x)))


def _inproj_kernel(x_ref, sh_ref, sc_ref, nw_ref, wa_ref, wb_ref, wz_ref, wg_ref, we_ref,
                   qnw_ref, knw_ref, bd_ref, alog_ref, dtb_ref, ltri_ref, utri_ref,
                   qa_ref, ka_ref, va_ref, qkvb_ref, z_ref, gate_ref, e_ref):
    x = x_ref[...]
    ms = jnp.mean(x * x, axis=-1, keepdims=True)
    h = x * lax.rsqrt(ms + EPS) * nw_ref[...]
    h = h * (1.0 + sc_ref[0]) + sh_ref[0]
    hb = h.astype(BF16)

    a = _dot(hb, wa_ref[...])
    bd = bd_ref[...]

    def head_rms(t, w):
        sq = (t * t).astype(BF16)
        ss = jnp.concatenate([_dot(sq[:, i * 256:(i + 1) * 256], bd) for i in range(W_A // 256)], axis=-1)
        return t * lax.rsqrt(ss * (1.0 / D_HA) + EPS) * w

    qa = head_rms(a[:, :W_A], qnw_ref[...]) * (D_HA ** -0.5)
    qa_ref[...] = qa.astype(qa_ref.dtype)
    ka_ref[...] = head_rms(a[:, W_A:2 * W_A], knw_ref[...]).astype(ka_ref.dtype)
    va_ref[...] = a[:, 2 * W_A:].astype(va_ref.dtype)

    qkvb_ref[...] = _dot(hb, wb_ref[...]).astype(qkvb_ref.dtype)
    z = _dot(hb, wz_ref[...])
    z_ref[...] = (z * jax.nn.sigmoid(z)).astype(z_ref.dtype)
    gate_ref[...] = jax.nn.sigmoid(_dot(hb, wg_ref[...])).astype(gate_ref.dtype)

    e = _dot(hb, we_ref[...])
    lane = lax.broadcasted_iota(jnp.int32, e.shape, 1) % LANES
    g = -jnp.exp(alog_ref[...]) * _softplus(e + dtb_ref[...])
    act = jnp.where(lane < 2 * HP, g, jax.nn.sigmoid(e))
    cf = jnp.dot(ltri_ref[...], act, preferred_element_type=F32, precision=HIGHEST)
    cb = jnp.dot(utri_ref[...], act, preferred_element_type=F32, precision=HIGHEST)
    e_ref[...] = jnp.where(lane < HP, cf, jnp.where(lane < 2 * HP, cb, act))


def _inproj_call(x, mod3, mod_stride, w, kv_dtype):
    B, L, _ = x.shape
    n_tok = B * L
    tm = INPROJ_TILE
    assert n_tok % tm == 0 and (L % tm == 0 or tm % L == 0)
    row = lambda n, dt: jax.ShapeDtypeStruct((n_tok, n), dt)
    rspec = lambda n: pl.BlockSpec((tm, n), lambda i: (i, 0))
    mspec = lambda k: pl.BlockSpec((1, 1, D_MODEL), lambda i: ((i * tm) // L * mod_stride + k, 0, 0))
    t = np.arange(tm)
    same = (t[:, None] // CHUNK) == (t[None, :] // CHUNK)
    ltri = jnp.asarray((same & (t[:, None] >= t[None, :])).astype(np.float32))
    utri = jnp.asarray((same & (t[:, None] <= t[None, :])).astype(np.float32))
    consts = [w["norm1_w"], w["wa"], w["wb"], w["wz"], w["wg"], w["we"], w["qnw"], w["knw"],
              w["bd256"], w["alog_l"], w["dtb_l"], ltri, utri]
    outs = pl.pallas_call(
        _inproj_kernel,
        grid=(n_tok // tm,),
        in_specs=[rspec(D_MODEL), mspec(0), mspec(1)] + [_const_spec(c.shape) for c in consts],
        out_specs=[rspec(W_A), rspec(W_A), rspec(W_A), rspec(3 * W_B), rspec(W_B),
                   rspec(2 * D_MODEL), rspec(N_HG * LANES)],
        out_shape=[row(W_A, BF16), row(W_A, kv_dtype), row(W_A, kv_dtype), row(3 * W_B, BF16),
                   row(W_B, BF16), row(2 * D_MODEL, BF16), row(N_HG * LANES, F32)],
        compiler_params=_cparams(("parallel",)),
        name="inproj",
    )(x.reshape(n_tok, D_MODEL), mod3, mod3, *consts)
    return [o.reshape(B, L, o.shape[-1]) for o in outs]


def _pair_softmax_pv(s_parts, v_parts):
    m = s_parts[0].max(axis=-1, keepdims=True)
    for s in s_parts[1:]:
        m = jnp.maximum(m, s.max(axis=-1, keepdims=True))
    den = 0.0
    acc = 0.0
    for s, v in zip(s_parts, v_parts):
        p = jnp.exp(s - m)
        den = den + p.sum(axis=-1, keepdims=True)
        acc = acc + _dot(p.astype(BF16), v)
    return acc / den


def _stack_pair(qp):
    lane = lax.broadcasted_iota(jnp.int32, qp.shape, 1)
    zero = jnp.zeros_like(qp)
    return jnp.concatenate([jnp.where(lane < D_HA, qp, zero), jnp.where(lane >= D_HA, qp, zero)], axis=0)


def _unstack_pair(o2):
    t = o2.shape[0] // 2
    lane = lax.broadcasted_iota(jnp.int32, (t, LANES), 1)
    return jnp.where(lane < D_HA, o2[:t], o2[t:])


def _ctx_attn_kernel(q_ref, k_ref, v_ref, o_ref):
    for p in range(H_A // 2):
        sl = slice(p * LANES, (p + 1) * LANES)
        qs = _stack_pair(q_ref[0, :, sl])
        kp = k_ref[0, :, sl].astype(BF16)
        vp = v_ref[0, :, sl].astype(BF16)
        o2 = _pair_softmax_pv([_dot_nt(qs, kp)], [vp])
        o_ref[0, :, sl] = _unstack_pair(o2).astype(o_ref.dtype)


def _ctx_attn_call(q, k, v):
    B, L, _ = q.shape
    spec = pl.BlockSpec((1, L, W_A), lambda b: (b, 0, 0))
    return pl.pallas_call(
        _ctx_attn_kernel,
        grid=(B,),
        in_specs=[spec, spec, spec],
        out_specs=spec,
        out_shape=jax.ShapeDtypeStruct((B, L, W_A), BF16),
        compiler_params=_cparams(("parallel",)),
        name="ctx_attn",
    )(q, k, v)


N_DR = 2 * WIN_H - 1
N_DC = 2 * WIN_W - 1
NBR_ROWS = 4


def _bias_table_kernel(rpb_ref, o_ref):
    h = pl.program_id(0)
    c = lax.broadcasted_iota(jnp.int32, (GRID_W, 2 * GRID_W), 0)
    lane = lax.broadcasted_iota(jnp.int32, (GRID_W, 2 * GRID_W), 1)
    kc = lane % GRID_W
    ws = jnp.clip(c - WIN_W // 2, 0, GRID_W - WIN_W)
    in_win = (kc >= ws) & (kc < ws + WIN_W)
    dc = kc - c + (WIN_W - 1)
    for dr in range(N_DR - 1):
        base = (h * N_DR + dr) * N_DC
        val = jnp.zeros((GRID_W, 2 * GRID_W), F32)
        for d in range(N_DC):
            val = jnp.where(dc == d, jnp.where(lane < GRID_W, rpb_ref[base + d], rpb_ref[base + N_DC + d]), val)
        o_ref[0, dr] = jnp.where(in_win, val, NEG_INF)


def _bias_table_call(rpb):
    return pl.pallas_call(
        _bias_table_kernel,
        grid=(H_A,),
        in_specs=[pl.BlockSpec(memory_space=pltpu.SMEM)],
        out_specs=pl.BlockSpec((1, N_DR - 1, GRID_W, 2 * GRID_W), lambda h: (h, 0, 0, 0)),
        out_shape=jax.ShapeDtypeStruct((H_A, N_DR - 1, GRID_W, 2 * GRID_W), F32),
        compiler_params=_cparams(("arbitrary",)),
        name="bias_table",
    )(rpb.reshape(-1))


def _nbr_attn_kernel(q_ref, k_ref, v_ref, kc_ref, vc_ref, bias_ref, o_ref, *, rows):
    inst = []
    for rr in range(NBR_ROWS):
        r = pl.program_id(1) * NBR_ROWS + rr
        first = jnp.clip(r - WIN_H // 2, 0, rows - WIN_H)
        start = pl.multiple_of(first * GRID_W, GRID_W)
        shift = r - first
        for p in range(H_A // 2):
            inst.append((rr, p, start, shift))
    scores = []
    for rr, p, start, shift in inst:
        sl = slice(p * LANES, (p + 1) * LANES)
        qs = _stack_pair(q_ref[0, rr * GRID_W:(rr + 1) * GRID_W, sl])
        bias = jnp.concatenate(
            [jnp.concatenate([bias_ref[2 * p + hh, 2 * u - shift + WIN_H - 1] for u in range(WIN_H // 2)], axis=1)
             for hh in range(2)], axis=0)
        s_loc = _dot_nt(qs, k_ref[0, pl.ds(start, WIN_H * GRID_W), sl]) + bias
        s_ctx = _dot_nt(qs, kc_ref[0, :, sl])
        scores.append((s_loc, s_ctx))
    for (rr, p, start, shift), (s_loc, s_ctx) in zip(inst, scores):
        sl = slice(p * LANES, (p + 1) * LANES)
        o2 = _pair_softmax_pv([s_loc, s_ctx], [v_ref[0, pl.ds(start, WIN_H * GRID_W), sl], vc_ref[0, :, sl]])
        o_ref[0, rr * GRID_W:(rr + 1) * GRID_W, sl] = _unstack_pair(o2).astype(o_ref.dtype)


def _nbr_attn_call(q, k, v, k_ctx, v_ctx, bias):
    B, L, _ = q.shape
    rows = L // GRID_W
    lc = k_ctx.shape[1]
    full = pl.BlockSpec((1, L, W_A), lambda b, r: (b, 0, 0))
    ctx = pl.BlockSpec((1, lc, W_A), lambda b, r: (b, 0, 0))
    rowspec = pl.BlockSpec((1, NBR_ROWS * GRID_W, W_A), lambda b, r: (b, r, 0))
    return pl.pallas_call(
        functools.partial(_nbr_attn_kernel, rows=rows),
        grid=(B, rows // NBR_ROWS),
        in_specs=[rowspec, full, full, ctx, ctx, _const_spec(bias.shape)],
        out_specs=rowspec,
        out_shape=jax.ShapeDtypeStruct((B, L, W_A), BF16),
        compiler_params=_cparams(("parallel", "arbitrary")),
        name="nbr_attn",
    )(q, k, v, k_ctx, v_ctx, bias)


HALO = 16


def _conv_kernel(*refs, rope):
    if rope:
        x_ref, prev_ref, next_ref, w_ref, cos_ref, sin_ref, q_ref, k_ref, v_ref, xe_ref = refs
    else:
        x_ref, prev_ref, next_ref, w_ref, q_ref, k_ref, v_ref, xe_ref = refs
    i = pl.program_id(1)
    n_i = pl.num_programs(1)
    tl = x_ref.shape[1]
    xe_ref[0:HALO] = jnp.where(i > 0, prev_ref[0].astype(F32), 0.0)
    xe_ref[HALO:HALO + tl] = x_ref[0].astype(F32)
    xe_ref[HALO + tl:] = jnp.where(i < n_i - 1, next_ref[0].astype(F32), 0.0)
    half = CONV_K // 2
    y = w_ref[0:1, :] * xe_ref[HALO - half:HALO - half + tl]
    for j in range(1, CONV_K):
        y = y + w_ref[j:j + 1, :] * xe_ref[HALO - half + j:HALO - half + j + tl]
    y = y * jax.nn.sigmoid(y)

    def l2n(t):
        return t * lax.rsqrt(jnp.sum(t * t, axis=-1, keepdims=True) + EPS)

    def rot(t):
        if not rope:
            return t
        lane = lax.broadcasted_iota(jnp.int32, t.shape, 1)
        swapped = jnp.where(lane % 2 == 0, pltpu.roll(t, LANES - 1, 1), pltpu.roll(t, 1, 1))
        return t * cos_ref[...] + swapped * sin_ref[...]

    for h in range(H_B):
        sl = slice(h * D_K, (h + 1) * D_K)
        q_ref[0, :, sl] = (rot(l2n(y[:, sl])) * (D_K ** -0.5)).astype(q_ref.dtype)
        k_ref[0, :, sl] = rot(l2n(y[:, W_B + h * D_K:W_B + (h + 1) * D_K])).astype(k_ref.dtype)
    v_ref[0] = y[:, 2 * W_B:].astype(v_ref.dtype)


def _rope_tables(L):
    pos = jnp.arange(L)
    rows = (pos // GRID_W).astype(F32)
    cols = (pos % GRID_W).astype(F32)
    n_pairs = D_K // 4
    inv = ROPE_BASE ** (-jnp.arange(n_pairs, dtype=F32) / n_pairs)
    ang = jnp.concatenate([rows[:, None] * inv, cols[:, None] * inv], axis=-1)
    cos = jnp.repeat(jnp.cos(ang), 2, axis=-1)
    sin = jnp.repeat(jnp.sin(ang), 2, axis=-1)
    sign = jnp.asarray(np.tile(np.array([-1.0, 1.0], np.float32), D_K // 2))
    return cos, sin * sign


def _conv_call(x, conv_w, rope):
    B, L, C = x.shape
    tl = min(ROW_TILE, L)
    nh = tl // HALO
    n_halo = L // HALO
    cur = pl.BlockSpec((1, tl, C), lambda b, i: (b, i, 0))
    prev = pl.BlockSpec((1, HALO, C), lambda b, i: (b, jnp.maximum(i * nh - 1, 0), 0))
    nxt = pl.BlockSpec((1, HALO, C), lambda b, i: (b, jnp.minimum((i + 1) * nh, n_halo - 1), 0))
    ins = [x, x, x, conv_w]
    specs = [cur, prev, nxt, _const_spec(conv_w.shape)]
    if rope:
        cos, sin = _rope_tables(L)
        ins += [cos, sin]
        specs += [pl.BlockSpec((tl, D_K), lambda b, i: (i, 0))] * 2
    out = pl.BlockSpec((1, tl, W_B), lambda b, i: (b, i, 0))
    return pl.pallas_call(
        functools.partial(_conv_kernel, rope=rope),
        grid=(B, L // tl),
        in_specs=specs,
        out_specs=[out, out, out],
        out_shape=[jax.ShapeDtypeStruct((B, L, W_B), BF16)] * 3,
        scratch_shapes=[pltpu.VMEM((tl + 2 * HALO, C), F32)],
        compiler_params=_cparams(("parallel", "parallel")),
        name="conv",
    )(*ins)


DELTA_CT = 4
C4 = H_B * CHUNK


def _blockdiag_rows(x, n_blk):
    w = x.shape[1] // n_blk
    lane_blk = lax.broadcasted_iota(jnp.int32, x.shape, 1) // w
    return jnp.concatenate([jnp.where(lane_blk == b, x, 0.0) for b in range(n_blk)], axis=0).astype(BF16)


def _widen(ec, col0, width):
    blk = lax.broadcasted_iota(jnp.int32, (ec.shape[0], H_B * width), 1) // width
    out = ec[:, col0 + H_B - 1:col0 + H_B]
    for h in range(H_B - 2, -1, -1):
        out = jnp.where(blk == h, ec[:, col0 + h:col0 + h + 1], out)
    return out


def _delta_prep(groups):
    c = CHUNK
    row = lax.broadcasted_iota(jnp.int32, (c, C4), 0)
    coll = lax.broadcasted_iota(jnp.int32, (c, C4), 1) % c
    eye = jnp.where(row == coll, 1.0, 0.0)
    st = []
    for q, k, v, ec, gr, backward in groups:
        gi = H_B if backward else 0
        bi = 3 * H_B if backward else 2 * H_B
        gc64, gc128 = _widen(ec, gi, c), _widen(ec, gi, D_K)
        beta64, beta128 = _widen(ec, bi, c), _widen(ec, bi, D_K)
        incl = (row <= coll) if backward else (row >= coll)
        strict = (row < coll) if backward else (row > coll)
        decay = jnp.where(incl, jnp.exp(jnp.where(incl, gc64 - gr, 0.0)), 0.0)
        kf = k.astype(F32)
        qf = q.astype(F32)
        eg = jnp.exp(gc128)
        g_last = gc128[0:1, :] if backward else gc128[c - 1:c, :]
        st.append(dict(
            kf=kf, qf=qf, decay=decay, strict=strict, beta64=beta64,
            qg=(qf * eg).astype(BF16),
            kg=(kf * jnp.exp(g_last - gc128)).astype(BF16),
            rhs_v=(v.astype(F32) * beta128).astype(BF16),
            rhs_k=(kf * (beta128 * eg)).astype(BF16),
            egl=jnp.exp(g_last)))
    lane2 = lax.broadcasted_iota(jnp.int32, (c, 2 * D_K), 1)
    for s in st:
        outs = []
        for p in range(H_B // 2):
            kp = s["kf"][:, p * 2 * D_K:(p + 1) * 2 * D_K]
            qp = s["qf"][:, p * 2 * D_K:(p + 1) * 2 * D_K]
            rmat = jnp.concatenate([jnp.where(lane2 < D_K, kp, 0.0), jnp.where(lane2 >= D_K, kp, 0.0)], axis=0)
            outs.append(_dot_nt(jnp.concatenate([kp, qp], axis=0).astype(BF16), rmat.astype(BF16)))
        kk = jnp.concatenate([o[:c] for o in outs], axis=1)
        qk = jnp.concatenate([o[c:] for o in outs], axis=1)
        s["a"] = jnp.where(s["strict"], kk * s["decay"], 0.0) * s["beta64"]
        s["intra"] = (qk * s["decay"]).astype(BF16)
        s["t"] = eye - jnp.where(row // 2 == coll // 2, s["a"], 0.0)
    b = 2
    while b < c:
        join = (row // (2 * b) == coll // (2 * b)) & (row // b != coll // b)
        for s in st:
            s["tb"] = s["t"].astype(BF16)
            s["tl"] = _dot(s["tb"], _blockdiag_rows(jnp.where(join, s["a"], 0.0), H_B))
        for s in st:
            s["t"] = s["t"] - _dot(s["tl"].astype(BF16), _blockdiag_rows(s["t"], H_B))
        b *= 2
    zero = jnp.zeros((c, 2 * D_K), BF16)
    for s in st:
        tb = s["t"].astype(BF16)
        us, ws = [], []
        for h in range(H_B):
            rhs = jnp.concatenate([s["rhs_v"][:, h * D_V:(h + 1) * D_V], s["rhs_k"][:, h * D_K:(h + 1) * D_K]], axis=1)
            uw = _dot(tb, jnp.concatenate([zero] * h + [rhs] + [zero] * (H_B - 1 - h), axis=0))
            us.append(uw[:, :D_V])
            ws.append(uw[:, D_V:])
        s["u"] = jnp.concatenate(us, axis=1)
        s["w"] = jnp.concatenate(ws, axis=1).astype(BF16)
    return st


def _delta_scan_step(steps, states):
    c = CHUNK
    zero = jnp.zeros((D_K, D_V), BF16)
    mids = []
    for s, S in zip(steps, states):
        w_s, q_s = [], []
        for p in range(H_B // 2):
            sl = slice(p * 2 * D_K, (p + 1) * 2 * D_K)
            lhs = jnp.concatenate([s["w"][:, sl], s["qg"][:, sl]], axis=0)
            sa, sb = S[2 * p].astype(BF16), S[2 * p + 1].astype(BF16)
            sbd = jnp.concatenate([jnp.concatenate([sa, zero], axis=1), jnp.concatenate([zero, sb], axis=1)], axis=0)
            out = _dot(lhs, sbd)
            w_s.append(out[:c])
            q_s.append(out[c:])
        v_new = s["u"] - jnp.concatenate(w_s, axis=1)
        mids.append((v_new, jnp.concatenate(q_s, axis=1)))
    outs, new_states = [], []
    for s, S, (v_new, q_s) in zip(steps, states, mids):
        vnb = v_new.astype(BF16)
        outs.append(q_s + _dot(s["intra"], _blockdiag_rows(v_new, H_B)))
        new_states.append([S[h] * s["egl"][:, h * D_V:(h + 1) * D_V]
                           + _dot_tn(s["kg"][:, h * D_K:(h + 1) * D_K], vnb[:, h * D_V:(h + 1) * D_V])
                           for h in range(H_B)])
    return outs, new_states


def _delta_kernel(qf_ref, kf_ref, vf_ref, ecf_ref, erf_ref, qb_ref, kb_ref, vb_ref, ecb_ref, erb_ref,
                  s0f_ref, s0b_ref, of_ref, ob_ref, sf_ref, sb_ref, s_scr):
    j = pl.program_id(1)
    c = CHUNK

    @pl.when(j == 0)
    def _():
        s_scr[0] = s0f_ref[0]
        s_scr[1] = s0b_ref[0]

    def group(refs, ci, backward):
        q_ref, k_ref, v_ref, ec_ref, er_ref = refs
        rows = slice(ci * c, (ci + 1) * c)
        d = 1 if backward else 0
        return (q_ref[0, rows, :], k_ref[0, rows, :], v_ref[0, rows, :], ec_ref[0, rows, :],
                er_ref[0, ci, d:d + 1, :], backward)

    fwd = (qf_ref, kf_ref, vf_ref, ecf_ref, erf_ref)
    bwd = (qb_ref, kb_ref, vb_ref, ecb_ref, erb_ref)
    prep = _delta_prep([group(fwd, ci, False) for ci in range(DELTA_CT)]
                       + [group(bwd, ci, True) for ci in range(DELTA_CT)])
    states = [[s_scr[d, h] for h in range(H_B)] for d in range(2)]
    for step in range(DELTA_CT):
        cf, cb = step, DELTA_CT - 1 - step
        outs, states = _delta_scan_step([prep[cf], prep[DELTA_CT + cb]], states)
        of_ref[0, cf * c:(cf + 1) * c, :] = outs[0].astype(of_ref.dtype)
        ob_ref[0, cb * c:(cb + 1) * c, :] = outs[1].astype(ob_ref.dtype)
    for d in range(2):
        for h in range(H_B):
            s_scr[d, h] = states[d][h]

    @pl.when(j == pl.num_programs(1) - 1)
    def _():
        sf_ref[0] = s_scr[0]
        sb_ref[0] = s_scr[1]


def _delta_call(q, k, v, ecol, s0f, s0b):
    B, L, _ = q.shape
    n = L // CHUNK
    tl = DELTA_CT * CHUNK
    nb = L // tl
    erow = ecol[..., :2 * H_B].reshape(B, n, CHUNK, 2, H_B).transpose(0, 1, 3, 4, 2).reshape(B, n, 2, C4)
    fmap = lambda b, j: (b, j, 0)
    bmap = lambda b, j: (b, nb - 1 - j, 0)
    seq = lambda m: pl.BlockSpec((1, tl, W_B), m)
    ecs = lambda m: pl.BlockSpec((1, tl, LANES), m)
    ers = lambda m: pl.BlockSpec((1, DELTA_CT, 2, C4), lambda b, j: m(b, j) + (0,))
    st = pl.BlockSpec((1, H_B, D_K, D_V), lambda b, j: (b, 0, 0, 0))
    return pl.pallas_call(
        _delta_kernel,
        grid=(B, nb),
        in_specs=[seq(fmap), seq(fmap), seq(fmap), ecs(fmap), ers(fmap),
                  seq(bmap), seq(bmap), seq(bmap), ecs(bmap), ers(bmap), st, st],
        out_specs=[seq(fmap), seq(bmap), st, st],
        out_shape=[jax.ShapeDtypeStruct((B, L, W_B), BF16)] * 2
        + [jax.ShapeDtypeStruct((B, H_B, D_K, D_V), F32)] * 2,
        scratch_shapes=[pltpu.VMEM((2, H_B, D_K, D_V), F32)],
        compiler_params=_cparams(("parallel", "arbitrary")),
        name="delta",
    )(q, k, v, ecol, erow, q, k, v, ecol, erow, s0f, s0b)


def _merge_kernel(x_ref, oa_ref, of_ref, ob_ref, z_ref, gate_ref, g1_ref, sh2_ref, sc2_ref,
                  onw_ref, wpa_ref, wpb_ref, wo_ref, n2w_ref, wr_ref, br_ref, ltri_ref,
                  x1_ref, h2_ref, route_ref, cnt_ref, cnt_scr):
    o = of_ref[...].astype(F32) + ob_ref[...].astype(F32)
    parts = []
    for h in range(H_B):
        t = o[:, h * D_V:(h + 1) * D_V]
        parts.append(t * lax.rsqrt(jnp.mean(t * t, axis=-1, keepdims=True) + EPS))
    ob = jnp.concatenate(parts, axis=-1) * onw_ref[...] * z_ref[...].astype(F32)
    pa = _dot(oa_ref[...], wpa_ref[...])
    pb = _dot(ob.astype(BF16), wpb_ref[...])
    gate = gate_ref[...]
    merged = gate[:, :D_MODEL].astype(F32) * pa + gate[:, D_MODEL:].astype(F32) * pb
    mix = _dot(merged.astype(BF16), wo_ref[...])
    x1 = x_ref[...] + g1_ref[0] * mix
    x1_ref[...] = x1
    ms = jnp.mean(x1 * x1, axis=-1, keepdims=True)
    h2 = x1 * lax.rsqrt(ms + EPS) * n2w_ref[...]
    h2 = h2 * (1.0 + sc2_ref[0]) + sh2_ref[0]
    hi = h2.astype(BF16)
    _store_token_major(h2_ref, h2)
    lo = (h2 - hi.astype(F32)).astype(BF16)
    logits = _dot(jnp.concatenate([hi, lo, hi], axis=-1), wr_ref[...]) + br_ref[...]
    lane = lax.broadcasted_iota(jnp.int32, logits.shape, 1)
    big = jnp.int32(LANES)

    def first_max(vals, mask):
        mv = jnp.where(mask, vals, NEG_INF)
        m = mv.max(axis=-1, keepdims=True)
        idx = jnp.where(mask & (mv == m), lane, big).min(axis=-1, keepdims=True)
        return m, idx

    gmask = lane < N_GROUPS
    gm, grp = first_max(logits, gmask)
    p_grp = 1.0 / jnp.where(gmask, jnp.exp(logits - gm), 0.0).sum(axis=-1, keepdims=True)
    lo_lane = N_GROUPS + grp * EXP_PER_GROUP
    emask = (lane >= lo_lane) & (lane < lo_lane + EXP_PER_GROUP)
    em, _ = first_max(logits, emask)
    ex = jnp.where(emask, jnp.exp(logits - em), 0.0)
    probs = ex / ex.sum(axis=-1, keepdims=True)
    p1, i1 = first_max(probs, emask)
    p2, i2 = first_max(probs, emask & (lane != i1))
    tot = p1 + p2
    w1 = p_grp * p1 / tot
    w2 = p_grp * p2 / tot
    e1 = i1 - N_GROUPS
    e2 = i2 - N_GROUPS
    @pl.when(pl.program_id(0) == 0)
    def _():
        cnt_scr[...] = jnp.zeros_like(cnt_scr)

    hit = jnp.where(lane == e1, 1.0, 0.0) + jnp.where(lane == e2, 1.0, 0.0)
    before = _dot(ltri_ref[...], hit.astype(BF16)) + cnt_scr[...]
    r1 = jnp.where(lane == e1, before, 0.0).sum(axis=-1, keepdims=True)
    r2 = jnp.where(lane == e2, before, 0.0).sum(axis=-1, keepdims=True)
    cnt_scr[...] = cnt_scr[...] + hit.sum(axis=0, keepdims=True)
    cnt_ref[...] = cnt_scr[...]
    cols = [e1.astype(F32), e2.astype(F32), w1, w2, r1, r2]
    route = jnp.zeros(logits.shape, F32)
    for i, col in enumerate(cols):
        route = jnp.where(lane == i, col, route)
    route_ref[...] = route


def _merge_call(x, oa, of, ob, z, gate, mod3, mod_stride, w):
    B, L, _ = x.shape
    n_tok = B * L
    tm = ROW_TILE
    assert n_tok % tm == 0 and (L % tm == 0 or tm % L == 0)
    flat = lambda a: a.reshape(n_tok, a.shape[-1])
    rspec = lambda n: pl.BlockSpec((tm, n), lambda i: (i, 0))
    mspec = lambda k: pl.BlockSpec((1, 1, D_MODEL), lambda i: ((i * tm) // L * mod_stride + k, 0, 0))
    t = np.arange(tm)
    ltri = jnp.asarray((t[:, None] > t[None, :]).astype(np.float32)).astype(BF16)
    consts = [w["onw"], w["wpa"], w["wpb"], w["wo"], w["norm2_w"], w["wr3"], w["br"], ltri]
    return pl.pallas_call(
        _merge_kernel,
        grid=(n_tok // tm,),
        in_specs=[rspec(D_MODEL), rspec(W_A), rspec(W_B), rspec(W_B), rspec(W_B), rspec(2 * D_MODEL),
                  mspec(2), mspec(3), mspec(4)] + [_const_spec(c.shape) for c in consts],
        out_specs=[rspec(D_MODEL), pl.BlockSpec((tm * SUB, LANES), lambda i: (i, 0)), rspec(LANES),
                   _const_spec((1, LANES))],
        out_shape=[jax.ShapeDtypeStruct((n_tok, D_MODEL), F32),
                   jax.ShapeDtypeStruct((n_tok * SUB, LANES), F32),
                   jax.ShapeDtypeStruct((n_tok, LANES), F32),
                   jax.ShapeDtypeStruct((1, LANES), F32)],
        scratch_shapes=[pltpu.VMEM((1, LANES), F32)],
        compiler_params=_cparams(("arbitrary",)),
        name="merge",
    )(flat(x), flat(oa), flat(of), flat(ob), flat(z), flat(gate), mod3, mod3, mod3, *consts)


SUB = 8
ROW_CHUNKS = D_MODEL // LANES


def _store_token_major(ref, x):
    for s in range(ROW_CHUNKS):
        ref[pl.ds(s, x.shape[0], stride=ROW_CHUNKS), :] = x[:, s * LANES:(s + 1) * LANES]


def _load_token_major(ref, slot, row0, n_tok):
    return jnp.concatenate([ref[slot, pl.ds(row0 * ROW_CHUNKS + s, n_tok, stride=ROW_CHUNKS), :]
                            for s in range(ROW_CHUNKS)], axis=1)


def _row_gather_start(idx_ref, n_rows, src_hbm, buf, sem, slot):
    for r in range(n_rows):
        src = pl.multiple_of(idx_ref[0, 0, r] * SUB, SUB)
        pltpu.make_async_copy(src_hbm.at[pl.ds(src, SUB), :], buf.at[slot, pl.ds(r * SUB, SUB), :],
                              sem.at[slot]).start(priority=r % 2)


def _row_gather_wait(n_rows, src_hbm, buf, sem, slot):
    for r in range(n_rows):
        pltpu.make_async_copy(src_hbm.at[pl.ds(0, SUB), :], buf.at[slot, pl.ds(r * SUB, SUB), :],
                              sem.at[slot]).wait()


def _gather_pipeline(i, n_steps, idx_ref, idx_next_ref, n_rows, src_hbm, buf, sem):
    slot = lax.rem(i, 2)

    @pl.when(i == 0)
    def _():
        _row_gather_start(idx_ref, n_rows, src_hbm, buf, sem, 0)

    @pl.when(i + 1 < n_steps)
    def _():
        _row_gather_start(idx_next_ref, n_rows, src_hbm, buf, sem, 1 - slot)

    @pl.when(i < n_steps)
    def _():
        _row_gather_wait(n_rows, src_hbm, buf, sem, slot)

    return slot


def _moe_kernel(te_ref, nt_ref, idx_ref, idxn_ref, h2_hbm, wgu_ref, wd_ref, y_ref, xbuf, sem):
    i = pl.program_id(0)
    n_used = nt_ref[0]
    slot = _gather_pipeline(i, n_used, idx_ref, idxn_ref, MOE_TILE, h2_hbm, xbuf, sem)

    @pl.when(i < n_used)
    def _():
        x = _load_token_major(xbuf, slot, 0, MOE_TILE).astype(BF16)
        gu = _dot(x, wgu_ref[0])
        g = gu[:, :D_FF_E]
        hid = g * jax.nn.sigmoid(g) * gu[:, D_FF_E:]
        _store_token_major(y_ref, _dot(hid.astype(BF16), wd_ref[0]))

    @pl.when(i >= n_used)
    def _():
        y_ref[...] = jnp.zeros_like(y_ref)


def _moe_call(h2, src_tok, tile_expert, n_tiles_used, wgu, wd):
    p_pad = src_tok.shape[0]
    n_tiles = p_pad // MOE_TILE
    idx = src_tok.reshape(n_tiles, 1, MOE_TILE)
    smem_tile = lambda m: pl.BlockSpec((1, 1, MOE_TILE), m, memory_space=pltpu.SMEM)
    grid_spec = pltpu.PrefetchScalarGridSpec(
        num_scalar_prefetch=2,
        grid=(n_tiles,),
        in_specs=[smem_tile(lambda i, te, nt: (i, 0, 0)),
                  smem_tile(lambda i, te, nt: (jnp.minimum(i + 1, n_tiles - 1), 0, 0)),
                  pl.BlockSpec(memory_space=pl.ANY),
                  pl.BlockSpec((1, D_MODEL, 2 * D_FF_E), lambda i, te, nt: (te[i], 0, 0)),
                  pl.BlockSpec((1, D_FF_E, D_MODEL), lambda i, te, nt: (te[i], 0, 0))],
        out_specs=pl.BlockSpec((MOE_TILE * SUB, LANES), lambda i, te, nt: (i, 0)),
        scratch_shapes=[pltpu.VMEM((2, MOE_TILE * SUB, LANES), F32), pltpu.SemaphoreType.DMA((2,))],
    )
    return pl.pallas_call(
        _moe_kernel,
        grid_spec=grid_spec,
        out_shape=jax.ShapeDtypeStruct((p_pad * SUB, LANES), F32),
        compiler_params=_cparams(("arbitrary",)),
        name="moe",
    )(tile_expert, n_tiles_used, idx, idx, h2, wgu, wd)


COMBINE_TILE = 256


def _combine_kernel(pos_ref, posn_ref, x1_ref, route_ref, g2_ref, y_hbm, o_ref, ybuf, sem):
    i = pl.program_id(0)
    t = COMBINE_TILE
    slot = _gather_pipeline(i, pl.num_programs(0), pos_ref, posn_ref, 2 * t, y_hbm, ybuf, sem)
    route = route_ref[...]
    w1 = route[:, 2:3]
    w2 = route[:, 3:4]
    ya = _load_token_major(ybuf, slot, 0, t)
    yb = _load_token_major(ybuf, slot, t, t)
    o_ref[...] = x1_ref[...] + g2_ref[0] * (w1 * ya + w2 * yb)


def _combine_call(x1, route, pos2, y, mod3, mod_stride, seq_len):
    n = x1.shape[0]
    t = COMBINE_TILE
    n_tiles = n // t
    pos_tiles = pos2.reshape(n_tiles, t, 2).transpose(0, 2, 1).reshape(n_tiles, 1, 2 * t)
    smem_tile = lambda m: pl.BlockSpec((1, 1, 2 * t), m, memory_space=pltpu.SMEM)
    return pl.pallas_call(
        _combine_kernel,
        grid=(n_tiles,),
        in_specs=[smem_tile(lambda i: (i, 0, 0)),
                  smem_tile(lambda i: (jnp.minimum(i + 1, n_tiles - 1), 0, 0)),
                  pl.BlockSpec((t, D_MODEL), lambda i: (i, 0)),
                  pl.BlockSpec((t, LANES), lambda i: (i, 0)),
                  pl.BlockSpec((1, 1, D_MODEL), lambda i: ((i * t) // seq_len * mod_stride + 5, 0, 0)),
                  pl.BlockSpec(memory_space=pl.ANY)],
        out_specs=pl.BlockSpec((t, D_MODEL), lambda i: (i, 0)),
        out_shape=jax.ShapeDtypeStruct((n, D_MODEL), F32),
        scratch_shapes=[pltpu.VMEM((2, 2 * t * SUB, LANES), F32), pltpu.SemaphoreType.DMA((2,))],
        compiler_params=_cparams(("arbitrary",)),
        name="combine",
    )(pos_tiles, pos_tiles, x1, route, mod3, y)


def _moe_forward(h2, route, counts, wgu, wd):
    n = route.shape[0]
    eid = route[:, :2].astype(jnp.int32).reshape(-1)
    rank = route[:, 4:6].astype(jnp.int32).reshape(-1)
    tok = jnp.arange(2 * n, dtype=jnp.int32) // 2
    counts = counts[0, :N_EXPERTS].astype(jnp.int32)
    padded = ((counts + MOE_TILE - 1) // MOE_TILE) * MOE_TILE
    seg_end = jnp.cumsum(padded)
    seg_start = seg_end - padded
    pos = seg_start[eid] + rank
    p_pad = 2 * n + N_EXPERTS * MOE_TILE
    src_tok = (jnp.arange(p_pad, dtype=jnp.int32) % n).at[pos].set(tok)
    n_tiles = p_pad // MOE_TILE
    tile_start = jnp.arange(n_tiles, dtype=jnp.int32) * MOE_TILE
    tile_expert = jnp.minimum(jnp.sum((tile_start[:, None] >= seg_end[None, :]).astype(jnp.int32), axis=1),
                              N_EXPERTS - 1)
    n_used = (seg_end[-1] // MOE_TILE).astype(jnp.int32).reshape(1)
    y = _moe_call(h2, src_tok, tile_expert, n_used, wgu, wd)
    return y, pos.reshape(n, 2)


def _prep_weights(norm1_w, norm2_w, w_in, conv_w, a_log, dt_bias, q_norm_w, k_norm_w, o_norm_w,
                  w_proj_a, w_proj_b, w_out, w_router_g, b_router_g, w_router_e, b_router_e,
                  w_exp_gate, w_exp_up, w_exp_down):
    w = {}
    w["norm1_w"] = norm1_w.reshape(1, D_MODEL)
    w["norm2_w"] = norm2_w.reshape(1, D_MODEL)
    o = 0
    w["wa"] = w_in[:, o:o + 3 * W_A].astype(BF16); o += 3 * W_A
    w["wb"] = w_in[:, o:o + 3 * W_B].astype(BF16); o += 3 * W_B
    w["wz"] = w_in[:, o:o + W_B].astype(BF16); o += W_B
    small = w_in[:, o:o + 4 * H_B]; o += 4 * H_B
    w["wg"] = w_in[:, o:o + 2 * D_MODEL].astype(BF16)
    def group_lanes(t):
        r = t.shape[0]
        t = t.reshape(r, -1, N_HG, HP).transpose(0, 2, 1, 3).reshape(r, N_HG, -1)
        return jnp.pad(t, ((0, 0), (0, 0), (0, LANES - t.shape[-1]))).reshape(r, N_HG * LANES)

    w["we"] = group_lanes(small).astype(BF16)
    w["alog_l"] = group_lanes(a_log.reshape(1, 2 * H_B))
    w["dtb_l"] = group_lanes(dt_bias.reshape(1, 2 * H_B))
    w["qnw"] = jnp.tile(q_norm_w, H_A).reshape(1, W_A)
    w["knw"] = jnp.tile(k_norm_w, H_A).reshape(1, W_A)
    w["onw"] = jnp.tile(o_norm_w, H_B).reshape(1, W_B)
    blk = np.arange(256) // D_HA
    w["bd256"] = jnp.asarray((blk[:, None] == blk[None, :]).astype(np.float32)).astype(BF16)
    w["conv_w"] = conv_w
    w["wpa"] = w_proj_a.astype(BF16)
    w["wpb"] = w_proj_b.astype(BF16)
    w["wo"] = w_out.astype(BF16)
    wr = jnp.concatenate([w_router_g, w_router_e], axis=1)
    wr = jnp.pad(wr, ((0, 0), (0, LANES - wr.shape[1])))
    wr_hi = wr.astype(BF16)
    wr_lo = (wr - wr_hi.astype(F32)).astype(BF16)
    w["wr3"] = jnp.concatenate([wr_hi, wr_hi, wr_lo], axis=0)
    br = jnp.concatenate([b_router_g, b_router_e])
    w["br"] = jnp.pad(br, (0, LANES - br.shape[0])).reshape(1, LANES)
    w["wgu"] = jnp.concatenate([w_exp_gate, w_exp_up], axis=-1).astype(BF16)
    w["wd"] = w_exp_down.astype(BF16)
    return w


def _layer(x, mod3, mod_stride, w, ctx):
    B, L, _ = x.shape
    latent = ctx is not None
    kv_dtype = BF16 if latent else F32
    qa, ka, va, qkvb, z, gate, ecol = _inproj_call(x, mod3, mod_stride, w, kv_dtype)
    if latent:
        k_ctx, v_ctx, s0f, s0b, bias = ctx
        oa = _nbr_attn_call(qa, ka, va, k_ctx, v_ctx, bias)
    else:
        s0f = s0b = jnp.zeros((B, H_B, D_K, D_V), F32)
        oa = _ctx_attn_call(qa, ka, va)
    qd, kd, vd = _conv_call(qkvb, w["conv_w"], rope=latent)
    of, ob, s_f, s_b = _delta_call(qd, kd, vd, ecol, s0f, s0b)
    x1, h2, route, counts = _merge_call(x, oa, of, ob, z, gate, mod3, mod_stride, w)
    ys, pos2 = _moe_forward(h2, route, counts, w["wgu"], w["wd"])
    y = _combine_call(x1, route, pos2, ys, mod3, mod_stride, L)
    return y.reshape(B, L, D_MODEL), ka, va, s_f, s_b


def kernel(x_prompt, x_sample, c, cache_attn_k, cache_attn_v, state_delta_fwd, state_delta_bwd, c_ctx,
           norm1_w, norm2_w, w_mod, b_mod, w_in, conv_w, a_log, dt_bias, q_norm_w, k_norm_w, rpb,
           o_norm_w, w_proj_a, w_proj_b, w_out, w_router_g, b_router_g, w_router_e, b_router_e,
           w_exp_gate, w_exp_up, w_exp_down):
    assert norm1_w.shape[0] == 1, "single-layer trunk"
    Bp, Lp, _ = x_prompt.shape
    Bs, Ls, _ = x_sample.shape
    l = 0
    w = _prep_weights(norm1_w[l], norm2_w[l], w_in[l], conv_w[l], a_log[l], dt_bias[l], q_norm_w[l],
                      k_norm_w[l], o_norm_w[l], w_proj_a[l], w_proj_b[l], w_out[l], w_router_g[l],
                      b_router_g[l], w_router_e[l], b_router_e[l], w_exp_gate[l], w_exp_up[l],
                      w_exp_down[l])
    cond = jnp.concatenate([c_ctx[None, :], c, jnp.zeros((8 - 1 - Bs, D_MODEL), F32)], axis=0)
    mod = _mod_call(cond, w_mod[l], b_mod[l])
    mod3 = mod.reshape(8 * 6, 1, D_MODEL)
    bias = _bias_table_call(rpb[l])
    y_p, k_a, v_a, s_f, s_b = _layer(x_prompt, mod3[:6], 0, w, None)
    lc = cache_attn_k.shape[2]
    ctx = (cache_attn_k[:, l].reshape(Bs, lc, W_A).astype(BF16),
           cache_attn_v[:, l].reshape(Bs, lc, W_A).astype(BF16),
           state_delta_fwd[:, l], state_delta_bwd[:, l], bias)
    y_s, _, _, _, _ = _layer(x_sample, mod3[6:6 + 6 * Bs], 6, w, ctx)
    new_k = k_a.reshape(Bp, 1, Lp, H_A, D_HA)
    new_v = v_a.reshape(Bp, 1, Lp, H_A, D_HA)
    return (y_p, y_s, new_k, new_v, s_f[:, None], s_b[:, None])
```

```python
import functools
import math

import numpy as np
import jax
import jax.numpy as jnp
from jax import lax
from jax.experimental import pallas as pl
from jax.experimental.pallas import tpu as pltpu

F32 = jnp.float32
BF16 = jnp.bfloat16

D_MODEL = 1024
GRID_W = 64
H_A = 8
D_HA = 64
W_A = H_A * D_HA
WIN_H = 8
WIN_W = 16
H_B = 4
D_K = 128
D_V = 128
W_B = H_B * D_V
CONV_K = 5
CHUNK = 64
ROPE_BASE = 10000.0
N_GROUPS = 4
EXP_PER_GROUP = 8
N_EXPERTS = N_GROUPS * EXP_PER_GROUP
D_FF_E = 256
EPS = 1e-6
NEG_INF = -1e30

LANES = 128
HP = H_B
N_HG = H_B // HP
ROW_TILE = 512
INPROJ_TILE = 256
MOE_TILE = 256
VMEM_LIMIT = 56 * 1024 * 1024

HIGHEST = lax.Precision.HIGHEST


def _cparams(sem):
    return pltpu.CompilerParams(dimension_semantics=sem, vmem_limit_bytes=VMEM_LIMIT)


def _dot(a, b):
    return jnp.dot(a, b, preferred_element_type=F32)


def _dot_nt(a, b):
    return lax.dot_general(a, b, (((1,), (1,)), ((), ())), preferred_element_type=F32)


def _dot_tn(a, b):
    return lax.dot_general(a, b, (((0,), (0,)), ((), ())), preferred_element_type=F32)


def _const_spec(shape):
    nd = len(shape)
    return pl.BlockSpec(shape, lambda *_: (0,) * nd)


def _mod_kernel(c_ref, w_ref, b_ref, o_ref):
    c = c_ref[...]
    s = c * jax.nn.sigmoid(c)
    o_ref[...] = jnp.dot(s, w_ref[...], preferred_element_type=F32, precision=HIGHEST) + b_ref[...]


def _mod_call(cond8, w_mod, b_mod):
    n_out = w_mod.shape[1]
    tn = 1024
    return pl.pallas_call(
        _mod_kernel,
        grid=(n_out // tn,),
        in_specs=[pl.BlockSpec((8, D_MODEL), lambda j: (0, 0)),
                  pl.BlockSpec((D_MODEL, tn), lambda j: (0, j)),
                  pl.BlockSpec((1, tn), lambda j: (0, j))],
        out_specs=pl.BlockSpec((8, tn), lambda j: (0, j)),
        out_shape=jax.ShapeDtypeStruct((8, n_out), F32),
        compiler_params=_cparams(("arbitrary",)),
        name="mod",
    )(cond8, w_mod, b_mod.reshape(1, n_out))


def _softplus(x):
    return jnp.maximum(x, 0.0) + jnp.log1p(jnp.exp(-jnp.abs(x)))


def _inproj_kernel(x_ref, sh_ref, sc_ref, nw_ref, wa_ref, wb_ref, wz_ref, wg_ref, we_ref,
                   qnw_ref, knw_ref, bd_ref, alog_ref, dtb_ref, tri_ref,
                   qa_ref, ka_ref, va_ref, qkvb_ref, z_ref, gate_ref, e_ref, *kv5_refs):
    x = x_ref[...]
    ms = jnp.mean(x * x, axis=-1, keepdims=True)
    h = x * lax.rsqrt(ms + EPS) * nw_ref[...]
    h = h * (1.0 + sc_ref[0]) + sh_ref[0]
    hb = h.astype(BF16)

    a = _dot(hb, wa_ref[...])
    bd = bd_ref[...]

    def head_rms(t, w):
        sq = (t * t).astype(BF16)
        ss = jnp.concatenate([_dot(sq[:, i * 256:(i + 1) * 256], bd) for i in range(W_A // 256)], axis=-1)
        return t * lax.rsqrt(ss * (1.0 / D_HA) + EPS) * w

    qa = head_rms(a[:, :W_A], qnw_ref[...]) * (D_HA ** -0.5)
    qa_ref[...] = qa.astype(qa_ref.dtype)
    ka = head_rms(a[:, W_A:2 * W_A], knw_ref[...])
    va = a[:, 2 * W_A:]
    ka_ref[...] = ka.astype(ka_ref.dtype)
    va_ref[...] = va.astype(va_ref.dtype)
    if kv5_refs:
        for ref, t in zip(kv5_refs, (ka, va)):
            for hd in range(H_A):
                ref[0, 0, :, hd, :] = t[:, hd * D_HA:(hd + 1) * D_HA]

    qkvb_ref[...] = _dot(hb, wb_ref[...]).astype(qkvb_ref.dtype)
    z = _dot(hb, wz_ref[...])
    z_ref[...] = (z * jax.nn.sigmoid(z)).astype(z_ref.dtype)
    gate_ref[...] = jax.nn.sigmoid(_dot(hb, wg_ref[...])).astype(gate_ref.dtype)

    e = _dot(hb, we_ref[...])
    lane = lax.broadcasted_iota(jnp.int32, e.shape, 1) % LANES
    g = -jnp.exp(alog_ref[...]) * _softplus(e + dtb_ref[...])
    act = jnp.where(lane < 2 * HP, g, jax.nn.sigmoid(e))
    a_hi = act.astype(BF16)
    rem = act - a_hi.astype(F32)
    a_mid = rem.astype(BF16)
    a_lo = (rem - a_mid.astype(F32)).astype(BF16)
    tri = tri_ref[...]
    cum = _dot(tri, a_hi) + _dot(tri, a_mid) + _dot(tri, a_lo)
    tm = act.shape[0]
    e_ref[...] = jnp.where(lane < HP, cum[:tm], jnp.where(lane < 2 * HP, cum[tm:], act))


def _inproj_call(x, mod3, mod_stride, w, new_cache):
    B, L, _ = x.shape
    n_tok = B * L
    tm = INPROJ_TILE
    assert n_tok % tm == 0 and L % tm == 0
    row = lambda n, dt: jax.ShapeDtypeStruct((n_tok, n), dt)
    rspec = lambda n: pl.BlockSpec((tm, n), lambda i: (i, 0))
    mspec = lambda k: pl.BlockSpec((1, 1, D_MODEL), lambda i: ((i * tm) // L * mod_stride + k, 0, 0))
    t = np.arange(tm)
    same = (t[:, None] // CHUNK) == (t[None, :] // CHUNK)
    tri = np.concatenate([same & (t[:, None] >= t[None, :]), same & (t[:, None] <= t[None, :])], axis=0)
    consts = [w["norm1_w"], w["wa"], w["wb"], w["wz"], w["wg"], w["we"], w["qnw"], w["knw"],
              w["bd256"], w["alog_l"], w["dtb_l"], jnp.asarray(tri.astype(np.float32)).astype(BF16)]
    out_specs = [rspec(W_A), rspec(W_A), rspec(W_A), rspec(3 * W_B), rspec(W_B),
                 rspec(2 * D_MODEL), rspec(N_HG * LANES)]
    out_shape = [row(W_A, BF16), row(W_A, BF16), row(W_A, BF16), row(3 * W_B, BF16),
                 row(W_B, BF16), row(2 * D_MODEL, BF16), row(N_HG * LANES, F32)]
    if new_cache:
        per_seq = L // tm
        out_specs += [pl.BlockSpec((1, 1, tm, H_A, D_HA), lambda i: (i // per_seq, 0, i % per_seq, 0, 0))] * 2
        out_shape += [jax.ShapeDtypeStruct((B, 1, L, H_A, D_HA), F32)] * 2
    outs = pl.pallas_call(
        _inproj_kernel,
        grid=(n_tok // tm,),
        in_specs=[rspec(D_MODEL), mspec(0), mspec(1)] + [_const_spec(c.shape) for c in consts],
        out_specs=out_specs,
        out_shape=out_shape,
        compiler_params=_cparams(("parallel",)),
        name="inproj",
    )(x.reshape(n_tok, D_MODEL), mod3, mod3, *consts)
    return [o.reshape(B, L, o.shape[-1]) for o in outs[:7]] + list(outs[7:])


def _pair_softmax_pv(s_parts, v_parts):
    m = s_parts[0].max(axis=-1, keepdims=True)
    for s in s_parts[1:]:
        m = jnp.maximum(m, s.max(axis=-1, keepdims=True))
    den = 0.0
    acc = 0.0
    for s, v in zip(s_parts, v_parts):
        p = jnp.exp(s - m)
        den = den + p.sum(axis=-1, keepdims=True)
        acc = acc + _dot(p.astype(BF16), v)
    return acc / den


def _stack_pair(qp):
    lane = lax.broadcasted_iota(jnp.int32, qp.shape, 1)
    zero = jnp.zeros_like(qp)
    return jnp.concatenate([jnp.where(lane < D_HA, qp, zero), jnp.where(lane >= D_HA, qp, zero)], axis=0)


def _unstack_pair(o2):
    t = o2.shape[0] // 2
    lane = lax.broadcasted_iota(jnp.int32, (t, LANES), 1)
    return jnp.where(lane < D_HA, o2[:t], o2[t:])


def _ctx_attn_kernel(q_ref, k_ref, v_ref, o_ref):
    for p in range(H_A // 2):
        sl = slice(p * LANES, (p + 1) * LANES)
        qs = _stack_pair(q_ref[0, :, sl])
        kp = k_ref[0, :, sl].astype(BF16)
        vp = v_ref[0, :, sl].astype(BF16)
        o2 = _pair_softmax_pv([_dot_nt(qs, kp)], [vp])
        o_ref[0, :, sl] = _unstack_pair(o2).astype(o_ref.dtype)


def _ctx_attn_call(q, k, v):
    B, L, _ = q.shape
    spec = pl.BlockSpec((1, L, W_A), lambda b: (b, 0, 0))
    return pl.pallas_call(
        _ctx_attn_kernel,
        grid=(B,),
        in_specs=[spec, spec, spec],
        out_specs=spec,
        out_shape=jax.ShapeDtypeStruct((B, L, W_A), BF16),
        compiler_params=_cparams(("parallel",)),
        name="ctx_attn",
    )(q, k, v)


N_DR = 2 * WIN_H - 1
N_DC = 2 * WIN_W - 1
NBR_ROWS = 4


def _bias_table_kernel(rpb_ref, o_ref):
    h = pl.program_id(0)
    c = lax.broadcasted_iota(jnp.int32, (GRID_W, 2 * GRID_W), 0)
    lane = lax.broadcasted_iota(jnp.int32, (GRID_W, 2 * GRID_W), 1)
    kc = lane % GRID_W
    ws = jnp.clip(c - WIN_W // 2, 0, GRID_W - WIN_W)
    in_win = (kc >= ws) & (kc < ws + WIN_W)
    dc = kc - c + (WIN_W - 1)
    for dr in range(N_DR - 1):
        base = (h * N_DR + dr) * N_DC
        val = jnp.zeros((GRID_W, 2 * GRID_W), F32)
        for d in range(N_DC):
            val = jnp.where(dc == d, jnp.where(lane < GRID_W, rpb_ref[base + d], rpb_ref[base + N_DC + d]), val)
        o_ref[0, dr] = jnp.where(in_win, val, NEG_INF)


def _bias_table_call(rpb):
    return pl.pallas_call(
        _bias_table_kernel,
        grid=(H_A,),
        in_specs=[pl.BlockSpec(memory_space=pltpu.SMEM)],
        out_specs=pl.BlockSpec((1, N_DR - 1, GRID_W, 2 * GRID_W), lambda h: (h, 0, 0, 0)),
        out_shape=jax.ShapeDtypeStruct((H_A, N_DR - 1, GRID_W, 2 * GRID_W), F32),
        compiler_params=_cparams(("arbitrary",)),
        name="bias_table",
    )(rpb.reshape(-1))


def _nbr_attn_kernel(q_ref, k_ref, v_ref, kc_ref, vc_ref, bias_ref, o_ref, *, rows):
    inst = []
    for rr in range(NBR_ROWS):
        r = pl.program_id(1) * NBR_ROWS + rr
        first = jnp.clip(r - WIN_H // 2, 0, rows - WIN_H)
        start = pl.multiple_of(first * GRID_W, GRID_W)
        shift = r - first
        for p in range(H_A // 2):
            inst.append((rr, p, start, shift))
    scores = []
    for rr, p, start, shift in inst:
        sl = slice(p * LANES, (p + 1) * LANES)
        qs = _stack_pair(q_ref[0, rr * GRID_W:(rr + 1) * GRID_W, sl])
        bias = jnp.concatenate(
            [jnp.concatenate([bias_ref[2 * p + hh, 2 * u - shift + WIN_H - 1] for u in range(WIN_H // 2)], axis=1)
             for hh in range(2)], axis=0)
        s_loc = _dot_nt(qs, k_ref[0, pl.ds(start, WIN_H * GRID_W), sl]) + bias
        s_ctx = _dot_nt(qs, kc_ref[0, :, sl])
        scores.append((s_loc, s_ctx))
    for (rr, p, start, shift), (s_loc, s_ctx) in zip(inst, scores):
        sl = slice(p * LANES, (p + 1) * LANES)
        o2 = _pair_softmax_pv([s_loc, s_ctx], [v_ref[0, pl.ds(start, WIN_H * GRID_W), sl], vc_ref[0, :, sl]])
        o_ref[0, rr * GRID_W:(rr + 1) * GRID_W, sl] = _unstack_pair(o2).astype(o_ref.dtype)


def _nbr_attn_call(q, k, v, k_ctx, v_ctx, bias):
    B, L, _ = q.shape
    rows = L // GRID_W
    lc = k_ctx.shape[1]
    full = pl.BlockSpec((1, L, W_A), lambda b, r: (b, 0, 0))
    ctx = pl.BlockSpec((1, lc, W_A), lambda b, r: (b, 0, 0))
    rowspec = pl.BlockSpec((1, NBR_ROWS * GRID_W, W_A), lambda b, r: (b, r, 0))
    return pl.pallas_call(
        functools.partial(_nbr_attn_kernel, rows=rows),
        grid=(B, rows // NBR_ROWS),
        in_specs=[rowspec, full, full, ctx, ctx, _const_spec(bias.shape)],
        out_specs=rowspec,
        out_shape=jax.ShapeDtypeStruct((B, L, W_A), BF16),
        compiler_params=_cparams(("parallel", "arbitrary")),
        name="nbr_attn",
    )(q, k, v, k_ctx, v_ctx, bias)


HALO = 16


def _conv_kernel(*refs, rope):
    if rope:
        x_ref, prev_ref, next_ref, w_ref, cos_ref, sin_ref, q_ref, k_ref, v_ref, xe_ref = refs
    else:
        x_ref, prev_ref, next_ref, w_ref, q_ref, k_ref, v_ref, xe_ref = refs
    i = pl.program_id(1)
    n_i = pl.num_programs(1)
    tl = x_ref.shape[1]
    xe_ref[0:HALO] = jnp.where(i > 0, prev_ref[0].astype(F32), 0.0)
    xe_ref[HALO:HALO + tl] = x_ref[0].astype(F32)
    xe_ref[HALO + tl:] = jnp.where(i < n_i - 1, next_ref[0].astype(F32), 0.0)
    half = CONV_K // 2
    y = w_ref[0:1, :] * xe_ref[HALO - half:HALO - half + tl]
    for j in range(1, CONV_K):
        y = y + w_ref[j:j + 1, :] * xe_ref[HALO - half + j:HALO - half + j + tl]
    y = y * jax.nn.sigmoid(y)

    def l2n(t):
        return t * lax.rsqrt(jnp.sum(t * t, axis=-1, keepdims=True) + EPS)

    def rot(t):
        if not rope:
            return t
        lane = lax.broadcasted_iota(jnp.int32, t.shape, 1)
        swapped = jnp.where(lane % 2 == 0, pltpu.roll(t, LANES - 1, 1), pltpu.roll(t, 1, 1))
        return t * cos_ref[...] + swapped * sin_ref[...]

    for h in range(H_B):
        sl = slice(h * D_K, (h + 1) * D_K)
        q_ref[0, :, sl] = (rot(l2n(y[:, sl])) * (D_K ** -0.5)).astype(q_ref.dtype)
        k_ref[0, :, sl] = rot(l2n(y[:, W_B + h * D_K:W_B + (h + 1) * D_K])).astype(k_ref.dtype)
    v_ref[0] = y[:, 2 * W_B:].astype(v_ref.dtype)


def _rope_tables(L):
    pos = jnp.arange(L)
    rows = (pos // GRID_W).astype(F32)
    cols = (pos % GRID_W).astype(F32)
    n_pairs = D_K // 4
    inv = ROPE_BASE ** (-jnp.arange(n_pairs, dtype=F32) / n_pairs)
    ang = jnp.concatenate([rows[:, None] * inv, cols[:, None] * inv], axis=-1)
    cos = jnp.repeat(jnp.cos(ang), 2, axis=-1)
    sin = jnp.repeat(jnp.sin(ang), 2, axis=-1)
    sign = jnp.asarray(np.tile(np.array([-1.0, 1.0], np.float32), D_K // 2))
    return cos, sin * sign


def _conv_call(x, conv_w, rope):
    B, L, C = x.shape
    tl = min(ROW_TILE, L)
    nh = tl // HALO
    n_halo = L // HALO
    cur = pl.BlockSpec((1, tl, C), lambda b, i: (b, i, 0))
    prev = pl.BlockSpec((1, HALO, C), lambda b, i: (b, jnp.maximum(i * nh - 1, 0), 0))
    nxt = pl.BlockSpec((1, HALO, C), lambda b, i: (b, jnp.minimum((i + 1) * nh, n_halo - 1), 0))
    ins = [x, x, x, conv_w]
    specs = [cur, prev, nxt, _const_spec(conv_w.shape)]
    if rope:
        cos, sin = _rope_tables(L)
        ins += [cos, sin]
        specs += [pl.BlockSpec((tl, D_K), lambda b, i: (i, 0))] * 2
    out = pl.BlockSpec((1, tl, W_B), lambda b, i: (b, i, 0))
    return pl.pallas_call(
        functools.partial(_conv_kernel, rope=rope),
        grid=(B, L // tl),
        in_specs=specs,
        out_specs=[out, out, out],
        out_shape=[jax.ShapeDtypeStruct((B, L, W_B), BF16)] * 3,
        scratch_shapes=[pltpu.VMEM((tl + 2 * HALO, C), F32)],
        compiler_params=_cparams(("parallel", "parallel")),
        name="conv",
    )(*ins)


DELTA_CT = 4
C4 = H_B * CHUNK


def _blockdiag_rows(x, n_blk):
    w = x.shape[1] // n_blk
    lane_blk = lax.broadcasted_iota(jnp.int32, x.shape, 1) // w
    return jnp.concatenate([jnp.where(lane_blk == b, x, 0.0) for b in range(n_blk)], axis=0).astype(BF16)


def _widen(ec, col0, width):
    blk = lax.broadcasted_iota(jnp.int32, (ec.shape[0], H_B * width), 1) // width
    out = ec[:, col0 + H_B - 1:col0 + H_B]
    for h in range(H_B - 2, -1, -1):
        out = jnp.where(blk == h, ec[:, col0 + h:col0 + h + 1], out)
    return out


def _delta_prep(groups):
    c = CHUNK
    row = lax.broadcasted_iota(jnp.int32, (c, C4), 0)
    coll = lax.broadcasted_iota(jnp.int32, (c, C4), 1) % c
    eye = jnp.where(row == coll, 1.0, 0.0)
    st = []
    for q, k, v, ec, gr, backward in groups:
        gi = H_B if backward else 0
        bi = 3 * H_B if backward else 2 * H_B
        gc64, gc128 = _widen(ec, gi, c), _widen(ec, gi, D_K)
        beta64, beta128 = _widen(ec, bi, c), _widen(ec, bi, D_K)
        incl = (row <= coll) if backward else (row >= coll)
        strict = (row < coll) if backward else (row > coll)
        decay = jnp.where(incl, jnp.exp(jnp.where(incl, gc64 - gr, 0.0)), 0.0)
        kf = k.astype(F32)
        qf = q.astype(F32)
        eg = jnp.exp(gc128)
        g_last = gc128[0:1, :] if backward else gc128[c - 1:c, :]
        st.append(dict(
            kf=kf, qf=qf, decay=decay, strict=strict, beta64=beta64,
            qg=(qf * eg).astype(BF16),
            kg=(kf * jnp.exp(g_last - gc128)).astype(BF16),
            rhs_v=(v.astype(F32) * beta128).astype(BF16),
            rhs_k=(kf * (beta128 * eg)).astype(BF16),
            egl=jnp.exp(g_last)))
    lane2 = lax.broadcasted_iota(jnp.int32, (c, 2 * D_K), 1)
    for s in st:
        outs = []
        for p in range(H_B // 2):
            kp = s["kf"][:, p * 2 * D_K:(p + 1) * 2 * D_K]
            qp = s["qf"][:, p * 2 * D_K:(p + 1) * 2 * D_K]
            rmat = jnp.concatenate([jnp.where(lane2 < D_K, kp, 0.0), jnp.where(lane2 >= D_K, kp, 0.0)], axis=0)
            outs.append(_dot_nt(jnp.concatenate([kp, qp], axis=0).astype(BF16), rmat.astype(BF16)))
        kk = jnp.concatenate([o[:c] for o in outs], axis=1)
        qk = jnp.concatenate([o[c:] for o in outs], axis=1)
        s["a"] = jnp.where(s["strict"], kk * s["decay"], 0.0) * s["beta64"]
        s["intra"] = (qk * s["decay"]).astype(BF16)
        s["t"] = eye - jnp.where(row // 2 == coll // 2, s["a"], 0.0)
    b = 2
    while b < c:
        join = (row // (2 * b) == coll // (2 * b)) & (row // b != coll // b)
        for s in st:
            s["tb"] = s["t"].astype(BF16)
            s["tl"] = _dot(s["tb"], _blockdiag_rows(jnp.where(join, s["a"], 0.0), H_B))
        for s in st:
            s["t"] = s["t"] - _dot(s["tl"].astype(BF16), _blockdiag_rows(s["t"], H_B))
        b *= 2
    zero = jnp.zeros((c, 2 * D_K), BF16)
    for s in st:
        tb = s["t"].astype(BF16)
        us, ws = [], []
        for h in range(H_B):
            rhs = jnp.concatenate([s["rhs_v"][:, h * D_V:(h + 1) * D_V], s["rhs_k"][:, h * D_K:(h + 1) * D_K]], axis=1)
            uw = _dot(tb, jnp.concatenate([zero] * h + [rhs] + [zero] * (H_B - 1 - h), axis=0))
            us.append(uw[:, :D_V])
            ws.append(uw[:, D_V:])
        s["u"] = jnp.concatenate(us, axis=1)
        s["w"] = jnp.concatenate(ws, axis=1).astype(BF16)
    return st


def _delta_scan_step(steps, states):
    c = CHUNK
    zero = jnp.zeros((D_K, D_V), BF16)
    mids = []
    for s, S in zip(steps, states):
        w_s, q_s = [], []
        for p in range(H_B // 2):
            sl = slice(p * 2 * D_K, (p + 1) * 2 * D_K)
            lhs = jnp.concatenate([s["w"][:, sl], s["qg"][:, sl]], axis=0)
            sa, sb = S[2 * p].astype(BF16), S[2 * p + 1].astype(BF16)
            sbd = jnp.concatenate([jnp.concatenate([sa, zero], axis=1), jnp.concatenate([zero, sb], axis=1)], axis=0)
            out = _dot(lhs, sbd)
            w_s.append(out[:c])
            q_s.append(out[c:])
        v_new = s["u"] - jnp.concatenate(w_s, axis=1)
        mids.append((v_new, jnp.concatenate(q_s, axis=1)))
    outs, new_states = [], []
    for s, S, (v_new, q_s) in zip(steps, states, mids):
        vnb = v_new.astype(BF16)
        outs.append(q_s + _dot(s["intra"], _blockdiag_rows(v_new, H_B)))
        new_states.append([S[h] * s["egl"][:, h * D_V:(h + 1) * D_V]
                           + _dot_tn(s["kg"][:, h * D_K:(h + 1) * D_K], vnb[:, h * D_V:(h + 1) * D_V])
                           for h in range(H_B)])
    return outs, new_states


def _delta_kernel(qf_ref, kf_ref, vf_ref, ecf_ref, erf_ref, qb_ref, kb_ref, vb_ref, ecb_ref, erb_ref,
                  s0f_ref, s0b_ref, of_ref, ob_ref, sf_ref, sb_ref, s_scr):
    j = pl.program_id(1)
    c = CHUNK

    @pl.when(j == 0)
    def _():
        s_scr[0] = s0f_ref[0]
        s_scr[1] = s0b_ref[0]

    def group(refs, ci, backward):
        q_ref, k_ref, v_ref, ec_ref, er_ref = refs
        rows = slice(ci * c, (ci + 1) * c)
        d = 1 if backward else 0
        return (q_ref[0, rows, :], k_ref[0, rows, :], v_ref[0, rows, :], ec_ref[0, rows, :],
                er_ref[0, ci, d:d + 1, :], backward)

    fwd = (qf_ref, kf_ref, vf_ref, ecf_ref, erf_ref)
    bwd = (qb_ref, kb_ref, vb_ref, ecb_ref, erb_ref)
    prep = _delta_prep([group(fwd, ci, False) for ci in range(DELTA_CT)]
                       + [group(bwd, ci, True) for ci in range(DELTA_CT)])
    states = [[s_scr[d, h] for h in range(H_B)] for d in range(2)]
    for step in range(DELTA_CT):
        cf, cb = step, DELTA_CT - 1 - step
        outs, states = _delta_scan_step([prep[cf], prep[DELTA_CT + cb]], states)
        of_ref[0, cf * c:(cf + 1) * c, :] = outs[0].astype(of_ref.dtype)
        ob_ref[0, cb * c:(cb + 1) * c, :] = outs[1].astype(ob_ref.dtype)
    for d in range(2):
        for h in range(H_B):
            s_scr[d, h] = states[d][h]

    @pl.when(j == pl.num_programs(1) - 1)
    def _():
        sf_ref[0] = s_scr[0]
        sb_ref[0] = s_scr[1]


def _delta_call(q, k, v, ecol, s0f, s0b):
    B, L, _ = q.shape
    n = L // CHUNK
    tl = DELTA_CT * CHUNK
    nb = L // tl
    erow = ecol[..., :2 * H_B].reshape(B, n, CHUNK, 2, H_B).transpose(0, 1, 3, 4, 2).reshape(B, n, 2, C4)
    fmap = lambda b, j: (b, j, 0)
    bmap = lambda b, j: (b, nb - 1 - j, 0)
    seq = lambda m: pl.BlockSpec((1, tl, W_B), m)
    ecs = lambda m: pl.BlockSpec((1, tl, LANES), m)
    ers = lambda m: pl.BlockSpec((1, DELTA_CT, 2, C4), lambda b, j: m(b, j) + (0,))
    st = pl.BlockSpec((1, H_B, D_K, D_V), lambda b, j: (b, 0, 0, 0))
    return pl.pallas_call(
        _delta_kernel,
        grid=(B, nb),
        in_specs=[seq(fmap), seq(fmap), seq(fmap), ecs(fmap), ers(fmap),
                  seq(bmap), seq(bmap), seq(bmap), ecs(bmap), ers(bmap), st, st],
        out_specs=[seq(fmap), seq(bmap), st, st],
        out_shape=[jax.ShapeDtypeStruct((B, L, W_B), BF16)] * 2
        + [jax.ShapeDtypeStruct((B, H_B, D_K, D_V), F32)] * 2,
        scratch_shapes=[pltpu.VMEM((2, H_B, D_K, D_V), F32)],
        compiler_params=_cparams(("parallel", "arbitrary")),
        name="delta",
    )(q, k, v, ecol, erow, q, k, v, ecol, erow, s0f, s0b)


def _merge_kernel(x_ref, oa_ref, of_ref, ob_ref, z_ref, gate_ref, g1_ref, sh2_ref, sc2_ref,
                  onw_ref, wpa_ref, wpb_ref, wo_ref, n2w_ref, wr_ref, br_ref, ltri_ref,
                  x1_ref, h2_ref, route_ref, cnt_ref, cnt_scr):
    o = of_ref[...].astype(F32) + ob_ref[...].astype(F32)
    parts = []
    for h in range(H_B):
        t = o[:, h * D_V:(h + 1) * D_V]
        parts.append(t * lax.rsqrt(jnp.mean(t * t, axis=-1, keepdims=True) + EPS))
    ob = jnp.concatenate(parts, axis=-1) * onw_ref[...] * z_ref[...].astype(F32)
    pa = _dot(oa_ref[...], wpa_ref[...])
    pb = _dot(ob.astype(BF16), wpb_ref[...])
    gate = gate_ref[...]
    merged = gate[:, :D_MODEL].astype(F32) * pa + gate[:, D_MODEL:].astype(F32) * pb
    mix = _dot(merged.astype(BF16), wo_ref[...])
    x1 = x_ref[...] + g1_ref[0] * mix
    x1_ref[...] = x1
    ms = jnp.mean(x1 * x1, axis=-1, keepdims=True)
    h2 = x1 * lax.rsqrt(ms + EPS) * n2w_ref[...]
    h2 = h2 * (1.0 + sc2_ref[0]) + sh2_ref[0]
    hi = h2.astype(BF16)
    _store_token_major(h2_ref, h2)
    lo = (h2 - hi.astype(F32)).astype(BF16)
    logits = _dot(jnp.concatenate([hi, lo, hi], axis=-1), wr_ref[...]) + br_ref[...]
    lane = lax.broadcasted_iota(jnp.int32, logits.shape, 1)
    big = jnp.int32(LANES)

    def first_max(vals, mask):
        mv = jnp.where(mask, vals, NEG_INF)
        m = mv.max(axis=-1, keepdims=True)
        idx = jnp.where(mask & (mv == m), lane, big).min(axis=-1, keepdims=True)
        return m, idx

    gmask = lane < N_GROUPS
    gm, grp = first_max(logits, gmask)
    p_grp = 1.0 / jnp.where(gmask, jnp.exp(logits - gm), 0.0).sum(axis=-1, keepdims=True)
    lo_lane = N_GROUPS + grp * EXP_PER_GROUP
    emask = (lane >= lo_lane) & (lane < lo_lane + EXP_PER_GROUP)
    em, _ = first_max(logits, emask)
    ex = jnp.where(emask, jnp.exp(logits - em), 0.0)
    probs = ex / ex.sum(axis=-1, keepdims=True)
    p1, i1 = first_max(probs, emask)
    p2, i2 = first_max(probs, emask & (lane != i1))
    tot = p1 + p2
    w1 = p_grp * p1 / tot
    w2 = p_grp * p2 / tot
    e1 = i1 - N_GROUPS
    e2 = i2 - N_GROUPS
    @pl.when(pl.program_id(0) == 0)
    def _():
        cnt_scr[...] = jnp.zeros_like(cnt_scr)

    hit = jnp.where(lane == e1, 1.0, 0.0) + jnp.where(lane == e2, 1.0, 0.0)
    before = _dot(ltri_ref[...], hit.astype(BF16)) + cnt_scr[...]
    r1 = jnp.where(lane == e1, before, 0.0).sum(axis=-1, keepdims=True)
    r2 = jnp.where(lane == e2, before, 0.0).sum(axis=-1, keepdims=True)
    cnt_scr[...] = cnt_scr[...] + hit.sum(axis=0, keepdims=True)
    cnt_ref[...] = cnt_scr[...]
    cols = [e1.astype(F32), e2.astype(F32), w1, w2, r1, r2]
    route = jnp.zeros(logits.shape, F32)
    for i, col in enumerate(cols):
        route = jnp.where(lane == i, col, route)
    route_ref[...] = route


def _merge_call(x, oa, of, ob, z, gate, mod3, mod_stride, w):
    B, L, _ = x.shape
    n_tok = B * L
    tm = ROW_TILE
    assert n_tok % tm == 0 and (L % tm == 0 or tm % L == 0)
    flat = lambda a: a.reshape(n_tok, a.shape[-1])
    rspec = lambda n: pl.BlockSpec((tm, n), lambda i: (i, 0))
    mspec = lambda k: pl.BlockSpec((1, 1, D_MODEL), lambda i: ((i * tm) // L * mod_stride + k, 0, 0))
    t = np.arange(tm)
    ltri = jnp.asarray((t[:, None] > t[None, :]).astype(np.float32)).astype(BF16)
    consts = [w["onw"], w["wpa"], w["wpb"], w["wo"], w["norm2_w"], w["wr3"], w["br"], ltri]
    return pl.pallas_call(
        _merge_kernel,
        grid=(n_tok // tm,),
        in_specs=[rspec(D_MODEL), rspec(W_A), rspec(W_B), rspec(W_B), rspec(W_B), rspec(2 * D_MODEL),
                  mspec(2), mspec(3), mspec(4)] + [_const_spec(c.shape) for c in consts],
        out_specs=[rspec(D_MODEL), pl.BlockSpec((tm * SUB, LANES), lambda i: (i, 0)), rspec(LANES),
                   _const_spec((1, LANES))],
        out_shape=[jax.ShapeDtypeStruct((n_tok, D_MODEL), F32),
                   jax.ShapeDtypeStruct((n_tok * SUB, LANES), F32),
                   jax.ShapeDtypeStruct((n_tok, LANES), F32),
                   jax.ShapeDtypeStruct((1, LANES), F32)],
        scratch_shapes=[pltpu.VMEM((1, LANES), F32)],
        compiler_params=_cparams(("arbitrary",)),
        name="merge",
    )(flat(x), flat(oa), flat(of), flat(ob), flat(z), flat(gate), mod3, mod3, mod3, *consts)


SUB = 8
ROW_CHUNKS = D_MODEL // LANES


def _store_token_major(ref, x):
    for s in range(ROW_CHUNKS):
        ref[pl.ds(s, x.shape[0], stride=ROW_CHUNKS), :] = x[:, s * LANES:(s + 1) * LANES]


def _load_token_major(ref, slot, row0, n_tok):
    return jnp.concatenate([ref[slot, pl.ds(row0 * ROW_CHUNKS + s, n_tok, stride=ROW_CHUNKS), :]
                            for s in range(ROW_CHUNKS)], axis=1)


def _row_gather_start(idx_ref, n_rows, src_hbm, buf, sem, slot):
    for r in range(n_rows):
        src = pl.multiple_of(idx_ref[0, 0, r] * SUB, SUB)
        pltpu.make_async_copy(src_hbm.at[pl.ds(src, SUB), :], buf.at[slot, pl.ds(r * SUB, SUB), :],
                              sem.at[slot]).start(priority=r % 2)


def _row_gather_wait(n_rows, src_hbm, buf, sem, slot):
    for r in range(n_rows):
        pltpu.make_async_copy(src_hbm.at[pl.ds(0, SUB), :], buf.at[slot, pl.ds(r * SUB, SUB), :],
                              sem.at[slot]).wait()


def _gather_pipeline(i, n_steps, idx_ref, idx_next_ref, n_rows, src_hbm, buf, sem):
    slot = lax.rem(i, 2)

    @pl.when(i == 0)
    def _():
        _row_gather_start(idx_ref, n_rows, src_hbm, buf, sem, 0)

    @pl.when(i + 1 < n_steps)
    def _():
        _row_gather_start(idx_next_ref, n_rows, src_hbm, buf, sem, 1 - slot)

    @pl.when(i < n_steps)
    def _():
        _row_gather_wait(n_rows, src_hbm, buf, sem, slot)

    return slot


def _moe_kernel(te_ref, nt_ref, idx_ref, idxn_ref, h2_hbm, wgu_ref, wd_ref, y_ref, xbuf, sem):
    i = pl.program_id(0)
    n_used = nt_ref[0]
    slot = _gather_pipeline(i, n_used, idx_ref, idxn_ref, MOE_TILE, h2_hbm, xbuf, sem)

    @pl.when(i < n_used)
    def _():
        x = _load_token_major(xbuf, slot, 0, MOE_TILE).astype(BF16)
        gu = _dot(x, wgu_ref[0])
        g = gu[:, :D_FF_E]
        hid = g * jax.nn.sigmoid(g) * gu[:, D_FF_E:]
        _store_token_major(y_ref, _dot(hid.astype(BF16), wd_ref[0]))

    @pl.when(i >= n_used)
    def _():
        y_ref[...] = jnp.zeros_like(y_ref)


def _moe_call(h2, src_tok, tile_expert, n_tiles_used, wgu, wd):
    p_pad = src_tok.shape[0]
    n_tiles = p_pad // MOE_TILE
    idx = src_tok.reshape(n_tiles, 1, MOE_TILE)
    smem_tile = lambda m: pl.BlockSpec((1, 1, MOE_TILE), m, memory_space=pltpu.SMEM)
    grid_spec = pltpu.PrefetchScalarGridSpec(
        num_scalar_prefetch=2,
        grid=(n_tiles,),
        in_specs=[smem_tile(lambda i, te, nt: (i, 0, 0)),
                  smem_tile(lambda i, te, nt: (jnp.minimum(i + 1, n_tiles - 1), 0, 0)),
                  pl.BlockSpec(memory_space=pl.ANY),
                  pl.BlockSpec((1, D_MODEL, 2 * D_FF_E), lambda i, te, nt: (te[i], 0, 0)),
                  pl.BlockSpec((1, D_FF_E, D_MODEL), lambda i, te, nt: (te[i], 0, 0))],
        out_specs=pl.BlockSpec((MOE_TILE * SUB, LANES), lambda i, te, nt: (i, 0)),
        scratch_shapes=[pltpu.VMEM((2, MOE_TILE * SUB, LANES), F32), pltpu.SemaphoreType.DMA((2,))],
    )
    return pl.pallas_call(
        _moe_kernel,
        grid_spec=grid_spec,
        out_shape=jax.ShapeDtypeStruct((p_pad * SUB, LANES), F32),
        compiler_params=_cparams(("arbitrary",)),
        name="moe",
    )(tile_expert, n_tiles_used, idx, idx, h2, wgu, wd)


COMBINE_TILE = 256


def _combine_kernel(pos_ref, posn_ref, x1_ref, route_ref, g2_ref, y_hbm, o_ref, ybuf, sem):
    i = pl.program_id(0)
    t = COMBINE_TILE
    slot = _gather_pipeline(i, pl.num_programs(0), pos_ref, posn_ref, 2 * t, y_hbm, ybuf, sem)
    route = route_ref[...]
    w1 = route[:, 2:3]
    w2 = route[:, 3:4]
    ya = _load_token_major(ybuf, slot, 0, t)
    yb = _load_token_major(ybuf, slot, t, t)
    o_ref[...] = x1_ref[...] + g2_ref[0] * (w1 * ya + w2 * yb)


def _combine_call(x1, route, pos2, y, mod3, mod_stride, seq_len):
    n = x1.shape[0]
    t = COMBINE_TILE
    n_tiles = n // t
    pos_tiles = pos2.reshape(n_tiles, t, 2).transpose(0, 2, 1).reshape(n_tiles, 1, 2 * t)
    smem_tile = lambda m: pl.BlockSpec((1, 1, 2 * t), m, memory_space=pltpu.SMEM)
    return pl.pallas_call(
        _combine_kernel,
        grid=(n_tiles,),
        in_specs=[smem_tile(lambda i: (i, 0, 0)),
                  smem_tile(lambda i: (jnp.minimum(i + 1, n_tiles - 1), 0, 0)),
                  pl.BlockSpec((t, D_MODEL), lambda i: (i, 0)),
                  pl.BlockSpec((t, LANES), lambda i: (i, 0)),
                  pl.BlockSpec((1, 1, D_MODEL), lambda i: ((i * t) // seq_len * mod_stride + 5, 0, 0)),
                  pl.BlockSpec(memory_space=pl.ANY)],
        out_specs=pl.BlockSpec((t, D_MODEL), lambda i: (i, 0)),
        out_shape=jax.ShapeDtypeStruct((n, D_MODEL), F32),
        scratch_shapes=[pltpu.VMEM((2, 2 * t * SUB, LANES), F32), pltpu.SemaphoreType.DMA((2,))],
        compiler_params=_cparams(("arbitrary",)),
        name="combine",
    )(pos_tiles, pos_tiles, x1, route, mod3, y)


def _moe_forward(h2, route, counts, wgu, wd):
    n = route.shape[0]
    eid = route[:, :2].astype(jnp.int32).reshape(-1)
    rank = route[:, 4:6].astype(jnp.int32).reshape(-1)
    tok = jnp.arange(2 * n, dtype=jnp.int32) // 2
    counts = counts[0, :N_EXPERTS].astype(jnp.int32)
    padded = ((counts + MOE_TILE - 1) // MOE_TILE) * MOE_TILE
    seg_end = jnp.cumsum(padded)
    seg_start = seg_end - padded
    pos = seg_start[eid] + rank
    p_pad = 2 * n + N_EXPERTS * MOE_TILE
    src_tok = (jnp.arange(p_pad, dtype=jnp.int32) % n).at[pos].set(tok)
    n_tiles = p_pad // MOE_TILE
    tile_start = jnp.arange(n_tiles, dtype=jnp.int32) * MOE_TILE
    tile_expert = jnp.minimum(jnp.sum((tile_start[:, None] >= seg_end[None, :]).astype(jnp.int32), axis=1),
                              N_EXPERTS - 1)
    n_used = (seg_end[-1] // MOE_TILE).astype(jnp.int32).reshape(1)
    y = _moe_call(h2, src_tok, tile_expert, n_used, wgu, wd)
    return y, pos.reshape(n, 2)


def _prep_weights(norm1_w, norm2_w, w_in, conv_w, a_log, dt_bias, q_norm_w, k_norm_w, o_norm_w,
                  w_proj_a, w_proj_b, w_out, w_router_g, b_router_g, w_router_e, b_router_e,
                  w_exp_gate, w_exp_up, w_exp_down):
    w = {}
    w["norm1_w"] = norm1_w.reshape(1, D_MODEL)
    w["norm2_w"] = norm2_w.reshape(1, D_MODEL)
    o = 0
    w["wa"] = w_in[:, o:o + 3 * W_A].astype(BF16); o += 3 * W_A
    w["wb"] = w_in[:, o:o + 3 * W_B].astype(BF16); o += 3 * W_B
    w["wz"] = w_in[:, o:o + W_B].astype(BF16); o += W_B
    small = w_in[:, o:o + 4 * H_B]; o += 4 * H_B
    w["wg"] = w_in[:, o:o + 2 * D_MODEL].astype(BF16)
    def group_lanes(t):
        r = t.shape[0]
        t = t.reshape(r, -1, N_HG, HP).transpose(0, 2, 1, 3).reshape(r, N_HG, -1)
        return jnp.pad(t, ((0, 0), (0, 0), (0, LANES - t.shape[-1]))).reshape(r, N_HG * LANES)

    w["we"] = group_lanes(small).astype(BF16)
    w["alog_l"] = group_lanes(a_log.reshape(1, 2 * H_B))
    w["dtb_l"] = group_lanes(dt_bias.reshape(1, 2 * H_B))
    w["qnw"] = jnp.tile(q_norm_w, H_A).reshape(1, W_A)
    w["knw"] = jnp.tile(k_norm_w, H_A).reshape(1, W_A)
    w["onw"] = jnp.tile(o_norm_w, H_B).reshape(1, W_B)
    blk = np.arange(256) // D_HA
    w["bd256"] = jnp.asarray((blk[:, None] == blk[None, :]).astype(np.float32)).astype(BF16)
    w["conv_w"] = conv_w
    w["wpa"] = w_proj_a.astype(BF16)
    w["wpb"] = w_proj_b.astype(BF16)
    w["wo"] = w_out.astype(BF16)
    wr = jnp.concatenate([w_router_g, w_router_e], axis=1)
    wr = jnp.pad(wr, ((0, 0), (0, LANES - wr.shape[1])))
    wr_hi = wr.astype(BF16)
    wr_lo = (wr - wr_hi.astype(F32)).astype(BF16)
    w["wr3"] = jnp.concatenate([wr_hi, wr_hi, wr_lo], axis=0)
    br = jnp.concatenate([b_router_g, b_router_e])
    w["br"] = jnp.pad(br, (0, LANES - br.shape[0])).reshape(1, LANES)
    w["wgu"] = jnp.concatenate([w_exp_gate, w_exp_up], axis=-1).astype(BF16)
    w["wd"] = w_exp_down.astype(BF16)
    return w


def _layer(x, mod3, mod_stride, w, ctx):
    B, L, _ = x.shape
    latent = ctx is not None
    qa, ka, va, qkvb, z, gate, ecol, *new_kv = _inproj_call(x, mod3, mod_stride, w, not latent)
    if latent:
        k_ctx, v_ctx, s0f, s0b, bias = ctx
        oa = _nbr_attn_call(qa, ka, va, k_ctx, v_ctx, bias)
    else:
        s0f = s0b = jnp.zeros((B, H_B, D_K, D_V), F32)
        oa = _ctx_attn_call(qa, ka, va)
    qd, kd, vd = _conv_call(qkvb, w["conv_w"], rope=latent)
    of, ob, s_f, s_b = _delta_call(qd, kd, vd, ecol, s0f, s0b)
    x1, h2, route, counts = _merge_call(x, oa, of, ob, z, gate, mod3, mod_stride, w)
    ys, pos2 = _moe_forward(h2, route, counts, w["wgu"], w["wd"])
    y = _combine_call(x1, route, pos2, ys, mod3, mod_stride, L)
    return y.reshape(B, L, D_MODEL), new_kv, s_f, s_b


def kernel(x_prompt, x_sample, c, cache_attn_k, cache_attn_v, state_delta_fwd, state_delta_bwd, c_ctx,
           norm1_w, norm2_w, w_mod, b_mod, w_in, conv_w, a_log, dt_bias, q_norm_w, k_norm_w, rpb,
           o_norm_w, w_proj_a, w_proj_b, w_out, w_router_g, b_router_g, w_router_e, b_router_e,
           w_exp_gate, w_exp_up, w_exp_down):
    assert norm1_w.shape[0] == 1, "single-layer trunk"
    Bp, Lp, _ = x_prompt.shape
    Bs, Ls, _ = x_sample.shape
    l = 0
    w = _prep_weights(norm1_w[l], norm2_w[l], w_in[l], conv_w[l], a_log[l], dt_bias[l], q_norm_w[l],
                      k_norm_w[l], o_norm_w[l], w_proj_a[l], w_proj_b[l], w_out[l], w_router_g[l],
                      b_router_g[l], w_router_e[l], b_router_e[l], w_exp_gate[l], w_exp_up[l],
                      w_exp_down[l])
    cond = jnp.concatenate([c_ctx[None, :], c, jnp.zeros((8 - 1 - Bs, D_MODEL), F32)], axis=0)
    mod = _mod_call(cond, w_mod[l], b_mod[l])
    mod3 = mod.reshape(8 * 6, 1, D_MODEL)
    bias = _bias_table_call(rpb[l])
    y_p, (new_k, new_v), s_f, s_b = _layer(x_prompt, mod3[:6], 0, w, None)
    lc = cache_attn_k.shape[2]
    ctx = (cache_attn_k[:, l].reshape(Bs, lc, W_A).astype(BF16),
           cache_attn_v[:, l].reshape(Bs, lc, W_A).astype(BF16),
           state_delta_fwd[:, l], state_delta_bwd[:, l], bias)
    y_s, _, _, _ = _layer(x_sample, mod3[6:6 + 6 * Bs], 6, w, ctx)
    return (y_p, y_s, new_k, new_v, s_f[:, None], s_b[:, None])
```

```python
import functools
import math

import numpy as np
import jax
import jax.numpy as jnp
from jax import lax
from jax.experimental import pallas as pl
from jax.experimental.pallas import tpu as pltpu

F32 = jnp.float32
BF16 = jnp.bfloat16

D_MODEL = 1024
GRID_W = 64
H_A = 8
D_HA = 64
W_A = H_A * D_HA
WIN_H = 8
WIN_W = 16
H_B = 4
D_K = 128
D_V = 128
W_B = H_B * D_V
CONV_K = 5
CHUNK = 64
ROPE_BASE = 10000.0
N_GROUPS = 4
EXP_PER_GROUP = 8
N_EXPERTS = N_GROUPS * EXP_PER_GROUP
D_FF_E = 256
EPS = 1e-6
NEG_INF = -1e30

LANES = 128
HP = H_B
N_HG = H_B // HP
ROW_TILE = 512
INPROJ_TILE = 256
MOE_TILE = 256
VMEM_LIMIT = 56 * 1024 * 1024

HIGHEST = lax.Precision.HIGHEST


def _cparams(sem):
    return pltpu.CompilerParams(dimension_semantics=sem, vmem_limit_bytes=VMEM_LIMIT)


def _dot(a, b):
    return jnp.dot(a, b, preferred_element_type=F32)


def _dot_nt(a, b):
    return lax.dot_general(a, b, (((1,), (1,)), ((), ())), preferred_element_type=F32)


def _dot_tn(a, b):
    return lax.dot_general(a, b, (((0,), (0,)), ((), ())), preferred_element_type=F32)


def _const_spec(shape):
    nd = len(shape)
    return pl.BlockSpec(shape, lambda *_: (0,) * nd)


def _mod_kernel(c_ref, w_ref, b_ref, o_ref):
    c = c_ref[...]
    s = c * jax.nn.sigmoid(c)
    o_ref[...] = jnp.dot(s, w_ref[...], preferred_element_type=F32, precision=HIGHEST) + b_ref[...]


def _mod_call(cond8, w_mod, b_mod):
    n_out = w_mod.shape[1]
    tn = 1024
    return pl.pallas_call(
        _mod_kernel,
        grid=(n_out // tn,),
        in_specs=[pl.BlockSpec((8, D_MODEL), lambda j: (0, 0)),
                  pl.BlockSpec((D_MODEL, tn), lambda j: (0, j)),
                  pl.BlockSpec((1, tn), lambda j: (0, j))],
        out_specs=pl.BlockSpec((8, tn), lambda j: (0, j)),
        out_shape=jax.ShapeDtypeStruct((8, n_out), F32),
        compiler_params=_cparams(("arbitrary",)),
        name="mod",
    )(cond8, w_mod, b_mod.reshape(1, n_out))


def _softplus(x):
    return jnp.maximum(x, 0.0) + jnp.log1p(jnp.exp(-jnp.abs(x)))


def _inproj_kernel(x_ref, sh_ref, sc_ref, nw_ref, wa_ref, wb_ref, wz_ref, wg_ref, we_ref,
                   qnw_ref, knw_ref, bd_ref, alog_ref, dtb_ref, tri_ref,
                   qa_ref, ka_ref, va_ref, qkvb_ref, z_ref, gate_ref, e_ref, *kv5_refs):
    x = x_ref[...]
    ms = jnp.mean(x * x, axis=-1, keepdims=True)
    h = x * lax.rsqrt(ms + EPS) * nw_ref[...]
    h = h * (1.0 + sc_ref[0]) + sh_ref[0]
    hb = h.astype(BF16)

    a = _dot(hb, wa_ref[...])
    bd = bd_ref[...]

    def head_rms(t, w):
        sq = (t * t).astype(BF16)
        ss = jnp.concatenate([_dot(sq[:, i * 256:(i + 1) * 256], bd) for i in range(W_A // 256)], axis=-1)
        return t * lax.rsqrt(ss * (1.0 / D_HA) + EPS) * w

    qa = head_rms(a[:, :W_A], qnw_ref[...]) * (D_HA ** -0.5)
    qa_ref[...] = qa.astype(qa_ref.dtype)
    ka = head_rms(a[:, W_A:2 * W_A], knw_ref[...])
    va = a[:, 2 * W_A:]
    ka_ref[...] = ka.astype(ka_ref.dtype)
    va_ref[...] = va.astype(va_ref.dtype)
    if kv5_refs:
        for ref, t in zip(kv5_refs, (ka, va)):
            for hd in range(H_A):
                ref[0, 0, :, hd, :] = t[:, hd * D_HA:(hd + 1) * D_HA]

    qkvb_ref[...] = _dot(hb, wb_ref[...]).astype(qkvb_ref.dtype)
    z = _dot(hb, wz_ref[...])
    z_ref[...] = (z * jax.nn.sigmoid(z)).astype(z_ref.dtype)
    gate_ref[...] = jax.nn.sigmoid(_dot(hb, wg_ref[...])).astype(gate_ref.dtype)

    e = _dot(hb, we_ref[...])
    lane = lax.broadcasted_iota(jnp.int32, e.shape, 1) % LANES
    g = -jnp.exp(alog_ref[...]) * _softplus(e + dtb_ref[...])
    act = jnp.where(lane < 2 * HP, g, jax.nn.sigmoid(e))
    a_hi = act.astype(BF16)
    rem = act - a_hi.astype(F32)
    a_mid = rem.astype(BF16)
    a_lo = (rem - a_mid.astype(F32)).astype(BF16)
    tri = tri_ref[...]
    cum = _dot(tri, a_hi) + _dot(tri, a_mid) + _dot(tri, a_lo)
    tm = act.shape[0]
    e_ref[...] = jnp.where(lane < HP, cum[:tm], jnp.where(lane < 2 * HP, cum[tm:], act))


def _inproj_call(x, mod3, mod_stride, w, new_cache):
    B, L, _ = x.shape
    n_tok = B * L
    tm = INPROJ_TILE
    assert n_tok % tm == 0 and L % tm == 0
    row = lambda n, dt: jax.ShapeDtypeStruct((n_tok, n), dt)
    rspec = lambda n: pl.BlockSpec((tm, n), lambda i: (i, 0))
    mspec = lambda k: pl.BlockSpec((1, 1, D_MODEL), lambda i: ((i * tm) // L * mod_stride + k, 0, 0))
    t = np.arange(tm)
    same = (t[:, None] // CHUNK) == (t[None, :] // CHUNK)
    tri = np.concatenate([same & (t[:, None] >= t[None, :]), same & (t[:, None] <= t[None, :])], axis=0)
    consts = [w["norm1_w"], w["wa"], w["wb"], w["wz"], w["wg"], w["we"], w["qnw"], w["knw"],
              w["bd256"], w["alog_l"], w["dtb_l"], jnp.asarray(tri.astype(np.float32)).astype(BF16)]
    out_specs = [rspec(W_A), rspec(W_A), rspec(W_A), rspec(3 * W_B), rspec(W_B),
                 rspec(2 * D_MODEL), rspec(N_HG * LANES)]
    out_shape = [row(W_A, BF16), row(W_A, BF16), row(W_A, BF16), row(3 * W_B, BF16),
                 row(W_B, BF16), row(2 * D_MODEL, BF16), row(N_HG * LANES, F32)]
    if new_cache:
        per_seq = L // tm
        out_specs += [pl.BlockSpec((1, 1, tm, H_A, D_HA), lambda i: (i // per_seq, 0, i % per_seq, 0, 0))] * 2
        out_shape += [jax.ShapeDtypeStruct((B, 1, L, H_A, D_HA), F32)] * 2
    outs = pl.pallas_call(
        _inproj_kernel,
        grid=(n_tok // tm,),
        in_specs=[rspec(D_MODEL), mspec(0), mspec(1)] + [_const_spec(c.shape) for c in consts],
        out_specs=out_specs,
        out_shape=out_shape,
        compiler_params=_cparams(("parallel",)),
        name="inproj",
    )(x.reshape(n_tok, D_MODEL), mod3, mod3, *consts)
    return [o.reshape(B, L, o.shape[-1]) for o in outs[:7]] + list(outs[7:])


def _pair_softmax_pv(s_parts, v_parts):
    m = s_parts[0].max(axis=-1, keepdims=True)
    for s in s_parts[1:]:
        m = jnp.maximum(m, s.max(axis=-1, keepdims=True))
    den = 0.0
    acc = 0.0
    for s, v in zip(s_parts, v_parts):
        p = jnp.exp(s - m)
        den = den + p.sum(axis=-1, keepdims=True)
        acc = acc + _dot(p.astype(BF16), v)
    return acc / den


def _stack_pair(qp):
    lane = lax.broadcasted_iota(jnp.int32, qp.shape, 1)
    zero = jnp.zeros_like(qp)
    return jnp.concatenate([jnp.where(lane < D_HA, qp, zero), jnp.where(lane >= D_HA, qp, zero)], axis=0)


def _unstack_pair(o2):
    t = o2.shape[0] // 2
    lane = lax.broadcasted_iota(jnp.int32, (t, LANES), 1)
    return jnp.where(lane < D_HA, o2[:t], o2[t:])


def _ctx_attn_kernel(q_ref, k_ref, v_ref, o_ref):
    for p in range(H_A // 2):
        sl = slice(p * LANES, (p + 1) * LANES)
        qs = _stack_pair(q_ref[0, :, sl])
        kp = k_ref[0, :, sl].astype(BF16)
        vp = v_ref[0, :, sl].astype(BF16)
        o2 = _pair_softmax_pv([_dot_nt(qs, kp)], [vp])
        o_ref[0, :, sl] = _unstack_pair(o2).astype(o_ref.dtype)


def _ctx_attn_call(q, k, v):
    B, L, _ = q.shape
    spec = pl.BlockSpec((1, L, W_A), lambda b: (b, 0, 0))
    return pl.pallas_call(
        _ctx_attn_kernel,
        grid=(B,),
        in_specs=[spec, spec, spec],
        out_specs=spec,
        out_shape=jax.ShapeDtypeStruct((B, L, W_A), BF16),
        compiler_params=_cparams(("parallel",)),
        name="ctx_attn",
    )(q, k, v)


N_DR = 2 * WIN_H - 1
N_DC = 2 * WIN_W - 1
NBR_ROWS = 4


def _bias_table_kernel(rpb_ref, o_ref):
    h = pl.program_id(0)
    c = lax.broadcasted_iota(jnp.int32, (GRID_W, 2 * GRID_W), 0)
    lane = lax.broadcasted_iota(jnp.int32, (GRID_W, 2 * GRID_W), 1)
    kc = lane % GRID_W
    ws = jnp.clip(c - WIN_W // 2, 0, GRID_W - WIN_W)
    in_win = (kc >= ws) & (kc < ws + WIN_W)
    dc = kc - c + (WIN_W - 1)
    for dr in range(N_DR - 1):
        base = (h * N_DR + dr) * N_DC
        val = jnp.zeros((GRID_W, 2 * GRID_W), F32)
        for d in range(N_DC):
            val = jnp.where(dc == d, jnp.where(lane < GRID_W, rpb_ref[base + d], rpb_ref[base + N_DC + d]), val)
        o_ref[0, dr] = jnp.where(in_win, val, NEG_INF)


def _bias_table_call(rpb):
    return pl.pallas_call(
        _bias_table_kernel,
        grid=(H_A,),
        in_specs=[pl.BlockSpec(memory_space=pltpu.SMEM)],
        out_specs=pl.BlockSpec((1, N_DR - 1, GRID_W, 2 * GRID_W), lambda h: (h, 0, 0, 0)),
        out_shape=jax.ShapeDtypeStruct((H_A, N_DR - 1, GRID_W, 2 * GRID_W), F32),
        compiler_params=_cparams(("arbitrary",)),
        name="bias_table",
    )(rpb.reshape(-1))


def _nbr_attn_kernel(q_ref, k_ref, v_ref, kc_ref, vc_ref, bias_ref, o_ref, *, rows):
    inst = []
    for rr in range(NBR_ROWS):
        r = pl.program_id(1) * NBR_ROWS + rr
        first = jnp.clip(r - WIN_H // 2, 0, rows - WIN_H)
        start = pl.multiple_of(first * GRID_W, GRID_W)
        shift = r - first
        for p in range(H_A // 2):
            inst.append((rr, p, start, shift))
    scores = []
    for rr, p, start, shift in inst:
        sl = slice(p * LANES, (p + 1) * LANES)
        qs = _stack_pair(q_ref[0, rr * GRID_W:(rr + 1) * GRID_W, sl])
        bias = jnp.concatenate(
            [jnp.concatenate([bias_ref[2 * p + hh, 2 * u - shift + WIN_H - 1] for u in range(WIN_H // 2)], axis=1)
             for hh in range(2)], axis=0)
        s_loc = _dot_nt(qs, k_ref[0, pl.ds(start, WIN_H * GRID_W), sl]) + bias
        s_ctx = _dot_nt(qs, kc_ref[0, :, sl])
        scores.append((s_loc, s_ctx))
    for (rr, p, start, shift), (s_loc, s_ctx) in zip(inst, scores):
        sl = slice(p * LANES, (p + 1) * LANES)
        o2 = _pair_softmax_pv([s_loc, s_ctx], [v_ref[0, pl.ds(start, WIN_H * GRID_W), sl], vc_ref[0, :, sl]])
        o_ref[0, rr * GRID_W:(rr + 1) * GRID_W, sl] = _unstack_pair(o2).astype(o_ref.dtype)


def _nbr_attn_call(q, k, v, k_ctx, v_ctx, bias):
    B, L, _ = q.shape
    rows = L // GRID_W
    lc = k_ctx.shape[1]
    full = pl.BlockSpec((1, L, W_A), lambda b, r: (b, 0, 0))
    ctx = pl.BlockSpec((1, lc, W_A), lambda b, r: (b, 0, 0))
    rowspec = pl.BlockSpec((1, NBR_ROWS * GRID_W, W_A), lambda b, r: (b, r, 0))
    return pl.pallas_call(
        functools.partial(_nbr_attn_kernel, rows=rows),
        grid=(B, rows // NBR_ROWS),
        in_specs=[rowspec, full, full, ctx, ctx, _const_spec(bias.shape)],
        out_specs=rowspec,
        out_shape=jax.ShapeDtypeStruct((B, L, W_A), BF16),
        compiler_params=_cparams(("parallel", "arbitrary")),
        name="nbr_attn",
    )(q, k, v, k_ctx, v_ctx, bias)


HALO = 16


def _conv_kernel(*refs, rope):
    if rope:
        x_ref, prev_ref, next_ref, w_ref, cos_ref, sin_ref, q_ref, k_ref, v_ref, xe_ref = refs
    else:
        x_ref, prev_ref, next_ref, w_ref, q_ref, k_ref, v_ref, xe_ref = refs
    i = pl.program_id(1)
    n_i = pl.num_programs(1)
    tl = x_ref.shape[1]
    xe_ref[0:HALO] = jnp.where(i > 0, prev_ref[0].astype(F32), 0.0)
    xe_ref[HALO:HALO + tl] = x_ref[0].astype(F32)
    xe_ref[HALO + tl:] = jnp.where(i < n_i - 1, next_ref[0].astype(F32), 0.0)
    half = CONV_K // 2
    y = w_ref[0:1, :] * xe_ref[HALO - half:HALO - half + tl]
    for j in range(1, CONV_K):
        y = y + w_ref[j:j + 1, :] * xe_ref[HALO - half + j:HALO - half + j + tl]
    y = y * jax.nn.sigmoid(y)

    def l2n(t):
        return t * lax.rsqrt(jnp.sum(t * t, axis=-1, keepdims=True) + EPS)

    def rot(t):
        if not rope:
            return t
        lane = lax.broadcasted_iota(jnp.int32, t.shape, 1)
        swapped = jnp.where(lane % 2 == 0, pltpu.roll(t, LANES - 1, 1), pltpu.roll(t, 1, 1))
        return t * cos_ref[...] + swapped * sin_ref[...]

    for h in range(H_B):
        sl = slice(h * D_K, (h + 1) * D_K)
        q_ref[0, :, sl] = (rot(l2n(y[:, sl])) * (D_K ** -0.5)).astype(q_ref.dtype)
        k_ref[0, :, sl] = rot(l2n(y[:, W_B + h * D_K:W_B + (h + 1) * D_K])).astype(k_ref.dtype)
    v_ref[0] = y[:, 2 * W_B:].astype(v_ref.dtype)


def _rope_tables(L):
    pos = jnp.arange(L)
    rows = (pos // GRID_W).astype(F32)
    cols = (pos % GRID_W).astype(F32)
    n_pairs = D_K // 4
    inv = ROPE_BASE ** (-jnp.arange(n_pairs, dtype=F32) / n_pairs)
    ang = jnp.concatenate([rows[:, None] * inv, cols[:, None] * inv], axis=-1)
    cos = jnp.repeat(jnp.cos(ang), 2, axis=-1)
    sin = jnp.repeat(jnp.sin(ang), 2, axis=-1)
    sign = jnp.asarray(np.tile(np.array([-1.0, 1.0], np.float32), D_K // 2))
    return cos, sin * sign


def _conv_call(x, conv_w, rope):
    B, L, C = x.shape
    tl = min(ROW_TILE, L)
    nh = tl // HALO
    n_halo = L // HALO
    cur = pl.BlockSpec((1, tl, C), lambda b, i: (b, i, 0))
    prev = pl.BlockSpec((1, HALO, C), lambda b, i: (b, jnp.maximum(i * nh - 1, 0), 0))
    nxt = pl.BlockSpec((1, HALO, C), lambda b, i: (b, jnp.minimum((i + 1) * nh, n_halo - 1), 0))
    ins = [x, x, x, conv_w]
    specs = [cur, prev, nxt, _const_spec(conv_w.shape)]
    if rope:
        cos, sin = _rope_tables(L)
        ins += [cos, sin]
        specs += [pl.BlockSpec((tl, D_K), lambda b, i: (i, 0))] * 2
    out = pl.BlockSpec((1, tl, W_B), lambda b, i: (b, i, 0))
    return pl.pallas_call(
        functools.partial(_conv_kernel, rope=rope),
        grid=(B, L // tl),
        in_specs=specs,
        out_specs=[out, out, out],
        out_shape=[jax.ShapeDtypeStruct((B, L, W_B), BF16)] * 3,
        scratch_shapes=[pltpu.VMEM((tl + 2 * HALO, C), F32)],
        compiler_params=_cparams(("parallel", "parallel")),
        name="conv",
    )(*ins)


DELTA_CT = 4
C4 = H_B * CHUNK


def _blockdiag_rows(x, n_blk):
    w = x.shape[1] // n_blk
    lane_blk = lax.broadcasted_iota(jnp.int32, x.shape, 1) // w
    return jnp.concatenate([jnp.where(lane_blk == b, x, 0.0) for b in range(n_blk)], axis=0).astype(BF16)


def _widen(ec, col0, width):
    blk = lax.broadcasted_iota(jnp.int32, (ec.shape[0], H_B * width), 1) // width
    out = ec[:, col0 + H_B - 1:col0 + H_B]
    for h in range(H_B - 2, -1, -1):
        out = jnp.where(blk == h, ec[:, col0 + h:col0 + h + 1], out)
    return out


def _delta_prep(groups):
    c = CHUNK
    row = lax.broadcasted_iota(jnp.int32, (c, C4), 0)
    coll = lax.broadcasted_iota(jnp.int32, (c, C4), 1) % c
    eye = jnp.where(row == coll, 1.0, 0.0)
    st = []
    for q, k, v, ec, gr, backward in groups:
        gi = H_B if backward else 0
        bi = 3 * H_B if backward else 2 * H_B
        gc64, gc128 = _widen(ec, gi, c), _widen(ec, gi, D_K)
        beta64, beta128 = _widen(ec, bi, c), _widen(ec, bi, D_K)
        incl = (row <= coll) if backward else (row >= coll)
        strict = (row < coll) if backward else (row > coll)
        decay = jnp.where(incl, jnp.exp(jnp.where(incl, gc64 - gr, 0.0)), 0.0)
        kf = k.astype(F32)
        qf = q.astype(F32)
        eg = jnp.exp(gc128)
        g_last = gc128[0:1, :] if backward else gc128[c - 1:c, :]
        st.append(dict(
            kf=kf, qf=qf, decay=decay, strict=strict, beta64=beta64,
            qg=(qf * eg).astype(BF16),
            kg=(kf * jnp.exp(g_last - gc128)).astype(BF16),
            rhs_v=(v.astype(F32) * beta128).astype(BF16),
            rhs_k=(kf * (beta128 * eg)).astype(BF16),
            egl=jnp.exp(g_last)))
    lane2 = lax.broadcasted_iota(jnp.int32, (c, 2 * D_K), 1)
    for s in st:
        outs = []
        for p in range(H_B // 2):
            kp = s["kf"][:, p * 2 * D_K:(p + 1) * 2 * D_K]
            qp = s["qf"][:, p * 2 * D_K:(p + 1) * 2 * D_K]
            rmat = jnp.concatenate([jnp.where(lane2 < D_K, kp, 0.0), jnp.where(lane2 >= D_K, kp, 0.0)], axis=0)
            outs.append(_dot_nt(jnp.concatenate([kp, qp], axis=0).astype(BF16), rmat.astype(BF16)))
        kk = jnp.concatenate([o[:c] for o in outs], axis=1)
        qk = jnp.concatenate([o[c:] for o in outs], axis=1)
        s["a"] = jnp.where(s["strict"], kk * s["decay"], 0.0) * s["beta64"]
        s["intra"] = (qk * s["decay"]).astype(BF16)
        s["t"] = eye - jnp.where(row // 2 == coll // 2, s["a"], 0.0)
    b = 2
    while b < c:
        join = (row // (2 * b) == coll // (2 * b)) & (row // b != coll // b)
        for s in st:
            s["tb"] = s["t"].astype(BF16)
            s["tl"] = _dot(s["tb"], _blockdiag_rows(jnp.where(join, s["a"], 0.0), H_B))
        for s in st:
            s["t"] = s["t"] - _dot(s["tl"].astype(BF16), _blockdiag_rows(s["t"], H_B))
        b *= 2
    zero = jnp.zeros((c, 2 * D_K), BF16)
    for s in st:
        tb = s["t"].astype(BF16)
        us, ws = [], []
        for h in range(H_B):
            rhs = jnp.concatenate([s["rhs_v"][:, h * D_V:(h + 1) * D_V], s["rhs_k"][:, h * D_K:(h + 1) * D_K]], axis=1)
            uw = _dot(tb, jnp.concatenate([zero] * h + [rhs] + [zero] * (H_B - 1 - h), axis=0))
            us.append(uw[:, :D_V])
            ws.append(uw[:, D_V:])
        s["u"] = jnp.concatenate(us, axis=1)
        s["w"] = jnp.concatenate(ws, axis=1).astype(BF16)
    return st


def _delta_scan_step(steps, states):
    c = CHUNK
    zero = jnp.zeros((D_K, D_V), BF16)
    mids = []
    for s, S in zip(steps, states):
        w_s, q_s = [], []
        for p in range(H_B // 2):
            sl = slice(p * 2 * D_K, (p + 1) * 2 * D_K)
            lhs = jnp.concatenate([s["w"][:, sl], s["qg"][:, sl]], axis=0)
            sa, sb = S[2 * p].astype(BF16), S[2 * p + 1].astype(BF16)
            sbd = jnp.concatenate([jnp.concatenate([sa, zero], axis=1), jnp.concatenate([zero, sb], axis=1)], axis=0)
            out = _dot(lhs, sbd)
            w_s.append(out[:c])
            q_s.append(out[c:])
        v_new = s["u"] - jnp.concatenate(w_s, axis=1)
        mids.append((v_new, jnp.concatenate(q_s, axis=1)))
    outs, new_states = [], []
    for s, S, (v_new, q_s) in zip(steps, states, mids):
        vnb = v_new.astype(BF16)
        outs.append(q_s + _dot(s["intra"], _blockdiag_rows(v_new, H_B)))
        new_states.append([S[h] * s["egl"][:, h * D_V:(h + 1) * D_V]
                           + _dot_tn(s["kg"][:, h * D_K:(h + 1) * D_K], vnb[:, h * D_V:(h + 1) * D_V])
                           for h in range(H_B)])
    return outs, new_states


def _delta_kernel(qf_ref, kf_ref, vf_ref, ecf_ref, erf_ref, qb_ref, kb_ref, vb_ref, ecb_ref, erb_ref,
                  s0f_ref, s0b_ref, of_ref, ob_ref, sf_ref, sb_ref, s_scr):
    j = pl.program_id(1)
    c = CHUNK

    @pl.when(j == 0)
    def _():
        s_scr[0] = s0f_ref[0]
        s_scr[1] = s0b_ref[0]

    def group(refs, ci, backward):
        q_ref, k_ref, v_ref, ec_ref, er_ref = refs
        rows = slice(ci * c, (ci + 1) * c)
        d = 1 if backward else 0
        return (q_ref[0, rows, :], k_ref[0, rows, :], v_ref[0, rows, :], ec_ref[0, rows, :],
                er_ref[0, ci, d:d + 1, :], backward)

    fwd = (qf_ref, kf_ref, vf_ref, ecf_ref, erf_ref)
    bwd = (qb_ref, kb_ref, vb_ref, ecb_ref, erb_ref)
    prep = _delta_prep([group(fwd, ci, False) for ci in range(DELTA_CT)]
                       + [group(bwd, ci, True) for ci in range(DELTA_CT)])
    states = [[s_scr[d, h] for h in range(H_B)] for d in range(2)]
    for step in range(DELTA_CT):
        cf, cb = step, DELTA_CT - 1 - step
        outs, states = _delta_scan_step([prep[cf], prep[DELTA_CT + cb]], states)
        of_ref[0, cf * c:(cf + 1) * c, :] = outs[0].astype(of_ref.dtype)
        ob_ref[0, cb * c:(cb + 1) * c, :] = outs[1].astype(ob_ref.dtype)
    for d in range(2):
        for h in range(H_B):
            s_scr[d, h] = states[d][h]

    @pl.when(j == pl.num_programs(1) - 1)
    def _():
        sf_ref[0] = s_scr[0]
        sb_ref[0] = s_scr[1]


def _delta_call(q, k, v, ecol, s0f, s0b):
    B, L, _ = q.shape
    n = L // CHUNK
    tl = DELTA_CT * CHUNK
    nb = L // tl
    erow = ecol[..., :2 * H_B].reshape(B, n, CHUNK, 2, H_B).transpose(0, 1, 3, 4, 2).reshape(B, n, 2, C4)
    fmap = lambda b, j: (b, j, 0)
    bmap = lambda b, j: (b, nb - 1 - j, 0)
    seq = lambda m: pl.BlockSpec((1, tl, W_B), m)
    ecs = lambda m: pl.BlockSpec((1, tl, LANES), m)
    ers = lambda m: pl.BlockSpec((1, DELTA_CT, 2, C4), lambda b, j: m(b, j) + (0,))
    st = pl.BlockSpec((1, H_B, D_K, D_V), lambda b, j: (b, 0, 0, 0))
    return pl.pallas_call(
        _delta_kernel,
        grid=(B, nb),
        in_specs=[seq(fmap), seq(fmap), seq(fmap), ecs(fmap), ers(fmap),
                  seq(bmap), seq(bmap), seq(bmap), ecs(bmap), ers(bmap), st, st],
        out_specs=[seq(fmap), seq(bmap), st, st],
        out_shape=[jax.ShapeDtypeStruct((B, L, W_B), BF16)] * 2
        + [jax.ShapeDtypeStruct((B, H_B, D_K, D_V), F32)] * 2,
        scratch_shapes=[pltpu.VMEM((2, H_B, D_K, D_V), F32)],
        compiler_params=_cparams(("parallel", "arbitrary")),
        name="delta",
    )(q, k, v, ecol, erow, q, k, v, ecol, erow, s0f, s0b)


def _merge_kernel(x_ref, oa_ref, of_ref, ob_ref, z_ref, gate_ref, g1_ref, sh2_ref, sc2_ref,
                  onw_ref, wpa_ref, wpb_ref, wo_ref, n2w_ref, wr_ref, br_ref, ltri_ref,
                  x1_ref, h2_ref, route_ref, cnt_ref, cnt_scr):
    o = of_ref[...].astype(F32) + ob_ref[...].astype(F32)
    parts = []
    for h in range(H_B):
        t = o[:, h * D_V:(h + 1) * D_V]
        parts.append(t * lax.rsqrt(jnp.mean(t * t, axis=-1, keepdims=True) + EPS))
    ob = jnp.concatenate(parts, axis=-1) * onw_ref[...] * z_ref[...].astype(F32)
    pa = _dot(oa_ref[...], wpa_ref[...])
    pb = _dot(ob.astype(BF16), wpb_ref[...])
    gate = gate_ref[...]
    merged = gate[:, :D_MODEL].astype(F32) * pa + gate[:, D_MODEL:].astype(F32) * pb
    mix = _dot(merged.astype(BF16), wo_ref[...])
    x1 = x_ref[...] + g1_ref[0] * mix
    x1_ref[...] = x1
    ms = jnp.mean(x1 * x1, axis=-1, keepdims=True)
    h2 = x1 * lax.rsqrt(ms + EPS) * n2w_ref[...]
    h2 = h2 * (1.0 + sc2_ref[0]) + sh2_ref[0]
    hi = h2.astype(BF16)
    _store_token_major(h2_ref, h2)
    lo = (h2 - hi.astype(F32)).astype(BF16)
    logits = _dot(jnp.concatenate([hi, lo, hi], axis=-1), wr_ref[...]) + br_ref[...]
    lane = lax.broadcasted_iota(jnp.int32, logits.shape, 1)
    big = jnp.int32(LANES)

    def first_max(vals, mask):
        mv = jnp.where(mask, vals, NEG_INF)
        m = mv.max(axis=-1, keepdims=True)
        idx = jnp.where(mask & (mv == m), lane, big).min(axis=-1, keepdims=True)
        return m, idx

    gmask = lane < N_GROUPS
    gm, grp = first_max(logits, gmask)
    p_grp = 1.0 / jnp.where(gmask, jnp.exp(logits - gm), 0.0).sum(axis=-1, keepdims=True)
    lo_lane = N_GROUPS + grp * EXP_PER_GROUP
    emask = (lane >= lo_lane) & (lane < lo_lane + EXP_PER_GROUP)
    em, _ = first_max(logits, emask)
    ex = jnp.where(emask, jnp.exp(logits - em), 0.0)
    probs = ex / ex.sum(axis=-1, keepdims=True)
    p1, i1 = first_max(probs, emask)
    p2, i2 = first_max(probs, emask & (lane != i1))
    tot = p1 + p2
    w1 = p_grp * p1 / tot
    w2 = p_grp * p2 / tot
    e1 = i1 - N_GROUPS
    e2 = i2 - N_GROUPS
    @pl.when(pl.program_id(0) == 0)
    def _():
        cnt_scr[...] = jnp.zeros_like(cnt_scr)

    hit = jnp.where(lane == e1, 1.0, 0.0) + jnp.where(lane == e2, 1.0, 0.0)
    before = _dot(ltri_ref[...], hit.astype(BF16)) + cnt_scr[...]
    r1 = jnp.where(lane == e1, before, 0.0).sum(axis=-1, keepdims=True)
    r2 = jnp.where(lane == e2, before, 0.0).sum(axis=-1, keepdims=True)
    cnt_scr[...] = cnt_scr[...] + hit.sum(axis=0, keepdims=True)
    cnt_ref[...] = cnt_scr[...]
    cols = [e1.astype(F32), e2.astype(F32), w1, w2, r1, r2]
    route = jnp.zeros(logits.shape, F32)
    for i, col in enumerate(cols):
        route = jnp.where(lane == i, col, route)
    route_ref[...] = route


def _merge_call(x, oa, of, ob, z, gate, mod3, mod_stride, w):
    B, L, _ = x.shape
    n_tok = B * L
    tm = ROW_TILE
    assert n_tok % tm == 0 and (L % tm == 0 or tm % L == 0)
    flat = lambda a: a.reshape(n_tok, a.shape[-1])
    rspec = lambda n: pl.BlockSpec((tm, n), lambda i: (i, 0))
    mspec = lambda k: pl.BlockSpec((1, 1, D_MODEL), lambda i: ((i * tm) // L * mod_stride + k, 0, 0))
    t = np.arange(tm)
    ltri = jnp.asarray((t[:, None] > t[None, :]).astype(np.float32)).astype(BF16)
    consts = [w["onw"], w["wpa"], w["wpb"], w["wo"], w["norm2_w"], w["wr3"], w["br"], ltri]
    return pl.pallas_call(
        _merge_kernel,
        grid=(n_tok // tm,),
        in_specs=[rspec(D_MODEL), rspec(W_A), rspec(W_B), rspec(W_B), rspec(W_B), rspec(2 * D_MODEL),
                  mspec(2), mspec(3), mspec(4)] + [_const_spec(c.shape) for c in consts],
        out_specs=[rspec(D_MODEL), pl.BlockSpec((tm * SUB, LANES), lambda i: (i, 0)), rspec(LANES),
                   _const_spec((1, LANES))],
        out_shape=[jax.ShapeDtypeStruct((n_tok, D_MODEL), F32),
                   jax.ShapeDtypeStruct((n_tok * SUB, LANES), F32),
                   jax.ShapeDtypeStruct((n_tok, LANES), F32),
                   jax.ShapeDtypeStruct((1, LANES), F32)],
        scratch_shapes=[pltpu.VMEM((1, LANES), F32)],
        compiler_params=_cparams(("arbitrary",)),
        name="merge",
    )(flat(x), flat(oa), flat(of), flat(ob), flat(z), flat(gate), mod3, mod3, mod3, *consts)


SUB = 8
ROW_CHUNKS = D_MODEL // LANES


def _store_token_major(ref, x):
    for s in range(ROW_CHUNKS):
        ref[pl.ds(s, x.shape[0], stride=ROW_CHUNKS), :] = x[:, s * LANES:(s + 1) * LANES]


def _load_token_major(ref, slot, row0, n_tok):
    return jnp.concatenate([ref[slot, pl.ds(row0 * ROW_CHUNKS + s, n_tok, stride=ROW_CHUNKS), :]
                            for s in range(ROW_CHUNKS)], axis=1)


def _row_gather_start(idx_ref, n_rows, src_hbm, buf, sem, slot):
    for r in range(n_rows):
        src = pl.multiple_of(idx_ref[0, 0, r] * SUB, SUB)
        pltpu.make_async_copy(src_hbm.at[pl.ds(src, SUB), :], buf.at[slot, pl.ds(r * SUB, SUB), :],
                              sem.at[slot]).start(priority=r % 2)


def _row_gather_wait(n_rows, src_hbm, buf, sem, slot):
    for r in range(n_rows):
        pltpu.make_async_copy(src_hbm.at[pl.ds(0, SUB), :], buf.at[slot, pl.ds(r * SUB, SUB), :],
                              sem.at[slot]).wait()


def _gather_pipeline(i, n_steps, idx_ref, idx_next_ref, n_rows, src_hbm, buf, sem):
    slot = lax.rem(i, 2)

    @pl.when(i == 0)
    def _():
        _row_gather_start(idx_ref, n_rows, src_hbm, buf, sem, 0)

    @pl.when(i + 1 < n_steps)
    def _():
        _row_gather_start(idx_next_ref, n_rows, src_hbm, buf, sem, 1 - slot)

    @pl.when(i < n_steps)
    def _():
        _row_gather_wait(n_rows, src_hbm, buf, sem, slot)

    return slot


def _moe_kernel(te_ref, nt_ref, idx_ref, idxn_ref, h2_hbm, wg_ref, wu_ref, wd_ref, y_ref, xbuf, sem):
    i = pl.program_id(0)
    n_used = nt_ref[0]
    slot = _gather_pipeline(i, n_used, idx_ref, idxn_ref, MOE_TILE, h2_hbm, xbuf, sem)

    @pl.when(i < n_used)
    def _():
        x = _load_token_major(xbuf, slot, 0, MOE_TILE).astype(BF16)
        g = _dot(x, wg_ref[0])
        hid = g * jax.nn.sigmoid(g) * _dot(x, wu_ref[0])
        _store_token_major(y_ref, _dot(hid.astype(BF16), wd_ref[0]))

    @pl.when(i >= n_used)
    def _():
        y_ref[...] = jnp.zeros_like(y_ref)


def _moe_call(h2, src_tok, tile_expert, n_tiles_used, wg, wu, wd):
    p_pad = src_tok.shape[0]
    n_tiles = p_pad // MOE_TILE
    idx = src_tok.reshape(n_tiles, 1, MOE_TILE)
    smem_tile = lambda m: pl.BlockSpec((1, 1, MOE_TILE), m, memory_space=pltpu.SMEM)
    grid_spec = pltpu.PrefetchScalarGridSpec(
        num_scalar_prefetch=2,
        grid=(n_tiles,),
        in_specs=[smem_tile(lambda i, te, nt: (i, 0, 0)),
                  smem_tile(lambda i, te, nt: (jnp.minimum(i + 1, n_tiles - 1), 0, 0)),
                  pl.BlockSpec(memory_space=pl.ANY),
                  pl.BlockSpec((1, D_MODEL, D_FF_E), lambda i, te, nt: (te[i], 0, 0)),
                  pl.BlockSpec((1, D_MODEL, D_FF_E), lambda i, te, nt: (te[i], 0, 0)),
                  pl.BlockSpec((1, D_FF_E, D_MODEL), lambda i, te, nt: (te[i], 0, 0))],
        out_specs=pl.BlockSpec((MOE_TILE * SUB, LANES), lambda i, te, nt: (i, 0)),
        scratch_shapes=[pltpu.VMEM((2, MOE_TILE * SUB, LANES), F32), pltpu.SemaphoreType.DMA((2,))],
    )
    return pl.pallas_call(
        _moe_kernel,
        grid_spec=grid_spec,
        out_shape=jax.ShapeDtypeStruct((p_pad * SUB, LANES), F32),
        compiler_params=_cparams(("arbitrary",)),
        name="moe",
    )(tile_expert, n_tiles_used, idx, idx, h2, wg, wu, wd)


COMBINE_TILE = 256


def _combine_kernel(pos_ref, posn_ref, x1_ref, route_ref, g2_ref, y_hbm, o_ref, ybuf, sem):
    i = pl.program_id(0)
    t = COMBINE_TILE
    slot = _gather_pipeline(i, pl.num_programs(0), pos_ref, posn_ref, 2 * t, y_hbm, ybuf, sem)
    route = route_ref[...]
    w1 = route[:, 2:3]
    w2 = route[:, 3:4]
    ya = _load_token_major(ybuf, slot, 0, t)
    yb = _load_token_major(ybuf, slot, t, t)
    o_ref[...] = x1_ref[...] + g2_ref[0] * (w1 * ya + w2 * yb)


def _combine_call(x1, route, pos2, y, mod3, mod_stride, seq_len):
    n = x1.shape[0]
    t = COMBINE_TILE
    n_tiles = n // t
    pos_tiles = pos2.reshape(n_tiles, t, 2).transpose(0, 2, 1).reshape(n_tiles, 1, 2 * t)
    smem_tile = lambda m: pl.BlockSpec((1, 1, 2 * t), m, memory_space=pltpu.SMEM)
    return pl.pallas_call(
        _combine_kernel,
        grid=(n_tiles,),
        in_specs=[smem_tile(lambda i: (i, 0, 0)),
                  smem_tile(lambda i: (jnp.minimum(i + 1, n_tiles - 1), 0, 0)),
                  pl.BlockSpec((t, D_MODEL), lambda i: (i, 0)),
                  pl.BlockSpec((t, LANES), lambda i: (i, 0)),
                  pl.BlockSpec((1, 1, D_MODEL), lambda i: ((i * t) // seq_len * mod_stride + 5, 0, 0)),
                  pl.BlockSpec(memory_space=pl.ANY)],
        out_specs=pl.BlockSpec((t, D_MODEL), lambda i: (i, 0)),
        out_shape=jax.ShapeDtypeStruct((n, D_MODEL), F32),
        scratch_shapes=[pltpu.VMEM((2, 2 * t * SUB, LANES), F32), pltpu.SemaphoreType.DMA((2,))],
        compiler_params=_cparams(("arbitrary",)),
        name="combine",
    )(pos_tiles, pos_tiles, x1, route, mod3, y)


def _inverse_map_kernel(pos_ref, init_hbm, out_ref):
    pltpu.sync_copy(init_hbm, out_ref)

    def body(p, carry):
        out_ref[pos_ref[p]] = lax.shift_right_logical(p, 1)
        return carry

    lax.fori_loop(0, pos_ref.shape[0], body, 0, unroll=8)


def _inverse_map_call(pos, init):
    return pl.pallas_call(
        _inverse_map_kernel,
        in_specs=[pl.BlockSpec(memory_space=pltpu.SMEM), pl.BlockSpec(memory_space=pl.ANY)],
        out_specs=pl.BlockSpec(memory_space=pltpu.SMEM),
        out_shape=jax.ShapeDtypeStruct(init.shape, jnp.int32),
        name="inverse_map",
    )(pos, init)


def _moe_forward(h2, route, counts, wg, wu, wd):
    n = route.shape[0]
    eid = route[:, :2].astype(jnp.int32).reshape(-1)
    rank = route[:, 4:6].astype(jnp.int32).reshape(-1)
    counts = counts[0, :N_EXPERTS].astype(jnp.int32)
    padded = ((counts + MOE_TILE - 1) // MOE_TILE) * MOE_TILE
    seg_end = jnp.cumsum(padded)
    seg_start = seg_end - padded
    pos = seg_start[eid] + rank
    p_pad = 2 * n + N_EXPERTS * MOE_TILE
    src_tok = _inverse_map_call(pos, jnp.arange(p_pad, dtype=jnp.int32) % n)
    n_tiles = p_pad // MOE_TILE
    tile_start = jnp.arange(n_tiles, dtype=jnp.int32) * MOE_TILE
    tile_expert = jnp.minimum(jnp.sum((tile_start[:, None] >= seg_end[None, :]).astype(jnp.int32), axis=1),
                              N_EXPERTS - 1)
    n_used = (seg_end[-1] // MOE_TILE).astype(jnp.int32).reshape(1)
    y = _moe_call(h2, src_tok, tile_expert, n_used, wg, wu, wd)
    return y, pos.reshape(n, 2)


def _prep_weights(norm1_w, norm2_w, w_in, conv_w, a_log, dt_bias, q_norm_w, k_norm_w, o_norm_w,
                  w_proj_a, w_proj_b, w_out, w_router_g, b_router_g, w_router_e, b_router_e,
                  w_exp_gate, w_exp_up, w_exp_down):
    w = {}
    w["norm1_w"] = norm1_w.reshape(1, D_MODEL)
    w["norm2_w"] = norm2_w.reshape(1, D_MODEL)
    o = 0
    w["wa"] = w_in[:, o:o + 3 * W_A].astype(BF16); o += 3 * W_A
    w["wb"] = w_in[:, o:o + 3 * W_B].astype(BF16); o += 3 * W_B
    w["wz"] = w_in[:, o:o + W_B].astype(BF16); o += W_B
    small = w_in[:, o:o + 4 * H_B]; o += 4 * H_B
    w["wg"] = w_in[:, o:o + 2 * D_MODEL].astype(BF16)
    def group_lanes(t):
        r = t.shape[0]
        t = t.reshape(r, -1, N_HG, HP).transpose(0, 2, 1, 3).reshape(r, N_HG, -1)
        return jnp.pad(t, ((0, 0), (0, 0), (0, LANES - t.shape[-1]))).reshape(r, N_HG * LANES)

    w["we"] = group_lanes(small).astype(BF16)
    w["alog_l"] = group_lanes(a_log.reshape(1, 2 * H_B))
    w["dtb_l"] = group_lanes(dt_bias.reshape(1, 2 * H_B))
    w["qnw"] = jnp.tile(q_norm_w, H_A).reshape(1, W_A)
    w["knw"] = jnp.tile(k_norm_w, H_A).reshape(1, W_A)
    w["onw"] = jnp.tile(o_norm_w, H_B).reshape(1, W_B)
    blk = np.arange(256) // D_HA
    w["bd256"] = jnp.asarray((blk[:, None] == blk[None, :]).astype(np.float32)).astype(BF16)
    w["conv_w"] = conv_w
    w["wpa"] = w_proj_a.astype(BF16)
    w["wpb"] = w_proj_b.astype(BF16)
    w["wo"] = w_out.astype(BF16)
    wr = jnp.concatenate([w_router_g, w_router_e], axis=1)
    wr = jnp.pad(wr, ((0, 0), (0, LANES - wr.shape[1])))
    wr_hi = wr.astype(BF16)
    wr_lo = (wr - wr_hi.astype(F32)).astype(BF16)
    w["wr3"] = jnp.concatenate([wr_hi, wr_hi, wr_lo], axis=0)
    br = jnp.concatenate([b_router_g, b_router_e])
    w["br"] = jnp.pad(br, (0, LANES - br.shape[0])).reshape(1, LANES)
    w["wg_e"] = w_exp_gate.astype(BF16)
    w["wu_e"] = w_exp_up.astype(BF16)
    w["wd_e"] = w_exp_down.astype(BF16)
    return w


def _layer(x, mod3, mod_stride, w, ctx):
    B, L, _ = x.shape
    latent = ctx is not None
    qa, ka, va, qkvb, z, gate, ecol, *new_kv = _inproj_call(x, mod3, mod_stride, w, not latent)
    if latent:
        k_ctx, v_ctx, s0f, s0b, bias = ctx
        oa = _nbr_attn_call(qa, ka, va, k_ctx, v_ctx, bias)
    else:
        s0f = s0b = jnp.zeros((B, H_B, D_K, D_V), F32)
        oa = _ctx_attn_call(qa, ka, va)
    qd, kd, vd = _conv_call(qkvb, w["conv_w"], rope=latent)
    of, ob, s_f, s_b = _delta_call(qd, kd, vd, ecol, s0f, s0b)
    x1, h2, route, counts = _merge_call(x, oa, of, ob, z, gate, mod3, mod_stride, w)
    ys, pos2 = _moe_forward(h2, route, counts, w["wg_e"], w["wu_e"], w["wd_e"])
    y = _combine_call(x1, route, pos2, ys, mod3, mod_stride, L)
    return y.reshape(B, L, D_MODEL), new_kv, s_f, s_b


def kernel(x_prompt, x_sample, c, cache_attn_k, cache_attn_v, state_delta_fwd, state_delta_bwd, c_ctx,
           norm1_w, norm2_w, w_mod, b_mod, w_in, conv_w, a_log, dt_bias, q_norm_w, k_norm_w, rpb,
           o_norm_w, w_proj_a, w_proj_b, w_out, w_router_g, b_router_g, w_router_e, b_router_e,
           w_exp_gate, w_exp_up, w_exp_down):
    assert norm1_w.shape[0] == 1, "single-layer trunk"
    Bp, Lp, _ = x_prompt.shape
    Bs, Ls, _ = x_sample.shape
    l = 0
    w = _prep_weights(norm1_w[l], norm2_w[l], w_in[l], conv_w[l], a_log[l], dt_bias[l], q_norm_w[l],
                      k_norm_w[l], o_norm_w[l], w_proj_a[l], w_proj_b[l], w_out[l], w_router_g[l],
                      b_router_g[l], w_router_e[l], b_router_e[l], w_exp_gate[l], w_exp_up[l],
                      w_exp_down[l])
    cond = jnp.concatenate([c_ctx[None, :], c, jnp.zeros((8 - 1 - Bs, D_MODEL), F32)], axis=0)
    mod = _mod_call(cond, w_mod[l], b_mod[l])
    mod3 = mod.reshape(8 * 6, 1, D_MODEL)
    bias = _bias_table_call(rpb[l])
    y_p, (new_k, new_v), s_f, s_b = _layer(x_prompt, mod3[:6], 0, w, None)
    lc = cache_attn_k.shape[2]
    ctx = (cache_attn_k[:, l].reshape(Bs, lc, W_A).astype(BF16),
           cache_attn_v[:, l].reshape(Bs, lc, W_A).astype(BF16),
           state_delta_fwd[:, l], state_delta_bwd[:, l], bias)
    y_s, _, _, _ = _layer(x_sample, mod3[6:6 + 6 * Bs], 6, w, ctx)
    return (y_p, y_s, new_k, new_v, s_f[:, None], s_b[:, None])
```

```python
import functools
import math

import numpy as np
import jax
import jax.numpy as jnp
from jax import lax
from jax.experimental import pallas as pl
from jax.experimental.pallas import tpu as pltpu

F32 = jnp.float32
BF16 = jnp.bfloat16

D_MODEL = 1024
GRID_W = 64
H_A = 8
D_HA = 64
W_A = H_A * D_HA
WIN_H = 8
WIN_W = 16
H_B = 4
D_K = 128
D_V = 128
W_B = H_B * D_V
CONV_K = 5
CHUNK = 64
ROPE_BASE = 10000.0
N_GROUPS = 4
EXP_PER_GROUP = 8
N_EXPERTS = N_GROUPS * EXP_PER_GROUP
D_FF_E = 256
EPS = 1e-6
NEG_INF = -1e30

LANES = 128
HP = H_B
N_HG = H_B // HP
ROW_TILE = 512
INPROJ_TILE = 256
MERGE_PART = 256
MOE_TILE = 256
VMEM_LIMIT = 56 * 1024 * 1024

HIGHEST = lax.Precision.HIGHEST


def _cparams(sem):
    return pltpu.CompilerParams(dimension_semantics=sem, vmem_limit_bytes=VMEM_LIMIT)


def _dot(a, b):
    return jnp.dot(a, b, preferred_element_type=F32)


def _dot_nt(a, b):
    return lax.dot_general(a, b, (((1,), (1,)), ((), ())), preferred_element_type=F32)


def _dot_tn(a, b):
    return lax.dot_general(a, b, (((0,), (0,)), ((), ())), preferred_element_type=F32)


def _const_spec(shape):
    nd = len(shape)
    return pl.BlockSpec(shape, lambda *_: (0,) * nd)


def _mod_kernel(c_ref, w_ref, b_ref, o_ref):
    c = c_ref[...]
    s = c * jax.nn.sigmoid(c)
    o_ref[...] = jnp.dot(s, w_ref[...], preferred_element_type=F32, precision=HIGHEST) + b_ref[...]


def _mod_call(cond8, w_mod, b_mod):
    n_out = w_mod.shape[1]
    tn = 1024
    return pl.pallas_call(
        _mod_kernel,
        grid=(n_out // tn,),
        in_specs=[pl.BlockSpec((8, D_MODEL), lambda j: (0, 0)),
                  pl.BlockSpec((D_MODEL, tn), lambda j: (0, j)),
                  pl.BlockSpec((1, tn), lambda j: (0, j))],
        out_specs=pl.BlockSpec((8, tn), lambda j: (0, j)),
        out_shape=jax.ShapeDtypeStruct((8, n_out), F32),
        compiler_params=_cparams(("arbitrary",)),
        name="mod",
    )(cond8, w_mod, b_mod.reshape(1, n_out))


def _softplus(x):
    return jnp.maximum(x, 0.0) + jnp.log1p(jnp.exp(-jnp.abs(x)))


def _inproj_kernel(x_ref, sh_ref, sc_ref, nw_ref, wa_ref, wb_ref, wz_ref, wg_ref, we_ref,
                   qnw_ref, knw_ref, bd_ref, alog_ref, dtb_ref, tri_ref,
                   qa_ref, ka_ref, va_ref, qkvb_ref, z_ref, gate_ref, e_ref, *kv5_refs):
    x = x_ref[...]
    ms = jnp.mean(x * x, axis=-1, keepdims=True)
    h = x * lax.rsqrt(ms + EPS) * nw_ref[...]
    h = h * (1.0 + sc_ref[0]) + sh_ref[0]
    hb = h.astype(BF16)

    e = _dot(hb, we_ref[...])
    a = _dot(hb, wa_ref[...])
    qkvb_ref[...] = _dot(hb, wb_ref[...]).astype(qkvb_ref.dtype)
    bd = bd_ref[...]

    def head_rms(t, w):
        sq = (t * t).astype(BF16)
        ss = jnp.concatenate([_dot(sq[:, i * 256:(i + 1) * 256], bd) for i in range(W_A // 256)], axis=-1)
        return t * lax.rsqrt(ss * (1.0 / D_HA) + EPS) * w

    qa = head_rms(a[:, :W_A], qnw_ref[...]) * (D_HA ** -0.5)
    qa_ref[...] = qa.astype(qa_ref.dtype)
    ka = head_rms(a[:, W_A:2 * W_A], knw_ref[...])
    va = a[:, 2 * W_A:]
    ka_ref[...] = ka.astype(ka_ref.dtype)
    va_ref[...] = va.astype(va_ref.dtype)
    if kv5_refs:
        for ref, t in zip(kv5_refs, (ka, va)):
            for hd in range(H_A):
                ref[0, 0, :, hd, :] = t[:, hd * D_HA:(hd + 1) * D_HA]

    z = _dot(hb, wz_ref[...])
    z_ref[...] = (z * jax.nn.sigmoid(z)).astype(z_ref.dtype)

    lane = lax.broadcasted_iota(jnp.int32, e.shape, 1) % LANES
    g = -jnp.exp(alog_ref[...]) * _softplus(e + dtb_ref[...])
    act = jnp.where(lane < 2 * HP, g, jax.nn.sigmoid(e))
    a_hi = act.astype(BF16)
    rem = act - a_hi.astype(F32)
    a_mid = rem.astype(BF16)
    a_lo = (rem - a_mid.astype(F32)).astype(BF16)
    tri = tri_ref[...]
    cum = _dot(tri, a_hi) + _dot(tri, a_mid) + _dot(tri, a_lo)
    tm = act.shape[0]
    e_ref[...] = jnp.where(lane < HP, cum[:tm], jnp.where(lane < 2 * HP, cum[tm:], act))
    gate_ref[...] = jax.nn.sigmoid(_dot(hb, wg_ref[...])).astype(gate_ref.dtype)


def _inproj_call(x, mod3, mod_stride, w, new_cache):
    B, L, _ = x.shape
    n_tok = B * L
    tm = INPROJ_TILE
    assert n_tok % tm == 0 and L % tm == 0
    row = lambda n, dt: jax.ShapeDtypeStruct((n_tok, n), dt)
    rspec = lambda n: pl.BlockSpec((tm, n), lambda i: (i, 0))
    mspec = lambda k: pl.BlockSpec((1, 1, D_MODEL), lambda i: ((i * tm) // L * mod_stride + k, 0, 0))
    t = np.arange(tm)
    same = (t[:, None] // CHUNK) == (t[None, :] // CHUNK)
    tri = np.concatenate([same & (t[:, None] >= t[None, :]), same & (t[:, None] <= t[None, :])], axis=0)
    consts = [w["norm1_w"], w["wa"], w["wb"], w["wz"], w["wg"], w["we"], w["qnw"], w["knw"],
              w["bd256"], w["alog_l"], w["dtb_l"], jnp.asarray(tri.astype(np.float32)).astype(BF16)]
    out_specs = [rspec(W_A), rspec(W_A), rspec(W_A), rspec(3 * W_B), rspec(W_B),
                 rspec(2 * D_MODEL), rspec(N_HG * LANES)]
    out_shape = [row(W_A, BF16), row(W_A, BF16), row(W_A, BF16), row(3 * W_B, BF16),
                 row(W_B, BF16), row(2 * D_MODEL, BF16), row(N_HG * LANES, F32)]
    if new_cache:
        per_seq = L // tm
        out_specs += [pl.BlockSpec((1, 1, tm, H_A, D_HA), lambda i: (i // per_seq, 0, i % per_seq, 0, 0))] * 2
        out_shape += [jax.ShapeDtypeStruct((B, 1, L, H_A, D_HA), F32)] * 2
    outs = pl.pallas_call(
        _inproj_kernel,
        grid=(n_tok // tm,),
        in_specs=[rspec(D_MODEL), mspec(0), mspec(1)] + [_const_spec(c.shape) for c in consts],
        out_specs=out_specs,
        out_shape=out_shape,
        compiler_params=_cparams(("parallel",)),
        name="inproj",
    )(x.reshape(n_tok, D_MODEL), mod3, mod3, *consts)
    return [o.reshape(B, L, o.shape[-1]) for o in outs[:7]] + list(outs[7:])


def _pair_softmax_pv(s_parts, v_parts):
    m = s_parts[0].max(axis=-1, keepdims=True)
    for s in s_parts[1:]:
        m = jnp.maximum(m, s.max(axis=-1, keepdims=True))
    den = 0.0
    acc = 0.0
    for s, v in zip(s_parts, v_parts):
        p = jnp.exp(s - m)
        den = den + p.sum(axis=-1, keepdims=True)
        acc = acc + _dot(p.astype(BF16), v)
    return acc / den


def _stack_pair(qp):
    lane = lax.broadcasted_iota(jnp.int32, qp.shape, 1)
    zero = jnp.zeros_like(qp)
    return jnp.concatenate([jnp.where(lane < D_HA, qp, zero), jnp.where(lane >= D_HA, qp, zero)], axis=0)


def _unstack_pair(o2):
    t = o2.shape[0] // 2
    lane = lax.broadcasted_iota(jnp.int32, (t, LANES), 1)
    return jnp.where(lane < D_HA, o2[:t], o2[t:])


def _ctx_attn_kernel(q_ref, k_ref, v_ref, o_ref):
    for p in range(H_A // 2):
        sl = slice(p * LANES, (p + 1) * LANES)
        qs = _stack_pair(q_ref[0, :, sl])
        kp = k_ref[0, :, sl].astype(BF16)
        vp = v_ref[0, :, sl].astype(BF16)
        o2 = _pair_softmax_pv([_dot_nt(qs, kp)], [vp])
        o_ref[0, :, sl] = _unstack_pair(o2).astype(o_ref.dtype)


def _ctx_attn_call(q, k, v):
    B, L, _ = q.shape
    spec = pl.BlockSpec((1, L, W_A), lambda b: (b, 0, 0))
    return pl.pallas_call(
        _ctx_attn_kernel,
        grid=(B,),
        in_specs=[spec, spec, spec],
        out_specs=spec,
        out_shape=jax.ShapeDtypeStruct((B, L, W_A), BF16),
        compiler_params=_cparams(("parallel",)),
        name="ctx_attn",
    )(q, k, v)


N_DR = 2 * WIN_H - 1
N_DC = 2 * WIN_W - 1
NBR_ROWS = 4


def _bias_table_kernel(rpb_ref, o_ref):
    h = pl.program_id(0)
    c = lax.broadcasted_iota(jnp.int32, (GRID_W, 2 * GRID_W), 0)
    lane = lax.broadcasted_iota(jnp.int32, (GRID_W, 2 * GRID_W), 1)
    kc = lane % GRID_W
    ws = jnp.clip(c - WIN_W // 2, 0, GRID_W - WIN_W)
    in_win = (kc >= ws) & (kc < ws + WIN_W)
    dc = kc - c + (WIN_W - 1)
    for dr in range(N_DR - 1):
        base = (h * N_DR + dr) * N_DC
        val = jnp.zeros((GRID_W, 2 * GRID_W), F32)
        for d in range(N_DC):
            val = jnp.where(dc == d, jnp.where(lane < GRID_W, rpb_ref[base + d], rpb_ref[base + N_DC + d]), val)
        o_ref[0, dr] = jnp.where(in_win, val, NEG_INF)


def _bias_table_call(rpb):
    return pl.pallas_call(
        _bias_table_kernel,
        grid=(H_A,),
        in_specs=[pl.BlockSpec(memory_space=pltpu.SMEM)],
        out_specs=pl.BlockSpec((1, N_DR - 1, GRID_W, 2 * GRID_W), lambda h: (h, 0, 0, 0)),
        out_shape=jax.ShapeDtypeStruct((H_A, N_DR - 1, GRID_W, 2 * GRID_W), F32),
        compiler_params=_cparams(("arbitrary",)),
        name="bias_table",
    )(rpb.reshape(-1))


def _nbr_attn_kernel(q_ref, k_ref, v_ref, kc_ref, vc_ref, bias_ref, o_ref, *, rows):
    inst = []
    for rr in range(NBR_ROWS):
        r = pl.program_id(1) * NBR_ROWS + rr
        first = jnp.clip(r - WIN_H // 2, 0, rows - WIN_H)
        start = pl.multiple_of(first * GRID_W, GRID_W)
        shift = r - first
        for p in range(H_A // 2):
            inst.append((rr, p, start, shift))
    scores = []
    for rr, p, start, shift in inst:
        sl = slice(p * LANES, (p + 1) * LANES)
        qs = _stack_pair(q_ref[0, rr * GRID_W:(rr + 1) * GRID_W, sl])
        bias = jnp.concatenate(
            [jnp.concatenate([bias_ref[2 * p + hh, 2 * u - shift + WIN_H - 1] for u in range(WIN_H // 2)], axis=1)
             for hh in range(2)], axis=0)
        s_loc = _dot_nt(qs, k_ref[0, pl.ds(start, WIN_H * GRID_W), sl]) + bias
        s_ctx = _dot_nt(qs, kc_ref[0, :, sl])
        scores.append((s_loc, s_ctx))
    for (rr, p, start, shift), (s_loc, s_ctx) in zip(inst, scores):
        sl = slice(p * LANES, (p + 1) * LANES)
        o2 = _pair_softmax_pv([s_loc, s_ctx], [v_ref[0, pl.ds(start, WIN_H * GRID_W), sl], vc_ref[0, :, sl]])
        o_ref[0, rr * GRID_W:(rr + 1) * GRID_W, sl] = _unstack_pair(o2).astype(o_ref.dtype)


def _nbr_attn_call(q, k, v, k_ctx, v_ctx, bias):
    B, L, _ = q.shape
    rows = L // GRID_W
    lc = k_ctx.shape[1]
    full = pl.BlockSpec((1, L, W_A), lambda b, r: (b, 0, 0))
    ctx = pl.BlockSpec((1, lc, W_A), lambda b, r: (b, 0, 0))
    rowspec = pl.BlockSpec((1, NBR_ROWS * GRID_W, W_A), lambda b, r: (b, r, 0))
    return pl.pallas_call(
        functools.partial(_nbr_attn_kernel, rows=rows),
        grid=(B, rows // NBR_ROWS),
        in_specs=[rowspec, full, full, ctx, ctx, _const_spec(bias.shape)],
        out_specs=rowspec,
        out_shape=jax.ShapeDtypeStruct((B, L, W_A), BF16),
        compiler_params=_cparams(("parallel", "arbitrary")),
        name="nbr_attn",
    )(q, k, v, k_ctx, v_ctx, bias)


HALO = 16


def _conv_kernel(*refs, rope):
    if rope:
        x_ref, prev_ref, next_ref, w_ref, cos_ref, sin_ref, q_ref, k_ref, v_ref, xe_ref = refs
    else:
        x_ref, prev_ref, next_ref, w_ref, q_ref, k_ref, v_ref, xe_ref = refs
    i = pl.program_id(1)
    n_i = pl.num_programs(1)
    tl = x_ref.shape[1]
    xe_ref[0:HALO] = jnp.where(i > 0, prev_ref[0].astype(F32), 0.0)
    xe_ref[HALO:HALO + tl] = x_ref[0].astype(F32)
    xe_ref[HALO + tl:] = jnp.where(i < n_i - 1, next_ref[0].astype(F32), 0.0)
    half = CONV_K // 2
    y = w_ref[0:1, :] * xe_ref[HALO - half:HALO - half + tl]
    for j in range(1, CONV_K):
        y = y + w_ref[j:j + 1, :] * xe_ref[HALO - half + j:HALO - half + j + tl]
    y = y * jax.nn.sigmoid(y)

    def l2n(t):
        return t * lax.rsqrt(jnp.sum(t * t, axis=-1, keepdims=True) + EPS)

    def rot(t):
        if not rope:
            return t
        lane = lax.broadcasted_iota(jnp.int32, t.shape, 1)
        swapped = jnp.where(lane % 2 == 0, pltpu.roll(t, LANES - 1, 1), pltpu.roll(t, 1, 1))
        return t * cos_ref[...] + swapped * sin_ref[...]

    for h in range(H_B):
        sl = slice(h * D_K, (h + 1) * D_K)
        q_ref[0, :, sl] = (rot(l2n(y[:, sl])) * (D_K ** -0.5)).astype(q_ref.dtype)
        k_ref[0, :, sl] = rot(l2n(y[:, W_B + h * D_K:W_B + (h + 1) * D_K])).astype(k_ref.dtype)
    v_ref[0] = y[:, 2 * W_B:].astype(v_ref.dtype)


def _rope_tables(L):
    pos = jnp.arange(L)
    rows = (pos // GRID_W).astype(F32)
    cols = (pos % GRID_W).astype(F32)
    n_pairs = D_K // 4
    inv = ROPE_BASE ** (-jnp.arange(n_pairs, dtype=F32) / n_pairs)
    ang = jnp.concatenate([rows[:, None] * inv, cols[:, None] * inv], axis=-1)
    cos = jnp.repeat(jnp.cos(ang), 2, axis=-1)
    sin = jnp.repeat(jnp.sin(ang), 2, axis=-1)
    sign = jnp.asarray(np.tile(np.array([-1.0, 1.0], np.float32), D_K // 2))
    return cos, sin * sign


def _conv_call(x, conv_w, rope):
    B, L, C = x.shape
    tl = min(ROW_TILE, L)
    nh = tl // HALO
    n_halo = L // HALO
    cur = pl.BlockSpec((1, tl, C), lambda b, i: (b, i, 0))
    prev = pl.BlockSpec((1, HALO, C), lambda b, i: (b, jnp.maximum(i * nh - 1, 0), 0))
    nxt = pl.BlockSpec((1, HALO, C), lambda b, i: (b, jnp.minimum((i + 1) * nh, n_halo - 1), 0))
    ins = [x, x, x, conv_w]
    specs = [cur, prev, nxt, _const_spec(conv_w.shape)]
    if rope:
        cos, sin = _rope_tables(L)
        ins += [cos, sin]
        specs += [pl.BlockSpec((tl, D_K), lambda b, i: (i, 0))] * 2
    out = pl.BlockSpec((1, tl, W_B), lambda b, i: (b, i, 0))
    return pl.pallas_call(
        functools.partial(_conv_kernel, rope=rope),
        grid=(B, L // tl),
        in_specs=specs,
        out_specs=[out, out, out],
        out_shape=[jax.ShapeDtypeStruct((B, L, W_B), BF16)] * 3,
        scratch_shapes=[pltpu.VMEM((tl + 2 * HALO, C), F32)],
        compiler_params=_cparams(("parallel", "parallel")),
        name="conv",
    )(*ins)


DELTA_CT = 4
C4 = H_B * CHUNK


def _blockdiag_rows(x, n_blk):
    w = x.shape[1] // n_blk
    lane_blk = lax.broadcasted_iota(jnp.int32, x.shape, 1) // w
    return jnp.concatenate([jnp.where(lane_blk == b, x, 0.0) for b in range(n_blk)], axis=0).astype(BF16)


def _widen(ec, col0, width):
    blk = lax.broadcasted_iota(jnp.int32, (ec.shape[0], H_B * width), 1) // width
    out = ec[:, col0 + H_B - 1:col0 + H_B]
    for h in range(H_B - 2, -1, -1):
        out = jnp.where(blk == h, ec[:, col0 + h:col0 + h + 1], out)
    return out


def _delta_prep(groups):
    c = CHUNK
    row = lax.broadcasted_iota(jnp.int32, (c, C4), 0)
    coll = lax.broadcasted_iota(jnp.int32, (c, C4), 1) % c
    eye = jnp.where(row == coll, 1.0, 0.0)
    st = []
    for q, k, v, ec, gr, backward in groups:
        gi = H_B if backward else 0
        bi = 3 * H_B if backward else 2 * H_B
        gc64, gc128 = _widen(ec, gi, c), _widen(ec, gi, D_K)
        beta64, beta128 = _widen(ec, bi, c), _widen(ec, bi, D_K)
        incl = (row <= coll) if backward else (row >= coll)
        strict = (row < coll) if backward else (row > coll)
        decay = jnp.where(incl, jnp.exp(jnp.where(incl, gc64 - gr, 0.0)), 0.0)
        kf = k.astype(F32)
        qf = q.astype(F32)
        eg = jnp.exp(gc128)
        g_last = gc128[0:1, :] if backward else gc128[c - 1:c, :]
        st.append(dict(
            kf=kf, qf=qf, decay=decay, strict=strict, beta64=beta64,
            qg=(qf * eg).astype(BF16),
            kg=(kf * jnp.exp(g_last - gc128)).astype(BF16),
            rhs_v=(v.astype(F32) * beta128).astype(BF16),
            rhs_k=(kf * (beta128 * eg)).astype(BF16),
            egl=jnp.exp(g_last)))
    lane2 = lax.broadcasted_iota(jnp.int32, (c, 2 * D_K), 1)
    for s in st:
        outs = []
        for p in range(H_B // 2):
            kp = s["kf"][:, p * 2 * D_K:(p + 1) * 2 * D_K]
            qp = s["qf"][:, p * 2 * D_K:(p + 1) * 2 * D_K]
            rmat = jnp.concatenate([jnp.where(lane2 < D_K, kp, 0.0), jnp.where(lane2 >= D_K, kp, 0.0)], axis=0)
            outs.append(_dot_nt(jnp.concatenate([kp, qp], axis=0).astype(BF16), rmat.astype(BF16)))
        kk = jnp.concatenate([o[:c] for o in outs], axis=1)
        qk = jnp.concatenate([o[c:] for o in outs], axis=1)
        s["a"] = jnp.where(s["strict"], kk * s["decay"], 0.0) * s["beta64"]
        s["intra"] = (qk * s["decay"]).astype(BF16)
        s["t"] = eye - jnp.where(row // 2 == coll // 2, s["a"], 0.0)
    b = 2
    while b < c:
        join = (row // (2 * b) == coll // (2 * b)) & (row // b != coll // b)
        for s in st:
            s["tb"] = s["t"].astype(BF16)
            s["tl"] = _dot(s["tb"], _blockdiag_rows(jnp.where(join, s["a"], 0.0), H_B))
        for s in st:
            s["t"] = s["t"] - _dot(s["tl"].astype(BF16), _blockdiag_rows(s["t"], H_B))
        b *= 2
    zero = jnp.zeros((c, 2 * D_K), BF16)
    for s in st:
        tb = s["t"].astype(BF16)
        us, ws = [], []
        for h in range(H_B):
            rhs = jnp.concatenate([s["rhs_v"][:, h * D_V:(h + 1) * D_V], s["rhs_k"][:, h * D_K:(h + 1) * D_K]], axis=1)
            uw = _dot(tb, jnp.concatenate([zero] * h + [rhs] + [zero] * (H_B - 1 - h), axis=0))
            us.append(uw[:, :D_V])
            ws.append(uw[:, D_V:])
        s["u"] = jnp.concatenate(us, axis=1)
        s["w"] = jnp.concatenate(ws, axis=1).astype(BF16)
    return st


def _delta_scan_step(steps, states):
    c = CHUNK
    zero = jnp.zeros((D_K, D_V), BF16)
    mids = []
    for s, S in zip(steps, states):
        w_s, q_s = [], []
        for p in range(H_B // 2):
            sl = slice(p * 2 * D_K, (p + 1) * 2 * D_K)
            lhs = jnp.concatenate([s["w"][:, sl], s["qg"][:, sl]], axis=0)
            sa, sb = S[2 * p].astype(BF16), S[2 * p + 1].astype(BF16)
            sbd = jnp.concatenate([jnp.concatenate([sa, zero], axis=1), jnp.concatenate([zero, sb], axis=1)], axis=0)
            out = _dot(lhs, sbd)
            w_s.append(out[:c])
            q_s.append(out[c:])
        v_new = s["u"] - jnp.concatenate(w_s, axis=1)
        mids.append((v_new, jnp.concatenate(q_s, axis=1)))
    outs, new_states = [], []
    for s, S, (v_new, q_s) in zip(steps, states, mids):
        vnb = v_new.astype(BF16)
        outs.append(q_s + _dot(s["intra"], _blockdiag_rows(v_new, H_B)))
        new_states.append([S[h] * s["egl"][:, h * D_V:(h + 1) * D_V]
                           + _dot_tn(s["kg"][:, h * D_K:(h + 1) * D_K], vnb[:, h * D_V:(h + 1) * D_V])
                           for h in range(H_B)])
    return outs, new_states


def _delta_kernel(qf_ref, kf_ref, vf_ref, ecf_ref, erf_ref, qb_ref, kb_ref, vb_ref, ecb_ref, erb_ref,
                  s0f_ref, s0b_ref, of_ref, ob_ref, sf_ref, sb_ref, s_scr):
    j = pl.program_id(1)
    c = CHUNK

    @pl.when(j == 0)
    def _():
        s_scr[0] = s0f_ref[0]
        s_scr[1] = s0b_ref[0]

    def group(refs, ci, backward):
        q_ref, k_ref, v_ref, ec_ref, er_ref = refs
        rows = slice(ci * c, (ci + 1) * c)
        d = 1 if backward else 0
        return (q_ref[0, rows, :], k_ref[0, rows, :], v_ref[0, rows, :], ec_ref[0, rows, :],
                er_ref[0, ci, d:d + 1, :], backward)

    fwd = (qf_ref, kf_ref, vf_ref, ecf_ref, erf_ref)
    bwd = (qb_ref, kb_ref, vb_ref, ecb_ref, erb_ref)
    prep = _delta_prep([group(fwd, ci, False) for ci in range(DELTA_CT)]
                       + [group(bwd, ci, True) for ci in range(DELTA_CT)])
    states = [[s_scr[d, h] for h in range(H_B)] for d in range(2)]
    for step in range(DELTA_CT):
        cf, cb = step, DELTA_CT - 1 - step
        outs, states = _delta_scan_step([prep[cf], prep[DELTA_CT + cb]], states)
        of_ref[0, cf * c:(cf + 1) * c, :] = outs[0].astype(of_ref.dtype)
        ob_ref[0, cb * c:(cb + 1) * c, :] = outs[1].astype(ob_ref.dtype)
    for d in range(2):
        for h in range(H_B):
            s_scr[d, h] = states[d][h]

    @pl.when(j == pl.num_programs(1) - 1)
    def _():
        sf_ref[0] = s_scr[0]
        sb_ref[0] = s_scr[1]


def _delta_call(q, k, v, ecol, s0f, s0b):
    B, L, _ = q.shape
    n = L // CHUNK
    tl = DELTA_CT * CHUNK
    nb = L // tl
    erow = ecol[..., :2 * H_B].reshape(B, n, CHUNK, 2, H_B).transpose(0, 1, 3, 4, 2).reshape(B, n, 2, C4)
    fmap = lambda b, j: (b, j, 0)
    bmap = lambda b, j: (b, nb - 1 - j, 0)
    seq = lambda m: pl.BlockSpec((1, tl, W_B), m)
    ecs = lambda m: pl.BlockSpec((1, tl, LANES), m)
    ers = lambda m: pl.BlockSpec((1, DELTA_CT, 2, C4), lambda b, j: m(b, j) + (0,))
    st = pl.BlockSpec((1, H_B, D_K, D_V), lambda b, j: (b, 0, 0, 0))
    return pl.pallas_call(
        _delta_kernel,
        grid=(B, nb),
        in_specs=[seq(fmap), seq(fmap), seq(fmap), ecs(fmap), ers(fmap),
                  seq(bmap), seq(bmap), seq(bmap), ecs(bmap), ers(bmap), st, st],
        out_specs=[seq(fmap), seq(bmap), st, st],
        out_shape=[jax.ShapeDtypeStruct((B, L, W_B), BF16)] * 2
        + [jax.ShapeDtypeStruct((B, H_B, D_K, D_V), F32)] * 2,
        scratch_shapes=[pltpu.VMEM((2, H_B, D_K, D_V), F32)],
        compiler_params=_cparams(("parallel", "arbitrary")),
        name="delta",
    )(q, k, v, ecol, erow, q, k, v, ecol, erow, s0f, s0b)


def _merge_kernel(x_ref, oa_ref, of_ref, ob_ref, z_ref, gate_ref, g1_ref, sh2_ref, sc2_ref,
                  onw_ref, wpa_ref, wpb_ref, wo_ref, n2w_ref, wr_ref, br_ref, ltri_ref,
                  x1_ref, h2_ref, route_ref, cnt_ref, cnt_scr):
    tm = x_ref.shape[0]
    pm = ltri_ref.shape[0]
    parts = [dict(rows=slice(p * pm, (p + 1) * pm)) for p in range(tm // pm)]
    for s in parts:
        rows = s["rows"]
        o = of_ref[rows, :].astype(F32) + ob_ref[rows, :].astype(F32)
        heads = []
        for h in range(H_B):
            t = o[:, h * D_V:(h + 1) * D_V]
            heads.append(t * lax.rsqrt(jnp.mean(t * t, axis=-1, keepdims=True) + EPS))
        ob = jnp.concatenate(heads, axis=-1) * onw_ref[...] * z_ref[rows, :].astype(F32)
        s["pa"] = _dot(oa_ref[rows, :], wpa_ref[...])
        s["pb"] = _dot(ob.astype(BF16), wpb_ref[...])
    for s in parts:
        gate = gate_ref[s["rows"], :]
        merged = gate[:, :D_MODEL].astype(F32) * s["pa"] + gate[:, D_MODEL:].astype(F32) * s["pb"]
        s["mix"] = _dot(merged.astype(BF16), wo_ref[...])
    for s in parts:
        rows = s["rows"]
        x1 = x_ref[rows, :] + g1_ref[0] * s["mix"]
        x1_ref[rows, :] = x1
        ms = jnp.mean(x1 * x1, axis=-1, keepdims=True)
        h2 = x1 * lax.rsqrt(ms + EPS) * n2w_ref[...]
        h2 = h2 * (1.0 + sc2_ref[0]) + sh2_ref[0]
        hi = h2.astype(BF16)
        for c in range(ROW_CHUNKS):
            h2_ref[pl.ds(rows.start * ROW_CHUNKS + c, pm, stride=ROW_CHUNKS), :] = h2[:, c * LANES:(c + 1) * LANES]
        lo = (h2 - hi.astype(F32)).astype(BF16)
        s["logits"] = _dot(jnp.concatenate([hi, lo, hi], axis=-1), wr_ref[...]) + br_ref[...]
    lane = lax.broadcasted_iota(jnp.int32, (pm, LANES), 1)
    big = jnp.int32(LANES)

    def first_max(vals, mask):
        mv = jnp.where(mask, vals, NEG_INF)
        m = mv.max(axis=-1, keepdims=True)
        idx = jnp.where(mask & (mv == m), lane, big).min(axis=-1, keepdims=True)
        return m, idx

    for s in parts:
        logits = s["logits"]
        gmask = lane < N_GROUPS
        gm, grp = first_max(logits, gmask)
        p_grp = 1.0 / jnp.where(gmask, jnp.exp(logits - gm), 0.0).sum(axis=-1, keepdims=True)
        lo_lane = N_GROUPS + grp * EXP_PER_GROUP
        emask = (lane >= lo_lane) & (lane < lo_lane + EXP_PER_GROUP)
        em, _ = first_max(logits, emask)
        ex = jnp.where(emask, jnp.exp(logits - em), 0.0)
        probs = ex / ex.sum(axis=-1, keepdims=True)
        p1, i1 = first_max(probs, emask)
        p2, i2 = first_max(probs, emask & (lane != i1))
        tot = p1 + p2
        s["w1"] = p_grp * p1 / tot
        s["w2"] = p_grp * p2 / tot
        s["e1"] = i1 - N_GROUPS
        s["e2"] = i2 - N_GROUPS
        s["hit"] = jnp.where(lane == s["e1"], 1.0, 0.0) + jnp.where(lane == s["e2"], 1.0, 0.0)
        s["within"] = _dot(ltri_ref[...], s["hit"].astype(BF16))

    @pl.when(pl.program_id(0) == 0)
    def _():
        cnt_scr[...] = jnp.zeros_like(cnt_scr)

    running = cnt_scr[...]
    for s in parts:
        before = s["within"] + running
        running = running + s["hit"].sum(axis=0, keepdims=True)
        r1 = jnp.where(lane == s["e1"], before, 0.0).sum(axis=-1, keepdims=True)
        r2 = jnp.where(lane == s["e2"], before, 0.0).sum(axis=-1, keepdims=True)
        cols = [s["e1"].astype(F32), s["e2"].astype(F32), s["w1"], s["w2"], r1, r2]
        route = jnp.zeros((pm, LANES), F32)
        for i, col in enumerate(cols):
            route = jnp.where(lane == i, col, route)
        route_ref[s["rows"], :] = route
    cnt_scr[...] = running
    cnt_ref[...] = running


def _merge_call(x, oa, of, ob, z, gate, mod3, mod_stride, w):
    B, L, _ = x.shape
    n_tok = B * L
    tm = ROW_TILE
    assert n_tok % tm == 0 and (L % tm == 0 or tm % L == 0)
    flat = lambda a: a.reshape(n_tok, a.shape[-1])
    rspec = lambda n: pl.BlockSpec((tm, n), lambda i: (i, 0))
    mspec = lambda k: pl.BlockSpec((1, 1, D_MODEL), lambda i: ((i * tm) // L * mod_stride + k, 0, 0))
    t = np.arange(MERGE_PART)
    ltri = jnp.asarray((t[:, None] > t[None, :]).astype(np.float32)).astype(BF16)
    consts = [w["onw"], w["wpa"], w["wpb"], w["wo"], w["norm2_w"], w["wr3"], w["br"], ltri]
    return pl.pallas_call(
        _merge_kernel,
        grid=(n_tok // tm,),
        in_specs=[rspec(D_MODEL), rspec(W_A), rspec(W_B), rspec(W_B), rspec(W_B), rspec(2 * D_MODEL),
                  mspec(2), mspec(3), mspec(4)] + [_const_spec(c.shape) for c in consts],
        out_specs=[rspec(D_MODEL), pl.BlockSpec((tm * SUB, LANES), lambda i: (i, 0)), rspec(LANES),
                   _const_spec((1, LANES))],
        out_shape=[jax.ShapeDtypeStruct((n_tok, D_MODEL), F32),
                   jax.ShapeDtypeStruct((n_tok * SUB, LANES), F32),
                   jax.ShapeDtypeStruct((n_tok, LANES), F32),
                   jax.ShapeDtypeStruct((1, LANES), F32)],
        scratch_shapes=[pltpu.VMEM((1, LANES), F32)],
        compiler_params=_cparams(("arbitrary",)),
        name="merge",
    )(flat(x), flat(oa), flat(of), flat(ob), flat(z), flat(gate), mod3, mod3, mod3, *consts)


SUB = 8
ROW_CHUNKS = D_MODEL // LANES


def _store_token_major(ref, x):
    for s in range(ROW_CHUNKS):
        ref[pl.ds(s, x.shape[0], stride=ROW_CHUNKS), :] = x[:, s * LANES:(s + 1) * LANES]


def _load_token_major(ref, slot, row0, n_tok):
    return jnp.concatenate([ref[slot, pl.ds(row0 * ROW_CHUNKS + s, n_tok, stride=ROW_CHUNKS), :]
                            for s in range(ROW_CHUNKS)], axis=1)


def _row_gather_start(idx_ref, n_rows, src_hbm, buf, sem, slot):
    for r in range(n_rows):
        src = pl.multiple_of(idx_ref[0, 0, r] * SUB, SUB)
        pltpu.make_async_copy(src_hbm.at[pl.ds(src, SUB), :], buf.at[slot, pl.ds(r * SUB, SUB), :],
                              sem.at[slot]).start(priority=r % 2)


def _row_gather_wait(n_rows, src_hbm, buf, sem, slot):
    for r in range(n_rows):
        pltpu.make_async_copy(src_hbm.at[pl.ds(0, SUB), :], buf.at[slot, pl.ds(r * SUB, SUB), :],
                              sem.at[slot]).wait()


def _gather_pipeline(i, n_steps, idx_ref, idx_next_ref, n_rows, src_hbm, buf, sem):
    slot = lax.rem(i, 2)

    @pl.when(i == 0)
    def _():
        _row_gather_start(idx_ref, n_rows, src_hbm, buf, sem, 0)

    @pl.when(i + 1 < n_steps)
    def _():
        _row_gather_start(idx_next_ref, n_rows, src_hbm, buf, sem, 1 - slot)

    @pl.when(i < n_steps)
    def _():
        _row_gather_wait(n_rows, src_hbm, buf, sem, slot)

    return slot


def _moe_kernel(te_ref, nt_ref, idx_ref, idxn_ref, h2_hbm, wg_ref, wu_ref, wd_ref, y_ref, xbuf, sem):
    i = pl.program_id(0)
    n_used = nt_ref[0]
    slot = _gather_pipeline(i, n_used, idx_ref, idxn_ref, MOE_TILE, h2_hbm, xbuf, sem)

    @pl.when(i < n_used)
    def _():
        x = _load_token_major(xbuf, slot, 0, MOE_TILE).astype(BF16)
        g = _dot(x, wg_ref[0])
        hid = g * jax.nn.sigmoid(g) * _dot(x, wu_ref[0])
        _store_token_major(y_ref, _dot(hid.astype(BF16), wd_ref[0]))

    @pl.when(i >= n_used)
    def _():
        y_ref[...] = jnp.zeros_like(y_ref)


def _moe_call(h2, src_tok, tile_expert, n_tiles_used, wg, wu, wd):
    p_pad = src_tok.shape[0]
    n_tiles = p_pad // MOE_TILE
    idx = src_tok.reshape(n_tiles, 1, MOE_TILE)
    smem_tile = lambda m: pl.BlockSpec((1, 1, MOE_TILE), m, memory_space=pltpu.SMEM)
    grid_spec = pltpu.PrefetchScalarGridSpec(
        num_scalar_prefetch=2,
        grid=(n_tiles,),
        in_specs=[smem_tile(lambda i, te, nt: (i, 0, 0)),
                  smem_tile(lambda i, te, nt: (jnp.minimum(i + 1, n_tiles - 1), 0, 0)),
                  pl.BlockSpec(memory_space=pl.ANY),
                  pl.BlockSpec((1, D_MODEL, D_FF_E), lambda i, te, nt: (te[i], 0, 0)),
                  pl.BlockSpec((1, D_MODEL, D_FF_E), lambda i, te, nt: (te[i], 0, 0)),
                  pl.BlockSpec((1, D_FF_E, D_MODEL), lambda i, te, nt: (te[i], 0, 0))],
        out_specs=pl.BlockSpec((MOE_TILE * SUB, LANES), lambda i, te, nt: (i, 0)),
        scratch_shapes=[pltpu.VMEM((2, MOE_TILE * SUB, LANES), F32), pltpu.SemaphoreType.DMA((2,))],
    )
    return pl.pallas_call(
        _moe_kernel,
        grid_spec=grid_spec,
        out_shape=jax.ShapeDtypeStruct((p_pad * SUB, LANES), F32),
        compiler_params=_cparams(("arbitrary",)),
        name="moe",
    )(tile_expert, n_tiles_used, idx, idx, h2, wg, wu, wd)


COMBINE_TILE = 256


def _combine_kernel(pos_ref, posn_ref, x1_ref, route_ref, g2_ref, y_hbm, o_ref, ybuf, sem):
    i = pl.program_id(0)
    t = COMBINE_TILE
    slot = _gather_pipeline(i, pl.num_programs(0), pos_ref, posn_ref, 2 * t, y_hbm, ybuf, sem)
    route = route_ref[...]
    w1 = route[:, 2:3]
    w2 = route[:, 3:4]
    ya = _load_token_major(ybuf, slot, 0, t)
    yb = _load_token_major(ybuf, slot, t, t)
    o_ref[...] = x1_ref[...] + g2_ref[0] * (w1 * ya + w2 * yb)


def _combine_call(x1, route, pos2, y, mod3, mod_stride, seq_len):
    n = x1.shape[0]
    t = COMBINE_TILE
    n_tiles = n // t
    pos_tiles = pos2.reshape(n_tiles, t, 2).transpose(0, 2, 1).reshape(n_tiles, 1, 2 * t)
    smem_tile = lambda m: pl.BlockSpec((1, 1, 2 * t), m, memory_space=pltpu.SMEM)
    return pl.pallas_call(
        _combine_kernel,
        grid=(n_tiles,),
        in_specs=[smem_tile(lambda i: (i, 0, 0)),
                  smem_tile(lambda i: (jnp.minimum(i + 1, n_tiles - 1), 0, 0)),
                  pl.BlockSpec((t, D_MODEL), lambda i: (i, 0)),
                  pl.BlockSpec((t, LANES), lambda i: (i, 0)),
                  pl.BlockSpec((1, 1, D_MODEL), lambda i: ((i * t) // seq_len * mod_stride + 5, 0, 0)),
                  pl.BlockSpec(memory_space=pl.ANY)],
        out_specs=pl.BlockSpec((t, D_MODEL), lambda i: (i, 0)),
        out_shape=jax.ShapeDtypeStruct((n, D_MODEL), F32),
        scratch_shapes=[pltpu.VMEM((2, 2 * t * SUB, LANES), F32), pltpu.SemaphoreType.DMA((2,))],
        compiler_params=_cparams(("arbitrary",)),
        name="combine",
    )(pos_tiles, pos_tiles, x1, route, mod3, y)


def _inverse_map_kernel(pos_ref, init_hbm, out_ref):
    pltpu.sync_copy(init_hbm, out_ref)

    def body(p, carry):
        out_ref[pos_ref[p]] = lax.shift_right_logical(p, 1)
        return carry

    lax.fori_loop(0, pos_ref.shape[0], body, 0, unroll=8)


def _inverse_map_call(pos, init):
    return pl.pallas_call(
        _inverse_map_kernel,
        in_specs=[pl.BlockSpec(memory_space=pltpu.SMEM), pl.BlockSpec(memory_space=pl.ANY)],
        out_specs=pl.BlockSpec(memory_space=pltpu.SMEM),
        out_shape=jax.ShapeDtypeStruct(init.shape, jnp.int32),
        name="inverse_map",
    )(pos, init)


def _moe_forward(h2, route, counts, wg, wu, wd):
    n = route.shape[0]
    eid = route[:, :2].astype(jnp.int32).reshape(-1)
    rank = route[:, 4:6].astype(jnp.int32).reshape(-1)
    counts = counts[0, :N_EXPERTS].astype(jnp.int32)
    padded = ((counts + MOE_TILE - 1) // MOE_TILE) * MOE_TILE
    seg_end = jnp.cumsum(padded)
    seg_start = seg_end - padded
    pos = seg_start[eid] + rank
    p_pad = 2 * n + N_EXPERTS * MOE_TILE
    src_tok = _inverse_map_call(pos, jnp.arange(p_pad, dtype=jnp.int32) % n)
    n_tiles = p_pad // MOE_TILE
    tile_start = jnp.arange(n_tiles, dtype=jnp.int32) * MOE_TILE
    tile_expert = jnp.minimum(jnp.sum((tile_start[:, None] >= seg_end[None, :]).astype(jnp.int32), axis=1),
                              N_EXPERTS - 1)
    n_used = (seg_end[-1] // MOE_TILE).astype(jnp.int32).reshape(1)
    y = _moe_call(h2, src_tok, tile_expert, n_used, wg, wu, wd)
    return y, pos.reshape(n, 2)


def _prep_weights(norm1_w, norm2_w, w_in, conv_w, a_log, dt_bias, q_norm_w, k_norm_w, o_norm_w,
                  w_proj_a, w_proj_b, w_out, w_router_g, b_router_g, w_router_e, b_router_e,
                  w_exp_gate, w_exp_up, w_exp_down):
    w = {}
    w["norm1_w"] = norm1_w.reshape(1, D_MODEL)
    w["norm2_w"] = norm2_w.reshape(1, D_MODEL)
    o = 0
    w["wa"] = w_in[:, o:o + 3 * W_A].astype(BF16); o += 3 * W_A
    w["wb"] = w_in[:, o:o + 3 * W_B].astype(BF16); o += 3 * W_B
    w["wz"] = w_in[:, o:o + W_B].astype(BF16); o += W_B
    small = w_in[:, o:o + 4 * H_B]; o += 4 * H_B
    w["wg"] = w_in[:, o:o + 2 * D_MODEL].astype(BF16)
    def group_lanes(t):
        r = t.shape[0]
        t = t.reshape(r, -1, N_HG, HP).transpose(0, 2, 1, 3).reshape(r, N_HG, -1)
        return jnp.pad(t, ((0, 0), (0, 0), (0, LANES - t.shape[-1]))).reshape(r, N_HG * LANES)

    w["we"] = group_lanes(small).astype(BF16)
    w["alog_l"] = group_lanes(a_log.reshape(1, 2 * H_B))
    w["dtb_l"] = group_lanes(dt_bias.reshape(1, 2 * H_B))
    w["qnw"] = jnp.tile(q_norm_w, H_A).reshape(1, W_A)
    w["knw"] = jnp.tile(k_norm_w, H_A).reshape(1, W_A)
    w["onw"] = jnp.tile(o_norm_w, H_B).reshape(1, W_B)
    blk = np.arange(256) // D_HA
    w["bd256"] = jnp.asarray((blk[:, None] == blk[None, :]).astype(np.float32)).astype(BF16)
    w["conv_w"] = conv_w
    w["wpa"] = w_proj_a.astype(BF16)
    w["wpb"] = w_proj_b.astype(BF16)
    w["wo"] = w_out.astype(BF16)
    wr = jnp.concatenate([w_router_g, w_router_e], axis=1)
    wr = jnp.pad(wr, ((0, 0), (0, LANES - wr.shape[1])))
    wr_hi = wr.astype(BF16)
    wr_lo = (wr - wr_hi.astype(F32)).astype(BF16)
    w["wr3"] = jnp.concatenate([wr_hi, wr_hi, wr_lo], axis=0)
    br = jnp.concatenate([b_router_g, b_router_e])
    w["br"] = jnp.pad(br, (0, LANES - br.shape[0])).reshape(1, LANES)
    w["wg_e"] = w_exp_gate.astype(BF16)
    w["wu_e"] = w_exp_up.astype(BF16)
    w["wd_e"] = w_exp_down.astype(BF16)
    return w


def _layer(x, mod3, mod_stride, w, ctx):
    B, L, _ = x.shape
    latent = ctx is not None
    qa, ka, va, qkvb, z, gate, ecol, *new_kv = _inproj_call(x, mod3, mod_stride, w, not latent)
    if latent:
        k_ctx, v_ctx, s0f, s0b, bias = ctx
        oa = _nbr_attn_call(qa, ka, va, k_ctx, v_ctx, bias)
    else:
        s0f = s0b = jnp.zeros((B, H_B, D_K, D_V), F32)
        oa = _ctx_attn_call(qa, ka, va)
    qd, kd, vd = _conv_call(qkvb, w["conv_w"], rope=latent)
    of, ob, s_f, s_b = _delta_call(qd, kd, vd, ecol, s0f, s0b)
    x1, h2, route, counts = _merge_call(x, oa, of, ob, z, gate, mod3, mod_stride, w)
    ys, pos2 = _moe_forward(h2, route, counts, w["wg_e"], w["wu_e"], w["wd_e"])
    y = _combine_call(x1, route, pos2, ys, mod3, mod_stride, L)
    return y.reshape(B, L, D_MODEL), new_kv, s_f, s_b


def kernel(x_prompt, x_sample, c, cache_attn_k, cache_attn_v, state_delta_fwd, state_delta_bwd, c_ctx,
           norm1_w, norm2_w, w_mod, b_mod, w_in, conv_w, a_log, dt_bias, q_norm_w, k_norm_w, rpb,
           o_norm_w, w_proj_a, w_proj_b, w_out, w_router_g, b_router_g, w_router_e, b_router_e,
           w_exp_gate, w_exp_up, w_exp_down):
    assert norm1_w.shape[0] == 1, "single-layer trunk"
    Bp, Lp, _ = x_prompt.shape
    Bs, Ls, _ = x_sample.shape
    l = 0
    w = _prep_weights(norm1_w[l], norm2_w[l], w_in[l], conv_w[l], a_log[l], dt_bias[l], q_norm_w[l],
                      k_norm_w[l], o_norm_w[l], w_proj_a[l], w_proj_b[l], w_out[l], w_router_g[l],
                      b_router_g[l], w_router_e[l], b_router_e[l], w_exp_gate[l], w_exp_up[l],
                      w_exp_down[l])
    cond = jnp.concatenate([c_ctx[None, :], c, jnp.zeros((8 - 1 - Bs, D_MODEL), F32)], axis=0)
    mod = _mod_call(cond, w_mod[l], b_mod[l])
    mod3 = mod.reshape(8 * 6, 1, D_MODEL)
    bias = _bias_table_call(rpb[l])
    y_p, (new_k, new_v), s_f, s_b = _layer(x_prompt, mod3[:6], 0, w, None)
    lc = cache_attn_k.shape[2]
    ctx = (cache_attn_k[:, l].reshape(Bs, lc, W_A).astype(BF16),
           cache_attn_v[:, l].reshape(Bs, lc, W_A).astype(BF16),
           state_delta_fwd[:, l], state_delta_bwd[:, l], bias)
    y_s, _, _, _ = _layer(x_sample, mod3[6:6 + 6 * Bs], 6, w, ctx)
    return (y_p, y_s, new_k, new_v, s_f[:, None], s_b[:, None])
```

```python
import functools
import math

import numpy as np
import jax
import jax.numpy as jnp
from jax import lax
from jax.experimental import pallas as pl
from jax.experimental.pallas import tpu as pltpu

F32 = jnp.float32
BF16 = jnp.bfloat16

D_MODEL = 1024
GRID_W = 64
H_A = 8
D_HA = 64
W_A = H_A * D_HA
WIN_H = 8
WIN_W = 16
H_B = 4
D_K = 128
D_V = 128
W_B = H_B * D_V
CONV_K = 5
CHUNK = 64
ROPE_BASE = 10000.0
N_GROUPS = 4
EXP_PER_GROUP = 8
N_EXPERTS = N_GROUPS * EXP_PER_GROUP
D_FF_E = 256
EPS = 1e-6
NEG_INF = -1e30

LANES = 128
HP = H_B
N_HG = H_B // HP
ROW_TILE = 512
INPROJ_TILE = 512
INPROJ_PART = 256
MERGE_PART = 256
MOE_TILE = 256
VMEM_LIMIT = 56 * 1024 * 1024

HIGHEST = lax.Precision.HIGHEST


def _cparams(sem):
    return pltpu.CompilerParams(dimension_semantics=sem, vmem_limit_bytes=VMEM_LIMIT)


def _dot(a, b):
    return jnp.dot(a, b, preferred_element_type=F32)


def _dot_nt(a, b):
    return lax.dot_general(a, b, (((1,), (1,)), ((), ())), preferred_element_type=F32)


def _dot_tn(a, b):
    return lax.dot_general(a, b, (((0,), (0,)), ((), ())), preferred_element_type=F32)


def _const_spec(shape):
    nd = len(shape)
    return pl.BlockSpec(shape, lambda *_: (0,) * nd)


def _mod_kernel(c_ref, w_ref, b_ref, o_ref):
    c = c_ref[...]
    s = c * jax.nn.sigmoid(c)
    o_ref[...] = jnp.dot(s, w_ref[...], preferred_element_type=F32, precision=HIGHEST) + b_ref[...]


def _mod_call(cond8, w_mod, b_mod):
    n_out = w_mod.shape[1]
    tn = 1024
    return pl.pallas_call(
        _mod_kernel,
        grid=(n_out // tn,),
        in_specs=[pl.BlockSpec((8, D_MODEL), lambda j: (0, 0)),
                  pl.BlockSpec((D_MODEL, tn), lambda j: (0, j)),
                  pl.BlockSpec((1, tn), lambda j: (0, j))],
        out_specs=pl.BlockSpec((8, tn), lambda j: (0, j)),
        out_shape=jax.ShapeDtypeStruct((8, n_out), F32),
        compiler_params=_cparams(("arbitrary",)),
        name="mod",
    )(cond8, w_mod, b_mod.reshape(1, n_out))


def _softplus(x):
    return jnp.maximum(x, 0.0) + jnp.log1p(jnp.exp(-jnp.abs(x)))


def _inproj_kernel(x_ref, sh_ref, sc_ref, nw_ref, wa_ref, wb_ref, wz_ref, wg_ref, we_ref,
                   qnw_ref, knw_ref, bd_ref, alog_ref, dtb_ref, tri_ref,
                   qa_ref, ka_ref, va_ref, qkvb_ref, z_ref, gate_ref, e_ref, *kv5_refs):
    tm = x_ref.shape[0]
    pm = tri_ref.shape[1]
    parts = [dict(rows=slice(p * pm, (p + 1) * pm)) for p in range(tm // pm)]
    for s in parts:
        x = x_ref[s["rows"], :]
        ms = jnp.mean(x * x, axis=-1, keepdims=True)
        h = x * lax.rsqrt(ms + EPS) * nw_ref[...]
        h = h * (1.0 + sc_ref[0]) + sh_ref[0]
        s["hb"] = h.astype(BF16)
    for s in parts:
        s["e"] = _dot(s["hb"], we_ref[...])
    for s in parts:
        s["a"] = _dot(s["hb"], wa_ref[...])
    for s in parts:
        qkvb_ref[s["rows"], :] = _dot(s["hb"], wb_ref[...]).astype(qkvb_ref.dtype)
    bd = bd_ref[...]

    def head_rms(t, w):
        sq = (t * t).astype(BF16)
        ss = jnp.concatenate([_dot(sq[:, i * 256:(i + 1) * 256], bd) for i in range(W_A // 256)], axis=-1)
        return t * lax.rsqrt(ss * (1.0 / D_HA) + EPS) * w

    for p, s in enumerate(parts):
        rows, a = s["rows"], s["a"]
        qa = head_rms(a[:, :W_A], qnw_ref[...]) * (D_HA ** -0.5)
        qa_ref[rows, :] = qa.astype(qa_ref.dtype)
        ka = head_rms(a[:, W_A:2 * W_A], knw_ref[...])
        va = a[:, 2 * W_A:]
        ka_ref[rows, :] = ka.astype(ka_ref.dtype)
        va_ref[rows, :] = va.astype(va_ref.dtype)
        if kv5_refs:
            seq_len = kv5_refs[0].shape[2]
            b_in, off = (p * pm) // seq_len, (p * pm) % seq_len
            for ref, t in zip(kv5_refs, (ka, va)):
                for hd in range(H_A):
                    ref[b_in, 0, off:off + pm, hd, :] = t[:, hd * D_HA:(hd + 1) * D_HA]
    for s in parts:
        z = _dot(s["hb"], wz_ref[...])
        z_ref[s["rows"], :] = (z * jax.nn.sigmoid(z)).astype(z_ref.dtype)
    lane = lax.broadcasted_iota(jnp.int32, (pm, N_HG * LANES), 1) % LANES
    tri = tri_ref[...]
    for s in parts:
        e = s["e"]
        g = -jnp.exp(alog_ref[...]) * _softplus(e + dtb_ref[...])
        act = jnp.where(lane < 2 * HP, g, jax.nn.sigmoid(e))
        a_hi = act.astype(BF16)
        rem = act - a_hi.astype(F32)
        a_mid = rem.astype(BF16)
        a_lo = (rem - a_mid.astype(F32)).astype(BF16)
        cum = _dot(tri, a_hi) + _dot(tri, a_mid) + _dot(tri, a_lo)
        e_ref[s["rows"], :] = jnp.where(lane < HP, cum[:pm], jnp.where(lane < 2 * HP, cum[pm:], act))
    for s in parts:
        gate_ref[s["rows"], :] = jax.nn.sigmoid(_dot(s["hb"], wg_ref[...])).astype(gate_ref.dtype)


def _inproj_call(x, mod3, mod_stride, w, new_cache):
    B, L, _ = x.shape
    n_tok = B * L
    tm, pm = INPROJ_TILE, INPROJ_PART
    assert n_tok % tm == 0 and (L % tm == 0 or (tm % L == 0 and mod_stride == 0)) and L % pm == 0
    row = lambda n, dt: jax.ShapeDtypeStruct((n_tok, n), dt)
    rspec = lambda n: pl.BlockSpec((tm, n), lambda i: (i, 0))
    mspec = lambda k: pl.BlockSpec((1, 1, D_MODEL), lambda i: ((i * tm) // L * mod_stride + k, 0, 0))
    t = np.arange(pm)
    same = (t[:, None] // CHUNK) == (t[None, :] // CHUNK)
    tri = np.concatenate([same & (t[:, None] >= t[None, :]), same & (t[:, None] <= t[None, :])], axis=0)
    consts = [w["norm1_w"], w["wa"], w["wb"], w["wz"], w["wg"], w["we"], w["qnw"], w["knw"],
              w["bd256"], w["alog_l"], w["dtb_l"], jnp.asarray(tri.astype(np.float32)).astype(BF16)]
    out_specs = [rspec(W_A), rspec(W_A), rspec(W_A), rspec(3 * W_B), rspec(W_B),
                 rspec(2 * D_MODEL), rspec(N_HG * LANES)]
    out_shape = [row(W_A, BF16), row(W_A, BF16), row(W_A, BF16), row(3 * W_B, BF16),
                 row(W_B, BF16), row(2 * D_MODEL, BF16), row(N_HG * LANES, F32)]
    if new_cache:
        if tm <= L:
            per_seq = L // tm
            cache_spec = pl.BlockSpec((1, 1, tm, H_A, D_HA), lambda i: (i // per_seq, 0, i % per_seq, 0, 0))
        else:
            cache_spec = pl.BlockSpec((tm // L, 1, L, H_A, D_HA), lambda i: (i, 0, 0, 0, 0))
        out_specs += [cache_spec] * 2
        out_shape += [jax.ShapeDtypeStruct((B, 1, L, H_A, D_HA), F32)] * 2
    outs = pl.pallas_call(
        _inproj_kernel,
        grid=(n_tok // tm,),
        in_specs=[rspec(D_MODEL), mspec(0), mspec(1)] + [_const_spec(c.shape) for c in consts],
        out_specs=out_specs,
        out_shape=out_shape,
        compiler_params=_cparams(("parallel",)),
        name="inproj",
    )(x.reshape(n_tok, D_MODEL), mod3, mod3, *consts)
    return [o.reshape(B, L, o.shape[-1]) for o in outs[:7]] + list(outs[7:])


def _pair_softmax_pv(s_parts, v_parts):
    m = s_parts[0].max(axis=-1, keepdims=True)
    for s in s_parts[1:]:
        m = jnp.maximum(m, s.max(axis=-1, keepdims=True))
    den = 0.0
    acc = 0.0
    for s, v in zip(s_parts, v_parts):
        p = jnp.exp(s - m)
        den = den + p.sum(axis=-1, keepdims=True)
        acc = acc + _dot(p.astype(BF16), v)
    return acc / den


def _stack_pair(qp):
    lane = lax.broadcasted_iota(jnp.int32, qp.shape, 1)
    zero = jnp.zeros_like(qp)
    return jnp.concatenate([jnp.where(lane < D_HA, qp, zero), jnp.where(lane >= D_HA, qp, zero)], axis=0)


def _unstack_pair(o2):
    t = o2.shape[0] // 2
    lane = lax.broadcasted_iota(jnp.int32, (t, LANES), 1)
    return jnp.where(lane < D_HA, o2[:t], o2[t:])


def _ctx_attn_kernel(q_ref, k_ref, v_ref, o_ref):
    for p in range(H_A // 2):
        sl = slice(p * LANES, (p + 1) * LANES)
        qs = _stack_pair(q_ref[0, :, sl])
        kp = k_ref[0, :, sl].astype(BF16)
        vp = v_ref[0, :, sl].astype(BF16)
        o2 = _pair_softmax_pv([_dot_nt(qs, kp)], [vp])
        o_ref[0, :, sl] = _unstack_pair(o2).astype(o_ref.dtype)


def _ctx_attn_call(q, k, v):
    B, L, _ = q.shape
    spec = pl.BlockSpec((1, L, W_A), lambda b: (b, 0, 0))
    return pl.pallas_call(
        _ctx_attn_kernel,
        grid=(B,),
        in_specs=[spec, spec, spec],
        out_specs=spec,
        out_shape=jax.ShapeDtypeStruct((B, L, W_A), BF16),
        compiler_params=_cparams(("parallel",)),
        name="ctx_attn",
    )(q, k, v)


N_DR = 2 * WIN_H - 1
N_DC = 2 * WIN_W - 1
NBR_ROWS = 4


def _bias_table_kernel(rpb_ref, o_ref):
    h = pl.program_id(0)
    c = lax.broadcasted_iota(jnp.int32, (GRID_W, 2 * GRID_W), 0)
    lane = lax.broadcasted_iota(jnp.int32, (GRID_W, 2 * GRID_W), 1)
    kc = lane % GRID_W
    ws = jnp.clip(c - WIN_W // 2, 0, GRID_W - WIN_W)
    in_win = (kc >= ws) & (kc < ws + WIN_W)
    dc = kc - c + (WIN_W - 1)
    for dr in range(N_DR - 1):
        base = (h * N_DR + dr) * N_DC
        val = jnp.zeros((GRID_W, 2 * GRID_W), F32)
        for d in range(N_DC):
            val = jnp.where(dc == d, jnp.where(lane < GRID_W, rpb_ref[base + d], rpb_ref[base + N_DC + d]), val)
        o_ref[0, dr] = jnp.where(in_win, val, NEG_INF)


def _bias_table_call(rpb):
    return pl.pallas_call(
        _bias_table_kernel,
        grid=(H_A,),
        in_specs=[pl.BlockSpec(memory_space=pltpu.SMEM)],
        out_specs=pl.BlockSpec((1, N_DR - 1, GRID_W, 2 * GRID_W), lambda h: (h, 0, 0, 0)),
        out_shape=jax.ShapeDtypeStruct((H_A, N_DR - 1, GRID_W, 2 * GRID_W), F32),
        compiler_params=_cparams(("arbitrary",)),
        name="bias_table",
    )(rpb.reshape(-1))


def _nbr_attn_kernel(q_ref, k_ref, v_ref, kc_ref, vc_ref, bias_ref, o_ref, *, rows):
    inst = []
    for rr in range(NBR_ROWS):
        r = pl.program_id(1) * NBR_ROWS + rr
        first = jnp.clip(r - WIN_H // 2, 0, rows - WIN_H)
        start = pl.multiple_of(first * GRID_W, GRID_W)
        shift = r - first
        for p in range(H_A // 2):
            inst.append((rr, p, start, shift))
    scores = []
    for rr, p, start, shift in inst:
        sl = slice(p * LANES, (p + 1) * LANES)
        qs = _stack_pair(q_ref[0, rr * GRID_W:(rr + 1) * GRID_W, sl])
        bias = jnp.concatenate(
            [jnp.concatenate([bias_ref[2 * p + hh, 2 * u - shift + WIN_H - 1] for u in range(WIN_H // 2)], axis=1)
             for hh in range(2)], axis=0)
        s_loc = _dot_nt(qs, k_ref[0, pl.ds(start, WIN_H * GRID_W), sl]) + bias
        s_ctx = _dot_nt(qs, kc_ref[0, :, sl])
        scores.append((s_loc, s_ctx))
    for (rr, p, start, shift), (s_loc, s_ctx) in zip(inst, scores):
        sl = slice(p * LANES, (p + 1) * LANES)
        o2 = _pair_softmax_pv([s_loc, s_ctx], [v_ref[0, pl.ds(start, WIN_H * GRID_W), sl], vc_ref[0, :, sl]])
        o_ref[0, rr * GRID_W:(rr + 1) * GRID_W, sl] = _unstack_pair(o2).astype(o_ref.dtype)


def _nbr_attn_call(q, k, v, k_ctx, v_ctx, bias):
    B, L, _ = q.shape
    rows = L // GRID_W
    lc = k_ctx.shape[1]
    full = pl.BlockSpec((1, L, W_A), lambda b, r: (b, 0, 0))
    ctx = pl.BlockSpec((1, lc, W_A), lambda b, r: (b, 0, 0))
    rowspec = pl.BlockSpec((1, NBR_ROWS * GRID_W, W_A), lambda b, r: (b, r, 0))
    return pl.pallas_call(
        functools.partial(_nbr_attn_kernel, rows=rows),
        grid=(B, rows // NBR_ROWS),
        in_specs=[rowspec, full, full, ctx, ctx, _const_spec(bias.shape)],
        out_specs=rowspec,
        out_shape=jax.ShapeDtypeStruct((B, L, W_A), BF16),
        compiler_params=_cparams(("parallel", "arbitrary")),
        name="nbr_attn",
    )(q, k, v, k_ctx, v_ctx, bias)


HALO = 16


def _conv_kernel(*refs, rope):
    if rope:
        x_ref, prev_ref, next_ref, w_ref, cos_ref, sin_ref, q_ref, k_ref, v_ref, xe_ref = refs
    else:
        x_ref, prev_ref, next_ref, w_ref, q_ref, k_ref, v_ref, xe_ref = refs
    i = pl.program_id(1)
    n_i = pl.num_programs(1)
    tl = x_ref.shape[1]
    xe_ref[0:HALO] = jnp.where(i > 0, prev_ref[0].astype(F32), 0.0)
    xe_ref[HALO:HALO + tl] = x_ref[0].astype(F32)
    xe_ref[HALO + tl:] = jnp.where(i < n_i - 1, next_ref[0].astype(F32), 0.0)
    half = CONV_K // 2
    y = w_ref[0:1, :] * xe_ref[HALO - half:HALO - half + tl]
    for j in range(1, CONV_K):
        y = y + w_ref[j:j + 1, :] * xe_ref[HALO - half + j:HALO - half + j + tl]
    y = y * jax.nn.sigmoid(y)

    def l2n(t):
        return t * lax.rsqrt(jnp.sum(t * t, axis=-1, keepdims=True) + EPS)

    def rot(t):
        if not rope:
            return t
        lane = lax.broadcasted_iota(jnp.int32, t.shape, 1)
        swapped = jnp.where(lane % 2 == 0, pltpu.roll(t, LANES - 1, 1), pltpu.roll(t, 1, 1))
        return t * cos_ref[...] + swapped * sin_ref[...]

    for h in range(H_B):
        sl = slice(h * D_K, (h + 1) * D_K)
        q_ref[0, :, sl] = (rot(l2n(y[:, sl])) * (D_K ** -0.5)).astype(q_ref.dtype)
        k_ref[0, :, sl] = rot(l2n(y[:, W_B + h * D_K:W_B + (h + 1) * D_K])).astype(k_ref.dtype)
    v_ref[0] = y[:, 2 * W_B:].astype(v_ref.dtype)


def _rope_tables(L):
    pos = jnp.arange(L)
    rows = (pos // GRID_W).astype(F32)
    cols = (pos % GRID_W).astype(F32)
    n_pairs = D_K // 4
    inv = ROPE_BASE ** (-jnp.arange(n_pairs, dtype=F32) / n_pairs)
    ang = jnp.concatenate([rows[:, None] * inv, cols[:, None] * inv], axis=-1)
    cos = jnp.repeat(jnp.cos(ang), 2, axis=-1)
    sin = jnp.repeat(jnp.sin(ang), 2, axis=-1)
    sign = jnp.asarray(np.tile(np.array([-1.0, 1.0], np.float32), D_K // 2))
    return cos, sin * sign


def _conv_call(x, conv_w, rope):
    B, L, C = x.shape
    tl = min(ROW_TILE, L)
    nh = tl // HALO
    n_halo = L // HALO
    cur = pl.BlockSpec((1, tl, C), lambda b, i: (b, i, 0))
    prev = pl.BlockSpec((1, HALO, C), lambda b, i: (b, jnp.maximum(i * nh - 1, 0), 0))
    nxt = pl.BlockSpec((1, HALO, C), lambda b, i: (b, jnp.minimum((i + 1) * nh, n_halo - 1), 0))
    ins = [x, x, x, conv_w]
    specs = [cur, prev, nxt, _const_spec(conv_w.shape)]
    if rope:
        cos, sin = _rope_tables(L)
        ins += [cos, sin]
        specs += [pl.BlockSpec((tl, D_K), lambda b, i: (i, 0))] * 2
    out = pl.BlockSpec((1, tl, W_B), lambda b, i: (b, i, 0))
    return pl.pallas_call(
        functools.partial(_conv_kernel, rope=rope),
        grid=(B, L // tl),
        in_specs=specs,
        out_specs=[out, out, out],
        out_shape=[jax.ShapeDtypeStruct((B, L, W_B), BF16)] * 3,
        scratch_shapes=[pltpu.VMEM((tl + 2 * HALO, C), F32)],
        compiler_params=_cparams(("parallel", "parallel")),
        name="conv",
    )(*ins)


DELTA_CT = 4
C4 = H_B * CHUNK


def _blockdiag_rows(x, n_blk):
    w = x.shape[1] // n_blk
    lane_blk = lax.broadcasted_iota(jnp.int32, x.shape, 1) // w
    return jnp.concatenate([jnp.where(lane_blk == b, x, 0.0) for b in range(n_blk)], axis=0).astype(BF16)


def _widen(ec, col0, width):
    blk = lax.broadcasted_iota(jnp.int32, (ec.shape[0], H_B * width), 1) // width
    out = ec[:, col0 + H_B - 1:col0 + H_B]
    for h in range(H_B - 2, -1, -1):
        out = jnp.where(blk == h, ec[:, col0 + h:col0 + h + 1], out)
    return out


def _delta_prep(groups):
    c = CHUNK
    row = lax.broadcasted_iota(jnp.int32, (c, C4), 0)
    coll = lax.broadcasted_iota(jnp.int32, (c, C4), 1) % c
    eye = jnp.where(row == coll, 1.0, 0.0)
    st = []
    for q, k, v, ec, gr, backward in groups:
        gi = H_B if backward else 0
        bi = 3 * H_B if backward else 2 * H_B
        gc64, gc128 = _widen(ec, gi, c), _widen(ec, gi, D_K)
        beta64, beta128 = _widen(ec, bi, c), _widen(ec, bi, D_K)
        incl = (row <= coll) if backward else (row >= coll)
        strict = (row < coll) if backward else (row > coll)
        decay = jnp.where(incl, jnp.exp(jnp.where(incl, gc64 - gr, 0.0)), 0.0)
        kf = k.astype(F32)
        qf = q.astype(F32)
        eg = jnp.exp(gc128)
        g_last = gc128[0:1, :] if backward else gc128[c - 1:c, :]
        st.append(dict(
            kf=kf, qf=qf, decay=decay, strict=strict, beta64=beta64,
            qg=(qf * eg).astype(BF16),
            kg=(kf * jnp.exp(g_last - gc128)).astype(BF16),
            rhs_v=(v.astype(F32) * beta128).astype(BF16),
            rhs_k=(kf * (beta128 * eg)).astype(BF16),
            egl=jnp.exp(g_last)))
    lane2 = lax.broadcasted_iota(jnp.int32, (c, 2 * D_K), 1)
    for s in st:
        outs = []
        for p in range(H_B // 2):
            kp = s["kf"][:, p * 2 * D_K:(p + 1) * 2 * D_K]
            qp = s["qf"][:, p * 2 * D_K:(p + 1) * 2 * D_K]
            rmat = jnp.concatenate([jnp.where(lane2 < D_K, kp, 0.0), jnp.where(lane2 >= D_K, kp, 0.0)], axis=0)
            outs.append(_dot_nt(jnp.concatenate([kp, qp], axis=0).astype(BF16), rmat.astype(BF16)))
        kk = jnp.concatenate([o[:c] for o in outs], axis=1)
        qk = jnp.concatenate([o[c:] for o in outs], axis=1)
        s["a"] = jnp.where(s["strict"], kk * s["decay"], 0.0) * s["beta64"]
        s["intra"] = (qk * s["decay"]).astype(BF16)
        s["t"] = eye - jnp.where(row // 2 == coll // 2, s["a"], 0.0)
    b = 2
    while b < c:
        join = (row // (2 * b) == coll // (2 * b)) & (row // b != coll // b)
        for s in st:
            s["tb"] = s["t"].astype(BF16)
            s["tl"] = _dot(s["tb"], _blockdiag_rows(jnp.where(join, s["a"], 0.0), H_B))
        for s in st:
            s["t"] = s["t"] - _dot(s["tl"].astype(BF16), _blockdiag_rows(s["t"], H_B))
        b *= 2
    zero = jnp.zeros((c, 2 * D_K), BF16)
    for s in st:
        tb = s["t"].astype(BF16)
        us, ws = [], []
        for h in range(H_B):
            rhs = jnp.concatenate([s["rhs_v"][:, h * D_V:(h + 1) * D_V], s["rhs_k"][:, h * D_K:(h + 1) * D_K]], axis=1)
            uw = _dot(tb, jnp.concatenate([zero] * h + [rhs] + [zero] * (H_B - 1 - h), axis=0))
            us.append(uw[:, :D_V])
            ws.append(uw[:, D_V:])
        s["u"] = jnp.concatenate(us, axis=1)
        s["w"] = jnp.concatenate(ws, axis=1).astype(BF16)
    return st


def _delta_scan_step(steps, states):
    c = CHUNK
    zero = jnp.zeros((D_K, D_V), BF16)
    mids = []
    for s, S in zip(steps, states):
        w_s, q_s = [], []
        for p in range(H_B // 2):
            sl = slice(p * 2 * D_K, (p + 1) * 2 * D_K)
            lhs = jnp.concatenate([s["w"][:, sl], s["qg"][:, sl]], axis=0)
            sa, sb = S[2 * p].astype(BF16), S[2 * p + 1].astype(BF16)
            sbd = jnp.concatenate([jnp.concatenate([sa, zero], axis=1), jnp.concatenate([zero, sb], axis=1)], axis=0)
            out = _dot(lhs, sbd)
            w_s.append(out[:c])
            q_s.append(out[c:])
        v_new = s["u"] - jnp.concatenate(w_s, axis=1)
        mids.append((v_new, jnp.concatenate(q_s, axis=1)))
    outs, new_states = [], []
    for s, S, (v_new, q_s) in zip(steps, states, mids):
        vnb = v_new.astype(BF16)
        outs.append(q_s + _dot(s["intra"], _blockdiag_rows(v_new, H_B)))
        new_states.append([S[h] * s["egl"][:, h * D_V:(h + 1) * D_V]
                           + _dot_tn(s["kg"][:, h * D_K:(h + 1) * D_K], vnb[:, h * D_V:(h + 1) * D_V])
                           for h in range(H_B)])
    return outs, new_states


def _delta_kernel(qf_ref, kf_ref, vf_ref, ecf_ref, erf_ref, qb_ref, kb_ref, vb_ref, ecb_ref, erb_ref,
                  s0f_ref, s0b_ref, of_ref, ob_ref, sf_ref, sb_ref, s_scr):
    j = pl.program_id(1)
    c = CHUNK

    @pl.when(j == 0)
    def _():
        s_scr[0] = s0f_ref[0]
        s_scr[1] = s0b_ref[0]

    def group(refs, ci, backward):
        q_ref, k_ref, v_ref, ec_ref, er_ref = refs
        rows = slice(ci * c, (ci + 1) * c)
        d = 1 if backward else 0
        return (q_ref[0, rows, :], k_ref[0, rows, :], v_ref[0, rows, :], ec_ref[0, rows, :],
                er_ref[0, ci, d:d + 1, :], backward)

    fwd = (qf_ref, kf_ref, vf_ref, ecf_ref, erf_ref)
    bwd = (qb_ref, kb_ref, vb_ref, ecb_ref, erb_ref)
    prep = _delta_prep([group(fwd, ci, False) for ci in range(DELTA_CT)]
                       + [group(bwd, ci, True) for ci in range(DELTA_CT)])
    states = [[s_scr[d, h] for h in range(H_B)] for d in range(2)]
    for step in range(DELTA_CT):
        cf, cb = step, DELTA_CT - 1 - step
        outs, states = _delta_scan_step([prep[cf], prep[DELTA_CT + cb]], states)
        of_ref[0, cf * c:(cf + 1) * c, :] = outs[0].astype(of_ref.dtype)
        ob_ref[0, cb * c:(cb + 1) * c, :] = outs[1].astype(ob_ref.dtype)
    for d in range(2):
        for h in range(H_B):
            s_scr[d, h] = states[d][h]

    @pl.when(j == pl.num_programs(1) - 1)
    def _():
        sf_ref[0] = s_scr[0]
        sb_ref[0] = s_scr[1]


def _delta_call(q, k, v, ecol, s0f, s0b):
    B, L, _ = q.shape
    n = L // CHUNK
    tl = DELTA_CT * CHUNK
    nb = L // tl
    erow = ecol[..., :2 * H_B].reshape(B, n, CHUNK, 2, H_B).transpose(0, 1, 3, 4, 2).reshape(B, n, 2, C4)
    fmap = lambda b, j: (b, j, 0)
    bmap = lambda b, j: (b, nb - 1 - j, 0)
    seq = lambda m: pl.BlockSpec((1, tl, W_B), m)
    ecs = lambda m: pl.BlockSpec((1, tl, LANES), m)
    ers = lambda m: pl.BlockSpec((1, DELTA_CT, 2, C4), lambda b, j: m(b, j) + (0,))
    st = pl.BlockSpec((1, H_B, D_K, D_V), lambda b, j: (b, 0, 0, 0))
    return pl.pallas_call(
        _delta_kernel,
        grid=(B, nb),
        in_specs=[seq(fmap), seq(fmap), seq(fmap), ecs(fmap), ers(fmap),
                  seq(bmap), seq(bmap), seq(bmap), ecs(bmap), ers(bmap), st, st],
        out_specs=[seq(fmap), seq(bmap), st, st],
        out_shape=[jax.ShapeDtypeStruct((B, L, W_B), BF16)] * 2
        + [jax.ShapeDtypeStruct((B, H_B, D_K, D_V), F32)] * 2,
        scratch_shapes=[pltpu.VMEM((2, H_B, D_K, D_V), F32)],
        compiler_params=_cparams(("parallel", "arbitrary")),
        name="delta",
    )(q, k, v, ecol, erow, q, k, v, ecol, erow, s0f, s0b)


def _merge_kernel(x_ref, oa_ref, of_ref, ob_ref, z_ref, gate_ref, g1_ref, sh2_ref, sc2_ref,
                  onw_ref, wpa_ref, wpb_ref, wo_ref, n2w_ref, wr_ref, br_ref, ltri_ref,
                  x1_ref, h2_ref, route_ref, cnt_ref, cnt_scr):
    tm = x_ref.shape[0]
    pm = ltri_ref.shape[0]
    parts = [dict(rows=slice(p * pm, (p + 1) * pm)) for p in range(tm // pm)]
    for s in parts:
        rows = s["rows"]
        o = of_ref[rows, :].astype(F32) + ob_ref[rows, :].astype(F32)
        heads = []
        for h in range(H_B):
            t = o[:, h * D_V:(h + 1) * D_V]
            heads.append(t * lax.rsqrt(jnp.mean(t * t, axis=-1, keepdims=True) + EPS))
        ob = jnp.concatenate(heads, axis=-1) * onw_ref[...] * z_ref[rows, :].astype(F32)
        s["pa"] = _dot(oa_ref[rows, :], wpa_ref[...])
        s["pb"] = _dot(ob.astype(BF16), wpb_ref[...])
    for s in parts:
        gate = gate_ref[s["rows"], :]
        merged = gate[:, :D_MODEL].astype(F32) * s["pa"] + gate[:, D_MODEL:].astype(F32) * s["pb"]
        s["mix"] = _dot(merged.astype(BF16), wo_ref[...])
    for s in parts:
        rows = s["rows"]
        x1 = x_ref[rows, :] + g1_ref[0] * s["mix"]
        x1_ref[rows, :] = x1
        ms = jnp.mean(x1 * x1, axis=-1, keepdims=True)
        h2 = x1 * lax.rsqrt(ms + EPS) * n2w_ref[...]
        h2 = h2 * (1.0 + sc2_ref[0]) + sh2_ref[0]
        hi = h2.astype(BF16)
        for c in range(ROW_CHUNKS):
            h2_ref[pl.ds(rows.start * ROW_CHUNKS + c, pm, stride=ROW_CHUNKS), :] = h2[:, c * LANES:(c + 1) * LANES]
        lo = (h2 - hi.astype(F32)).astype(BF16)
        s["logits"] = _dot(jnp.concatenate([hi, lo, hi], axis=-1), wr_ref[...]) + br_ref[...]
    lane = lax.broadcasted_iota(jnp.int32, (pm, LANES), 1)
    big = jnp.int32(LANES)

    def first_max(vals, mask):
        mv = jnp.where(mask, vals, NEG_INF)
        m = mv.max(axis=-1, keepdims=True)
        idx = jnp.where(mask & (mv == m), lane, big).min(axis=-1, keepdims=True)
        return m, idx

    for s in parts:
        logits = s["logits"]
        gmask = lane < N_GROUPS
        gm, grp = first_max(logits, gmask)
        p_grp = 1.0 / jnp.where(gmask, jnp.exp(logits - gm), 0.0).sum(axis=-1, keepdims=True)
        lo_lane = N_GROUPS + grp * EXP_PER_GROUP
        emask = (lane >= lo_lane) & (lane < lo_lane + EXP_PER_GROUP)
        em, _ = first_max(logits, emask)
        ex = jnp.where(emask, jnp.exp(logits - em), 0.0)
        probs = ex / ex.sum(axis=-1, keepdims=True)
        p1, i1 = first_max(probs, emask)
        p2, i2 = first_max(probs, emask & (lane != i1))
        tot = p1 + p2
        s["w1"] = p_grp * p1 / tot
        s["w2"] = p_grp * p2 / tot
        s["e1"] = i1 - N_GROUPS
        s["e2"] = i2 - N_GROUPS
        s["hit"] = jnp.where(lane == s["e1"], 1.0, 0.0) + jnp.where(lane == s["e2"], 1.0, 0.0)
        s["within"] = _dot(ltri_ref[...], s["hit"].astype(BF16))

    @pl.when(pl.program_id(0) == 0)
    def _():
        cnt_scr[...] = jnp.zeros_like(cnt_scr)

    running = cnt_scr[...]
    for s in parts:
        before = s["within"] + running
        running = running + s["hit"].sum(axis=0, keepdims=True)
        r1 = jnp.where(lane == s["e1"], before, 0.0).sum(axis=-1, keepdims=True)
        r2 = jnp.where(lane == s["e2"], before, 0.0).sum(axis=-1, keepdims=True)
        cols = [s["e1"].astype(F32), s["e2"].astype(F32), s["w1"], s["w2"], r1, r2]
        route = jnp.zeros((pm, LANES), F32)
        for i, col in enumerate(cols):
            route = jnp.where(lane == i, col, route)
        route_ref[s["rows"], :] = route
    cnt_scr[...] = running
    cnt_ref[...] = running


def _merge_call(x, oa, of, ob, z, gate, mod3, mod_stride, w):
    B, L, _ = x.shape
    n_tok = B * L
    tm = ROW_TILE
    assert n_tok % tm == 0 and (L % tm == 0 or tm % L == 0)
    flat = lambda a: a.reshape(n_tok, a.shape[-1])
    rspec = lambda n: pl.BlockSpec((tm, n), lambda i: (i, 0))
    mspec = lambda k: pl.BlockSpec((1, 1, D_MODEL), lambda i: ((i * tm) // L * mod_stride + k, 0, 0))
    t = np.arange(MERGE_PART)
    ltri = jnp.asarray((t[:, None] > t[None, :]).astype(np.float32)).astype(BF16)
    consts = [w["onw"], w["wpa"], w["wpb"], w["wo"], w["norm2_w"], w["wr3"], w["br"], ltri]
    return pl.pallas_call(
        _merge_kernel,
        grid=(n_tok // tm,),
        in_specs=[rspec(D_MODEL), rspec(W_A), rspec(W_B), rspec(W_B), rspec(W_B), rspec(2 * D_MODEL),
                  mspec(2), mspec(3), mspec(4)] + [_const_spec(c.shape) for c in consts],
        out_specs=[rspec(D_MODEL), pl.BlockSpec((tm * SUB, LANES), lambda i: (i, 0)), rspec(LANES),
                   _const_spec((1, LANES))],
        out_shape=[jax.ShapeDtypeStruct((n_tok, D_MODEL), F32),
                   jax.ShapeDtypeStruct((n_tok * SUB, LANES), F32),
                   jax.ShapeDtypeStruct((n_tok, LANES), F32),
                   jax.ShapeDtypeStruct((1, LANES), F32)],
        scratch_shapes=[pltpu.VMEM((1, LANES), F32)],
        compiler_params=_cparams(("arbitrary",)),
        name="merge",
    )(flat(x), flat(oa), flat(of), flat(ob), flat(z), flat(gate), mod3, mod3, mod3, *consts)


SUB = 8
ROW_CHUNKS = D_MODEL // LANES


def _store_token_major(ref, x):
    for s in range(ROW_CHUNKS):
        ref[pl.ds(s, x.shape[0], stride=ROW_CHUNKS), :] = x[:, s * LANES:(s + 1) * LANES]


def _load_token_major(ref, slot, row0, n_tok):
    return jnp.concatenate([ref[slot, pl.ds(row0 * ROW_CHUNKS + s, n_tok, stride=ROW_CHUNKS), :]
                            for s in range(ROW_CHUNKS)], axis=1)


def _row_gather_start(idx_ref, n_rows, src_hbm, buf, sem, slot):
    for r in range(n_rows):
        src = pl.multiple_of(idx_ref[0, 0, r] * SUB, SUB)
        pltpu.make_async_copy(src_hbm.at[pl.ds(src, SUB), :], buf.at[slot, pl.ds(r * SUB, SUB), :],
                              sem.at[slot]).start(priority=r % 2)


def _row_gather_wait(n_rows, src_hbm, buf, sem, slot):
    for r in range(n_rows):
        pltpu.make_async_copy(src_hbm.at[pl.ds(0, SUB), :], buf.at[slot, pl.ds(r * SUB, SUB), :],
                              sem.at[slot]).wait()


def _gather_pipeline(i, n_steps, idx_ref, idx_next_ref, n_rows, src_hbm, buf, sem):
    slot = lax.rem(i, 2)

    @pl.when(i == 0)
    def _():
        _row_gather_start(idx_ref, n_rows, src_hbm, buf, sem, 0)

    @pl.when(i + 1 < n_steps)
    def _():
        _row_gather_start(idx_next_ref, n_rows, src_hbm, buf, sem, 1 - slot)

    @pl.when(i < n_steps)
    def _():
        _row_gather_wait(n_rows, src_hbm, buf, sem, slot)

    return slot


def _moe_kernel(te_ref, nt_ref, idx_ref, idxn_ref, h2_hbm, wg_ref, wu_ref, wd_ref, y_ref, xbuf, sem):
    i = pl.program_id(0)
    n_used = nt_ref[0]
    slot = _gather_pipeline(i, n_used, idx_ref, idxn_ref, MOE_TILE, h2_hbm, xbuf, sem)

    @pl.when(i < n_used)
    def _():
        x = _load_token_major(xbuf, slot, 0, MOE_TILE).astype(BF16)
        g = _dot(x, wg_ref[0])
        hid = g * jax.nn.sigmoid(g) * _dot(x, wu_ref[0])
        _store_token_major(y_ref, _dot(hid.astype(BF16), wd_ref[0]))

    @pl.when(i >= n_used)
    def _():
        y_ref[...] = jnp.zeros_like(y_ref)


def _moe_call(h2, src_tok, tile_expert, n_tiles_used, wg, wu, wd):
    p_pad = src_tok.shape[0]
    n_tiles = p_pad // MOE_TILE
    idx = src_tok.reshape(n_tiles, 1, MOE_TILE)
    smem_tile = lambda m: pl.BlockSpec((1, 1, MOE_TILE), m, memory_space=pltpu.SMEM)
    grid_spec = pltpu.PrefetchScalarGridSpec(
        num_scalar_prefetch=2,
        grid=(n_tiles,),
        in_specs=[smem_tile(lambda i, te, nt: (i, 0, 0)),
                  smem_tile(lambda i, te, nt: (jnp.minimum(i + 1, n_tiles - 1), 0, 0)),
                  pl.BlockSpec(memory_space=pl.ANY),
                  pl.BlockSpec((1, D_MODEL, D_FF_E), lambda i, te, nt: (te[i], 0, 0)),
                  pl.BlockSpec((1, D_MODEL, D_FF_E), lambda i, te, nt: (te[i], 0, 0)),
                  pl.BlockSpec((1, D_FF_E, D_MODEL), lambda i, te, nt: (te[i], 0, 0))],
        out_specs=pl.BlockSpec((MOE_TILE * SUB, LANES), lambda i, te, nt: (i, 0)),
        scratch_shapes=[pltpu.VMEM((2, MOE_TILE * SUB, LANES), F32), pltpu.SemaphoreType.DMA((2,))],
    )
    return pl.pallas_call(
        _moe_kernel,
        grid_spec=grid_spec,
        out_shape=jax.ShapeDtypeStruct((p_pad * SUB, LANES), F32),
        compiler_params=_cparams(("arbitrary",)),
        name="moe",
    )(tile_expert, n_tiles_used, idx, idx, h2, wg, wu, wd)


COMBINE_TILE = 256


def _combine_kernel(pos_ref, posn_ref, x1_ref, route_ref, g2_ref, y_hbm, o_ref, ybuf, sem):
    i = pl.program_id(0)
    t = COMBINE_TILE
    slot = _gather_pipeline(i, pl.num_programs(0), pos_ref, posn_ref, 2 * t, y_hbm, ybuf, sem)
    route = route_ref[...]
    w1 = route[:, 2:3]
    w2 = route[:, 3:4]
    ya = _load_token_major(ybuf, slot, 0, t)
    yb = _load_token_major(ybuf, slot, t, t)
    o_ref[...] = x1_ref[...] + g2_ref[0] * (w1 * ya + w2 * yb)


def _combine_call(x1, route, pos2, y, mod3, mod_stride, seq_len):
    n = x1.shape[0]
    t = COMBINE_TILE
    n_tiles = n // t
    pos_tiles = pos2.reshape(n_tiles, t, 2).transpose(0, 2, 1).reshape(n_tiles, 1, 2 * t)
    smem_tile = lambda m: pl.BlockSpec((1, 1, 2 * t), m, memory_space=pltpu.SMEM)
    return pl.pallas_call(
        _combine_kernel,
        grid=(n_tiles,),
        in_specs=[smem_tile(lambda i: (i, 0, 0)),
                  smem_tile(lambda i: (jnp.minimum(i + 1, n_tiles - 1), 0, 0)),
                  pl.BlockSpec((t, D_MODEL), lambda i: (i, 0)),
                  pl.BlockSpec((t, LANES), lambda i: (i, 0)),
                  pl.BlockSpec((1, 1, D_MODEL), lambda i: ((i * t) // seq_len * mod_stride + 5, 0, 0)),
                  pl.BlockSpec(memory_space=pl.ANY)],
        out_specs=pl.BlockSpec((t, D_MODEL), lambda i: (i, 0)),
        out_shape=jax.ShapeDtypeStruct((n, D_MODEL), F32),
        scratch_shapes=[pltpu.VMEM((2, 2 * t * SUB, LANES), F32), pltpu.SemaphoreType.DMA((2,))],
        compiler_params=_cparams(("arbitrary",)),
        name="combine",
    )(pos_tiles, pos_tiles, x1, route, mod3, y)


def _inverse_map_kernel(pos_ref, init_hbm, out_ref):
    pltpu.sync_copy(init_hbm, out_ref)

    def body(p, carry):
        out_ref[pos_ref[p]] = lax.shift_right_logical(p, 1)
        return carry

    lax.fori_loop(0, pos_ref.shape[0], body, 0, unroll=8)


def _inverse_map_call(pos, init):
    return pl.pallas_call(
        _inverse_map_kernel,
        in_specs=[pl.BlockSpec(memory_space=pltpu.SMEM), pl.BlockSpec(memory_space=pl.ANY)],
        out_specs=pl.BlockSpec(memory_space=pltpu.SMEM),
        out_shape=jax.ShapeDtypeStruct(init.shape, jnp.int32),
        name="inverse_map",
    )(pos, init)


def _moe_forward(h2, route, counts, wg, wu, wd):
    n = route.shape[0]
    eid = route[:, :2].astype(jnp.int32).reshape(-1)
    rank = route[:, 4:6].astype(jnp.int32).reshape(-1)
    counts = counts[0, :N_EXPERTS].astype(jnp.int32)
    padded = ((counts + MOE_TILE - 1) // MOE_TILE) * MOE_TILE
    seg_end = jnp.cumsum(padded)
    seg_start = seg_end - padded
    pos = seg_start[eid] + rank
    p_pad = 2 * n + N_EXPERTS * MOE_TILE
    src_tok = _inverse_map_call(pos, jnp.arange(p_pad, dtype=jnp.int32) % n)
    n_tiles = p_pad // MOE_TILE
    tile_start = jnp.arange(n_tiles, dtype=jnp.int32) * MOE_TILE
    tile_expert = jnp.minimum(jnp.sum((tile_start[:, None] >= seg_end[None, :]).astype(jnp.int32), axis=1),
                              N_EXPERTS - 1)
    n_used = (seg_end[-1] // MOE_TILE).astype(jnp.int32).reshape(1)
    y = _moe_call(h2, src_tok, tile_expert, n_used, wg, wu, wd)
    return y, pos.reshape(n, 2)


def _prep_weights(norm1_w, norm2_w, w_in, conv_w, a_log, dt_bias, q_norm_w, k_norm_w, o_norm_w,
                  w_proj_a, w_proj_b, w_out, w_router_g, b_router_g, w_router_e, b_router_e,
                  w_exp_gate, w_exp_up, w_exp_down):
    w = {}
    w["norm1_w"] = norm1_w.reshape(1, D_MODEL)
    w["norm2_w"] = norm2_w.reshape(1, D_MODEL)
    o = 0
    w["wa"] = w_in[:, o:o + 3 * W_A].astype(BF16); o += 3 * W_A
    w["wb"] = w_in[:, o:o + 3 * W_B].astype(BF16); o += 3 * W_B
    w["wz"] = w_in[:, o:o + W_B].astype(BF16); o += W_B
    small = w_in[:, o:o + 4 * H_B]; o += 4 * H_B
    w["wg"] = w_in[:, o:o + 2 * D_MODEL].astype(BF16)
    def group_lanes(t):
        r = t.shape[0]
        t = t.reshape(r, -1, N_HG, HP).transpose(0, 2, 1, 3).reshape(r, N_HG, -1)
        return jnp.pad(t, ((0, 0), (0, 0), (0, LANES - t.shape[-1]))).reshape(r, N_HG * LANES)

    w["we"] = group_lanes(small).astype(BF16)
    w["alog_l"] = group_lanes(a_log.reshape(1, 2 * H_B))
    w["dtb_l"] = group_lanes(dt_bias.reshape(1, 2 * H_B))
    w["qnw"] = jnp.tile(q_norm_w, H_A).reshape(1, W_A)
    w["knw"] = jnp.tile(k_norm_w, H_A).reshape(1, W_A)
    w["onw"] = jnp.tile(o_norm_w, H_B).reshape(1, W_B)
    blk = np.arange(256) // D_HA
    w["bd256"] = jnp.asarray((blk[:, None] == blk[None, :]).astype(np.float32)).astype(BF16)
    w["conv_w"] = conv_w
    w["wpa"] = w_proj_a.astype(BF16)
    w["wpb"] = w_proj_b.astype(BF16)
    w["wo"] = w_out.astype(BF16)
    wr = jnp.concatenate([w_router_g, w_router_e], axis=1)
    wr = jnp.pad(wr, ((0, 0), (0, LANES - wr.shape[1])))
    wr_hi = wr.astype(BF16)
    wr_lo = (wr - wr_hi.astype(F32)).astype(BF16)
    w["wr3"] = jnp.concatenate([wr_hi, wr_hi, wr_lo], axis=0)
    br = jnp.concatenate([b_router_g, b_router_e])
    w["br"] = jnp.pad(br, (0, LANES - br.shape[0])).reshape(1, LANES)
    w["wg_e"] = w_exp_gate.astype(BF16)
    w["wu_e"] = w_exp_up.astype(BF16)
    w["wd_e"] = w_exp_down.astype(BF16)
    return w


def _layer(x, mod3, mod_stride, w, ctx):
    B, L, _ = x.shape
    latent = ctx is not None
    qa, ka, va, qkvb, z, gate, ecol, *new_kv = _inproj_call(x, mod3, mod_stride, w, not latent)
    if latent:
        k_ctx, v_ctx, s0f, s0b, bias = ctx
        oa = _nbr_attn_call(qa, ka, va, k_ctx, v_ctx, bias)
    else:
        s0f = s0b = jnp.zeros((B, H_B, D_K, D_V), F32)
        oa = _ctx_attn_call(qa, ka, va)
    qd, kd, vd = _conv_call(qkvb, w["conv_w"], rope=latent)
    of, ob, s_f, s_b = _delta_call(qd, kd, vd, ecol, s0f, s0b)
    x1, h2, route, counts = _merge_call(x, oa, of, ob, z, gate, mod3, mod_stride, w)
    ys, pos2 = _moe_forward(h2, route, counts, w["wg_e"], w["wu_e"], w["wd_e"])
    y = _combine_call(x1, route, pos2, ys, mod3, mod_stride, L)
    return y.reshape(B, L, D_MODEL), new_kv, s_f, s_b


def kernel(x_prompt, x_sample, c, cache_attn_k, cache_attn_v, state_delta_fwd, state_delta_bwd, c_ctx,
           norm1_w, norm2_w, w_mod, b_mod, w_in, conv_w, a_log, dt_bias, q_norm_w, k_norm_w, rpb,
           o_norm_w, w_proj_a, w_proj_b, w_out, w_router_g, b_router_g, w_router_e, b_router_e,
           w_exp_gate, w_exp_up, w_exp_down):
    assert norm1_w.shape[0] == 1, "single-layer trunk"
    Bp, Lp, _ = x_prompt.shape
    Bs, Ls, _ = x_sample.shape
    l = 0
    w = _prep_weights(norm1_w[l], norm2_w[l], w_in[l], conv_w[l], a_log[l], dt_bias[l], q_norm_w[l],
                      k_norm_w[l], o_norm_w[l], w_proj_a[l], w_proj_b[l], w_out[l], w_router_g[l],
                      b_router_g[l], w_router_e[l], b_router_e[l], w_exp_gate[l], w_exp_up[l],
                      w_exp_down[l])
    cond = jnp.concatenate([c_ctx[None, :], c, jnp.zeros((8 - 1 - Bs, D_MODEL), F32)], axis=0)
    mod = _mod_call(cond, w_mod[l], b_mod[l])
    mod3 = mod.reshape(8 * 6, 1, D_MODEL)
    bias = _bias_table_call(rpb[l])
    y_p, (new_k, new_v), s_f, s_b = _layer(x_prompt, mod3[:6], 0, w, None)
    lc = cache_attn_k.shape[2]
    ctx = (cache_attn_k[:, l].reshape(Bs, lc, W_A).astype(BF16),
           cache_attn_v[:, l].reshape(Bs, lc, W_A).astype(BF16),
           state_delta_fwd[:, l], state_delta_bwd[:, l], bias)
    y_s, _, _, _ = _layer(x_sample, mod3[6:6 + 6 * Bs], 6, w, ctx)
    return (y_p, y_s, new_k, new_v, s_f[:, None], s_b[:, None])
```

```python
import functools
import math

import numpy as np
import jax
import jax.numpy as jnp
from jax import lax
from jax.experimental import pallas as pl
from jax.experimental.pallas import tpu as pltpu

F32 = jnp.float32
BF16 = jnp.bfloat16

D_MODEL = 1024
GRID_W = 64
H_A = 8
D_HA = 64
W_A = H_A * D_HA
WIN_H = 8
WIN_W = 16
H_B = 4
D_K = 128
D_V = 128
W_B = H_B * D_V
CONV_K = 5
CHUNK = 64
ROPE_BASE = 10000.0
N_GROUPS = 4
EXP_PER_GROUP = 8
N_EXPERTS = N_GROUPS * EXP_PER_GROUP
D_FF_E = 256
EPS = 1e-6
NEG_INF = -1e30

LANES = 128
HP = H_B
N_HG = H_B // HP
ROW_TILE = 512
INPROJ_TILE = 256
INPROJ_PART = 256
MERGE_PART = 256
MOE_TILE = 256
VMEM_LIMIT = 56 * 1024 * 1024

HIGHEST = lax.Precision.HIGHEST


def _cparams(sem):
    return pltpu.CompilerParams(dimension_semantics=sem, vmem_limit_bytes=VMEM_LIMIT)


def _dot(a, b):
    return jnp.dot(a, b, preferred_element_type=F32)


def _dot_nt(a, b):
    return lax.dot_general(a, b, (((1,), (1,)), ((), ())), preferred_element_type=F32)


def _dot_tn(a, b):
    return lax.dot_general(a, b, (((0,), (0,)), ((), ())), preferred_element_type=F32)


def _const_spec(shape):
    nd = len(shape)
    return pl.BlockSpec(shape, lambda *_: (0,) * nd)


def _mod_kernel(c_ref, w_ref, b_ref, o_ref):
    c = c_ref[...]
    s = c * jax.nn.sigmoid(c)
    o_ref[...] = jnp.dot(s, w_ref[...], preferred_element_type=F32, precision=HIGHEST) + b_ref[...]


def _mod_call(cond8, w_mod, b_mod):
    n_out = w_mod.shape[1]
    tn = 1024
    return pl.pallas_call(
        _mod_kernel,
        grid=(n_out // tn,),
        in_specs=[pl.BlockSpec((8, D_MODEL), lambda j: (0, 0)),
                  pl.BlockSpec((D_MODEL, tn), lambda j: (0, j)),
                  pl.BlockSpec((1, tn), lambda j: (0, j))],
        out_specs=pl.BlockSpec((8, tn), lambda j: (0, j)),
        out_shape=jax.ShapeDtypeStruct((8, n_out), F32),
        compiler_params=_cparams(("arbitrary",)),
        name="mod",
    )(cond8, w_mod, b_mod.reshape(1, n_out))


def _softplus(x):
    return jnp.maximum(x, 0.0) + jnp.log1p(jnp.exp(-jnp.abs(x)))


def _inproj_kernel(x_ref, sh_ref, sc_ref, nw_ref, wa_ref, wb_ref, wz_ref, wg_ref, we_ref,
                   qnw_ref, knw_ref, bd_ref, alog_ref, dtb_ref, tri_ref,
                   qa_ref, ka_ref, va_ref, qkvb_ref, z_ref, gate_ref, e_ref, *kv5_refs):
    tm = x_ref.shape[0]
    pm = tri_ref.shape[1]
    parts = [dict(rows=slice(p * pm, (p + 1) * pm)) for p in range(tm // pm)]
    for s in parts:
        x = x_ref[s["rows"], :]
        ms = jnp.mean(x * x, axis=-1, keepdims=True)
        h = x * lax.rsqrt(ms + EPS) * nw_ref[...]
        h = h * (1.0 + sc_ref[0]) + sh_ref[0]
        s["hb"] = h.astype(BF16)
    for s in parts:
        s["e"] = _dot(s["hb"], we_ref[...])
    for s in parts:
        s["a"] = _dot(s["hb"], wa_ref[...])
    for s in parts:
        qkvb_ref[s["rows"], :] = _dot(s["hb"], wb_ref[...]).astype(qkvb_ref.dtype)
    bd = bd_ref[...]

    def head_rms(t, w):
        sq = (t * t).astype(BF16)
        ss = jnp.concatenate([_dot(sq[:, i * 256:(i + 1) * 256], bd) for i in range(W_A // 256)], axis=-1)
        return t * lax.rsqrt(ss * (1.0 / D_HA) + EPS) * w

    for p, s in enumerate(parts):
        rows, a = s["rows"], s["a"]
        qa = head_rms(a[:, :W_A], qnw_ref[...]) * (D_HA ** -0.5)
        qa_ref[rows, :] = qa.astype(qa_ref.dtype)
        ka = head_rms(a[:, W_A:2 * W_A], knw_ref[...])
        va = a[:, 2 * W_A:]
        ka_ref[rows, :] = ka.astype(ka_ref.dtype)
        va_ref[rows, :] = va.astype(va_ref.dtype)
        if kv5_refs:
            seq_len = kv5_refs[0].shape[2]
            b_in, off = (p * pm) // seq_len, (p * pm) % seq_len
            for ref, t in zip(kv5_refs, (ka, va)):
                for hd in range(H_A):
                    ref[b_in, 0, off:off + pm, hd, :] = t[:, hd * D_HA:(hd + 1) * D_HA]
    for s in parts:
        z = _dot(s["hb"], wz_ref[...])
        z_ref[s["rows"], :] = (z * jax.nn.sigmoid(z)).astype(z_ref.dtype)
    lane = lax.broadcasted_iota(jnp.int32, (pm, N_HG * LANES), 1) % LANES
    tri = tri_ref[...]
    for s in parts:
        e = s["e"]
        g = -jnp.exp(alog_ref[...]) * _softplus(e + dtb_ref[...])
        act = jnp.where(lane < 2 * HP, g, jax.nn.sigmoid(e))
        a_hi = act.astype(BF16)
        rem = act - a_hi.astype(F32)
        a_mid = rem.astype(BF16)
        a_lo = (rem - a_mid.astype(F32)).astype(BF16)
        cum = _dot(tri, a_hi) + _dot(tri, a_mid) + _dot(tri, a_lo)
        e_ref[s["rows"], :] = jnp.where(lane < HP, cum[:pm], jnp.where(lane < 2 * HP, cum[pm:], act))
    for s in parts:
        gate_ref[s["rows"], :] = jax.nn.sigmoid(_dot(s["hb"], wg_ref[...])).astype(gate_ref.dtype)


def _inproj_call(x, mod3, mod_stride, w, new_cache):
    B, L, _ = x.shape
    n_tok = B * L
    tm, pm = INPROJ_TILE, INPROJ_PART
    assert n_tok % tm == 0 and (L % tm == 0 or (tm % L == 0 and mod_stride == 0)) and L % pm == 0
    row = lambda n, dt: jax.ShapeDtypeStruct((n_tok, n), dt)
    rspec = lambda n: pl.BlockSpec((tm, n), lambda i: (i, 0))
    mspec = lambda k: pl.BlockSpec((1, 1, D_MODEL), lambda i: ((i * tm) // L * mod_stride + k, 0, 0))
    t = np.arange(pm)
    same = (t[:, None] // CHUNK) == (t[None, :] // CHUNK)
    tri = np.concatenate([same & (t[:, None] >= t[None, :]), same & (t[:, None] <= t[None, :])], axis=0)
    consts = [w["norm1_w"], w["wa"], w["wb"], w["wz"], w["wg"], w["we"], w["qnw"], w["knw"],
              w["bd256"], w["alog_l"], w["dtb_l"], jnp.asarray(tri.astype(np.float32)).astype(BF16)]
    out_specs = [rspec(W_A), rspec(W_A), rspec(W_A), rspec(3 * W_B), rspec(W_B),
                 rspec(2 * D_MODEL), rspec(N_HG * LANES)]
    out_shape = [row(W_A, BF16), row(W_A, BF16), row(W_A, BF16), row(3 * W_B, BF16),
                 row(W_B, BF16), row(2 * D_MODEL, BF16), row(N_HG * LANES, F32)]
    if new_cache:
        if tm <= L:
            per_seq = L // tm
            cache_spec = pl.BlockSpec((1, 1, tm, H_A, D_HA), lambda i: (i // per_seq, 0, i % per_seq, 0, 0))
        else:
            cache_spec = pl.BlockSpec((tm // L, 1, L, H_A, D_HA), lambda i: (i, 0, 0, 0, 0))
        out_specs += [cache_spec] * 2
        out_shape += [jax.ShapeDtypeStruct((B, 1, L, H_A, D_HA), F32)] * 2
    outs = pl.pallas_call(
        _inproj_kernel,
        grid=(n_tok // tm,),
        in_specs=[rspec(D_MODEL), mspec(0), mspec(1)] + [_const_spec(c.shape) for c in consts],
        out_specs=out_specs,
        out_shape=out_shape,
        compiler_params=_cparams(("parallel",)),
        name="inproj",
    )(x.reshape(n_tok, D_MODEL), mod3, mod3, *consts)
    return [o.reshape(B, L, o.shape[-1]) for o in outs[:7]] + list(outs[7:])


def _pair_softmax_pv(s_parts, v_parts):
    m = s_parts[0].max(axis=-1, keepdims=True)
    for s in s_parts[1:]:
        m = jnp.maximum(m, s.max(axis=-1, keepdims=True))
    den = 0.0
    acc = 0.0
    for s, v in zip(s_parts, v_parts):
        p = jnp.exp(s - m)
        den = den + p.sum(axis=-1, keepdims=True)
        acc = acc + _dot(p.astype(BF16), v)
    return acc / den


def _stack_pair(qp):
    lane = lax.broadcasted_iota(jnp.int32, qp.shape, 1)
    zero = jnp.zeros_like(qp)
    return jnp.concatenate([jnp.where(lane < D_HA, qp, zero), jnp.where(lane >= D_HA, qp, zero)], axis=0)


def _unstack_pair(o2):
    t = o2.shape[0] // 2
    lane = lax.broadcasted_iota(jnp.int32, (t, LANES), 1)
    return jnp.where(lane < D_HA, o2[:t], o2[t:])


def _ctx_attn_kernel(q_ref, k_ref, v_ref, o_ref):
    for p in range(H_A // 2):
        sl = slice(p * LANES, (p + 1) * LANES)
        qs = _stack_pair(q_ref[0, :, sl])
        kp = k_ref[0, :, sl].astype(BF16)
        vp = v_ref[0, :, sl].astype(BF16)
        o2 = _pair_softmax_pv([_dot_nt(qs, kp)], [vp])
        o_ref[0, :, sl] = _unstack_pair(o2).astype(o_ref.dtype)


def _ctx_attn_call(q, k, v):
    B, L, _ = q.shape
    spec = pl.BlockSpec((1, L, W_A), lambda b: (b, 0, 0))
    return pl.pallas_call(
        _ctx_attn_kernel,
        grid=(B,),
        in_specs=[spec, spec, spec],
        out_specs=spec,
        out_shape=jax.ShapeDtypeStruct((B, L, W_A), BF16),
        compiler_params=_cparams(("parallel",)),
        name="ctx_attn",
    )(q, k, v)


N_DR = 2 * WIN_H - 1
N_DC = 2 * WIN_W - 1
NBR_ROWS = 4


def _bias_table_kernel(rpb_ref, o_ref):
    h = pl.program_id(0)
    c = lax.broadcasted_iota(jnp.int32, (GRID_W, 2 * GRID_W), 0)
    lane = lax.broadcasted_iota(jnp.int32, (GRID_W, 2 * GRID_W), 1)
    kc = lane % GRID_W
    ws = jnp.clip(c - WIN_W // 2, 0, GRID_W - WIN_W)
    in_win = (kc >= ws) & (kc < ws + WIN_W)
    dc = kc - c + (WIN_W - 1)
    for dr in range(N_DR - 1):
        base = (h * N_DR + dr) * N_DC
        val = jnp.zeros((GRID_W, 2 * GRID_W), F32)
        for d in range(N_DC):
            val = jnp.where(dc == d, jnp.where(lane < GRID_W, rpb_ref[base + d], rpb_ref[base + N_DC + d]), val)
        o_ref[0, dr] = jnp.where(in_win, val, NEG_INF)


def _bias_table_call(rpb):
    return pl.pallas_call(
        _bias_table_kernel,
        grid=(H_A,),
        in_specs=[pl.BlockSpec(memory_space=pltpu.SMEM)],
        out_specs=pl.BlockSpec((1, N_DR - 1, GRID_W, 2 * GRID_W), lambda h: (h, 0, 0, 0)),
        out_shape=jax.ShapeDtypeStruct((H_A, N_DR - 1, GRID_W, 2 * GRID_W), F32),
        compiler_params=_cparams(("arbitrary",)),
        name="bias_table",
    )(rpb.reshape(-1))


def _nbr_attn_kernel(q_ref, k_ref, v_ref, kc_ref, vc_ref, bias_ref, o_ref, *, rows):
    inst = []
    for rr in range(NBR_ROWS):
        r = pl.program_id(1) * NBR_ROWS + rr
        first = jnp.clip(r - WIN_H // 2, 0, rows - WIN_H)
        start = pl.multiple_of(first * GRID_W, GRID_W)
        shift = r - first
        for p in range(H_A // 2):
            inst.append((rr, p, start, shift))
    scores = []
    for rr, p, start, shift in inst:
        sl = slice(p * LANES, (p + 1) * LANES)
        qs = _stack_pair(q_ref[0, rr * GRID_W:(rr + 1) * GRID_W, sl])
        bias = jnp.concatenate(
            [jnp.concatenate([bias_ref[2 * p + hh, 2 * u - shift + WIN_H - 1] for u in range(WIN_H // 2)], axis=1)
             for hh in range(2)], axis=0)
        s_loc = _dot_nt(qs, k_ref[0, pl.ds(start, WIN_H * GRID_W), sl]) + bias
        s_ctx = _dot_nt(qs, kc_ref[0, :, sl])
        scores.append((s_loc, s_ctx))
    for (rr, p, start, shift), (s_loc, s_ctx) in zip(inst, scores):
        sl = slice(p * LANES, (p + 1) * LANES)
        o2 = _pair_softmax_pv([s_loc, s_ctx], [v_ref[0, pl.ds(start, WIN_H * GRID_W), sl], vc_ref[0, :, sl]])
        o_ref[0, rr * GRID_W:(rr + 1) * GRID_W, sl] = _unstack_pair(o2).astype(o_ref.dtype)


def _nbr_attn_call(q, k, v, k_ctx, v_ctx, bias):
    B, L, _ = q.shape
    rows = L // GRID_W
    lc = k_ctx.shape[1]
    full = pl.BlockSpec((1, L, W_A), lambda b, r: (b, 0, 0))
    ctx = pl.BlockSpec((1, lc, W_A), lambda b, r: (b, 0, 0))
    rowspec = pl.BlockSpec((1, NBR_ROWS * GRID_W, W_A), lambda b, r: (b, r, 0))
    return pl.pallas_call(
        functools.partial(_nbr_attn_kernel, rows=rows),
        grid=(B, rows // NBR_ROWS),
        in_specs=[rowspec, full, full, ctx, ctx, _const_spec(bias.shape)],
        out_specs=rowspec,
        out_shape=jax.ShapeDtypeStruct((B, L, W_A), BF16),
        compiler_params=_cparams(("parallel", "arbitrary")),
        name="nbr_attn",
    )(q, k, v, k_ctx, v_ctx, bias)


HALO = 16


def _conv_kernel(*refs, rope):
    if rope:
        x_ref, prev_ref, next_ref, w_ref, cos_ref, sin_ref, q_ref, k_ref, v_ref, xe_ref = refs
    else:
        x_ref, prev_ref, next_ref, w_ref, q_ref, k_ref, v_ref, xe_ref = refs
    i = pl.program_id(1)
    n_i = pl.num_programs(1)
    tl = x_ref.shape[1]
    xe_ref[0:HALO] = jnp.where(i > 0, prev_ref[0].astype(F32), 0.0)
    xe_ref[HALO:HALO + tl] = x_ref[0].astype(F32)
    xe_ref[HALO + tl:] = jnp.where(i < n_i - 1, next_ref[0].astype(F32), 0.0)
    half = CONV_K // 2
    y = w_ref[0:1, :] * xe_ref[HALO - half:HALO - half + tl]
    for j in range(1, CONV_K):
        y = y + w_ref[j:j + 1, :] * xe_ref[HALO - half + j:HALO - half + j + tl]
    y = y * jax.nn.sigmoid(y)

    def l2n(t):
        return t * lax.rsqrt(jnp.sum(t * t, axis=-1, keepdims=True) + EPS)

    def rot(t):
        if not rope:
            return t
        lane = lax.broadcasted_iota(jnp.int32, t.shape, 1)
        swapped = jnp.where(lane % 2 == 0, pltpu.roll(t, LANES - 1, 1), pltpu.roll(t, 1, 1))
        return t * cos_ref[...] + swapped * sin_ref[...]

    for h in range(H_B):
        sl = slice(h * D_K, (h + 1) * D_K)
        q_ref[0, :, sl] = (rot(l2n(y[:, sl])) * (D_K ** -0.5)).astype(q_ref.dtype)
        k_ref[0, :, sl] = rot(l2n(y[:, W_B + h * D_K:W_B + (h + 1) * D_K])).astype(k_ref.dtype)
    v_ref[0] = y[:, 2 * W_B:].astype(v_ref.dtype)


def _rope_tables(L):
    pos = jnp.arange(L)
    rows = (pos // GRID_W).astype(F32)
    cols = (pos % GRID_W).astype(F32)
    n_pairs = D_K // 4
    inv = ROPE_BASE ** (-jnp.arange(n_pairs, dtype=F32) / n_pairs)
    ang = jnp.concatenate([rows[:, None] * inv, cols[:, None] * inv], axis=-1)
    cos = jnp.repeat(jnp.cos(ang), 2, axis=-1)
    sin = jnp.repeat(jnp.sin(ang), 2, axis=-1)
    sign = jnp.asarray(np.tile(np.array([-1.0, 1.0], np.float32), D_K // 2))
    return cos, sin * sign


def _conv_call(x, conv_w, rope):
    B, L, C = x.shape
    tl = min(ROW_TILE, L)
    nh = tl // HALO
    n_halo = L // HALO
    cur = pl.BlockSpec((1, tl, C), lambda b, i: (b, i, 0))
    prev = pl.BlockSpec((1, HALO, C), lambda b, i: (b, jnp.maximum(i * nh - 1, 0), 0))
    nxt = pl.BlockSpec((1, HALO, C), lambda b, i: (b, jnp.minimum((i + 1) * nh, n_halo - 1), 0))
    ins = [x, x, x, conv_w]
    specs = [cur, prev, nxt, _const_spec(conv_w.shape)]
    if rope:
        cos, sin = _rope_tables(L)
        ins += [cos, sin]
        specs += [pl.BlockSpec((tl, D_K), lambda b, i: (i, 0))] * 2
    out = pl.BlockSpec((1, tl, W_B), lambda b, i: (b, i, 0))
    return pl.pallas_call(
        functools.partial(_conv_kernel, rope=rope),
        grid=(B, L // tl),
        in_specs=specs,
        out_specs=[out, out, out],
        out_shape=[jax.ShapeDtypeStruct((B, L, W_B), BF16)] * 3,
        scratch_shapes=[pltpu.VMEM((tl + 2 * HALO, C), F32)],
        compiler_params=_cparams(("parallel", "parallel")),
        name="conv",
    )(*ins)


DELTA_CT = 4
C4 = H_B * CHUNK


def _blockdiag_rows(x, n_blk):
    w = x.shape[1] // n_blk
    lane_blk = lax.broadcasted_iota(jnp.int32, x.shape, 1) // w
    return jnp.concatenate([jnp.where(lane_blk == b, x, 0.0) for b in range(n_blk)], axis=0).astype(BF16)


def _widen(ec, col0, width):
    blk = lax.broadcasted_iota(jnp.int32, (ec.shape[0], H_B * width), 1) // width
    out = ec[:, col0 + H_B - 1:col0 + H_B]
    for h in range(H_B - 2, -1, -1):
        out = jnp.where(blk == h, ec[:, col0 + h:col0 + h + 1], out)
    return out


def _delta_prep(groups):
    c = CHUNK
    row = lax.broadcasted_iota(jnp.int32, (c, C4), 0)
    coll = lax.broadcasted_iota(jnp.int32, (c, C4), 1) % c
    eye = jnp.where(row == coll, 1.0, 0.0)
    st = []
    for q, k, v, ec, gr, backward in groups:
        gi = H_B if backward else 0
        bi = 3 * H_B if backward else 2 * H_B
        gc64, gc128 = _widen(ec, gi, c), _widen(ec, gi, D_K)
        beta64, beta128 = _widen(ec, bi, c), _widen(ec, bi, D_K)
        incl = (row <= coll) if backward else (row >= coll)
        strict = (row < coll) if backward else (row > coll)
        decay = jnp.where(incl, jnp.exp(jnp.where(incl, gc64 - gr, 0.0)), 0.0)
        kf = k.astype(F32)
        qf = q.astype(F32)
        eg = jnp.exp(gc128)
        g_last = gc128[0:1, :] if backward else gc128[c - 1:c, :]
        st.append(dict(
            kf=kf, qf=qf, decay=decay, strict=strict, beta64=beta64,
            qg=(qf * eg).astype(BF16),
            kg=(kf * jnp.exp(g_last - gc128)).astype(BF16),
            rhs_v=(v.astype(F32) * beta128).astype(BF16),
            rhs_k=(kf * (beta128 * eg)).astype(BF16),
            egl=jnp.exp(g_last)))
    lane2 = lax.broadcasted_iota(jnp.int32, (c, 2 * D_K), 1)
    for s in st:
        outs = []
        for p in range(H_B // 2):
            kp = s["kf"][:, p * 2 * D_K:(p + 1) * 2 * D_K]
            qp = s["qf"][:, p * 2 * D_K:(p + 1) * 2 * D_K]
            rmat = jnp.concatenate([jnp.where(lane2 < D_K, kp, 0.0), jnp.where(lane2 >= D_K, kp, 0.0)], axis=0)
            outs.append(_dot_nt(jnp.concatenate([kp, qp], axis=0).astype(BF16), rmat.astype(BF16)))
        kk = jnp.concatenate([o[:c] for o in outs], axis=1)
        qk = jnp.concatenate([o[c:] for o in outs], axis=1)
        s["a"] = jnp.where(s["strict"], kk * s["decay"], 0.0) * s["beta64"]
        s["intra"] = (qk * s["decay"]).astype(BF16)
        s["t"] = eye - jnp.where(row // 2 == coll // 2, s["a"], 0.0)
    b = 2
    while b < c:
        join = (row // (2 * b) == coll // (2 * b)) & (row // b != coll // b)
        for s in st:
            s["tb"] = s["t"].astype(BF16)
            s["tl"] = _dot(s["tb"], _blockdiag_rows(jnp.where(join, s["a"], 0.0), H_B))
        for s in st:
            s["t"] = s["t"] - _dot(s["tl"].astype(BF16), _blockdiag_rows(s["t"], H_B))
        b *= 2
    zero = jnp.zeros((c, 2 * D_K), BF16)
    for s in st:
        tb = s["t"].astype(BF16)
        us, ws = [], []
        for h in range(H_B):
            rhs = jnp.concatenate([s["rhs_v"][:, h * D_V:(h + 1) * D_V], s["rhs_k"][:, h * D_K:(h + 1) * D_K]], axis=1)
            uw = _dot(tb, jnp.concatenate([zero] * h + [rhs] + [zero] * (H_B - 1 - h), axis=0))
            us.append(uw[:, :D_V])
            ws.append(uw[:, D_V:])
        s["u"] = jnp.concatenate(us, axis=1)
        s["w"] = jnp.concatenate(ws, axis=1).astype(BF16)
    return st


def _delta_scan_step(steps, states):
    c = CHUNK
    zero = jnp.zeros((D_K, D_V), BF16)
    mids = []
    for s, S in zip(steps, states):
        w_s, q_s = [], []
        for p in range(H_B // 2):
            sl = slice(p * 2 * D_K, (p + 1) * 2 * D_K)
            lhs = jnp.concatenate([s["w"][:, sl], s["qg"][:, sl]], axis=0)
            sa, sb = S[2 * p].astype(BF16), S[2 * p + 1].astype(BF16)
            sbd = jnp.concatenate([jnp.concatenate([sa, zero], axis=1), jnp.concatenate([zero, sb], axis=1)], axis=0)
            out = _dot(lhs, sbd)
            w_s.append(out[:c])
            q_s.append(out[c:])
        v_new = s["u"] - jnp.concatenate(w_s, axis=1)
        mids.append((v_new, jnp.concatenate(q_s, axis=1)))
    outs, new_states = [], []
    for s, S, (v_new, q_s) in zip(steps, states, mids):
        vnb = v_new.astype(BF16)
        outs.append(q_s + _dot(s["intra"], _blockdiag_rows(v_new, H_B)))
        new_states.append([S[h] * s["egl"][:, h * D_V:(h + 1) * D_V]
                           + _dot_tn(s["kg"][:, h * D_K:(h + 1) * D_K], vnb[:, h * D_V:(h + 1) * D_V])
                           for h in range(H_B)])
    return outs, new_states


def _delta_kernel(qf_ref, kf_ref, vf_ref, ecf_ref, erf_ref, qb_ref, kb_ref, vb_ref, ecb_ref, erb_ref,
                  s0f_ref, s0b_ref, of_ref, ob_ref, sf_ref, sb_ref, s_scr):
    j = pl.program_id(1)
    c = CHUNK

    @pl.when(j == 0)
    def _():
        s_scr[0] = s0f_ref[0]
        s_scr[1] = s0b_ref[0]

    def group(refs, ci, backward):
        q_ref, k_ref, v_ref, ec_ref, er_ref = refs
        rows = slice(ci * c, (ci + 1) * c)
        d = 1 if backward else 0
        return (q_ref[0, rows, :], k_ref[0, rows, :], v_ref[0, rows, :], ec_ref[0, rows, :],
                er_ref[0, ci, d:d + 1, :], backward)

    fwd = (qf_ref, kf_ref, vf_ref, ecf_ref, erf_ref)
    bwd = (qb_ref, kb_ref, vb_ref, ecb_ref, erb_ref)
    prep = _delta_prep([group(fwd, ci, False) for ci in range(DELTA_CT)]
                       + [group(bwd, ci, True) for ci in range(DELTA_CT)])
    states = [[s_scr[d, h] for h in range(H_B)] for d in range(2)]
    for step in range(DELTA_CT):
        cf, cb = step, DELTA_CT - 1 - step
        outs, states = _delta_scan_step([prep[cf], prep[DELTA_CT + cb]], states)
        of_ref[0, cf * c:(cf + 1) * c, :] = outs[0].astype(of_ref.dtype)
        ob_ref[0, cb * c:(cb + 1) * c, :] = outs[1].astype(ob_ref.dtype)
    for d in range(2):
        for h in range(H_B):
            s_scr[d, h] = states[d][h]

    @pl.when(j == pl.num_programs(1) - 1)
    def _():
        sf_ref[0] = s_scr[0]
        sb_ref[0] = s_scr[1]


def _delta_call(q, k, v, ecol, s0f, s0b):
    B, L, _ = q.shape
    n = L // CHUNK
    tl = DELTA_CT * CHUNK
    nb = L // tl
    erow = ecol[..., :2 * H_B].reshape(B, n, CHUNK, 2, H_B).transpose(0, 1, 3, 4, 2).reshape(B, n, 2, C4)
    fmap = lambda b, j: (b, j, 0)
    bmap = lambda b, j: (b, nb - 1 - j, 0)
    seq = lambda m: pl.BlockSpec((1, tl, W_B), m)
    ecs = lambda m: pl.BlockSpec((1, tl, LANES), m)
    ers = lambda m: pl.BlockSpec((1, DELTA_CT, 2, C4), lambda b, j: m(b, j) + (0,))
    st = pl.BlockSpec((1, H_B, D_K, D_V), lambda b, j: (b, 0, 0, 0))
    return pl.pallas_call(
        _delta_kernel,
        grid=(B, nb),
        in_specs=[seq(fmap), seq(fmap), seq(fmap), ecs(fmap), ers(fmap),
                  seq(bmap), seq(bmap), seq(bmap), ecs(bmap), ers(bmap), st, st],
        out_specs=[seq(fmap), seq(bmap), st, st],
        out_shape=[jax.ShapeDtypeStruct((B, L, W_B), BF16)] * 2
        + [jax.ShapeDtypeStruct((B, H_B, D_K, D_V), F32)] * 2,
        scratch_shapes=[pltpu.VMEM((2, H_B, D_K, D_V), F32)],
        compiler_params=_cparams(("parallel", "arbitrary")),
        name="delta",
    )(q, k, v, ecol, erow, q, k, v, ecol, erow, s0f, s0b)


def _merge_kernel(x_ref, oa_ref, of_ref, ob_ref, z_ref, gate_ref, g1_ref, sh2_ref, sc2_ref,
                  onw_ref, wpa_ref, wpb_ref, wo_ref, n2w_ref, wr_ref, br_ref, ltri_ref,
                  x1_ref, h2_ref, route_ref, cnt_ref, cnt_scr):
    tm = x_ref.shape[0]
    pm = ltri_ref.shape[0]
    parts = [dict(rows=slice(p * pm, (p + 1) * pm)) for p in range(tm // pm)]
    for s in parts:
        rows = s["rows"]
        o = of_ref[rows, :].astype(F32) + ob_ref[rows, :].astype(F32)
        heads = []
        for h in range(H_B):
            t = o[:, h * D_V:(h + 1) * D_V]
            heads.append(t * lax.rsqrt(jnp.mean(t * t, axis=-1, keepdims=True) + EPS))
        ob = jnp.concatenate(heads, axis=-1) * onw_ref[...] * z_ref[rows, :].astype(F32)
        s["pa"] = _dot(oa_ref[rows, :], wpa_ref[...])
        s["pb"] = _dot(ob.astype(BF16), wpb_ref[...])
    for s in parts:
        gate = gate_ref[s["rows"], :]
        merged = gate[:, :D_MODEL].astype(F32) * s["pa"] + gate[:, D_MODEL:].astype(F32) * s["pb"]
        s["mix"] = _dot(merged.astype(BF16), wo_ref[...])
    for s in parts:
        rows = s["rows"]
        x1 = x_ref[rows, :] + g1_ref[0] * s["mix"]
        x1_ref[rows, :] = x1
        ms = jnp.mean(x1 * x1, axis=-1, keepdims=True)
        h2 = x1 * lax.rsqrt(ms + EPS) * n2w_ref[...]
        h2 = h2 * (1.0 + sc2_ref[0]) + sh2_ref[0]
        hi = h2.astype(BF16)
        for c in range(ROW_CHUNKS):
            h2_ref[pl.ds(rows.start * ROW_CHUNKS + c, pm, stride=ROW_CHUNKS), :] = h2[:, c * LANES:(c + 1) * LANES]
        lo = (h2 - hi.astype(F32)).astype(BF16)
        s["logits"] = _dot(jnp.concatenate([hi, lo, hi], axis=-1), wr_ref[...]) + br_ref[...]
    lane = lax.broadcasted_iota(jnp.int32, (pm, LANES), 1)
    big = jnp.int32(LANES)

    def first_max(vals, mask):
        mv = jnp.where(mask, vals, NEG_INF)
        m = mv.max(axis=-1, keepdims=True)
        idx = jnp.where(mask & (mv == m), lane, big).min(axis=-1, keepdims=True)
        return m, idx

    for s in parts:
        logits = s["logits"]
        gmask = lane < N_GROUPS
        gm, grp = first_max(logits, gmask)
        p_grp = 1.0 / jnp.where(gmask, jnp.exp(logits - gm), 0.0).sum(axis=-1, keepdims=True)
        lo_lane = N_GROUPS + grp * EXP_PER_GROUP
        emask = (lane >= lo_lane) & (lane < lo_lane + EXP_PER_GROUP)
        em, _ = first_max(logits, emask)
        ex = jnp.where(emask, jnp.exp(logits - em), 0.0)
        probs = ex / ex.sum(axis=-1, keepdims=True)
        p1, i1 = first_max(probs, emask)
        p2, i2 = first_max(probs, emask & (lane != i1))
        tot = p1 + p2
        s["w1"] = p_grp * p1 / tot
        s["w2"] = p_grp * p2 / tot
        s["e1"] = i1 - N_GROUPS
        s["e2"] = i2 - N_GROUPS
        s["hit"] = jnp.where(lane == s["e1"], 1.0, 0.0) + jnp.where(lane == s["e2"], 1.0, 0.0)
        s["within"] = _dot(ltri_ref[...], s["hit"].astype(BF16))

    @pl.when(pl.program_id(0) == 0)
    def _():
        cnt_scr[...] = jnp.zeros_like(cnt_scr)

    running = cnt_scr[...]
    for s in parts:
        before = s["within"] + running
        running = running + s["hit"].sum(axis=0, keepdims=True)
        r1 = jnp.where(lane == s["e1"], before, 0.0).sum(axis=-1, keepdims=True)
        r2 = jnp.where(lane == s["e2"], before, 0.0).sum(axis=-1, keepdims=True)
        cols = [s["e1"].astype(F32), s["e2"].astype(F32), s["w1"], s["w2"], r1, r2]
        route = jnp.zeros((pm, LANES), F32)
        for i, col in enumerate(cols):
            route = jnp.where(lane == i, col, route)
        route_ref[s["rows"], :] = route
    cnt_scr[...] = running
    cnt_ref[...] = running


def _merge_call(x, oa, of, ob, z, gate, mod3, mod_stride, w):
    B, L, _ = x.shape
    n_tok = B * L
    tm = ROW_TILE
    assert n_tok % tm == 0 and (L % tm == 0 or tm % L == 0)
    flat = lambda a: a.reshape(n_tok, a.shape[-1])
    rspec = lambda n: pl.BlockSpec((tm, n), lambda i: (i, 0))
    mspec = lambda k: pl.BlockSpec((1, 1, D_MODEL), lambda i: ((i * tm) // L * mod_stride + k, 0, 0))
    t = np.arange(MERGE_PART)
    ltri = jnp.asarray((t[:, None] > t[None, :]).astype(np.float32)).astype(BF16)
    consts = [w["onw"], w["wpa"], w["wpb"], w["wo"], w["norm2_w"], w["wr3"], w["br"], ltri]
    return pl.pallas_call(
        _merge_kernel,
        grid=(n_tok // tm,),
        in_specs=[rspec(D_MODEL), rspec(W_A), rspec(W_B), rspec(W_B), rspec(W_B), rspec(2 * D_MODEL),
                  mspec(2), mspec(3), mspec(4)] + [_const_spec(c.shape) for c in consts],
        out_specs=[rspec(D_MODEL), pl.BlockSpec((tm * SUB, LANES), lambda i: (i, 0)), rspec(LANES),
                   _const_spec((1, LANES))],
        out_shape=[jax.ShapeDtypeStruct((n_tok, D_MODEL), F32),
                   jax.ShapeDtypeStruct((n_tok * SUB, LANES), F32),
                   jax.ShapeDtypeStruct((n_tok, LANES), F32),
                   jax.ShapeDtypeStruct((1, LANES), F32)],
        scratch_shapes=[pltpu.VMEM((1, LANES), F32)],
        compiler_params=_cparams(("arbitrary",)),
        name="merge",
    )(flat(x), flat(oa), flat(of), flat(ob), flat(z), flat(gate), mod3, mod3, mod3, *consts)


SUB = 8
ROW_CHUNKS = D_MODEL // LANES


def _store_token_major(ref, x):
    for s in range(ROW_CHUNKS):
        ref[pl.ds(s, x.shape[0], stride=ROW_CHUNKS), :] = x[:, s * LANES:(s + 1) * LANES]


def _load_token_major(ref, slot, row0, n_tok):
    return jnp.concatenate([ref[slot, pl.ds(row0 * ROW_CHUNKS + s, n_tok, stride=ROW_CHUNKS), :]
                            for s in range(ROW_CHUNKS)], axis=1)


GATHER_CHUNK = 32


def _row_chunks(n_rows, n_valid, fn):
    if n_valid is None:
        for r in range(n_rows):
            fn(r)
        return
    for c0 in range(0, n_rows, GATHER_CHUNK):
        @pl.when(c0 < n_valid)
        def _():
            for r in range(c0, c0 + GATHER_CHUNK):
                fn(r)


def _row_gather_start(idx_ref, n_rows, n_valid, src_hbm, buf, sem, slot):
    def start(r):
        src = pl.multiple_of(idx_ref[0, 0, r] * SUB, SUB)
        pltpu.make_async_copy(src_hbm.at[pl.ds(src, SUB), :], buf.at[slot, pl.ds(r * SUB, SUB), :],
                              sem.at[slot]).start(priority=r % 2)

    _row_chunks(n_rows, n_valid, start)


def _row_gather_wait(n_rows, n_valid, src_hbm, buf, sem, slot):
    def wait(r):
        pltpu.make_async_copy(src_hbm.at[pl.ds(0, SUB), :], buf.at[slot, pl.ds(r * SUB, SUB), :],
                              sem.at[slot]).wait()

    _row_chunks(n_rows, n_valid, wait)


def _gather_pipeline(i, n_steps, idx_ref, idx_next_ref, n_rows, src_hbm, buf, sem, valid=None, valid_next=None):
    slot = lax.rem(i, 2)

    @pl.when(i == 0)
    def _():
        _row_gather_start(idx_ref, n_rows, valid, src_hbm, buf, sem, 0)

    @pl.when(i + 1 < n_steps)
    def _():
        _row_gather_start(idx_next_ref, n_rows, valid_next, src_hbm, buf, sem, 1 - slot)

    @pl.when(i < n_steps)
    def _():
        _row_gather_wait(n_rows, valid, src_hbm, buf, sem, slot)

    return slot


def _moe_kernel(te_ref, nt_ref, tv_ref, idx_ref, idxn_ref, h2_hbm, wg_ref, wu_ref, wd_ref, y_ref, xbuf, sem):
    i = pl.program_id(0)
    n_used = nt_ref[0]

    @pl.when(i == 0)
    def _():
        xbuf[...] = jnp.zeros_like(xbuf)

    slot = _gather_pipeline(i, n_used, idx_ref, idxn_ref, MOE_TILE, h2_hbm, xbuf, sem,
                            valid=tv_ref[i], valid_next=tv_ref[jnp.minimum(i + 1, pl.num_programs(0) - 1)])

    @pl.when(i < n_used)
    def _():
        x = _load_token_major(xbuf, slot, 0, MOE_TILE).astype(BF16)
        g = _dot(x, wg_ref[0])
        hid = g * jax.nn.sigmoid(g) * _dot(x, wu_ref[0])
        _store_token_major(y_ref, _dot(hid.astype(BF16), wd_ref[0]))

    @pl.when(i >= n_used)
    def _():
        y_ref[...] = jnp.zeros_like(y_ref)


def _moe_call(h2, src_tok, tile_expert, n_tiles_used, tile_valid, wg, wu, wd):
    p_pad = src_tok.shape[0]
    n_tiles = p_pad // MOE_TILE
    idx = src_tok.reshape(n_tiles, 1, MOE_TILE)
    smem_tile = lambda m: pl.BlockSpec((1, 1, MOE_TILE), m, memory_space=pltpu.SMEM)
    grid_spec = pltpu.PrefetchScalarGridSpec(
        num_scalar_prefetch=3,
        grid=(n_tiles,),
        in_specs=[smem_tile(lambda i, te, nt, tv: (i, 0, 0)),
                  smem_tile(lambda i, te, nt, tv: (jnp.minimum(i + 1, n_tiles - 1), 0, 0)),
                  pl.BlockSpec(memory_space=pl.ANY),
                  pl.BlockSpec((1, D_MODEL, D_FF_E), lambda i, te, nt, tv: (te[i], 0, 0)),
                  pl.BlockSpec((1, D_MODEL, D_FF_E), lambda i, te, nt, tv: (te[i], 0, 0)),
                  pl.BlockSpec((1, D_FF_E, D_MODEL), lambda i, te, nt, tv: (te[i], 0, 0))],
        out_specs=pl.BlockSpec((MOE_TILE * SUB, LANES), lambda i, te, nt, tv: (i, 0)),
        scratch_shapes=[pltpu.VMEM((2, MOE_TILE * SUB, LANES), F32), pltpu.SemaphoreType.DMA((2,))],
    )
    return pl.pallas_call(
        _moe_kernel,
        grid_spec=grid_spec,
        out_shape=jax.ShapeDtypeStruct((p_pad * SUB, LANES), F32),
        compiler_params=_cparams(("arbitrary",)),
        name="moe",
    )(tile_expert, n_tiles_used, tile_valid, idx, idx, h2, wg, wu, wd)


COMBINE_TILE = 256


def _combine_kernel(pos_ref, posn_ref, x1_ref, route_ref, g2_ref, y_hbm, o_ref, ybuf, sem):
    i = pl.program_id(0)
    t = COMBINE_TILE
    slot = _gather_pipeline(i, pl.num_programs(0), pos_ref, posn_ref, 2 * t, y_hbm, ybuf, sem)
    route = route_ref[...]
    w1 = route[:, 2:3]
    w2 = route[:, 3:4]
    ya = _load_token_major(ybuf, slot, 0, t)
    yb = _load_token_major(ybuf, slot, t, t)
    o_ref[...] = x1_ref[...] + g2_ref[0] * (w1 * ya + w2 * yb)


def _combine_call(x1, route, pos2, y, mod3, mod_stride, seq_len):
    n = x1.shape[0]
    t = COMBINE_TILE
    n_tiles = n // t
    pos_tiles = pos2.reshape(n_tiles, t, 2).transpose(0, 2, 1).reshape(n_tiles, 1, 2 * t)
    smem_tile = lambda m: pl.BlockSpec((1, 1, 2 * t), m, memory_space=pltpu.SMEM)
    return pl.pallas_call(
        _combine_kernel,
        grid=(n_tiles,),
        in_specs=[smem_tile(lambda i: (i, 0, 0)),
                  smem_tile(lambda i: (jnp.minimum(i + 1, n_tiles - 1), 0, 0)),
                  pl.BlockSpec((t, D_MODEL), lambda i: (i, 0)),
                  pl.BlockSpec((t, LANES), lambda i: (i, 0)),
                  pl.BlockSpec((1, 1, D_MODEL), lambda i: ((i * t) // seq_len * mod_stride + 5, 0, 0)),
                  pl.BlockSpec(memory_space=pl.ANY)],
        out_specs=pl.BlockSpec((t, D_MODEL), lambda i: (i, 0)),
        out_shape=jax.ShapeDtypeStruct((n, D_MODEL), F32),
        scratch_shapes=[pltpu.VMEM((2, 2 * t * SUB, LANES), F32), pltpu.SemaphoreType.DMA((2,))],
        compiler_params=_cparams(("arbitrary",)),
        name="combine",
    )(pos_tiles, pos_tiles, x1, route, mod3, y)


def _inverse_map_kernel(pos_ref, init_hbm, out_ref):
    pltpu.sync_copy(init_hbm, out_ref)

    def body(p, carry):
        out_ref[pos_ref[p]] = lax.shift_right_logical(p, 1)
        return carry

    lax.fori_loop(0, pos_ref.shape[0], body, 0, unroll=8)


def _inverse_map_call(pos, init):
    return pl.pallas_call(
        _inverse_map_kernel,
        in_specs=[pl.BlockSpec(memory_space=pltpu.SMEM), pl.BlockSpec(memory_space=pl.ANY)],
        out_specs=pl.BlockSpec(memory_space=pltpu.SMEM),
        out_shape=jax.ShapeDtypeStruct(init.shape, jnp.int32),
        name="inverse_map",
    )(pos, init)


def _moe_forward(h2, route, counts, wg, wu, wd):
    n = route.shape[0]
    eid = route[:, :2].astype(jnp.int32).reshape(-1)
    rank = route[:, 4:6].astype(jnp.int32).reshape(-1)
    counts = counts[0, :N_EXPERTS].astype(jnp.int32)
    padded = ((counts + MOE_TILE - 1) // MOE_TILE) * MOE_TILE
    seg_end = jnp.cumsum(padded)
    seg_start = seg_end - padded
    pos = seg_start[eid] + rank
    p_pad = 2 * n + N_EXPERTS * MOE_TILE
    src_tok = _inverse_map_call(pos, jnp.arange(p_pad, dtype=jnp.int32) % n)
    n_tiles = p_pad // MOE_TILE
    tile_start = jnp.arange(n_tiles, dtype=jnp.int32) * MOE_TILE
    tile_expert = jnp.minimum(jnp.sum((tile_start[:, None] >= seg_end[None, :]).astype(jnp.int32), axis=1),
                              N_EXPERTS - 1)
    n_used = (seg_end[-1] // MOE_TILE).astype(jnp.int32).reshape(1)
    tile_valid = jnp.clip(seg_start[tile_expert] + counts[tile_expert] - tile_start, 0, MOE_TILE)
    tile_valid = jnp.where(tile_start < seg_end[-1], tile_valid, 0).astype(jnp.int32)
    y = _moe_call(h2, src_tok, tile_expert, n_used, tile_valid, wg, wu, wd)
    return y, pos.reshape(n, 2)


def _prep_weights(norm1_w, norm2_w, w_in, conv_w, a_log, dt_bias, q_norm_w, k_norm_w, o_norm_w,
                  w_proj_a, w_proj_b, w_out, w_router_g, b_router_g, w_router_e, b_router_e,
                  w_exp_gate, w_exp_up, w_exp_down):
    w = {}
    w["norm1_w"] = norm1_w.reshape(1, D_MODEL)
    w["norm2_w"] = norm2_w.reshape(1, D_MODEL)
    o = 0
    w["wa"] = w_in[:, o:o + 3 * W_A].astype(BF16); o += 3 * W_A
    w["wb"] = w_in[:, o:o + 3 * W_B].astype(BF16); o += 3 * W_B
    w["wz"] = w_in[:, o:o + W_B].astype(BF16); o += W_B
    small = w_in[:, o:o + 4 * H_B]; o += 4 * H_B
    w["wg"] = w_in[:, o:o + 2 * D_MODEL].astype(BF16)
    def group_lanes(t):
        r = t.shape[0]
        t = t.reshape(r, -1, N_HG, HP).transpose(0, 2, 1, 3).reshape(r, N_HG, -1)
        return jnp.pad(t, ((0, 0), (0, 0), (0, LANES - t.shape[-1]))).reshape(r, N_HG * LANES)

    w["we"] = group_lanes(small).astype(BF16)
    w["alog_l"] = group_lanes(a_log.reshape(1, 2 * H_B))
    w["dtb_l"] = group_lanes(dt_bias.reshape(1, 2 * H_B))
    w["qnw"] = jnp.tile(q_norm_w, H_A).reshape(1, W_A)
    w["knw"] = jnp.tile(k_norm_w, H_A).reshape(1, W_A)
    w["onw"] = jnp.tile(o_norm_w, H_B).reshape(1, W_B)
    blk = np.arange(256) // D_HA
    w["bd256"] = jnp.asarray((blk[:, None] == blk[None, :]).astype(np.float32)).astype(BF16)
    w["conv_w"] = conv_w
    w["wpa"] = w_proj_a.astype(BF16)
    w["wpb"] = w_proj_b.astype(BF16)
    w["wo"] = w_out.astype(BF16)
    wr = jnp.concatenate([w_router_g, w_router_e], axis=1)
    wr = jnp.pad(wr, ((0, 0), (0, LANES - wr.shape[1])))
    wr_hi = wr.astype(BF16)
    wr_lo = (wr - wr_hi.astype(F32)).astype(BF16)
    w["wr3"] = jnp.concatenate([wr_hi, wr_hi, wr_lo], axis=0)
    br = jnp.concatenate([b_router_g, b_router_e])
    w["br"] = jnp.pad(br, (0, LANES - br.shape[0])).reshape(1, LANES)
    w["wg_e"] = w_exp_gate.astype(BF16)
    w["wu_e"] = w_exp_up.astype(BF16)
    w["wd_e"] = w_exp_down.astype(BF16)
    return w


def _layer(x, mod3, mod_stride, w, ctx):
    B, L, _ = x.shape
    latent = ctx is not None
    qa, ka, va, qkvb, z, gate, ecol, *new_kv = _inproj_call(x, mod3, mod_stride, w, not latent)
    if latent:
        k_ctx, v_ctx, s0f, s0b, bias = ctx
        oa = _nbr_attn_call(qa, ka, va, k_ctx, v_ctx, bias)
    else:
        s0f = s0b = jnp.zeros((B, H_B, D_K, D_V), F32)
        oa = _ctx_attn_call(qa, ka, va)
    qd, kd, vd = _conv_call(qkvb, w["conv_w"], rope=latent)
    of, ob, s_f, s_b = _delta_call(qd, kd, vd, ecol, s0f, s0b)
    x1, h2, route, counts = _merge_call(x, oa, of, ob, z, gate, mod3, mod_stride, w)
    ys, pos2 = _moe_forward(h2, route, counts, w["wg_e"], w["wu_e"], w["wd_e"])
    y = _combine_call(x1, route, pos2, ys, mod3, mod_stride, L)
    return y.reshape(B, L, D_MODEL), new_kv, s_f, s_b


def kernel(x_prompt, x_sample, c, cache_attn_k, cache_attn_v, state_delta_fwd, state_delta_bwd, c_ctx,
           norm1_w, norm2_w, w_mod, b_mod, w_in, conv_w, a_log, dt_bias, q_norm_w, k_norm_w, rpb,
           o_norm_w, w_proj_a, w_proj_b, w_out, w_router_g, b_router_g, w_router_e, b_router_e,
           w_exp_gate, w_exp_up, w_exp_down):
    assert norm1_w.shape[0] == 1, "single-layer trunk"
    Bp, Lp, _ = x_prompt.shape
    Bs, Ls, _ = x_sample.shape
    l = 0
    w = _prep_weights(norm1_w[l], norm2_w[l], w_in[l], conv_w[l], a_log[l], dt_bias[l], q_norm_w[l],
                      k_norm_w[l], o_norm_w[l], w_proj_a[l], w_proj_b[l], w_out[l], w_router_g[l],
                      b_router_g[l], w_router_e[l], b_router_e[l], w_exp_gate[l], w_exp_up[l],
                      w_exp_down[l])
    cond = jnp.concatenate([c_ctx[None, :], c, jnp.zeros((8 - 1 - Bs, D_MODEL), F32)], axis=0)
    mod = _mod_call(cond, w_mod[l], b_mod[l])
    mod3 = mod.reshape(8 * 6, 1, D_MODEL)
    bias = _bias_table_call(rpb[l])
    y_p, (new_k, new_v), s_f, s_b = _layer(x_prompt, mod3[:6], 0, w, None)
    lc = cache_attn_k.shape[2]
    ctx = (cache_attn_k[:, l].reshape(Bs, lc, W_A).astype(BF16),
           cache_attn_v[:, l].reshape(Bs, lc, W_A).astype(BF16),
           state_delta_fwd[:, l], state_delta_bwd[:, l], bias)
    y_s, _, _, _ = _layer(x_sample, mod3[6:6 + 6 * Bs], 6, w, ctx)
    return (y_p, y_s, new_k, new_v, s_f[:, None], s_b[:, None])
```

```python
import functools
import math

import numpy as np
import jax
import jax.numpy as jnp
from jax import lax
from jax.experimental import pallas as pl
from jax.experimental.pallas import tpu as pltpu

F32 = jnp.float32
BF16 = jnp.bfloat16

D_MODEL = 1024
GRID_W = 64
H_A = 8
D_HA = 64
W_A = H_A * D_HA
WIN_H = 8
WIN_W = 16
H_B = 4
D_K = 128
D_V = 128
W_B = H_B * D_V
CONV_K = 5
CHUNK = 64
ROPE_BASE = 10000.0
N_GROUPS = 4
EXP_PER_GROUP = 8
N_EXPERTS = N_GROUPS * EXP_PER_GROUP
D_FF_E = 256
EPS = 1e-6
NEG_INF = -1e30

LANES = 128
HP = H_B
N_HG = H_B // HP
ROW_TILE = 512
INPROJ_TILE = 256
INPROJ_PART = 256
MERGE_PART = 256
MOE_TILE = 256
VMEM_LIMIT = 56 * 1024 * 1024

HIGHEST = lax.Precision.HIGHEST


def _cparams(sem):
    return pltpu.CompilerParams(dimension_semantics=sem, vmem_limit_bytes=VMEM_LIMIT)


def _dot(a, b):
    return jnp.dot(a, b, preferred_element_type=F32)


def _dot_nt(a, b):
    return lax.dot_general(a, b, (((1,), (1,)), ((), ())), preferred_element_type=F32)


def _dot_tn(a, b):
    return lax.dot_general(a, b, (((0,), (0,)), ((), ())), preferred_element_type=F32)


def _const_spec(shape):
    nd = len(shape)
    return pl.BlockSpec(shape, lambda *_: (0,) * nd)


def _mod_kernel(c_ref, w_ref, b_ref, o_ref):
    c = c_ref[...]
    s = c * jax.nn.sigmoid(c)
    o_ref[...] = jnp.dot(s, w_ref[...], preferred_element_type=F32, precision=HIGHEST) + b_ref[...]


def _mod_call(cond8, w_mod, b_mod):
    n_out = w_mod.shape[1]
    tn = 1024
    return pl.pallas_call(
        _mod_kernel,
        grid=(n_out // tn,),
        in_specs=[pl.BlockSpec((8, D_MODEL), lambda j: (0, 0)),
                  pl.BlockSpec((D_MODEL, tn), lambda j: (0, j)),
                  pl.BlockSpec((1, tn), lambda j: (0, j))],
        out_specs=pl.BlockSpec((8, tn), lambda j: (0, j)),
        out_shape=jax.ShapeDtypeStruct((8, n_out), F32),
        compiler_params=_cparams(("arbitrary",)),
        name="mod",
    )(cond8, w_mod, b_mod.reshape(1, n_out))


def _softplus(x):
    return jnp.maximum(x, 0.0) + jnp.log1p(jnp.exp(-jnp.abs(x)))


def _inproj_kernel(x_ref, sh_ref, sc_ref, nw_ref, wa_ref, wb_ref, wz_ref, wg_ref, we_ref,
                   qnw_ref, knw_ref, bd_ref, alog_ref, dtb_ref, tri_ref,
                   qa_ref, ka_ref, va_ref, qkvb_ref, z_ref, gate_ref, e_ref, *kv5_refs):
    tm = x_ref.shape[0]
    pm = tri_ref.shape[1]
    parts = [dict(rows=slice(p * pm, (p + 1) * pm)) for p in range(tm // pm)]
    for s in parts:
        x = x_ref[s["rows"], :]
        ms = jnp.mean(x * x, axis=-1, keepdims=True)
        h = x * lax.rsqrt(ms + EPS) * nw_ref[...]
        h = h * (1.0 + sc_ref[0]) + sh_ref[0]
        s["hb"] = h.astype(BF16)
    for s in parts:
        s["e"] = _dot(s["hb"], we_ref[...])
    for s in parts:
        s["a"] = _dot(s["hb"], wa_ref[...])
    for s in parts:
        qkvb_ref[s["rows"], :] = _dot(s["hb"], wb_ref[...]).astype(qkvb_ref.dtype)
    bd = bd_ref[...]

    def head_rms(t, w):
        sq = (t * t).astype(BF16)
        ss = jnp.concatenate([_dot(sq[:, i * 256:(i + 1) * 256], bd) for i in range(W_A // 256)], axis=-1)
        return t * lax.rsqrt(ss * (1.0 / D_HA) + EPS) * w

    for p, s in enumerate(parts):
        rows, a = s["rows"], s["a"]
        qa = head_rms(a[:, :W_A], qnw_ref[...]) * (D_HA ** -0.5)
        qa_ref[rows, :] = qa.astype(qa_ref.dtype)
        ka = head_rms(a[:, W_A:2 * W_A], knw_ref[...])
        va = a[:, 2 * W_A:]
        ka_ref[rows, :] = ka.astype(ka_ref.dtype)
        va_ref[rows, :] = va.astype(va_ref.dtype)
        if kv5_refs:
            seq_len = kv5_refs[0].shape[2]
            b_in, off = (p * pm) // seq_len, (p * pm) % seq_len
            for ref, t in zip(kv5_refs, (ka, va)):
                for hd in range(H_A):
                    ref[b_in, 0, off:off + pm, hd, :] = t[:, hd * D_HA:(hd + 1) * D_HA]
    for s in parts:
        z = _dot(s["hb"], wz_ref[...])
        z_ref[s["rows"], :] = (z * jax.nn.sigmoid(z)).astype(z_ref.dtype)
    lane = lax.broadcasted_iota(jnp.int32, (pm, N_HG * LANES), 1) % LANES
    tri = tri_ref[...]
    for s in parts:
        e = s["e"]
        g = -jnp.exp(alog_ref[...]) * _softplus(e + dtb_ref[...])
        act = jnp.where(lane < 2 * HP, g, jax.nn.sigmoid(e))
        a_hi = act.astype(BF16)
        rem = act - a_hi.astype(F32)
        a_mid = rem.astype(BF16)
        a_lo = (rem - a_mid.astype(F32)).astype(BF16)
        cum = _dot(tri, a_hi) + _dot(tri, a_mid) + _dot(tri, a_lo)
        e_ref[s["rows"], :] = jnp.where(lane < HP, cum[:pm], jnp.where(lane < 2 * HP, cum[pm:], act))
    for s in parts:
        gate_ref[s["rows"], :] = jax.nn.sigmoid(_dot(s["hb"], wg_ref[...])).astype(gate_ref.dtype)


def _inproj_call(x, mod3, mod_stride, w, new_cache):
    B, L, _ = x.shape
    n_tok = B * L
    tm, pm = INPROJ_TILE, INPROJ_PART
    assert n_tok % tm == 0 and (L % tm == 0 or (tm % L == 0 and mod_stride == 0)) and L % pm == 0
    row = lambda n, dt: jax.ShapeDtypeStruct((n_tok, n), dt)
    rspec = lambda n: pl.BlockSpec((tm, n), lambda i: (i, 0))
    mspec = lambda k: pl.BlockSpec((1, 1, D_MODEL), lambda i: ((i * tm) // L * mod_stride + k, 0, 0))
    t = np.arange(pm)
    same = (t[:, None] // CHUNK) == (t[None, :] // CHUNK)
    tri = np.concatenate([same & (t[:, None] >= t[None, :]), same & (t[:, None] <= t[None, :])], axis=0)
    consts = [w["norm1_w"], w["wa"], w["wb"], w["wz"], w["wg"], w["we"], w["qnw"], w["knw"],
              w["bd256"], w["alog_l"], w["dtb_l"], jnp.asarray(tri.astype(np.float32)).astype(BF16)]
    out_specs = [rspec(W_A), rspec(W_A), rspec(W_A), rspec(3 * W_B), rspec(W_B),
                 rspec(2 * D_MODEL), rspec(N_HG * LANES)]
    out_shape = [row(W_A, BF16), row(W_A, BF16), row(W_A, BF16), row(3 * W_B, BF16),
                 row(W_B, BF16), row(2 * D_MODEL, BF16), row(N_HG * LANES, F32)]
    if new_cache:
        if tm <= L:
            per_seq = L // tm
            cache_spec = pl.BlockSpec((1, 1, tm, H_A, D_HA), lambda i: (i // per_seq, 0, i % per_seq, 0, 0))
        else:
            cache_spec = pl.BlockSpec((tm // L, 1, L, H_A, D_HA), lambda i: (i, 0, 0, 0, 0))
        out_specs += [cache_spec] * 2
        out_shape += [jax.ShapeDtypeStruct((B, 1, L, H_A, D_HA), F32)] * 2
    outs = pl.pallas_call(
        _inproj_kernel,
        grid=(n_tok // tm,),
        in_specs=[rspec(D_MODEL), mspec(0), mspec(1)] + [_const_spec(c.shape) for c in consts],
        out_specs=out_specs,
        out_shape=out_shape,
        compiler_params=_cparams(("parallel",)),
        name="inproj",
    )(x.reshape(n_tok, D_MODEL), mod3, mod3, *consts)
    return [o.reshape(B, L, o.shape[-1]) for o in outs[:7]] + list(outs[7:])


def _pair_softmax_pv(s_parts, v_parts):
    m = s_parts[0].max(axis=-1, keepdims=True)
    for s in s_parts[1:]:
        m = jnp.maximum(m, s.max(axis=-1, keepdims=True))
    den = 0.0
    acc = 0.0
    for s, v in zip(s_parts, v_parts):
        p = jnp.exp(s - m)
        den = den + p.sum(axis=-1, keepdims=True)
        acc = acc + _dot(p.astype(BF16), v)
    return acc / den


def _stack_pair(qp):
    lane = lax.broadcasted_iota(jnp.int32, qp.shape, 1)
    zero = jnp.zeros_like(qp)
    return jnp.concatenate([jnp.where(lane < D_HA, qp, zero), jnp.where(lane >= D_HA, qp, zero)], axis=0)


def _unstack_pair(o2):
    t = o2.shape[0] // 2
    lane = lax.broadcasted_iota(jnp.int32, (t, LANES), 1)
    return jnp.where(lane < D_HA, o2[:t], o2[t:])


def _ctx_attn_kernel(q_ref, k_ref, v_ref, o_ref):
    for p in range(H_A // 2):
        sl = slice(p * LANES, (p + 1) * LANES)
        qs = _stack_pair(q_ref[0, :, sl])
        kp = k_ref[0, :, sl].astype(BF16)
        vp = v_ref[0, :, sl].astype(BF16)
        o2 = _pair_softmax_pv([_dot_nt(qs, kp)], [vp])
        o_ref[0, :, sl] = _unstack_pair(o2).astype(o_ref.dtype)


def _ctx_attn_call(q, k, v):
    B, L, _ = q.shape
    spec = pl.BlockSpec((1, L, W_A), lambda b: (b, 0, 0))
    return pl.pallas_call(
        _ctx_attn_kernel,
        grid=(B,),
        in_specs=[spec, spec, spec],
        out_specs=spec,
        out_shape=jax.ShapeDtypeStruct((B, L, W_A), BF16),
        compiler_params=_cparams(("parallel",)),
        name="ctx_attn",
    )(q, k, v)


N_DR = 2 * WIN_H - 1
N_DC = 2 * WIN_W - 1
NBR_ROWS = 8


def _bias_table_kernel(rpb_ref, o_ref):
    h = pl.program_id(0)
    c = lax.broadcasted_iota(jnp.int32, (GRID_W, 2 * GRID_W), 0)
    lane = lax.broadcasted_iota(jnp.int32, (GRID_W, 2 * GRID_W), 1)
    kc = lane % GRID_W
    ws = jnp.clip(c - WIN_W // 2, 0, GRID_W - WIN_W)
    in_win = (kc >= ws) & (kc < ws + WIN_W)
    dc = kc - c + (WIN_W - 1)
    for dr in range(N_DR - 1):
        base = (h * N_DR + dr) * N_DC
        val = jnp.zeros((GRID_W, 2 * GRID_W), F32)
        for d in range(N_DC):
            val = jnp.where(dc == d, jnp.where(lane < GRID_W, rpb_ref[base + d], rpb_ref[base + N_DC + d]), val)
        o_ref[0, dr] = jnp.where(in_win, val, NEG_INF)


def _bias_table_call(rpb):
    return pl.pallas_call(
        _bias_table_kernel,
        grid=(H_A,),
        in_specs=[pl.BlockSpec(memory_space=pltpu.SMEM)],
        out_specs=pl.BlockSpec((1, N_DR - 1, GRID_W, 2 * GRID_W), lambda h: (h, 0, 0, 0)),
        out_shape=jax.ShapeDtypeStruct((H_A, N_DR - 1, GRID_W, 2 * GRID_W), F32),
        compiler_params=_cparams(("arbitrary",)),
        name="bias_table",
    )(rpb.reshape(-1))


def _nbr_attn_kernel(q_ref, k_ref, v_ref, kc_ref, vc_ref, bias_ref, o_ref, *, rows):
    inst = []
    for rr in range(NBR_ROWS):
        r = pl.program_id(1) * NBR_ROWS + rr
        first = jnp.clip(r - WIN_H // 2, 0, rows - WIN_H)
        start = pl.multiple_of(first * GRID_W, GRID_W)
        shift = r - first
        for p in range(H_A // 2):
            inst.append((rr, p, start, shift))
    scores = []
    for rr, p, start, shift in inst:
        sl = slice(p * LANES, (p + 1) * LANES)
        qs = _stack_pair(q_ref[0, rr * GRID_W:(rr + 1) * GRID_W, sl])
        bias = jnp.concatenate(
            [jnp.concatenate([bias_ref[2 * p + hh, 2 * u - shift + WIN_H - 1] for u in range(WIN_H // 2)], axis=1)
             for hh in range(2)], axis=0)
        s_loc = _dot_nt(qs, k_ref[0, pl.ds(start, WIN_H * GRID_W), sl]) + bias
        s_ctx = _dot_nt(qs, kc_ref[0, :, sl])
        scores.append((s_loc, s_ctx))
    for (rr, p, start, shift), (s_loc, s_ctx) in zip(inst, scores):
        sl = slice(p * LANES, (p + 1) * LANES)
        o2 = _pair_softmax_pv([s_loc, s_ctx], [v_ref[0, pl.ds(start, WIN_H * GRID_W), sl], vc_ref[0, :, sl]])
        o_ref[0, rr * GRID_W:(rr + 1) * GRID_W, sl] = _unstack_pair(o2).astype(o_ref.dtype)


def _nbr_attn_call(q, k, v, k_ctx, v_ctx, bias):
    B, L, _ = q.shape
    rows = L // GRID_W
    lc = k_ctx.shape[1]
    full = pl.BlockSpec((1, L, W_A), lambda b, r: (b, 0, 0))
    ctx = pl.BlockSpec((1, lc, W_A), lambda b, r: (b, 0, 0))
    rowspec = pl.BlockSpec((1, NBR_ROWS * GRID_W, W_A), lambda b, r: (b, r, 0))
    return pl.pallas_call(
        functools.partial(_nbr_attn_kernel, rows=rows),
        grid=(B, rows // NBR_ROWS),
        in_specs=[rowspec, full, full, ctx, ctx, _const_spec(bias.shape)],
        out_specs=rowspec,
        out_shape=jax.ShapeDtypeStruct((B, L, W_A), BF16),
        compiler_params=_cparams(("parallel", "arbitrary")),
        name="nbr_attn",
    )(q, k, v, k_ctx, v_ctx, bias)


HALO = 16


def _conv_kernel(*refs, rope):
    if rope:
        x_ref, prev_ref, next_ref, w_ref, cos_ref, sin_ref, q_ref, k_ref, v_ref, xe_ref = refs
    else:
        x_ref, prev_ref, next_ref, w_ref, q_ref, k_ref, v_ref, xe_ref = refs
    i = pl.program_id(1)
    n_i = pl.num_programs(1)
    tl = x_ref.shape[1]
    xe_ref[0:HALO] = jnp.where(i > 0, prev_ref[0].astype(F32), 0.0)
    xe_ref[HALO:HALO + tl] = x_ref[0].astype(F32)
    xe_ref[HALO + tl:] = jnp.where(i < n_i - 1, next_ref[0].astype(F32), 0.0)
    half = CONV_K // 2
    y = w_ref[0:1, :] * xe_ref[HALO - half:HALO - half + tl]
    for j in range(1, CONV_K):
        y = y + w_ref[j:j + 1, :] * xe_ref[HALO - half + j:HALO - half + j + tl]
    y = y * jax.nn.sigmoid(y)

    def l2n(t):
        return t * lax.rsqrt(jnp.sum(t * t, axis=-1, keepdims=True) + EPS)

    def rot(t):
        if not rope:
            return t
        lane = lax.broadcasted_iota(jnp.int32, t.shape, 1)
        swapped = jnp.where(lane % 2 == 0, pltpu.roll(t, LANES - 1, 1), pltpu.roll(t, 1, 1))
        return t * cos_ref[...] + swapped * sin_ref[...]

    for h in range(H_B):
        sl = slice(h * D_K, (h + 1) * D_K)
        q_ref[0, :, sl] = (rot(l2n(y[:, sl])) * (D_K ** -0.5)).astype(q_ref.dtype)
        k_ref[0, :, sl] = rot(l2n(y[:, W_B + h * D_K:W_B + (h + 1) * D_K])).astype(k_ref.dtype)
    v_ref[0] = y[:, 2 * W_B:].astype(v_ref.dtype)


def _rope_tables(L):
    pos = jnp.arange(L)
    rows = (pos // GRID_W).astype(F32)
    cols = (pos % GRID_W).astype(F32)
    n_pairs = D_K // 4
    inv = ROPE_BASE ** (-jnp.arange(n_pairs, dtype=F32) / n_pairs)
    ang = jnp.concatenate([rows[:, None] * inv, cols[:, None] * inv], axis=-1)
    cos = jnp.repeat(jnp.cos(ang), 2, axis=-1)
    sin = jnp.repeat(jnp.sin(ang), 2, axis=-1)
    sign = jnp.asarray(np.tile(np.array([-1.0, 1.0], np.float32), D_K // 2))
    return cos, sin * sign


def _conv_call(x, conv_w, rope):
    B, L, C = x.shape
    tl = min(ROW_TILE, L)
    nh = tl // HALO
    n_halo = L // HALO
    cur = pl.BlockSpec((1, tl, C), lambda b, i: (b, i, 0))
    prev = pl.BlockSpec((1, HALO, C), lambda b, i: (b, jnp.maximum(i * nh - 1, 0), 0))
    nxt = pl.BlockSpec((1, HALO, C), lambda b, i: (b, jnp.minimum((i + 1) * nh, n_halo - 1), 0))
    ins = [x, x, x, conv_w]
    specs = [cur, prev, nxt, _const_spec(conv_w.shape)]
    if rope:
        cos, sin = _rope_tables(L)
        ins += [cos, sin]
        specs += [pl.BlockSpec((tl, D_K), lambda b, i: (i, 0))] * 2
    out = pl.BlockSpec((1, tl, W_B), lambda b, i: (b, i, 0))
    return pl.pallas_call(
        functools.partial(_conv_kernel, rope=rope),
        grid=(B, L // tl),
        in_specs=specs,
        out_specs=[out, out, out],
        out_shape=[jax.ShapeDtypeStruct((B, L, W_B), BF16)] * 3,
        scratch_shapes=[pltpu.VMEM((tl + 2 * HALO, C), F32)],
        compiler_params=_cparams(("parallel", "parallel")),
        name="conv",
    )(*ins)


DELTA_CT = 4
DELTA_CT_LONG = 8
C4 = H_B * CHUNK


def _blockdiag_rows(x, n_blk):
    w = x.shape[1] // n_blk
    lane_blk = lax.broadcasted_iota(jnp.int32, x.shape, 1) // w
    return jnp.concatenate([jnp.where(lane_blk == b, x, 0.0) for b in range(n_blk)], axis=0).astype(BF16)


def _widen(ec, col0, width):
    blk = lax.broadcasted_iota(jnp.int32, (ec.shape[0], H_B * width), 1) // width
    out = ec[:, col0 + H_B - 1:col0 + H_B]
    for h in range(H_B - 2, -1, -1):
        out = jnp.where(blk == h, ec[:, col0 + h:col0 + h + 1], out)
    return out


def _delta_prep(groups):
    c = CHUNK
    row = lax.broadcasted_iota(jnp.int32, (c, C4), 0)
    coll = lax.broadcasted_iota(jnp.int32, (c, C4), 1) % c
    eye = jnp.where(row == coll, 1.0, 0.0)
    st = []
    for q, k, v, ec, gr, backward in groups:
        gi = H_B if backward else 0
        bi = 3 * H_B if backward else 2 * H_B
        gc64, gc128 = _widen(ec, gi, c), _widen(ec, gi, D_K)
        beta64, beta128 = _widen(ec, bi, c), _widen(ec, bi, D_K)
        incl = (row <= coll) if backward else (row >= coll)
        strict = (row < coll) if backward else (row > coll)
        decay = jnp.where(incl, jnp.exp(jnp.where(incl, gc64 - gr, 0.0)), 0.0)
        kf = k.astype(F32)
        qf = q.astype(F32)
        eg = jnp.exp(gc128)
        g_last = gc128[0:1, :] if backward else gc128[c - 1:c, :]
        st.append(dict(
            kf=kf, qf=qf, decay=decay, strict=strict, beta64=beta64,
            qg=(qf * eg).astype(BF16),
            kg=(kf * jnp.exp(g_last - gc128)).astype(BF16),
            rhs_v=(v.astype(F32) * beta128).astype(BF16),
            rhs_k=(kf * (beta128 * eg)).astype(BF16),
            egl=jnp.exp(g_last)))
    lane2 = lax.broadcasted_iota(jnp.int32, (c, 2 * D_K), 1)
    for s in st:
        outs = []
        for p in range(H_B // 2):
            kp = s["kf"][:, p * 2 * D_K:(p + 1) * 2 * D_K]
            qp = s["qf"][:, p * 2 * D_K:(p + 1) * 2 * D_K]
            rmat = jnp.concatenate([jnp.where(lane2 < D_K, kp, 0.0), jnp.where(lane2 >= D_K, kp, 0.0)], axis=0)
            outs.append(_dot_nt(jnp.concatenate([kp, qp], axis=0).astype(BF16), rmat.astype(BF16)))
        kk = jnp.concatenate([o[:c] for o in outs], axis=1)
        qk = jnp.concatenate([o[c:] for o in outs], axis=1)
        s["a"] = jnp.where(s["strict"], kk * s["decay"], 0.0) * s["beta64"]
        s["intra"] = (qk * s["decay"]).astype(BF16)
        s["t"] = eye - jnp.where(row // 2 == coll // 2, s["a"], 0.0)
    b = 2
    while b < c:
        join = (row // (2 * b) == coll // (2 * b)) & (row // b != coll // b)
        for s in st:
            s["tb"] = s["t"].astype(BF16)
            s["tl"] = _dot(s["tb"], _blockdiag_rows(jnp.where(join, s["a"], 0.0), H_B))
        for s in st:
            s["t"] = s["t"] - _dot(s["tl"].astype(BF16), _blockdiag_rows(s["t"], H_B))
        b *= 2
    zero = jnp.zeros((c, 2 * D_K), BF16)
    for s in st:
        tb = s["t"].astype(BF16)
        us, ws = [], []
        for h in range(H_B):
            rhs = jnp.concatenate([s["rhs_v"][:, h * D_V:(h + 1) * D_V], s["rhs_k"][:, h * D_K:(h + 1) * D_K]], axis=1)
            uw = _dot(tb, jnp.concatenate([zero] * h + [rhs] + [zero] * (H_B - 1 - h), axis=0))
            us.append(uw[:, :D_V])
            ws.append(uw[:, D_V:])
        s["u"] = jnp.concatenate(us, axis=1)
        s["w"] = jnp.concatenate(ws, axis=1).astype(BF16)
    return st


def _delta_scan_step(steps, states):
    c = CHUNK
    zero = jnp.zeros((D_K, D_V), BF16)
    mids = []
    for s, S in zip(steps, states):
        w_s, q_s = [], []
        for p in range(H_B // 2):
            sl = slice(p * 2 * D_K, (p + 1) * 2 * D_K)
            lhs = jnp.concatenate([s["w"][:, sl], s["qg"][:, sl]], axis=0)
            sa, sb = S[2 * p].astype(BF16), S[2 * p + 1].astype(BF16)
            sbd = jnp.concatenate([jnp.concatenate([sa, zero], axis=1), jnp.concatenate([zero, sb], axis=1)], axis=0)
            out = _dot(lhs, sbd)
            w_s.append(out[:c])
            q_s.append(out[c:])
        v_new = s["u"] - jnp.concatenate(w_s, axis=1)
        mids.append((v_new, jnp.concatenate(q_s, axis=1)))
    outs, new_states = [], []
    for s, S, (v_new, q_s) in zip(steps, states, mids):
        vnb = v_new.astype(BF16)
        outs.append(q_s + _dot(s["intra"], _blockdiag_rows(v_new, H_B)))
        new_states.append([S[h] * s["egl"][:, h * D_V:(h + 1) * D_V]
                           + _dot_tn(s["kg"][:, h * D_K:(h + 1) * D_K], vnb[:, h * D_V:(h + 1) * D_V])
                           for h in range(H_B)])
    return outs, new_states


def _delta_kernel(qf_ref, kf_ref, vf_ref, ecf_ref, erf_ref, qb_ref, kb_ref, vb_ref, ecb_ref, erb_ref,
                  s0f_ref, s0b_ref, of_ref, ob_ref, sf_ref, sb_ref, s_scr):
    j = pl.program_id(1)
    c = CHUNK

    @pl.when(j == 0)
    def _():
        s_scr[0] = s0f_ref[0]
        s_scr[1] = s0b_ref[0]

    def group(refs, ci, backward):
        q_ref, k_ref, v_ref, ec_ref, er_ref = refs
        rows = slice(ci * c, (ci + 1) * c)
        d = 1 if backward else 0
        return (q_ref[0, rows, :], k_ref[0, rows, :], v_ref[0, rows, :], ec_ref[0, rows, :],
                er_ref[0, ci, d:d + 1, :], backward)

    fwd = (qf_ref, kf_ref, vf_ref, ecf_ref, erf_ref)
    bwd = (qb_ref, kb_ref, vb_ref, ecb_ref, erb_ref)
    ct = erf_ref.shape[1]
    prep = _delta_prep([group(fwd, ci, False) for ci in range(ct)]
                       + [group(bwd, ci, True) for ci in range(ct)])
    states = [[s_scr[d, h] for h in range(H_B)] for d in range(2)]
    for step in range(ct):
        cf, cb = step, ct - 1 - step
        outs, states = _delta_scan_step([prep[cf], prep[ct + cb]], states)
        of_ref[0, cf * c:(cf + 1) * c, :] = outs[0].astype(of_ref.dtype)
        ob_ref[0, cb * c:(cb + 1) * c, :] = outs[1].astype(ob_ref.dtype)
    for d in range(2):
        for h in range(H_B):
            s_scr[d, h] = states[d][h]

    @pl.when(j == pl.num_programs(1) - 1)
    def _():
        sf_ref[0] = s_scr[0]
        sb_ref[0] = s_scr[1]


def _delta_call(q, k, v, ecol, s0f, s0b):
    B, L, _ = q.shape
    n = L // CHUNK
    ct = DELTA_CT_LONG if n % DELTA_CT_LONG == 0 else DELTA_CT
    tl = ct * CHUNK
    nb = L // tl
    erow = ecol[..., :2 * H_B].reshape(B, n, CHUNK, 2, H_B).transpose(0, 1, 3, 4, 2).reshape(B, n, 2, C4)
    fmap = lambda b, j: (b, j, 0)
    bmap = lambda b, j: (b, nb - 1 - j, 0)
    seq = lambda m: pl.BlockSpec((1, tl, W_B), m)
    ecs = lambda m: pl.BlockSpec((1, tl, LANES), m)
    ers = lambda m: pl.BlockSpec((1, ct, 2, C4), lambda b, j: m(b, j) + (0,))
    st = pl.BlockSpec((1, H_B, D_K, D_V), lambda b, j: (b, 0, 0, 0))
    return pl.pallas_call(
        _delta_kernel,
        grid=(B, nb),
        in_specs=[seq(fmap), seq(fmap), seq(fmap), ecs(fmap), ers(fmap),
                  seq(bmap), seq(bmap), seq(bmap), ecs(bmap), ers(bmap), st, st],
        out_specs=[seq(fmap), seq(bmap), st, st],
        out_shape=[jax.ShapeDtypeStruct((B, L, W_B), BF16)] * 2
        + [jax.ShapeDtypeStruct((B, H_B, D_K, D_V), F32)] * 2,
        scratch_shapes=[pltpu.VMEM((2, H_B, D_K, D_V), F32)],
        compiler_params=_cparams(("parallel", "arbitrary")),
        name="delta",
    )(q, k, v, ecol, erow, q, k, v, ecol, erow, s0f, s0b)


def _merge_kernel(x_ref, oa_ref, of_ref, ob_ref, z_ref, gate_ref, g1_ref, sh2_ref, sc2_ref,
                  onw_ref, wpa_ref, wpb_ref, wo_ref, n2w_ref, wr_ref, br_ref, ltri_ref,
                  x1_ref, h2_ref, route_ref, cnt_ref, cnt_scr):
    tm = x_ref.shape[0]
    pm = ltri_ref.shape[0]
    parts = [dict(rows=slice(p * pm, (p + 1) * pm)) for p in range(tm // pm)]
    for s in parts:
        rows = s["rows"]
        o = of_ref[rows, :].astype(F32) + ob_ref[rows, :].astype(F32)
        heads = []
        for h in range(H_B):
            t = o[:, h * D_V:(h + 1) * D_V]
            heads.append(t * lax.rsqrt(jnp.mean(t * t, axis=-1, keepdims=True) + EPS))
        ob = jnp.concatenate(heads, axis=-1) * onw_ref[...] * z_ref[rows, :].astype(F32)
        s["pa"] = _dot(oa_ref[rows, :], wpa_ref[...])
        s["pb"] = _dot(ob.astype(BF16), wpb_ref[...])
    for s in parts:
        gate = gate_ref[s["rows"], :]
        merged = gate[:, :D_MODEL].astype(F32) * s["pa"] + gate[:, D_MODEL:].astype(F32) * s["pb"]
        s["mix"] = _dot(merged.astype(BF16), wo_ref[...])
    for s in parts:
        rows = s["rows"]
        x1 = x_ref[rows, :] + g1_ref[0] * s["mix"]
        x1_ref[rows, :] = x1
        ms = jnp.mean(x1 * x1, axis=-1, keepdims=True)
        h2 = x1 * lax.rsqrt(ms + EPS) * n2w_ref[...]
        h2 = h2 * (1.0 + sc2_ref[0]) + sh2_ref[0]
        hi = h2.astype(BF16)
        for c in range(ROW_CHUNKS):
            h2_ref[pl.ds(rows.start * ROW_CHUNKS + c, pm, stride=ROW_CHUNKS), :] = h2[:, c * LANES:(c + 1) * LANES]
        lo = (h2 - hi.astype(F32)).astype(BF16)
        s["logits"] = _dot(jnp.concatenate([hi, lo, hi], axis=-1), wr_ref[...]) + br_ref[...]
    lane = lax.broadcasted_iota(jnp.int32, (pm, LANES), 1)
    big = jnp.int32(LANES)

    def first_max(vals, mask):
        mv = jnp.where(mask, vals, NEG_INF)
        m = mv.max(axis=-1, keepdims=True)
        idx = jnp.where(mask & (mv == m), lane, big).min(axis=-1, keepdims=True)
        return m, idx

    for s in parts:
        logits = s["logits"]
        gmask = lane < N_GROUPS
        gm, grp = first_max(logits, gmask)
        p_grp = 1.0 / jnp.where(gmask, jnp.exp(logits - gm), 0.0).sum(axis=-1, keepdims=True)
        lo_lane = N_GROUPS + grp * EXP_PER_GROUP
        emask = (lane >= lo_lane) & (lane < lo_lane + EXP_PER_GROUP)
        em, _ = first_max(logits, emask)
        ex = jnp.where(emask, jnp.exp(logits - em), 0.0)
        probs = ex / ex.sum(axis=-1, keepdims=True)
        p1, i1 = first_max(probs, emask)
        p2, i2 = first_max(probs, emask & (lane != i1))
        tot = p1 + p2
        s["w1"] = p_grp * p1 / tot
        s["w2"] = p_grp * p2 / tot
        s["e1"] = i1 - N_GROUPS
        s["e2"] = i2 - N_GROUPS
        s["hit"] = jnp.where(lane == s["e1"], 1.0, 0.0) + jnp.where(lane == s["e2"], 1.0, 0.0)
        s["within"] = _dot(ltri_ref[...], s["hit"].astype(BF16))

    @pl.when(pl.program_id(0) == 0)
    def _():
        cnt_scr[...] = jnp.zeros_like(cnt_scr)

    running = cnt_scr[...]
    for s in parts:
        before = s["within"] + running
        running = running + s["hit"].sum(axis=0, keepdims=True)
        r1 = jnp.where(lane == s["e1"], before, 0.0).sum(axis=-1, keepdims=True)
        r2 = jnp.where(lane == s["e2"], before, 0.0).sum(axis=-1, keepdims=True)
        cols = [s["e1"].astype(F32), s["e2"].astype(F32), s["w1"], s["w2"], r1, r2]
        route = jnp.zeros((pm, LANES), F32)
        for i, col in enumerate(cols):
            route = jnp.where(lane == i, col, route)
        route_ref[s["rows"], :] = route
    cnt_scr[...] = running
    cnt_ref[...] = running


def _merge_call(x, oa, of, ob, z, gate, mod3, mod_stride, w):
    B, L, _ = x.shape
    n_tok = B * L
    tm = ROW_TILE
    assert n_tok % tm == 0 and (L % tm == 0 or tm % L == 0)
    flat = lambda a: a.reshape(n_tok, a.shape[-1])
    rspec = lambda n: pl.BlockSpec((tm, n), lambda i: (i, 0))
    mspec = lambda k: pl.BlockSpec((1, 1, D_MODEL), lambda i: ((i * tm) // L * mod_stride + k, 0, 0))
    t = np.arange(MERGE_PART)
    ltri = jnp.asarray((t[:, None] > t[None, :]).astype(np.float32)).astype(BF16)
    consts = [w["onw"], w["wpa"], w["wpb"], w["wo"], w["norm2_w"], w["wr3"], w["br"], ltri]
    return pl.pallas_call(
        _merge_kernel,
        grid=(n_tok // tm,),
        in_specs=[rspec(D_MODEL), rspec(W_A), rspec(W_B), rspec(W_B), rspec(W_B), rspec(2 * D_MODEL),
                  mspec(2), mspec(3), mspec(4)] + [_const_spec(c.shape) for c in consts],
        out_specs=[rspec(D_MODEL), pl.BlockSpec((tm * SUB, LANES), lambda i: (i, 0)), rspec(LANES),
                   _const_spec((1, LANES))],
        out_shape=[jax.ShapeDtypeStruct((n_tok, D_MODEL), F32),
                   jax.ShapeDtypeStruct((n_tok * SUB, LANES), F32),
                   jax.ShapeDtypeStruct((n_tok, LANES), F32),
                   jax.ShapeDtypeStruct((1, LANES), F32)],
        scratch_shapes=[pltpu.VMEM((1, LANES), F32)],
        compiler_params=_cparams(("arbitrary",)),
        name="merge",
    )(flat(x), flat(oa), flat(of), flat(ob), flat(z), flat(gate), mod3, mod3, mod3, *consts)


SUB = 8
ROW_CHUNKS = D_MODEL // LANES


def _store_token_major(ref, x):
    for s in range(ROW_CHUNKS):
        ref[pl.ds(s, x.shape[0], stride=ROW_CHUNKS), :] = x[:, s * LANES:(s + 1) * LANES]


def _load_token_major(ref, slot, row0, n_tok):
    return jnp.concatenate([ref[slot, pl.ds(row0 * ROW_CHUNKS + s, n_tok, stride=ROW_CHUNKS), :]
                            for s in range(ROW_CHUNKS)], axis=1)


def _row_gather_start(idx_ref, n_rows, src_hbm, buf, sem, slot):
    for r in range(n_rows):
        src = pl.multiple_of(idx_ref[0, 0, r] * SUB, SUB)
        pltpu.make_async_copy(src_hbm.at[pl.ds(src, SUB), :], buf.at[slot, pl.ds(r * SUB, SUB), :],
                              sem.at[slot]).start(priority=r % 2)


def _row_gather_wait(n_rows, src_hbm, buf, sem, slot):
    for r in range(n_rows):
        pltpu.make_async_copy(src_hbm.at[pl.ds(0, SUB), :], buf.at[slot, pl.ds(r * SUB, SUB), :],
                              sem.at[slot]).wait()


def _gather_pipeline(i, n_steps, idx_ref, idx_next_ref, n_rows, src_hbm, buf, sem):
    slot = lax.rem(i, 2)

    @pl.when(i == 0)
    def _():
        _row_gather_start(idx_ref, n_rows, src_hbm, buf, sem, 0)

    @pl.when(i + 1 < n_steps)
    def _():
        _row_gather_start(idx_next_ref, n_rows, src_hbm, buf, sem, 1 - slot)

    @pl.when(i < n_steps)
    def _():
        _row_gather_wait(n_rows, src_hbm, buf, sem, slot)

    return slot


def _moe_kernel(te_ref, nt_ref, idx_ref, idxn_ref, h2_hbm, wg_ref, wu_ref, wd_ref, y_ref, xbuf, sem):
    i = pl.program_id(0)
    n_used = nt_ref[0]
    slot = _gather_pipeline(i, n_used, idx_ref, idxn_ref, MOE_TILE, h2_hbm, xbuf, sem)

    @pl.when(i < n_used)
    def _():
        x = _load_token_major(xbuf, slot, 0, MOE_TILE).astype(BF16)
        g = _dot(x, wg_ref[0])
        hid = g * jax.nn.sigmoid(g) * _dot(x, wu_ref[0])
        _store_token_major(y_ref, _dot(hid.astype(BF16), wd_ref[0]))

    @pl.when(i >= n_used)
    def _():
        y_ref[...] = jnp.zeros_like(y_ref)


def _moe_call(h2, src_tok, tile_expert, n_tiles_used, wg, wu, wd):
    p_pad = src_tok.shape[0]
    n_tiles = p_pad // MOE_TILE
    idx = src_tok.reshape(n_tiles, 1, MOE_TILE)
    smem_tile = lambda m: pl.BlockSpec((1, 1, MOE_TILE), m, memory_space=pltpu.SMEM)
    grid_spec = pltpu.PrefetchScalarGridSpec(
        num_scalar_prefetch=2,
        grid=(n_tiles,),
        in_specs=[smem_tile(lambda i, te, nt: (i, 0, 0)),
                  smem_tile(lambda i, te, nt: (jnp.minimum(i + 1, n_tiles - 1), 0, 0)),
                  pl.BlockSpec(memory_space=pl.ANY),
                  pl.BlockSpec((1, D_MODEL, D_FF_E), lambda i, te, nt: (te[i], 0, 0)),
                  pl.BlockSpec((1, D_MODEL, D_FF_E), lambda i, te, nt: (te[i], 0, 0)),
                  pl.BlockSpec((1, D_FF_E, D_MODEL), lambda i, te, nt: (te[i], 0, 0))],
        out_specs=pl.BlockSpec((MOE_TILE * SUB, LANES), lambda i, te, nt: (i, 0)),
        scratch_shapes=[pltpu.VMEM((2, MOE_TILE * SUB, LANES), F32), pltpu.SemaphoreType.DMA((2,))],
    )
    return pl.pallas_call(
        _moe_kernel,
        grid_spec=grid_spec,
        out_shape=jax.ShapeDtypeStruct((p_pad * SUB, LANES), F32),
        compiler_params=_cparams(("arbitrary",)),
        name="moe",
    )(tile_expert, n_tiles_used, idx, idx, h2, wg, wu, wd)


COMBINE_TILE = 256


def _combine_kernel(pos_ref, posn_ref, x1_ref, route_ref, g2_ref, y_hbm, o_ref, ybuf, sem):
    i = pl.program_id(0)
    t = COMBINE_TILE
    slot = _gather_pipeline(i, pl.num_programs(0), pos_ref, posn_ref, 2 * t, y_hbm, ybuf, sem)
    route = route_ref[...]
    w1 = route[:, 2:3]
    w2 = route[:, 3:4]
    ya = _load_token_major(ybuf, slot, 0, t)
    yb = _load_token_major(ybuf, slot, t, t)
    o_ref[...] = x1_ref[...] + g2_ref[0] * (w1 * ya + w2 * yb)


def _combine_call(x1, route, pos2, y, mod3, mod_stride, seq_len):
    n = x1.shape[0]
    t = COMBINE_TILE
    n_tiles = n // t
    pos_tiles = pos2.reshape(n_tiles, t, 2).transpose(0, 2, 1).reshape(n_tiles, 1, 2 * t)
    smem_tile = lambda m: pl.BlockSpec((1, 1, 2 * t), m, memory_space=pltpu.SMEM)
    return pl.pallas_call(
        _combine_kernel,
        grid=(n_tiles,),
        in_specs=[smem_tile(lambda i: (i, 0, 0)),
                  smem_tile(lambda i: (jnp.minimum(i + 1, n_tiles - 1), 0, 0)),
                  pl.BlockSpec((t, D_MODEL), lambda i: (i, 0)),
                  pl.BlockSpec((t, LANES), lambda i: (i, 0)),
                  pl.BlockSpec((1, 1, D_MODEL), lambda i: ((i * t) // seq_len * mod_stride + 5, 0, 0)),
                  pl.BlockSpec(memory_space=pl.ANY)],
        out_specs=pl.BlockSpec((t, D_MODEL), lambda i: (i, 0)),
        out_shape=jax.ShapeDtypeStruct((n, D_MODEL), F32),
        scratch_shapes=[pltpu.VMEM((2, 2 * t * SUB, LANES), F32), pltpu.SemaphoreType.DMA((2,))],
        compiler_params=_cparams(("arbitrary",)),
        name="combine",
    )(pos_tiles, pos_tiles, x1, route, mod3, y)


def _inverse_map_kernel(pos_ref, init_hbm, out_ref):
    pltpu.sync_copy(init_hbm, out_ref)

    def body(p, carry):
        out_ref[pos_ref[p]] = lax.shift_right_logical(p, 1)
        return carry

    lax.fori_loop(0, pos_ref.shape[0], body, 0, unroll=8)


def _inverse_map_call(pos, init):
    return pl.pallas_call(
        _inverse_map_kernel,
        in_specs=[pl.BlockSpec(memory_space=pltpu.SMEM), pl.BlockSpec(memory_space=pl.ANY)],
        out_specs=pl.BlockSpec(memory_space=pltpu.SMEM),
        out_shape=jax.ShapeDtypeStruct(init.shape, jnp.int32),
        name="inverse_map",
    )(pos, init)


def _moe_forward(h2, route, counts, wg, wu, wd):
    n = route.shape[0]
    eid = route[:, :2].astype(jnp.int32).reshape(-1)
    rank = route[:, 4:6].astype(jnp.int32).reshape(-1)
    counts = counts[0, :N_EXPERTS].astype(jnp.int32)
    padded = ((counts + MOE_TILE - 1) // MOE_TILE) * MOE_TILE
    seg_end = jnp.cumsum(padded)
    seg_start = seg_end - padded
    pos = seg_start[eid] + rank
    p_pad = 2 * n + N_EXPERTS * MOE_TILE
    src_tok = _inverse_map_call(pos, jnp.arange(p_pad, dtype=jnp.int32) % n)
    n_tiles = p_pad // MOE_TILE
    tile_start = jnp.arange(n_tiles, dtype=jnp.int32) * MOE_TILE
    tile_expert = jnp.minimum(jnp.sum((tile_start[:, None] >= seg_end[None, :]).astype(jnp.int32), axis=1),
                              N_EXPERTS - 1)
    n_used = (seg_end[-1] // MOE_TILE).astype(jnp.int32).reshape(1)
    y = _moe_call(h2, src_tok, tile_expert, n_used, wg, wu, wd)
    return y, pos.reshape(n, 2)


def _prep_weights(norm1_w, norm2_w, w_in, conv_w, a_log, dt_bias, q_norm_w, k_norm_w, o_norm_w,
                  w_proj_a, w_proj_b, w_out, w_router_g, b_router_g, w_router_e, b_router_e,
                  w_exp_gate, w_exp_up, w_exp_down):
    w = {}
    w["norm1_w"] = norm1_w.reshape(1, D_MODEL)
    w["norm2_w"] = norm2_w.reshape(1, D_MODEL)
    o = 0
    w["wa"] = w_in[:, o:o + 3 * W_A].astype(BF16); o += 3 * W_A
    w["wb"] = w_in[:, o:o + 3 * W_B].astype(BF16); o += 3 * W_B
    w["wz"] = w_in[:, o:o + W_B].astype(BF16); o += W_B
    small = w_in[:, o:o + 4 * H_B]; o += 4 * H_B
    w["wg"] = w_in[:, o:o + 2 * D_MODEL].astype(BF16)
    def group_lanes(t):
        r = t.shape[0]
        t = t.reshape(r, -1, N_HG, HP).transpose(0, 2, 1, 3).reshape(r, N_HG, -1)
        return jnp.pad(t, ((0, 0), (0, 0), (0, LANES - t.shape[-1]))).reshape(r, N_HG * LANES)

    w["we"] = group_lanes(small).astype(BF16)
    w["alog_l"] = group_lanes(a_log.reshape(1, 2 * H_B))
    w["dtb_l"] = group_lanes(dt_bias.reshape(1, 2 * H_B))
    w["qnw"] = jnp.tile(q_norm_w, H_A).reshape(1, W_A)
    w["knw"] = jnp.tile(k_norm_w, H_A).reshape(1, W_A)
    w["onw"] = jnp.tile(o_norm_w, H_B).reshape(1, W_B)
    blk = np.arange(256) // D_HA
    w["bd256"] = jnp.asarray((blk[:, None] == blk[None, :]).astype(np.float32)).astype(BF16)
    w["conv_w"] = conv_w
    w["wpa"] = w_proj_a.astype(BF16)
    w["wpb"] = w_proj_b.astype(BF16)
    w["wo"] = w_out.astype(BF16)
    wr = jnp.concatenate([w_router_g, w_router_e], axis=1)
    wr = jnp.pad(wr, ((0, 0), (0, LANES - wr.shape[1])))
    wr_hi = wr.astype(BF16)
    wr_lo = (wr - wr_hi.astype(F32)).astype(BF16)
    w["wr3"] = jnp.concatenate([wr_hi, wr_hi, wr_lo], axis=0)
    br = jnp.concatenate([b_router_g, b_router_e])
    w["br"] = jnp.pad(br, (0, LANES - br.shape[0])).reshape(1, LANES)
    w["wg_e"] = w_exp_gate.astype(BF16)
    w["wu_e"] = w_exp_up.astype(BF16)
    w["wd_e"] = w_exp_down.astype(BF16)
    return w


def _layer(x, mod3, mod_stride, w, ctx):
    B, L, _ = x.shape
    latent = ctx is not None
    qa, ka, va, qkvb, z, gate, ecol, *new_kv = _inproj_call(x, mod3, mod_stride, w, not latent)
    if latent:
        k_ctx, v_ctx, s0f, s0b, bias = ctx
        oa = _nbr_attn_call(qa, ka, va, k_ctx, v_ctx, bias)
    else:
        s0f = s0b = jnp.zeros((B, H_B, D_K, D_V), F32)
        oa = _ctx_attn_call(qa, ka, va)
    qd, kd, vd = _conv_call(qkvb, w["conv_w"], rope=latent)
    of, ob, s_f, s_b = _delta_call(qd, kd, vd, ecol, s0f, s0b)
    x1, h2, route, counts = _merge_call(x, oa, of, ob, z, gate, mod3, mod_stride, w)
    ys, pos2 = _moe_forward(h2, route, counts, w["wg_e"], w["wu_e"], w["wd_e"])
    y = _combine_call(x1, route, pos2, ys, mod3, mod_stride, L)
    return y.reshape(B, L, D_MODEL), new_kv, s_f, s_b


def kernel(x_prompt, x_sample, c, cache_attn_k, cache_attn_v, state_delta_fwd, state_delta_bwd, c_ctx,
           norm1_w, norm2_w, w_mod, b_mod, w_in, conv_w, a_log, dt_bias, q_norm_w, k_norm_w, rpb,
           o_norm_w, w_proj_a, w_proj_b, w_out, w_router_g, b_router_g, w_router_e, b_router_e,
           w_exp_gate, w_exp_up, w_exp_down):
    assert norm1_w.shape[0] == 1, "single-layer trunk"
    Bp, Lp, _ = x_prompt.shape
    Bs, Ls, _ = x_sample.shape
    l = 0
    w = _prep_weights(norm1_w[l], norm2_w[l], w_in[l], conv_w[l], a_log[l], dt_bias[l], q_norm_w[l],
                      k_norm_w[l], o_norm_w[l], w_proj_a[l], w_proj_b[l], w_out[l], w_router_g[l],
                      b_router_g[l], w_router_e[l], b_router_e[l], w_exp_gate[l], w_exp_up[l],
                      w_exp_down[l])
    cond = jnp.concatenate([c_ctx[None, :], c, jnp.zeros((8 - 1 - Bs, D_MODEL), F32)], axis=0)
    mod = _mod_call(cond, w_mod[l], b_mod[l])
    mod3 = mod.reshape(8 * 6, 1, D_MODEL)
    bias = _bias_table_call(rpb[l])
    y_p, (new_k, new_v), s_f, s_b = _layer(x_prompt, mod3[:6], 0, w, None)
    lc = cache_attn_k.shape[2]
    ctx = (cache_attn_k[:, l].reshape(Bs, lc, W_A).astype(BF16),
           cache_attn_v[:, l].reshape(Bs, lc, W_A).astype(BF16),
           state_delta_fwd[:, l], state_delta_bwd[:, l], bias)
    y_s, _, _, _ = _layer(x_sample, mod3[6:6 + 6 * Bs], 6, w, ctx)
    return (y_p, y_s, new_k, new_v, s_f[:, None], s_b[:, None])
```

```python
import functools
import math

import numpy as np
import jax
import jax.numpy as jnp
from jax import lax
from jax.experimental import pallas as pl
from jax.experimental.pallas import tpu as pltpu

F32 = jnp.float32
BF16 = jnp.bfloat16

D_MODEL = 1024
GRID_W = 64
H_A = 8
D_HA = 64
W_A = H_A * D_HA
WIN_H = 8
WIN_W = 16
H_B = 4
D_K = 128
D_V = 128
W_B = H_B * D_V
CONV_K = 5
CHUNK = 64
ROPE_BASE = 10000.0
N_GROUPS = 4
EXP_PER_GROUP = 8
N_EXPERTS = N_GROUPS * EXP_PER_GROUP
D_FF_E = 256
EPS = 1e-6
NEG_INF = -1e30

LANES = 128
HP = H_B
N_HG = H_B // HP
ROW_TILE = 512
INPROJ_TILE = 256
INPROJ_PART = 256
MERGE_PART = 256
MOE_TILE = 256
VMEM_LIMIT = 56 * 1024 * 1024

HIGHEST = lax.Precision.HIGHEST


def _cparams(sem):
    return pltpu.CompilerParams(dimension_semantics=sem, vmem_limit_bytes=VMEM_LIMIT)


def _dot(a, b):
    return jnp.dot(a, b, preferred_element_type=F32)


def _dot_nt(a, b):
    return lax.dot_general(a, b, (((1,), (1,)), ((), ())), preferred_element_type=F32)


def _dot_tn(a, b):
    return lax.dot_general(a, b, (((0,), (0,)), ((), ())), preferred_element_type=F32)


def _const_spec(shape):
    nd = len(shape)
    return pl.BlockSpec(shape, lambda *_: (0,) * nd)


def _mod_kernel(c_ref, w_ref, b_ref, o_ref):
    c = c_ref[...]
    s = c * jax.nn.sigmoid(c)
    o_ref[...] = jnp.dot(s, w_ref[...], preferred_element_type=F32, precision=HIGHEST) + b_ref[...]


def _mod_call(cond8, w_mod, b_mod):
    n_out = w_mod.shape[1]
    tn = 1024
    return pl.pallas_call(
        _mod_kernel,
        grid=(n_out // tn,),
        in_specs=[pl.BlockSpec((8, D_MODEL), lambda j: (0, 0)),
                  pl.BlockSpec((D_MODEL, tn), lambda j: (0, j)),
                  pl.BlockSpec((1, tn), lambda j: (0, j))],
        out_specs=pl.BlockSpec((8, tn), lambda j: (0, j)),
        out_shape=jax.ShapeDtypeStruct((8, n_out), F32),
        compiler_params=_cparams(("arbitrary",)),
        name="mod",
    )(cond8, w_mod, b_mod.reshape(1, n_out))


def _softplus(x):
    return jnp.maximum(x, 0.0) + jnp.log1p(jnp.exp(-jnp.abs(x)))


def _inproj_kernel(x_ref, sh_ref, sc_ref, nw_ref, wa_ref, wb_ref, wz_ref, wg_ref, we_ref,
                   qnw_ref, knw_ref, bd_ref, alog_ref, dtb_ref, tri_ref,
                   qa_ref, ka_ref, va_ref, qkvb_ref, z_ref, gate_ref, e_ref, *kv5_refs):
    tm = x_ref.shape[0]
    pm = tri_ref.shape[1]
    parts = [dict(rows=slice(p * pm, (p + 1) * pm)) for p in range(tm // pm)]
    for s in parts:
        x = x_ref[s["rows"], :]
        ms = jnp.mean(x * x, axis=-1, keepdims=True)
        h = x * lax.rsqrt(ms + EPS) * nw_ref[...]
        h = h * (1.0 + sc_ref[0]) + sh_ref[0]
        s["hb"] = h.astype(BF16)
    for s in parts:
        s["e"] = _dot(s["hb"], we_ref[...])
    for s in parts:
        s["a"] = _dot(s["hb"], wa_ref[...])
    for s in parts:
        qkvb_ref[s["rows"], :] = _dot(s["hb"], wb_ref[...]).astype(qkvb_ref.dtype)
    bd = bd_ref[...]

    def head_rms(t, w):
        sq = (t * t).astype(BF16)
        ss = jnp.concatenate([_dot(sq[:, i * 256:(i + 1) * 256], bd) for i in range(W_A // 256)], axis=-1)
        return t * lax.rsqrt(ss * (1.0 / D_HA) + EPS) * w

    for p, s in enumerate(parts):
        rows, a = s["rows"], s["a"]
        qa = head_rms(a[:, :W_A], qnw_ref[...]) * (D_HA ** -0.5)
        qa_ref[rows, :] = qa.astype(qa_ref.dtype)
        ka = head_rms(a[:, W_A:2 * W_A], knw_ref[...])
        va = a[:, 2 * W_A:]
        ka_ref[rows, :] = ka.astype(ka_ref.dtype)
        va_ref[rows, :] = va.astype(va_ref.dtype)
        if kv5_refs:
            seq_len = kv5_refs[0].shape[2]
            b_in, off = (p * pm) // seq_len, (p * pm) % seq_len
            for ref, t in zip(kv5_refs, (ka, va)):
                for hd in range(H_A):
                    ref[b_in, 0, off:off + pm, hd, :] = t[:, hd * D_HA:(hd + 1) * D_HA]
    for s in parts:
        z = _dot(s["hb"], wz_ref[...])
        z_ref[s["rows"], :] = (z * jax.nn.sigmoid(z)).astype(z_ref.dtype)
    lane = lax.broadcasted_iota(jnp.int32, (pm, N_HG * LANES), 1) % LANES
    tri = tri_ref[...]
    for s in parts:
        e = s["e"]
        g = -jnp.exp(alog_ref[...]) * _softplus(e + dtb_ref[...])
        act = jnp.where(lane < 2 * HP, g, jax.nn.sigmoid(e))
        a_hi = act.astype(BF16)
        rem = act - a_hi.astype(F32)
        a_mid = rem.astype(BF16)
        a_lo = (rem - a_mid.astype(F32)).astype(BF16)
        cum = _dot(tri, a_hi) + _dot(tri, a_mid) + _dot(tri, a_lo)
        e_ref[s["rows"], :] = jnp.where(lane < HP, cum[:pm], jnp.where(lane < 2 * HP, cum[pm:], act))
    for s in parts:
        gate_ref[s["rows"], :] = jax.nn.sigmoid(_dot(s["hb"], wg_ref[...])).astype(gate_ref.dtype)


def _inproj_call(x, mod3, mod_stride, w, new_cache):
    B, L, _ = x.shape
    n_tok = B * L
    tm, pm = INPROJ_TILE, INPROJ_PART
    assert n_tok % tm == 0 and (L % tm == 0 or (tm % L == 0 and mod_stride == 0)) and L % pm == 0
    row = lambda n, dt: jax.ShapeDtypeStruct((n_tok, n), dt)
    rspec = lambda n: pl.BlockSpec((tm, n), lambda i: (i, 0))
    mspec = lambda k: pl.BlockSpec((1, 1, D_MODEL), lambda i: ((i * tm) // L * mod_stride + k, 0, 0))
    t = np.arange(pm)
    same = (t[:, None] // CHUNK) == (t[None, :] // CHUNK)
    tri = np.concatenate([same & (t[:, None] >= t[None, :]), same & (t[:, None] <= t[None, :])], axis=0)
    consts = [w["norm1_w"], w["wa"], w["wb"], w["wz"], w["wg"], w["we"], w["qnw"], w["knw"],
              w["bd256"], w["alog_l"], w["dtb_l"], jnp.asarray(tri.astype(np.float32)).astype(BF16)]
    out_specs = [rspec(W_A), rspec(W_A), rspec(W_A), rspec(3 * W_B), rspec(W_B),
                 rspec(2 * D_MODEL), rspec(N_HG * LANES)]
    out_shape = [row(W_A, BF16), row(W_A, BF16), row(W_A, BF16), row(3 * W_B, BF16),
                 row(W_B, BF16), row(2 * D_MODEL, BF16), row(N_HG * LANES, F32)]
    if new_cache:
        if tm <= L:
            per_seq = L // tm
            cache_spec = pl.BlockSpec((1, 1, tm, H_A, D_HA), lambda i: (i // per_seq, 0, i % per_seq, 0, 0))
        else:
            cache_spec = pl.BlockSpec((tm // L, 1, L, H_A, D_HA), lambda i: (i, 0, 0, 0, 0))
        out_specs += [cache_spec] * 2
        out_shape += [jax.ShapeDtypeStruct((B, 1, L, H_A, D_HA), F32)] * 2
    outs = pl.pallas_call(
        _inproj_kernel,
        grid=(n_tok // tm,),
        in_specs=[rspec(D_MODEL), mspec(0), mspec(1)] + [_const_spec(c.shape) for c in consts],
        out_specs=out_specs,
        out_shape=out_shape,
        compiler_params=_cparams(("parallel",)),
        name="inproj",
    )(x.reshape(n_tok, D_MODEL), mod3, mod3, *consts)
    return [o.reshape(B, L, o.shape[-1]) for o in outs[:7]] + list(outs[7:])


def _pair_softmax_pv(s_parts, v_parts):
    m = s_parts[0].max(axis=-1, keepdims=True)
    for s in s_parts[1:]:
        m = jnp.maximum(m, s.max(axis=-1, keepdims=True))
    den = 0.0
    acc = 0.0
    for s, v in zip(s_parts, v_parts):
        p = jnp.exp(s - m)
        den = den + p.sum(axis=-1, keepdims=True)
        acc = acc + _dot(p.astype(BF16), v)
    return acc / den


def _stack_pair(qp):
    lane = lax.broadcasted_iota(jnp.int32, qp.shape, 1)
    zero = jnp.zeros_like(qp)
    return jnp.concatenate([jnp.where(lane < D_HA, qp, zero), jnp.where(lane >= D_HA, qp, zero)], axis=0)


def _unstack_pair(o2):
    t = o2.shape[0] // 2
    lane = lax.broadcasted_iota(jnp.int32, (t, LANES), 1)
    return jnp.where(lane < D_HA, o2[:t], o2[t:])


def _ctx_attn_kernel(q_ref, k_ref, v_ref, o_ref):
    for p in range(H_A // 2):
        sl = slice(p * LANES, (p + 1) * LANES)
        qs = _stack_pair(q_ref[0, :, sl])
        kp = k_ref[0, :, sl].astype(BF16)
        vp = v_ref[0, :, sl].astype(BF16)
        o2 = _pair_softmax_pv([_dot_nt(qs, kp)], [vp])
        o_ref[0, :, sl] = _unstack_pair(o2).astype(o_ref.dtype)


def _ctx_attn_call(q, k, v):
    B, L, _ = q.shape
    spec = pl.BlockSpec((1, L, W_A), lambda b: (b, 0, 0))
    return pl.pallas_call(
        _ctx_attn_kernel,
        grid=(B,),
        in_specs=[spec, spec, spec],
        out_specs=spec,
        out_shape=jax.ShapeDtypeStruct((B, L, W_A), BF16),
        compiler_params=_cparams(("parallel",)),
        name="ctx_attn",
    )(q, k, v)


N_DR = 2 * WIN_H - 1
N_DC = 2 * WIN_W - 1
NBR_ROWS = 8


def _bias_table_kernel(rpb_ref, o_ref):
    h = pl.program_id(0)
    c = lax.broadcasted_iota(jnp.int32, (GRID_W, 2 * GRID_W), 0)
    lane = lax.broadcasted_iota(jnp.int32, (GRID_W, 2 * GRID_W), 1)
    kc = lane % GRID_W
    ws = jnp.clip(c - WIN_W // 2, 0, GRID_W - WIN_W)
    in_win = (kc >= ws) & (kc < ws + WIN_W)
    dc = kc - c + (WIN_W - 1)
    for dr in range(N_DR - 1):
        base = (h * N_DR + dr) * N_DC
        val = jnp.zeros((GRID_W, 2 * GRID_W), F32)
        for d in range(N_DC):
            val = jnp.where(dc == d, jnp.where(lane < GRID_W, rpb_ref[base + d], rpb_ref[base + N_DC + d]), val)
        o_ref[0, dr] = jnp.where(in_win, val, NEG_INF)


def _bias_table_call(rpb):
    return pl.pallas_call(
        _bias_table_kernel,
        grid=(H_A,),
        in_specs=[pl.BlockSpec(memory_space=pltpu.SMEM)],
        out_specs=pl.BlockSpec((1, N_DR - 1, GRID_W, 2 * GRID_W), lambda h: (h, 0, 0, 0)),
        out_shape=jax.ShapeDtypeStruct((H_A, N_DR - 1, GRID_W, 2 * GRID_W), F32),
        compiler_params=_cparams(("arbitrary",)),
        name="bias_table",
    )(rpb.reshape(-1))


def _nbr_attn_kernel(q_ref, k_ref, v_ref, kc_ref, vc_ref, bias_ref, o_ref, *, rows):
    inst = []
    for rr in range(NBR_ROWS):
        r = pl.program_id(1) * NBR_ROWS + rr
        first = jnp.clip(r - WIN_H // 2, 0, rows - WIN_H)
        start = pl.multiple_of(first * GRID_W, GRID_W)
        shift = r - first
        for p in range(H_A // 2):
            inst.append((rr, p, start, shift))
    scores = []
    for rr, p, start, shift in inst:
        sl = slice(p * LANES, (p + 1) * LANES)
        qs = _stack_pair(q_ref[0, rr * GRID_W:(rr + 1) * GRID_W, sl])
        bias = jnp.concatenate(
            [jnp.concatenate([bias_ref[2 * p + hh, 2 * u - shift + WIN_H - 1] for u in range(WIN_H // 2)], axis=1)
             for hh in range(2)], axis=0)
        s_loc = _dot_nt(qs, k_ref[0, pl.ds(start, WIN_H * GRID_W), sl]) + bias
        s_ctx = _dot_nt(qs, kc_ref[0, :, sl])
        scores.append((s_loc, s_ctx))
    for (rr, p, start, shift), (s_loc, s_ctx) in zip(inst, scores):
        sl = slice(p * LANES, (p + 1) * LANES)
        o2 = _pair_softmax_pv([s_loc, s_ctx], [v_ref[0, pl.ds(start, WIN_H * GRID_W), sl], vc_ref[0, :, sl]])
        o_ref[0, rr * GRID_W:(rr + 1) * GRID_W, sl] = _unstack_pair(o2).astype(o_ref.dtype)


def _nbr_attn_call(q, k, v, k_ctx, v_ctx, bias):
    B, L, _ = q.shape
    rows = L // GRID_W
    lc = k_ctx.shape[1]
    full = pl.BlockSpec((1, L, W_A), lambda b, r: (b, 0, 0))
    ctx = pl.BlockSpec((1, lc, W_A), lambda b, r: (b, 0, 0))
    rowspec = pl.BlockSpec((1, NBR_ROWS * GRID_W, W_A), lambda b, r: (b, r, 0))
    return pl.pallas_call(
        functools.partial(_nbr_attn_kernel, rows=rows),
        grid=(B, rows // NBR_ROWS),
        in_specs=[rowspec, full, full, ctx, ctx, _const_spec(bias.shape)],
        out_specs=rowspec,
        out_shape=jax.ShapeDtypeStruct((B, L, W_A), BF16),
        compiler_params=_cparams(("parallel", "arbitrary")),
        name="nbr_attn",
    )(q, k, v, k_ctx, v_ctx, bias)


HALO = 64
CONV_BLK = 128


def _conv_kernel(*refs, rope):
    if rope:
        x_ref, prev_ref, next_ref, w_ref, sh_ref, cos_ref, sin_ref, q_ref, k_ref, v_ref, xe_ref = refs
    else:
        x_ref, prev_ref, next_ref, w_ref, sh_ref, q_ref, k_ref, v_ref, xe_ref = refs
    i = pl.program_id(1)
    n_i = pl.num_programs(1)
    tl = x_ref.shape[1]
    zero = jnp.zeros_like(prev_ref[0])
    xe_ref[0:HALO] = jnp.where(i > 0, prev_ref[0], zero)
    xe_ref[HALO:HALO + tl] = x_ref[0]
    xe_ref[HALO + tl:] = jnp.where(i < n_i - 1, next_ref[0], zero)
    half = CONV_K // 2
    blocks = []
    for r0 in range(0, tl, CONV_BLK):
        win = xe_ref[r0:r0 + CONV_BLK + 2 * HALO]
        shifted = _dot(sh_ref[...], win)
        yb = w_ref[half:half + 1, :] * win[HALO:HALO + CONV_BLK].astype(F32)
        for n, j in enumerate([j for j in range(CONV_K) if j != half]):
            yb = yb + w_ref[j:j + 1, :] * shifted[n * CONV_BLK:(n + 1) * CONV_BLK]
        blocks.append(yb)
    y = jnp.concatenate(blocks, axis=0)
    y = y * jax.nn.sigmoid(y)

    def l2n(t):
        return t * lax.rsqrt(jnp.sum(t * t, axis=-1, keepdims=True) + EPS)

    def rot(t):
        if not rope:
            return t
        lane = lax.broadcasted_iota(jnp.int32, t.shape, 1)
        swapped = jnp.where(lane % 2 == 0, pltpu.roll(t, LANES - 1, 1), pltpu.roll(t, 1, 1))
        return t * cos_ref[...] + swapped * sin_ref[...]

    for h in range(H_B):
        sl = slice(h * D_K, (h + 1) * D_K)
        q_ref[0, :, sl] = (rot(l2n(y[:, sl])) * (D_K ** -0.5)).astype(q_ref.dtype)
        k_ref[0, :, sl] = rot(l2n(y[:, W_B + h * D_K:W_B + (h + 1) * D_K])).astype(k_ref.dtype)
    v_ref[0] = y[:, 2 * W_B:].astype(v_ref.dtype)


def _rope_tables(L):
    pos = jnp.arange(L)
    rows = (pos // GRID_W).astype(F32)
    cols = (pos % GRID_W).astype(F32)
    n_pairs = D_K // 4
    inv = ROPE_BASE ** (-jnp.arange(n_pairs, dtype=F32) / n_pairs)
    ang = jnp.concatenate([rows[:, None] * inv, cols[:, None] * inv], axis=-1)
    cos = jnp.repeat(jnp.cos(ang), 2, axis=-1)
    sin = jnp.repeat(jnp.sin(ang), 2, axis=-1)
    sign = jnp.asarray(np.tile(np.array([-1.0, 1.0], np.float32), D_K // 2))
    return cos, sin * sign


def _conv_call(x, conv_w, rope):
    B, L, C = x.shape
    tl = min(ROW_TILE, L)
    nh = tl // HALO
    n_halo = L // HALO
    cur = pl.BlockSpec((1, tl, C), lambda b, i: (b, i, 0))
    prev = pl.BlockSpec((1, HALO, C), lambda b, i: (b, jnp.maximum(i * nh - 1, 0), 0))
    nxt = pl.BlockSpec((1, HALO, C), lambda b, i: (b, jnp.minimum((i + 1) * nh, n_halo - 1), 0))
    half = CONV_K // 2
    win_rows = CONV_BLK + 2 * HALO
    sh = np.zeros((CONV_K - 1, CONV_BLK, win_rows), np.float32)
    for n, j in enumerate([j for j in range(CONV_K) if j != half]):
        sh[n, np.arange(CONV_BLK), HALO + np.arange(CONV_BLK) + j - half] = 1.0
    sh = jnp.asarray(sh.reshape((CONV_K - 1) * CONV_BLK, win_rows)).astype(BF16)
    assert tl % CONV_BLK == 0 and L % HALO == 0
    ins = [x, x, x, conv_w, sh]
    specs = [cur, prev, nxt, _const_spec(conv_w.shape), _const_spec(sh.shape)]
    if rope:
        cos, sin = _rope_tables(L)
        ins += [cos, sin]
        specs += [pl.BlockSpec((tl, D_K), lambda b, i: (i, 0))] * 2
    out = pl.BlockSpec((1, tl, W_B), lambda b, i: (b, i, 0))
    return pl.pallas_call(
        functools.partial(_conv_kernel, rope=rope),
        grid=(B, L // tl),
        in_specs=specs,
        out_specs=[out, out, out],
        out_shape=[jax.ShapeDtypeStruct((B, L, W_B), BF16)] * 3,
        scratch_shapes=[pltpu.VMEM((tl + 2 * HALO, C), BF16)],
        compiler_params=_cparams(("parallel", "parallel")),
        name="conv",
    )(*ins)


DELTA_CT = 4
DELTA_CT_LONG = 8
C4 = H_B * CHUNK


def _blockdiag_rows(x, n_blk):
    w = x.shape[1] // n_blk
    lane_blk = lax.broadcasted_iota(jnp.int32, x.shape, 1) // w
    return jnp.concatenate([jnp.where(lane_blk == b, x, 0.0) for b in range(n_blk)], axis=0).astype(BF16)


def _widen(ec, col0, width):
    blk = lax.broadcasted_iota(jnp.int32, (ec.shape[0], H_B * width), 1) // width
    out = ec[:, col0 + H_B - 1:col0 + H_B]
    for h in range(H_B - 2, -1, -1):
        out = jnp.where(blk == h, ec[:, col0 + h:col0 + h + 1], out)
    return out


def _delta_prep(groups):
    c = CHUNK
    row = lax.broadcasted_iota(jnp.int32, (c, C4), 0)
    coll = lax.broadcasted_iota(jnp.int32, (c, C4), 1) % c
    eye = jnp.where(row == coll, 1.0, 0.0)
    st = []
    for q, k, v, ec, gr, backward in groups:
        gi = H_B if backward else 0
        bi = 3 * H_B if backward else 2 * H_B
        gc64, gc128 = _widen(ec, gi, c), _widen(ec, gi, D_K)
        beta64, beta128 = _widen(ec, bi, c), _widen(ec, bi, D_K)
        incl = (row <= coll) if backward else (row >= coll)
        strict = (row < coll) if backward else (row > coll)
        decay = jnp.where(incl, jnp.exp(jnp.where(incl, gc64 - gr, 0.0)), 0.0)
        kf = k.astype(F32)
        qf = q.astype(F32)
        eg = jnp.exp(gc128)
        g_last = gc128[0:1, :] if backward else gc128[c - 1:c, :]
        st.append(dict(
            kf=kf, qf=qf, decay=decay, strict=strict, beta64=beta64,
            qg=(qf * eg).astype(BF16),
            kg=(kf * jnp.exp(g_last - gc128)).astype(BF16),
            rhs_v=(v.astype(F32) * beta128).astype(BF16),
            rhs_k=(kf * (beta128 * eg)).astype(BF16),
            egl=jnp.exp(g_last)))
    lane2 = lax.broadcasted_iota(jnp.int32, (c, 2 * D_K), 1)
    for s in st:
        outs = []
        for p in range(H_B // 2):
            kp = s["kf"][:, p * 2 * D_K:(p + 1) * 2 * D_K]
            qp = s["qf"][:, p * 2 * D_K:(p + 1) * 2 * D_K]
            rmat = jnp.concatenate([jnp.where(lane2 < D_K, kp, 0.0), jnp.where(lane2 >= D_K, kp, 0.0)], axis=0)
            outs.append(_dot_nt(jnp.concatenate([kp, qp], axis=0).astype(BF16), rmat.astype(BF16)))
        kk = jnp.concatenate([o[:c] for o in outs], axis=1)
        qk = jnp.concatenate([o[c:] for o in outs], axis=1)
        s["a"] = jnp.where(s["strict"], kk * s["decay"], 0.0) * s["beta64"]
        s["intra"] = (qk * s["decay"]).astype(BF16)
        s["t"] = eye - jnp.where(row // 2 == coll // 2, s["a"], 0.0)
    b = 2
    while b < c:
        join = (row // (2 * b) == coll // (2 * b)) & (row // b != coll // b)
        for s in st:
            s["tb"] = s["t"].astype(BF16)
            s["tl"] = _dot(s["tb"], _blockdiag_rows(jnp.where(join, s["a"], 0.0), H_B))
        for s in st:
            s["t"] = s["t"] - _dot(s["tl"].astype(BF16), _blockdiag_rows(s["t"], H_B))
        b *= 2
    zero = jnp.zeros((c, 2 * D_K), BF16)
    for s in st:
        tb = s["t"].astype(BF16)
        us, ws = [], []
        for h in range(H_B):
            rhs = jnp.concatenate([s["rhs_v"][:, h * D_V:(h + 1) * D_V], s["rhs_k"][:, h * D_K:(h + 1) * D_K]], axis=1)
            uw = _dot(tb, jnp.concatenate([zero] * h + [rhs] + [zero] * (H_B - 1 - h), axis=0))
            us.append(uw[:, :D_V])
            ws.append(uw[:, D_V:])
        s["u"] = jnp.concatenate(us, axis=1)
        s["w"] = jnp.concatenate(ws, axis=1).astype(BF16)
    return st


def _delta_scan_step(steps, states):
    c = CHUNK
    zero = jnp.zeros((D_K, D_V), BF16)
    mids = []
    for s, S in zip(steps, states):
        w_s, q_s = [], []
        for p in range(H_B // 2):
            sl = slice(p * 2 * D_K, (p + 1) * 2 * D_K)
            lhs = jnp.concatenate([s["w"][:, sl], s["qg"][:, sl]], axis=0)
            sa, sb = S[2 * p].astype(BF16), S[2 * p + 1].astype(BF16)
            sbd = jnp.concatenate([jnp.concatenate([sa, zero], axis=1), jnp.concatenate([zero, sb], axis=1)], axis=0)
            out = _dot(lhs, sbd)
            w_s.append(out[:c])
            q_s.append(out[c:])
        v_new = s["u"] - jnp.concatenate(w_s, axis=1)
        mids.append((v_new, jnp.concatenate(q_s, axis=1)))
    outs, new_states = [], []
    for s, S, (v_new, q_s) in zip(steps, states, mids):
        vnb = v_new.astype(BF16)
        outs.append(q_s + _dot(s["intra"], _blockdiag_rows(v_new, H_B)))
        new_states.append([S[h] * s["egl"][:, h * D_V:(h + 1) * D_V]
                           + _dot_tn(s["kg"][:, h * D_K:(h + 1) * D_K], vnb[:, h * D_V:(h + 1) * D_V])
                           for h in range(H_B)])
    return outs, new_states


def _delta_kernel(qf_ref, kf_ref, vf_ref, ecf_ref, erf_ref, qb_ref, kb_ref, vb_ref, ecb_ref, erb_ref,
                  s0f_ref, s0b_ref, of_ref, ob_ref, sf_ref, sb_ref, s_scr):
    j = pl.program_id(1)
    c = CHUNK

    @pl.when(j == 0)
    def _():
        s_scr[0] = s0f_ref[0]
        s_scr[1] = s0b_ref[0]

    def group(refs, ci, backward):
        q_ref, k_ref, v_ref, ec_ref, er_ref = refs
        rows = slice(ci * c, (ci + 1) * c)
        d = 1 if backward else 0
        return (q_ref[0, rows, :], k_ref[0, rows, :], v_ref[0, rows, :], ec_ref[0, rows, :],
                er_ref[0, ci, d:d + 1, :], backward)

    fwd = (qf_ref, kf_ref, vf_ref, ecf_ref, erf_ref)
    bwd = (qb_ref, kb_ref, vb_ref, ecb_ref, erb_ref)
    ct = erf_ref.shape[1]
    prep = _delta_prep([group(fwd, ci, False) for ci in range(ct)]
                       + [group(bwd, ci, True) for ci in range(ct)])
    states = [[s_scr[d, h] for h in range(H_B)] for d in range(2)]
    for step in range(ct):
        cf, cb = step, ct - 1 - step
        outs, states = _delta_scan_step([prep[cf], prep[ct + cb]], states)
        of_ref[0, cf * c:(cf + 1) * c, :] = outs[0].astype(of_ref.dtype)
        ob_ref[0, cb * c:(cb + 1) * c, :] = outs[1].astype(ob_ref.dtype)
    for d in range(2):
        for h in range(H_B):
            s_scr[d, h] = states[d][h]

    @pl.when(j == pl.num_programs(1) - 1)
    def _():
        sf_ref[0] = s_scr[0]
        sb_ref[0] = s_scr[1]


def _delta_call(q, k, v, ecol, s0f, s0b):
    B, L, _ = q.shape
    n = L // CHUNK
    ct = DELTA_CT_LONG if n % DELTA_CT_LONG == 0 else DELTA_CT
    tl = ct * CHUNK
    nb = L // tl
    erow = ecol[..., :2 * H_B].reshape(B, n, CHUNK, 2, H_B).transpose(0, 1, 3, 4, 2).reshape(B, n, 2, C4)
    fmap = lambda b, j: (b, j, 0)
    bmap = lambda b, j: (b, nb - 1 - j, 0)
    seq = lambda m: pl.BlockSpec((1, tl, W_B), m)
    ecs = lambda m: pl.BlockSpec((1, tl, LANES), m)
    ers = lambda m: pl.BlockSpec((1, ct, 2, C4), lambda b, j: m(b, j) + (0,))
    st = pl.BlockSpec((1, H_B, D_K, D_V), lambda b, j: (b, 0, 0, 0))
    return pl.pallas_call(
        _delta_kernel,
        grid=(B, nb),
        in_specs=[seq(fmap), seq(fmap), seq(fmap), ecs(fmap), ers(fmap),
                  seq(bmap), seq(bmap), seq(bmap), ecs(bmap), ers(bmap), st, st],
        out_specs=[seq(fmap), seq(bmap), st, st],
        out_shape=[jax.ShapeDtypeStruct((B, L, W_B), BF16)] * 2
        + [jax.ShapeDtypeStruct((B, H_B, D_K, D_V), F32)] * 2,
        scratch_shapes=[pltpu.VMEM((2, H_B, D_K, D_V), F32)],
        compiler_params=_cparams(("parallel", "arbitrary")),
        name="delta",
    )(q, k, v, ecol, erow, q, k, v, ecol, erow, s0f, s0b)


def _merge_kernel(x_ref, oa_ref, of_ref, ob_ref, z_ref, gate_ref, g1_ref, sh2_ref, sc2_ref,
                  onw_ref, wpa_ref, wpb_ref, wo_ref, n2w_ref, wr_ref, br_ref, ltri_ref,
                  x1_ref, h2_ref, route_ref, cnt_ref, cnt_scr):
    tm = x_ref.shape[0]
    pm = ltri_ref.shape[0]
    parts = [dict(rows=slice(p * pm, (p + 1) * pm)) for p in range(tm // pm)]
    for s in parts:
        rows = s["rows"]
        o = of_ref[rows, :].astype(F32) + ob_ref[rows, :].astype(F32)
        heads = []
        for h in range(H_B):
            t = o[:, h * D_V:(h + 1) * D_V]
            heads.append(t * lax.rsqrt(jnp.mean(t * t, axis=-1, keepdims=True) + EPS))
        ob = jnp.concatenate(heads, axis=-1) * onw_ref[...] * z_ref[rows, :].astype(F32)
        s["pa"] = _dot(oa_ref[rows, :], wpa_ref[...])
        s["pb"] = _dot(ob.astype(BF16), wpb_ref[...])
    for s in parts:
        gate = gate_ref[s["rows"], :]
        merged = gate[:, :D_MODEL].astype(F32) * s["pa"] + gate[:, D_MODEL:].astype(F32) * s["pb"]
        s["mix"] = _dot(merged.astype(BF16), wo_ref[...])
    for s in parts:
        rows = s["rows"]
        x1 = x_ref[rows, :] + g1_ref[0] * s["mix"]
        x1_ref[rows, :] = x1
        ms = jnp.mean(x1 * x1, axis=-1, keepdims=True)
        h2 = x1 * lax.rsqrt(ms + EPS) * n2w_ref[...]
        h2 = h2 * (1.0 + sc2_ref[0]) + sh2_ref[0]
        hi = h2.astype(BF16)
        for c in range(ROW_CHUNKS):
            h2_ref[pl.ds(rows.start * ROW_CHUNKS + c, pm, stride=ROW_CHUNKS), :] = h2[:, c * LANES:(c + 1) * LANES]
        lo = (h2 - hi.astype(F32)).astype(BF16)
        s["logits"] = _dot(jnp.concatenate([hi, lo, hi], axis=-1), wr_ref[...]) + br_ref[...]
    lane = lax.broadcasted_iota(jnp.int32, (pm, LANES), 1)
    big = jnp.int32(LANES)

    def first_max(vals, mask):
        mv = jnp.where(mask, vals, NEG_INF)
        m = mv.max(axis=-1, keepdims=True)
        idx = jnp.where(mask & (mv == m), lane, big).min(axis=-1, keepdims=True)
        return m, idx

    for s in parts:
        logits = s["logits"]
        gmask = lane < N_GROUPS
        gm, grp = first_max(logits, gmask)
        p_grp = 1.0 / jnp.where(gmask, jnp.exp(logits - gm), 0.0).sum(axis=-1, keepdims=True)
        lo_lane = N_GROUPS + grp * EXP_PER_GROUP
        emask = (lane >= lo_lane) & (lane < lo_lane + EXP_PER_GROUP)
        em, _ = first_max(logits, emask)
        ex = jnp.where(emask, jnp.exp(logits - em), 0.0)
        probs = ex / ex.sum(axis=-1, keepdims=True)
        p1, i1 = first_max(probs, emask)
        p2, i2 = first_max(probs, emask & (lane != i1))
        tot = p1 + p2
        s["w1"] = p_grp * p1 / tot
        s["w2"] = p_grp * p2 / tot
        s["e1"] = i1 - N_GROUPS
        s["e2"] = i2 - N_GROUPS
        s["hit"] = jnp.where(lane == s["e1"], 1.0, 0.0) + jnp.where(lane == s["e2"], 1.0, 0.0)
        s["within"] = _dot(ltri_ref[...], s["hit"].astype(BF16))

    @pl.when(pl.program_id(0) == 0)
    def _():
        cnt_scr[...] = jnp.zeros_like(cnt_scr)

    running = cnt_scr[...]
    for s in parts:
        before = s["within"] + running
        running = running + s["hit"].sum(axis=0, keepdims=True)
        r1 = jnp.where(lane == s["e1"], before, 0.0).sum(axis=-1, keepdims=True)
        r2 = jnp.where(lane == s["e2"], before, 0.0).sum(axis=-1, keepdims=True)
        cols = [s["e1"].astype(F32), s["e2"].astype(F32), s["w1"], s["w2"], r1, r2]
        route = jnp.zeros((pm, LANES), F32)
        for i, col in enumerate(cols):
            route = jnp.where(lane == i, col, route)
        route_ref[s["rows"], :] = route
    cnt_scr[...] = running
    cnt_ref[...] = running


def _merge_call(x, oa, of, ob, z, gate, mod3, mod_stride, w):
    B, L, _ = x.shape
    n_tok = B * L
    tm = ROW_TILE
    assert n_tok % tm == 0 and (L % tm == 0 or tm % L == 0)
    flat = lambda a: a.reshape(n_tok, a.shape[-1])
    rspec = lambda n: pl.BlockSpec((tm, n), lambda i: (i, 0))
    mspec = lambda k: pl.BlockSpec((1, 1, D_MODEL), lambda i: ((i * tm) // L * mod_stride + k, 0, 0))
    t = np.arange(MERGE_PART)
    ltri = jnp.asarray((t[:, None] > t[None, :]).astype(np.float32)).astype(BF16)
    consts = [w["onw"], w["wpa"], w["wpb"], w["wo"], w["norm2_w"], w["wr3"], w["br"], ltri]
    return pl.pallas_call(
        _merge_kernel,
        grid=(n_tok // tm,),
        in_specs=[rspec(D_MODEL), rspec(W_A), rspec(W_B), rspec(W_B), rspec(W_B), rspec(2 * D_MODEL),
                  mspec(2), mspec(3), mspec(4)] + [_const_spec(c.shape) for c in consts],
        out_specs=[rspec(D_MODEL), pl.BlockSpec((tm * SUB, LANES), lambda i: (i, 0)), rspec(LANES),
                   _const_spec((1, LANES))],
        out_shape=[jax.ShapeDtypeStruct((n_tok, D_MODEL), F32),
                   jax.ShapeDtypeStruct((n_tok * SUB, LANES), F32),
                   jax.ShapeDtypeStruct((n_tok, LANES), F32),
                   jax.ShapeDtypeStruct((1, LANES), F32)],
        scratch_shapes=[pltpu.VMEM((1, LANES), F32)],
        compiler_params=_cparams(("arbitrary",)),
        name="merge",
    )(flat(x), flat(oa), flat(of), flat(ob), flat(z), flat(gate), mod3, mod3, mod3, *consts)


SUB = 8
ROW_CHUNKS = D_MODEL // LANES


def _store_token_major(ref, x):
    for s in range(ROW_CHUNKS):
        ref[pl.ds(s, x.shape[0], stride=ROW_CHUNKS), :] = x[:, s * LANES:(s + 1) * LANES]


def _load_token_major(ref, slot, row0, n_tok):
    return jnp.concatenate([ref[slot, pl.ds(row0 * ROW_CHUNKS + s, n_tok, stride=ROW_CHUNKS), :]
                            for s in range(ROW_CHUNKS)], axis=1)


def _row_gather_start(idx_ref, n_rows, src_hbm, buf, sem, slot):
    for r in range(n_rows):
        src = pl.multiple_of(idx_ref[0, 0, r] * SUB, SUB)
        pltpu.make_async_copy(src_hbm.at[pl.ds(src, SUB), :], buf.at[slot, pl.ds(r * SUB, SUB), :],
                              sem.at[slot]).start(priority=r % 2)


def _row_gather_wait(n_rows, src_hbm, buf, sem, slot):
    for r in range(n_rows):
        pltpu.make_async_copy(src_hbm.at[pl.ds(0, SUB), :], buf.at[slot, pl.ds(r * SUB, SUB), :],
                              sem.at[slot]).wait()


def _gather_pipeline(i, n_steps, idx_ref, idx_next_ref, n_rows, src_hbm, buf, sem):
    slot = lax.rem(i, 2)

    @pl.when(i == 0)
    def _():
        _row_gather_start(idx_ref, n_rows, src_hbm, buf, sem, 0)

    @pl.when(i + 1 < n_steps)
    def _():
        _row_gather_start(idx_next_ref, n_rows, src_hbm, buf, sem, 1 - slot)

    @pl.when(i < n_steps)
    def _():
        _row_gather_wait(n_rows, src_hbm, buf, sem, slot)

    return slot


def _moe_kernel(te_ref, nt_ref, idx_ref, idxn_ref, h2_hbm, wg_ref, wu_ref, wd_ref, y_ref, xbuf, sem):
    i = pl.program_id(0)
    n_used = nt_ref[0]
    slot = _gather_pipeline(i, n_used, idx_ref, idxn_ref, MOE_TILE, h2_hbm, xbuf, sem)

    @pl.when(i < n_used)
    def _():
        x = _load_token_major(xbuf, slot, 0, MOE_TILE).astype(BF16)
        g = _dot(x, wg_ref[0])
        hid = g * jax.nn.sigmoid(g) * _dot(x, wu_ref[0])
        _store_token_major(y_ref, _dot(hid.astype(BF16), wd_ref[0]))

    @pl.when(i >= n_used)
    def _():
        y_ref[...] = jnp.zeros_like(y_ref)


def _moe_call(h2, src_tok, tile_expert, n_tiles_used, wg, wu, wd):
    p_pad = src_tok.shape[0]
    n_tiles = p_pad // MOE_TILE
    idx = src_tok.reshape(n_tiles, 1, MOE_TILE)
    smem_tile = lambda m: pl.BlockSpec((1, 1, MOE_TILE), m, memory_space=pltpu.SMEM)
    grid_spec = pltpu.PrefetchScalarGridSpec(
        num_scalar_prefetch=2,
        grid=(n_tiles,),
        in_specs=[smem_tile(lambda i, te, nt: (i, 0, 0)),
                  smem_tile(lambda i, te, nt: (jnp.minimum(i + 1, n_tiles - 1), 0, 0)),
                  pl.BlockSpec(memory_space=pl.ANY),
                  pl.BlockSpec((1, D_MODEL, D_FF_E), lambda i, te, nt: (te[i], 0, 0)),
                  pl.BlockSpec((1, D_MODEL, D_FF_E), lambda i, te, nt: (te[i], 0, 0)),
                  pl.BlockSpec((1, D_FF_E, D_MODEL), lambda i, te, nt: (te[i], 0, 0))],
        out_specs=pl.BlockSpec((MOE_TILE * SUB, LANES), lambda i, te, nt: (i, 0)),
        scratch_shapes=[pltpu.VMEM((2, MOE_TILE * SUB, LANES), F32), pltpu.SemaphoreType.DMA((2,))],
    )
    return pl.pallas_call(
        _moe_kernel,
        grid_spec=grid_spec,
        out_shape=jax.ShapeDtypeStruct((p_pad * SUB, LANES), F32),
        compiler_params=_cparams(("arbitrary",)),
        name="moe",
    )(tile_expert, n_tiles_used, idx, idx, h2, wg, wu, wd)


COMBINE_TILE = 256


def _combine_kernel(pos_ref, posn_ref, x1_ref, route_ref, g2_ref, y_hbm, o_ref, ybuf, sem):
    i = pl.program_id(0)
    t = COMBINE_TILE
    slot = _gather_pipeline(i, pl.num_programs(0), pos_ref, posn_ref, 2 * t, y_hbm, ybuf, sem)
    route = route_ref[...]
    w1 = route[:, 2:3]
    w2 = route[:, 3:4]
    ya = _load_token_major(ybuf, slot, 0, t)
    yb = _load_token_major(ybuf, slot, t, t)
    o_ref[...] = x1_ref[...] + g2_ref[0] * (w1 * ya + w2 * yb)


def _combine_call(x1, route, pos2, y, mod3, mod_stride, seq_len):
    n = x1.shape[0]
    t = COMBINE_TILE
    n_tiles = n // t
    pos_tiles = pos2.reshape(n_tiles, t, 2).transpose(0, 2, 1).reshape(n_tiles, 1, 2 * t)
    smem_tile = lambda m: pl.BlockSpec((1, 1, 2 * t), m, memory_space=pltpu.SMEM)
    return pl.pallas_call(
        _combine_kernel,
        grid=(n_tiles,),
        in_specs=[smem_tile(lambda i: (i, 0, 0)),
                  smem_tile(lambda i: (jnp.minimum(i + 1, n_tiles - 1), 0, 0)),
                  pl.BlockSpec((t, D_MODEL), lambda i: (i, 0)),
                  pl.BlockSpec((t, LANES), lambda i: (i, 0)),
                  pl.BlockSpec((1, 1, D_MODEL), lambda i: ((i * t) // seq_len * mod_stride + 5, 0, 0)),
                  pl.BlockSpec(memory_space=pl.ANY)],
        out_specs=pl.BlockSpec((t, D_MODEL), lambda i: (i, 0)),
        out_shape=jax.ShapeDtypeStruct((n, D_MODEL), F32),
        scratch_shapes=[pltpu.VMEM((2, 2 * t * SUB, LANES), F32), pltpu.SemaphoreType.DMA((2,))],
        compiler_params=_cparams(("arbitrary",)),
        name="combine",
    )(pos_tiles, pos_tiles, x1, route, mod3, y)


def _inverse_map_kernel(pos_ref, init_hbm, out_ref):
    pltpu.sync_copy(init_hbm, out_ref)

    def body(p, carry):
        out_ref[pos_ref[p]] = lax.shift_right_logical(p, 1)
        return carry

    lax.fori_loop(0, pos_ref.shape[0], body, 0, unroll=8)


def _inverse_map_call(pos, init):
    return pl.pallas_call(
        _inverse_map_kernel,
        in_specs=[pl.BlockSpec(memory_space=pltpu.SMEM), pl.BlockSpec(memory_space=pl.ANY)],
        out_specs=pl.BlockSpec(memory_space=pltpu.SMEM),
        out_shape=jax.ShapeDtypeStruct(init.shape, jnp.int32),
        name="inverse_map",
    )(pos, init)


def _moe_forward(h2, route, counts, wg, wu, wd):
    n = route.shape[0]
    eid = route[:, :2].astype(jnp.int32).reshape(-1)
    rank = route[:, 4:6].astype(jnp.int32).reshape(-1)
    counts = counts[0, :N_EXPERTS].astype(jnp.int32)
    padded = ((counts + MOE_TILE - 1) // MOE_TILE) * MOE_TILE
    seg_end = jnp.cumsum(padded)
    seg_start = seg_end - padded
    pos = seg_start[eid] + rank
    p_pad = 2 * n + N_EXPERTS * MOE_TILE
    src_tok = _inverse_map_call(pos, jnp.arange(p_pad, dtype=jnp.int32) % n)
    n_tiles = p_pad // MOE_TILE
    tile_start = jnp.arange(n_tiles, dtype=jnp.int32) * MOE_TILE
    tile_expert = jnp.minimum(jnp.sum((tile_start[:, None] >= seg_end[None, :]).astype(jnp.int32), axis=1),
                              N_EXPERTS - 1)
    n_used = (seg_end[-1] // MOE_TILE).astype(jnp.int32).reshape(1)
    y = _moe_call(h2, src_tok, tile_expert, n_used, wg, wu, wd)
    return y, pos.reshape(n, 2)


def _prep_weights(norm1_w, norm2_w, w_in, conv_w, a_log, dt_bias, q_norm_w, k_norm_w, o_norm_w,
                  w_proj_a, w_proj_b, w_out, w_router_g, b_router_g, w_router_e, b_router_e,
                  w_exp_gate, w_exp_up, w_exp_down):
    w = {}
    w["norm1_w"] = norm1_w.reshape(1, D_MODEL)
    w["norm2_w"] = norm2_w.reshape(1, D_MODEL)
    o = 0
    w["wa"] = w_in[:, o:o + 3 * W_A].astype(BF16); o += 3 * W_A
    w["wb"] = w_in[:, o:o + 3 * W_B].astype(BF16); o += 3 * W_B
    w["wz"] = w_in[:, o:o + W_B].astype(BF16); o += W_B
    small = w_in[:, o:o + 4 * H_B]; o += 4 * H_B
    w["wg"] = w_in[:, o:o + 2 * D_MODEL].astype(BF16)
    def group_lanes(t):
        r = t.shape[0]
        t = t.reshape(r, -1, N_HG, HP).transpose(0, 2, 1, 3).reshape(r, N_HG, -1)
        return jnp.pad(t, ((0, 0), (0, 0), (0, LANES - t.shape[-1]))).reshape(r, N_HG * LANES)

    w["we"] = group_lanes(small).astype(BF16)
    w["alog_l"] = group_lanes(a_log.reshape(1, 2 * H_B))
    w["dtb_l"] = group_lanes(dt_bias.reshape(1, 2 * H_B))
    w["qnw"] = jnp.tile(q_norm_w, H_A).reshape(1, W_A)
    w["knw"] = jnp.tile(k_norm_w, H_A).reshape(1, W_A)
    w["onw"] = jnp.tile(o_norm_w, H_B).reshape(1, W_B)
    blk = np.arange(256) // D_HA
    w["bd256"] = jnp.asarray((blk[:, None] == blk[None, :]).astype(np.float32)).astype(BF16)
    w["conv_w"] = conv_w
    w["wpa"] = w_proj_a.astype(BF16)
    w["wpb"] = w_proj_b.astype(BF16)
    w["wo"] = w_out.astype(BF16)
    wr = jnp.concatenate([w_router_g, w_router_e], axis=1)
    wr = jnp.pad(wr, ((0, 0), (0, LANES - wr.shape[1])))
    wr_hi = wr.astype(BF16)
    wr_lo = (wr - wr_hi.astype(F32)).astype(BF16)
    w["wr3"] = jnp.concatenate([wr_hi, wr_hi, wr_lo], axis=0)
    br = jnp.concatenate([b_router_g, b_router_e])
    w["br"] = jnp.pad(br, (0, LANES - br.shape[0])).reshape(1, LANES)
    w["wg_e"] = w_exp_gate.astype(BF16)
    w["wu_e"] = w_exp_up.astype(BF16)
    w["wd_e"] = w_exp_down.astype(BF16)
    return w


def _layer(x, mod3, mod_stride, w, ctx):
    B, L, _ = x.shape
    latent = ctx is not None
    qa, ka, va, qkvb, z, gate, ecol, *new_kv = _inproj_call(x, mod3, mod_stride, w, not latent)
    if latent:
        k_ctx, v_ctx, s0f, s0b, bias = ctx
        oa = _nbr_attn_call(qa, ka, va, k_ctx, v_ctx, bias)
    else:
        s0f = s0b = jnp.zeros((B, H_B, D_K, D_V), F32)
        oa = _ctx_attn_call(qa, ka, va)
    qd, kd, vd = _conv_call(qkvb, w["conv_w"], rope=latent)
    of, ob, s_f, s_b = _delta_call(qd, kd, vd, ecol, s0f, s0b)
    x1, h2, route, counts = _merge_call(x, oa, of, ob, z, gate, mod3, mod_stride, w)
    ys, pos2 = _moe_forward(h2, route, counts, w["wg_e"], w["wu_e"], w["wd_e"])
    y = _combine_call(x1, route, pos2, ys, mod3, mod_stride, L)
    return y.reshape(B, L, D_MODEL), new_kv, s_f, s_b


def kernel(x_prompt, x_sample, c, cache_attn_k, cache_attn_v, state_delta_fwd, state_delta_bwd, c_ctx,
           norm1_w, norm2_w, w_mod, b_mod, w_in, conv_w, a_log, dt_bias, q_norm_w, k_norm_w, rpb,
           o_norm_w, w_proj_a, w_proj_b, w_out, w_router_g, b_router_g, w_router_e, b_router_e,
           w_exp_gate, w_exp_up, w_exp_down):
    assert norm1_w.shape[0] == 1, "single-layer trunk"
    Bp, Lp, _ = x_prompt.shape
    Bs, Ls, _ = x_sample.shape
    l = 0
    w = _prep_weights(norm1_w[l], norm2_w[l], w_in[l], conv_w[l], a_log[l], dt_bias[l], q_norm_w[l],
                      k_norm_w[l], o_norm_w[l], w_proj_a[l], w_proj_b[l], w_out[l], w_router_g[l],
                      b_router_g[l], w_router_e[l], b_router_e[l], w_exp_gate[l], w_exp_up[l],
                      w_exp_down[l])
    cond = jnp.concatenate([c_ctx[None, :], c, jnp.zeros((8 - 1 - Bs, D_MODEL), F32)], axis=0)
    mod = _mod_call(cond, w_mod[l], b_mod[l])
    mod3 = mod.reshape(8 * 6, 1, D_MODEL)
    bias = _bias_table_call(rpb[l])
    y_p, (new_k, new_v), s_f, s_b = _layer(x_prompt, mod3[:6], 0, w, None)
    lc = cache_attn_k.shape[2]
    ctx = (cache_attn_k[:, l].reshape(Bs, lc, W_A).astype(BF16),
           cache_attn_v[:, l].reshape(Bs, lc, W_A).astype(BF16),
           state_delta_fwd[:, l], state_delta_bwd[:, l], bias)
    y_s, _, _, _ = _layer(x_sample, mod3[6:6 + 6 * Bs], 6, w, ctx)
    return (y_p, y_s, new_k, new_v, s_f[:, None], s_b[:, None])
```

```python
import functools
import math

import numpy as np
import jax
import jax.numpy as jnp
from jax import lax
from jax.experimental import pallas as pl
from jax.experimental.pallas import tpu as pltpu

F32 = jnp.float32
BF16 = jnp.bfloat16

D_MODEL = 1024
GRID_W = 64
H_A = 8
D_HA = 64
W_A = H_A * D_HA
WIN_H = 8
WIN_W = 16
H_B = 4
D_K = 128
D_V = 128
W_B = H_B * D_V
CONV_K = 5
CHUNK = 64
ROPE_BASE = 10000.0
N_GROUPS = 4
EXP_PER_GROUP = 8
N_EXPERTS = N_GROUPS * EXP_PER_GROUP
D_FF_E = 256
EPS = 1e-6
NEG_INF = -1e30

LANES = 128
HP = H_B
N_HG = H_B // HP
ROW_TILE = 512
INPROJ_TILE = 256
INPROJ_PART = 256
MERGE_PART = 256
MOE_TILE = 256
VMEM_LIMIT = 56 * 1024 * 1024

HIGHEST = lax.Precision.HIGHEST


def _cparams(sem):
    return pltpu.CompilerParams(dimension_semantics=sem, vmem_limit_bytes=VMEM_LIMIT)


def _dot(a, b):
    return jnp.dot(a, b, preferred_element_type=F32)


def _dot_nt(a, b):
    return lax.dot_general(a, b, (((1,), (1,)), ((), ())), preferred_element_type=F32)


def _dot_tn(a, b):
    return lax.dot_general(a, b, (((0,), (0,)), ((), ())), preferred_element_type=F32)


def _const_spec(shape):
    nd = len(shape)
    return pl.BlockSpec(shape, lambda *_: (0,) * nd)


def _mod_kernel(c_ref, w_ref, b_ref, o_ref):
    c = c_ref[...]
    s = c * jax.nn.sigmoid(c)
    o_ref[...] = jnp.dot(s, w_ref[...], preferred_element_type=F32, precision=HIGHEST) + b_ref[...]


def _mod_call(cond8, w_mod, b_mod):
    n_out = w_mod.shape[1]
    tn = 1024
    return pl.pallas_call(
        _mod_kernel,
        grid=(n_out // tn,),
        in_specs=[pl.BlockSpec((8, D_MODEL), lambda j: (0, 0)),
                  pl.BlockSpec((D_MODEL, tn), lambda j: (0, j)),
                  pl.BlockSpec((1, tn), lambda j: (0, j))],
        out_specs=pl.BlockSpec((8, tn), lambda j: (0, j)),
        out_shape=jax.ShapeDtypeStruct((8, n_out), F32),
        compiler_params=_cparams(("arbitrary",)),
        name="mod",
    )(cond8, w_mod, b_mod.reshape(1, n_out))


def _softplus(x):
    return jnp.maximum(x, 0.0) + jnp.log1p(jnp.exp(-jnp.abs(x)))


def _inproj_kernel(x_ref, sh_ref, sc_ref, nw_ref, wa_ref, wb_ref, wz_ref, wg_ref, we_ref,
                   qnw_ref, knw_ref, bd_ref, alog_ref, dtb_ref, tri_ref,
                   qa_ref, ka_ref, va_ref, qkvb_ref, z_ref, gate_ref, e_ref, *kv5_refs):
    tm = x_ref.shape[0]
    pm = tri_ref.shape[1]
    parts = [dict(rows=slice(p * pm, (p + 1) * pm)) for p in range(tm // pm)]
    for s in parts:
        x = x_ref[s["rows"], :]
        ms = jnp.mean(x * x, axis=-1, keepdims=True)
        h = x * lax.rsqrt(ms + EPS) * nw_ref[...]
        h = h * (1.0 + sc_ref[0]) + sh_ref[0]
        s["hb"] = h.astype(BF16)
    for s in parts:
        s["e"] = _dot(s["hb"], we_ref[...])
    for s in parts:
        s["a"] = _dot(s["hb"], wa_ref[...])
    for s in parts:
        qkvb_ref[s["rows"], :] = _dot(s["hb"], wb_ref[...]).astype(qkvb_ref.dtype)
    bd = bd_ref[...]

    def head_rms(t, w):
        sq = (t * t).astype(BF16)
        ss = jnp.concatenate([_dot(sq[:, i * 256:(i + 1) * 256], bd) for i in range(W_A // 256)], axis=-1)
        return t * lax.rsqrt(ss * (1.0 / D_HA) + EPS) * w

    for p, s in enumerate(parts):
        rows, a = s["rows"], s["a"]
        qa = head_rms(a[:, :W_A], qnw_ref[...]) * (D_HA ** -0.5)
        qa_ref[rows, :] = qa.astype(qa_ref.dtype)
        ka = head_rms(a[:, W_A:2 * W_A], knw_ref[...])
        va = a[:, 2 * W_A:]
        ka_ref[rows, :] = ka.astype(ka_ref.dtype)
        va_ref[rows, :] = va.astype(va_ref.dtype)
        if kv5_refs:
            seq_len = kv5_refs[0].shape[2]
            b_in, off = (p * pm) // seq_len, (p * pm) % seq_len
            for ref, t in zip(kv5_refs, (ka, va)):
                for hd in range(H_A):
                    ref[b_in, 0, off:off + pm, hd, :] = t[:, hd * D_HA:(hd + 1) * D_HA]
    for s in parts:
        z = _dot(s["hb"], wz_ref[...])
        z_ref[s["rows"], :] = (z * jax.nn.sigmoid(z)).astype(z_ref.dtype)
    lane = lax.broadcasted_iota(jnp.int32, (pm, N_HG * LANES), 1) % LANES
    tri = tri_ref[...]
    for s in parts:
        e = s["e"]
        g = -jnp.exp(alog_ref[...]) * _softplus(e + dtb_ref[...])
        act = jnp.where(lane < 2 * HP, g, jax.nn.sigmoid(e))
        a_hi = act.astype(BF16)
        rem = act - a_hi.astype(F32)
        a_mid = rem.astype(BF16)
        a_lo = (rem - a_mid.astype(F32)).astype(BF16)
        cum = _dot(tri, a_hi) + _dot(tri, a_mid) + _dot(tri, a_lo)
        e_ref[s["rows"], :] = jnp.where(lane < HP, cum[:pm], jnp.where(lane < 2 * HP, cum[pm:], act))
    for s in parts:
        gate_ref[s["rows"], :] = jax.nn.sigmoid(_dot(s["hb"], wg_ref[...])).astype(gate_ref.dtype)


def _inproj_call(x, mod3, mod_stride, w, new_cache):
    B, L, _ = x.shape
    n_tok = B * L
    tm, pm = INPROJ_TILE, INPROJ_PART
    assert n_tok % tm == 0 and (L % tm == 0 or (tm % L == 0 and mod_stride == 0)) and L % pm == 0
    row = lambda n, dt: jax.ShapeDtypeStruct((n_tok, n), dt)
    rspec = lambda n: pl.BlockSpec((tm, n), lambda i: (i, 0))
    mspec = lambda k: pl.BlockSpec((1, 1, D_MODEL), lambda i: ((i * tm) // L * mod_stride + k, 0, 0))
    t = np.arange(pm)
    same = (t[:, None] // CHUNK) == (t[None, :] // CHUNK)
    tri = np.concatenate([same & (t[:, None] >= t[None, :]), same & (t[:, None] <= t[None, :])], axis=0)
    consts = [w["norm1_w"], w["wa"], w["wb"], w["wz"], w["wg"], w["we"], w["qnw"], w["knw"],
              w["bd256"], w["alog_l"], w["dtb_l"], jnp.asarray(tri.astype(np.float32)).astype(BF16)]
    out_specs = [rspec(W_A), rspec(W_A), rspec(W_A), rspec(3 * W_B), rspec(W_B),
                 rspec(2 * D_MODEL), rspec(N_HG * LANES)]
    out_shape = [row(W_A, BF16), row(W_A, BF16), row(W_A, BF16), row(3 * W_B, BF16),
                 row(W_B, BF16), row(2 * D_MODEL, BF16), row(N_HG * LANES, F32)]
    if new_cache:
        if tm <= L:
            per_seq = L // tm
            cache_spec = pl.BlockSpec((1, 1, tm, H_A, D_HA), lambda i: (i // per_seq, 0, i % per_seq, 0, 0))
        else:
            cache_spec = pl.BlockSpec((tm // L, 1, L, H_A, D_HA), lambda i: (i, 0, 0, 0, 0))
        out_specs += [cache_spec] * 2
        out_shape += [jax.ShapeDtypeStruct((B, 1, L, H_A, D_HA), F32)] * 2
    outs = pl.pallas_call(
        _inproj_kernel,
        grid=(n_tok // tm,),
        in_specs=[rspec(D_MODEL), mspec(0), mspec(1)] + [_const_spec(c.shape) for c in consts],
        out_specs=out_specs,
        out_shape=out_shape,
        compiler_params=_cparams(("parallel",)),
        name="inproj",
    )(x.reshape(n_tok, D_MODEL), mod3, mod3, *consts)
    return [o.reshape(B, L, o.shape[-1]) for o in outs[:7]] + list(outs[7:])


def _pair_softmax_pv(s_parts, v_parts):
    m = s_parts[0].max(axis=-1, keepdims=True)
    for s in s_parts[1:]:
        m = jnp.maximum(m, s.max(axis=-1, keepdims=True))
    den = 0.0
    acc = 0.0
    for s, v in zip(s_parts, v_parts):
        p = jnp.exp(s - m)
        den = den + p.sum(axis=-1, keepdims=True)
        acc = acc + _dot(p.astype(BF16), v)
    return acc / den


def _stack_pair(qp):
    lane = lax.broadcasted_iota(jnp.int32, qp.shape, 1)
    zero = jnp.zeros_like(qp)
    return jnp.concatenate([jnp.where(lane < D_HA, qp, zero), jnp.where(lane >= D_HA, qp, zero)], axis=0)


def _unstack_pair(o2):
    t = o2.shape[0] // 2
    lane = lax.broadcasted_iota(jnp.int32, (t, LANES), 1)
    return jnp.where(lane < D_HA, o2[:t], o2[t:])


def _ctx_attn_kernel(q_ref, k_ref, v_ref, o_ref):
    for p in range(H_A // 2):
        sl = slice(p * LANES, (p + 1) * LANES)
        qs = _stack_pair(q_ref[0, :, sl])
        kp = k_ref[0, :, sl].astype(BF16)
        vp = v_ref[0, :, sl].astype(BF16)
        o2 = _pair_softmax_pv([_dot_nt(qs, kp)], [vp])
        o_ref[0, :, sl] = _unstack_pair(o2).astype(o_ref.dtype)


def _ctx_attn_call(q, k, v):
    B, L, _ = q.shape
    spec = pl.BlockSpec((1, L, W_A), lambda b: (b, 0, 0))
    return pl.pallas_call(
        _ctx_attn_kernel,
        grid=(B,),
        in_specs=[spec, spec, spec],
        out_specs=spec,
        out_shape=jax.ShapeDtypeStruct((B, L, W_A), BF16),
        compiler_params=_cparams(("parallel",)),
        name="ctx_attn",
    )(q, k, v)


N_DR = 2 * WIN_H - 1
N_DC = 2 * WIN_W - 1
NBR_ROWS = 8


def _bias_table_kernel(rpb_ref, o_ref):
    h = pl.program_id(0)
    c = lax.broadcasted_iota(jnp.int32, (GRID_W, 2 * GRID_W), 0)
    lane = lax.broadcasted_iota(jnp.int32, (GRID_W, 2 * GRID_W), 1)
    kc = lane % GRID_W
    ws = jnp.clip(c - WIN_W // 2, 0, GRID_W - WIN_W)
    in_win = (kc >= ws) & (kc < ws + WIN_W)
    dc = kc - c + (WIN_W - 1)
    for dr in range(N_DR - 1):
        base = (h * N_DR + dr) * N_DC
        val = jnp.zeros((GRID_W, 2 * GRID_W), F32)
        for d in range(N_DC):
            val = jnp.where(dc == d, jnp.where(lane < GRID_W, rpb_ref[base + d], rpb_ref[base + N_DC + d]), val)
        o_ref[0, dr] = jnp.where(in_win, val, NEG_INF)


def _bias_table_call(rpb):
    return pl.pallas_call(
        _bias_table_kernel,
        grid=(H_A,),
        in_specs=[pl.BlockSpec(memory_space=pltpu.SMEM)],
        out_specs=pl.BlockSpec((1, N_DR - 1, GRID_W, 2 * GRID_W), lambda h: (h, 0, 0, 0)),
        out_shape=jax.ShapeDtypeStruct((H_A, N_DR - 1, GRID_W, 2 * GRID_W), F32),
        compiler_params=_cparams(("arbitrary",)),
        name="bias_table",
    )(rpb.reshape(-1))


def _nbr_attn_kernel(q_ref, k_ref, v_ref, kc_ref, vc_ref, bias_ref, o_ref, *, rows):
    inst = []
    for rr in range(NBR_ROWS):
        r = pl.program_id(1) * NBR_ROWS + rr
        first = jnp.clip(r - WIN_H // 2, 0, rows - WIN_H)
        start = pl.multiple_of(first * GRID_W, GRID_W)
        shift = r - first
        for p in range(H_A // 2):
            inst.append((rr, p, start, shift))
    scores = []
    for rr, p, start, shift in inst:
        sl = slice(p * LANES, (p + 1) * LANES)
        qs = _stack_pair(q_ref[0, rr * GRID_W:(rr + 1) * GRID_W, sl])
        bias = jnp.concatenate(
            [jnp.concatenate([bias_ref[2 * p + hh, 2 * u - shift + WIN_H - 1] for u in range(WIN_H // 2)], axis=1)
             for hh in range(2)], axis=0)
        s_loc = _dot_nt(qs, k_ref[0, pl.ds(start, WIN_H * GRID_W), sl]) + bias
        s_ctx = _dot_nt(qs, kc_ref[0, :, sl])
        scores.append((s_loc, s_ctx))
    for (rr, p, start, shift), (s_loc, s_ctx) in zip(inst, scores):
        sl = slice(p * LANES, (p + 1) * LANES)
        o2 = _pair_softmax_pv([s_loc, s_ctx], [v_ref[0, pl.ds(start, WIN_H * GRID_W), sl], vc_ref[0, :, sl]])
        o_ref[0, rr * GRID_W:(rr + 1) * GRID_W, sl] = _unstack_pair(o2).astype(o_ref.dtype)


def _nbr_attn_call(q, k, v, k_ctx, v_ctx, bias):
    B, L, _ = q.shape
    rows = L // GRID_W
    lc = k_ctx.shape[1]
    full = pl.BlockSpec((1, L, W_A), lambda b, r: (b, 0, 0))
    ctx = pl.BlockSpec((1, lc, W_A), lambda b, r: (b, 0, 0))
    rowspec = pl.BlockSpec((1, NBR_ROWS * GRID_W, W_A), lambda b, r: (b, r, 0))
    return pl.pallas_call(
        functools.partial(_nbr_attn_kernel, rows=rows),
        grid=(B, rows // NBR_ROWS),
        in_specs=[rowspec, full, full, ctx, ctx, _const_spec(bias.shape)],
        out_specs=rowspec,
        out_shape=jax.ShapeDtypeStruct((B, L, W_A), BF16),
        compiler_params=_cparams(("parallel", "arbitrary")),
        name="nbr_attn",
    )(q, k, v, k_ctx, v_ctx, bias)


HALO = 64
CONV_BLK = 128


def _conv_kernel(*refs, rope):
    if rope:
        x_ref, prev_ref, next_ref, w_ref, sh_ref, cos_ref, sin_ref, q_ref, k_ref, v_ref, xe_ref = refs
    else:
        x_ref, prev_ref, next_ref, w_ref, sh_ref, q_ref, k_ref, v_ref, xe_ref = refs
    i = pl.program_id(1)
    n_i = pl.num_programs(1)
    tl = x_ref.shape[1]
    zero = jnp.zeros_like(prev_ref[0])
    xe_ref[0:HALO] = jnp.where(i > 0, prev_ref[0], zero)
    xe_ref[HALO:HALO + tl] = x_ref[0]
    xe_ref[HALO + tl:] = jnp.where(i < n_i - 1, next_ref[0], zero)
    half = CONV_K // 2
    blocks = []
    for r0 in range(0, tl, CONV_BLK):
        win = xe_ref[r0:r0 + CONV_BLK + 2 * HALO]
        shifted = _dot(sh_ref[...], win)
        yb = w_ref[half:half + 1, :] * win[HALO:HALO + CONV_BLK].astype(F32)
        for n, j in enumerate([j for j in range(CONV_K) if j != half]):
            yb = yb + w_ref[j:j + 1, :] * shifted[n * CONV_BLK:(n + 1) * CONV_BLK]
        blocks.append(yb)
    y = jnp.concatenate(blocks, axis=0)
    y = y * jax.nn.sigmoid(y)

    def l2n(t):
        return t * lax.rsqrt(jnp.sum(t * t, axis=-1, keepdims=True) + EPS)

    def rot(t):
        if not rope:
            return t
        lane = lax.broadcasted_iota(jnp.int32, t.shape, 1)
        swapped = jnp.where(lane % 2 == 0, pltpu.roll(t, LANES - 1, 1), pltpu.roll(t, 1, 1))
        return t * cos_ref[...] + swapped * sin_ref[...]

    for h in range(H_B):
        sl = slice(h * D_K, (h + 1) * D_K)
        q_ref[0, :, sl] = (rot(l2n(y[:, sl])) * (D_K ** -0.5)).astype(q_ref.dtype)
        k_ref[0, :, sl] = rot(l2n(y[:, W_B + h * D_K:W_B + (h + 1) * D_K])).astype(k_ref.dtype)
    v_ref[0] = y[:, 2 * W_B:].astype(v_ref.dtype)


def _rope_tables(L):
    pos = jnp.arange(L)
    rows = (pos // GRID_W).astype(F32)
    cols = (pos % GRID_W).astype(F32)
    n_pairs = D_K // 4
    inv = ROPE_BASE ** (-jnp.arange(n_pairs, dtype=F32) / n_pairs)
    ang = jnp.concatenate([rows[:, None] * inv, cols[:, None] * inv], axis=-1)
    cos = jnp.repeat(jnp.cos(ang), 2, axis=-1)
    sin = jnp.repeat(jnp.sin(ang), 2, axis=-1)
    sign = jnp.asarray(np.tile(np.array([-1.0, 1.0], np.float32), D_K // 2))
    return cos, sin * sign


def _conv_call(x, conv_w, rope):
    B, L, C = x.shape
    tl = min(ROW_TILE, L)
    nh = tl // HALO
    n_halo = L // HALO
    cur = pl.BlockSpec((1, tl, C), lambda b, i: (b, i, 0))
    prev = pl.BlockSpec((1, HALO, C), lambda b, i: (b, jnp.maximum(i * nh - 1, 0), 0))
    nxt = pl.BlockSpec((1, HALO, C), lambda b, i: (b, jnp.minimum((i + 1) * nh, n_halo - 1), 0))
    half = CONV_K // 2
    win_rows = CONV_BLK + 2 * HALO
    sh = np.zeros((CONV_K - 1, CONV_BLK, win_rows), np.float32)
    for n, j in enumerate([j for j in range(CONV_K) if j != half]):
        sh[n, np.arange(CONV_BLK), HALO + np.arange(CONV_BLK) + j - half] = 1.0
    sh = jnp.asarray(sh.reshape((CONV_K - 1) * CONV_BLK, win_rows)).astype(BF16)
    assert tl % CONV_BLK == 0 and L % HALO == 0
    ins = [x, x, x, conv_w, sh]
    specs = [cur, prev, nxt, _const_spec(conv_w.shape), _const_spec(sh.shape)]
    if rope:
        cos, sin = _rope_tables(L)
        ins += [cos, sin]
        specs += [pl.BlockSpec((tl, D_K), lambda b, i: (i, 0))] * 2
    out = pl.BlockSpec((1, tl, W_B), lambda b, i: (b, i, 0))
    return pl.pallas_call(
        functools.partial(_conv_kernel, rope=rope),
        grid=(B, L // tl),
        in_specs=specs,
        out_specs=[out, out, out],
        out_shape=[jax.ShapeDtypeStruct((B, L, W_B), BF16)] * 3,
        scratch_shapes=[pltpu.VMEM((tl + 2 * HALO, C), BF16)],
        compiler_params=_cparams(("parallel", "parallel")),
        name="conv",
    )(*ins)


DELTA_CT = 4
DELTA_CT_LONG = 8
C4 = H_B * CHUNK


def _blockdiag_rows(x, n_blk):
    w = x.shape[1] // n_blk
    lane_blk = lax.broadcasted_iota(jnp.int32, x.shape, 1) // w
    return jnp.concatenate([jnp.where(lane_blk == b, x, 0.0) for b in range(n_blk)], axis=0).astype(BF16)


def _widen(ec, col0, width):
    blk = lax.broadcasted_iota(jnp.int32, (ec.shape[0], H_B * width), 1) // width
    out = ec[:, col0 + H_B - 1:col0 + H_B]
    for h in range(H_B - 2, -1, -1):
        out = jnp.where(blk == h, ec[:, col0 + h:col0 + h + 1], out)
    return out


def _delta_prep(groups):
    c = CHUNK
    row = lax.broadcasted_iota(jnp.int32, (c, C4), 0)
    coll = lax.broadcasted_iota(jnp.int32, (c, C4), 1) % c
    eye = jnp.where(row == coll, 1.0, 0.0)
    st = []
    for q, k, v, ec, gr, backward in groups:
        gi = H_B if backward else 0
        bi = 3 * H_B if backward else 2 * H_B
        gc64, gc128 = _widen(ec, gi, c), _widen(ec, gi, D_K)
        beta64, beta128 = _widen(ec, bi, c), _widen(ec, bi, D_K)
        incl = (row <= coll) if backward else (row >= coll)
        strict = (row < coll) if backward else (row > coll)
        decay = jnp.where(incl, jnp.exp(jnp.where(incl, gc64 - gr, 0.0)), 0.0)
        kf = k.astype(F32)
        qf = q.astype(F32)
        eg = jnp.exp(gc128)
        g_last = gc128[0:1, :] if backward else gc128[c - 1:c, :]
        st.append(dict(
            kf=kf, qf=qf, decay=decay, strict=strict, beta64=beta64,
            qg=(qf * eg).astype(BF16),
            kg=(kf * jnp.exp(g_last - gc128)).astype(BF16),
            rhs_v=(v.astype(F32) * beta128).astype(BF16),
            rhs_k=(kf * (beta128 * eg)).astype(BF16),
            egl=jnp.exp(g_last)))
    lane2 = lax.broadcasted_iota(jnp.int32, (c, 2 * D_K), 1)
    for s in st:
        outs = []
        for p in range(H_B // 2):
            kp = s["kf"][:, p * 2 * D_K:(p + 1) * 2 * D_K]
            qp = s["qf"][:, p * 2 * D_K:(p + 1) * 2 * D_K]
            rmat = jnp.concatenate([jnp.where(lane2 < D_K, kp, 0.0), jnp.where(lane2 >= D_K, kp, 0.0)], axis=0)
            outs.append(_dot_nt(jnp.concatenate([kp, qp], axis=0).astype(BF16), rmat.astype(BF16)))
        kk = jnp.concatenate([o[:c] for o in outs], axis=1)
        qk = jnp.concatenate([o[c:] for o in outs], axis=1)
        s["a"] = jnp.where(s["strict"], kk * s["decay"], 0.0) * s["beta64"]
        s["intra"] = (qk * s["decay"]).astype(BF16)
        s["t"] = eye - jnp.where(row // 2 == coll // 2, s["a"], 0.0)
    b = 2
    while b < c:
        join = (row // (2 * b) == coll // (2 * b)) & (row // b != coll // b)
        for s in st:
            s["tb"] = s["t"].astype(BF16)
            s["tl"] = _dot(s["tb"], _blockdiag_rows(jnp.where(join, s["a"], 0.0), H_B))
        for s in st:
            s["t"] = s["t"] - _dot(s["tl"].astype(BF16), _blockdiag_rows(s["t"], H_B))
        b *= 2
    zero = jnp.zeros((c, 2 * D_K), BF16)
    for s in st:
        tb = s["t"].astype(BF16)
        us, ws = [], []
        for h in range(H_B):
            rhs = jnp.concatenate([s["rhs_v"][:, h * D_V:(h + 1) * D_V], s["rhs_k"][:, h * D_K:(h + 1) * D_K]], axis=1)
            uw = _dot(tb, jnp.concatenate([zero] * h + [rhs] + [zero] * (H_B - 1 - h), axis=0))
            us.append(uw[:, :D_V])
            ws.append(uw[:, D_V:])
        s["u"] = jnp.concatenate(us, axis=1)
        s["w"] = jnp.concatenate(ws, axis=1).astype(BF16)
    return st


def _delta_scan_step(steps, states):
    c = CHUNK
    zero = jnp.zeros((D_K, D_V), BF16)
    mids = []
    for s, S in zip(steps, states):
        w_s, q_s = [], []
        for p in range(H_B // 2):
            sl = slice(p * 2 * D_K, (p + 1) * 2 * D_K)
            lhs = jnp.concatenate([s["w"][:, sl], s["qg"][:, sl]], axis=0)
            sa, sb = S[2 * p].astype(BF16), S[2 * p + 1].astype(BF16)
            sbd = jnp.concatenate([jnp.concatenate([sa, zero], axis=1), jnp.concatenate([zero, sb], axis=1)], axis=0)
            out = _dot(lhs, sbd)
            w_s.append(out[:c])
            q_s.append(out[c:])
        v_new = s["u"] - jnp.concatenate(w_s, axis=1)
        mids.append((v_new, jnp.concatenate(q_s, axis=1)))
    outs, new_states = [], []
    for s, S, (v_new, q_s) in zip(steps, states, mids):
        vnb = v_new.astype(BF16)
        outs.append(q_s + _dot(s["intra"], _blockdiag_rows(v_new, H_B)))
        new_states.append([S[h] * s["egl"][:, h * D_V:(h + 1) * D_V]
                           + _dot_tn(s["kg"][:, h * D_K:(h + 1) * D_K], vnb[:, h * D_V:(h + 1) * D_V])
                           for h in range(H_B)])
    return outs, new_states


def _delta_kernel(qf_ref, kf_ref, vf_ref, ecf_ref, erf_ref, qb_ref, kb_ref, vb_ref, ecb_ref, erb_ref,
                  s0f_ref, s0b_ref, of_ref, ob_ref, sf_ref, sb_ref, s_scr):
    j = pl.program_id(1)
    c = CHUNK

    @pl.when(j == 0)
    def _():
        s_scr[0] = s0f_ref[0]
        s_scr[1] = s0b_ref[0]

    def group(refs, ci, backward):
        q_ref, k_ref, v_ref, ec_ref, er_ref = refs
        rows = slice(ci * c, (ci + 1) * c)
        d = 1 if backward else 0
        return (q_ref[0, rows, :], k_ref[0, rows, :], v_ref[0, rows, :], ec_ref[0, rows, :],
                er_ref[0, ci, d:d + 1, :], backward)

    fwd = (qf_ref, kf_ref, vf_ref, ecf_ref, erf_ref)
    bwd = (qb_ref, kb_ref, vb_ref, ecb_ref, erb_ref)
    ct = erf_ref.shape[1]
    prep = _delta_prep([group(fwd, ci, False) for ci in range(ct)]
                       + [group(bwd, ci, True) for ci in range(ct)])
    states = [[s_scr[d, h] for h in range(H_B)] for d in range(2)]
    for step in range(ct):
        cf, cb = step, ct - 1 - step
        outs, states = _delta_scan_step([prep[cf], prep[ct + cb]], states)
        of_ref[0, cf * c:(cf + 1) * c, :] = outs[0].astype(of_ref.dtype)
        ob_ref[0, cb * c:(cb + 1) * c, :] = outs[1].astype(ob_ref.dtype)
    for d in range(2):
        for h in range(H_B):
            s_scr[d, h] = states[d][h]

    @pl.when(j == pl.num_programs(1) - 1)
    def _():
        sf_ref[0] = s_scr[0]
        sb_ref[0] = s_scr[1]


def _delta_call(q, k, v, ecol, s0f, s0b):
    B, L, _ = q.shape
    n = L // CHUNK
    ct = DELTA_CT_LONG if n % DELTA_CT_LONG == 0 else DELTA_CT
    tl = ct * CHUNK
    nb = L // tl
    erow = ecol[..., :2 * H_B].reshape(B, n, CHUNK, 2, H_B).transpose(0, 1, 3, 4, 2).reshape(B, n, 2, C4)
    fmap = lambda b, j: (b, j, 0)
    bmap = lambda b, j: (b, nb - 1 - j, 0)
    seq = lambda m: pl.BlockSpec((1, tl, W_B), m)
    ecs = lambda m: pl.BlockSpec((1, tl, LANES), m)
    ers = lambda m: pl.BlockSpec((1, ct, 2, C4), lambda b, j: m(b, j) + (0,))
    st = pl.BlockSpec((1, H_B, D_K, D_V), lambda b, j: (b, 0, 0, 0))
    return pl.pallas_call(
        _delta_kernel,
        grid=(B, nb),
        in_specs=[seq(fmap), seq(fmap), seq(fmap), ecs(fmap), ers(fmap),
                  seq(bmap), seq(bmap), seq(bmap), ecs(bmap), ers(bmap), st, st],
        out_specs=[seq(fmap), seq(bmap), st, st],
        out_shape=[jax.ShapeDtypeStruct((B, L, W_B), BF16)] * 2
        + [jax.ShapeDtypeStruct((B, H_B, D_K, D_V), F32)] * 2,
        scratch_shapes=[pltpu.VMEM((2, H_B, D_K, D_V), F32)],
        compiler_params=_cparams(("parallel", "arbitrary")),
        name="delta",
    )(q, k, v, ecol, erow, q, k, v, ecol, erow, s0f, s0b)


def _merge_kernel(x_ref, oa_ref, of_ref, ob_ref, z_ref, gate_ref, g1_ref, sh2_ref, sc2_ref,
                  onw_ref, wpa_ref, wpb_ref, wo_ref, n2w_ref, wr_ref, br_ref, ltri_ref,
                  x1_ref, h2_ref, route_ref, cnt_ref, cnt_scr):
    tm = x_ref.shape[0]
    pm = ltri_ref.shape[0]
    parts = [dict(rows=slice(p * pm, (p + 1) * pm)) for p in range(tm // pm)]
    for s in parts:
        rows = s["rows"]
        o = of_ref[rows, :].astype(F32) + ob_ref[rows, :].astype(F32)
        heads = []
        for h in range(H_B):
            t = o[:, h * D_V:(h + 1) * D_V]
            heads.append(t * lax.rsqrt(jnp.mean(t * t, axis=-1, keepdims=True) + EPS))
        ob = jnp.concatenate(heads, axis=-1) * onw_ref[...] * z_ref[rows, :].astype(F32)
        s["pa"] = _dot(oa_ref[rows, :], wpa_ref[...])
        s["pb"] = _dot(ob.astype(BF16), wpb_ref[...])
    for s in parts:
        gate = gate_ref[s["rows"], :]
        merged = gate[:, :D_MODEL].astype(F32) * s["pa"] + gate[:, D_MODEL:].astype(F32) * s["pb"]
        s["mix"] = _dot(merged.astype(BF16), wo_ref[...])
    for s in parts:
        rows = s["rows"]
        x1 = x_ref[rows, :] + g1_ref[0] * s["mix"]
        x1_ref[rows, :] = x1
        ms = jnp.mean(x1 * x1, axis=-1, keepdims=True)
        h2 = x1 * lax.rsqrt(ms + EPS) * n2w_ref[...]
        h2 = h2 * (1.0 + sc2_ref[0]) + sh2_ref[0]
        hi = h2.astype(BF16)
        for c in range(ROW_CHUNKS):
            h2_ref[pl.ds(rows.start * ROW_CHUNKS + c, pm, stride=ROW_CHUNKS), :] = h2[:, c * LANES:(c + 1) * LANES]
        lo = (h2 - hi.astype(F32)).astype(BF16)
        s["logits"] = _dot(jnp.concatenate([hi, lo, hi], axis=-1), wr_ref[...]) + br_ref[...]
    lane = lax.broadcasted_iota(jnp.int32, (pm, LANES), 1)
    big = jnp.int32(LANES)

    def first_max(vals, mask):
        mv = jnp.where(mask, vals, NEG_INF)
        m = mv.max(axis=-1, keepdims=True)
        idx = jnp.where(mask & (mv == m), lane, big).min(axis=-1, keepdims=True)
        return m, idx

    for s in parts:
        logits = s["logits"]
        gmask = lane < N_GROUPS
        gm, grp = first_max(logits, gmask)
        p_grp = 1.0 / jnp.where(gmask, jnp.exp(logits - gm), 0.0).sum(axis=-1, keepdims=True)
        lo_lane = N_GROUPS + grp * EXP_PER_GROUP
        emask = (lane >= lo_lane) & (lane < lo_lane + EXP_PER_GROUP)
        em, _ = first_max(logits, emask)
        ex = jnp.where(emask, jnp.exp(logits - em), 0.0)
        probs = ex / ex.sum(axis=-1, keepdims=True)
        p1, i1 = first_max(probs, emask)
        p2, i2 = first_max(probs, emask & (lane != i1))
        tot = p1 + p2
        s["w1"] = p_grp * p1 / tot
        s["w2"] = p_grp * p2 / tot
        s["e1"] = i1 - N_GROUPS
        s["e2"] = i2 - N_GROUPS
        s["hit"] = jnp.where(lane == s["e1"], 1.0, 0.0) + jnp.where(lane == s["e2"], 1.0, 0.0)
        s["within"] = _dot(ltri_ref[...], s["hit"].astype(BF16))

    @pl.when(pl.program_id(0) == 0)
    def _():
        cnt_scr[...] = jnp.zeros_like(cnt_scr)

    running = cnt_scr[...]
    for s in parts:
        before = s["within"] + running
        running = running + s["hit"].sum(axis=0, keepdims=True)
        r1 = jnp.where(lane == s["e1"], before, 0.0).sum(axis=-1, keepdims=True)
        r2 = jnp.where(lane == s["e2"], before, 0.0).sum(axis=-1, keepdims=True)
        cols = [s["e1"].astype(F32), s["e2"].astype(F32), s["w1"], s["w2"], r1, r2]
        route = jnp.zeros((pm, LANES), F32)
        for i, col in enumerate(cols):
            route = jnp.where(lane == i, col, route)
        route_ref[s["rows"], :] = route
    cnt_scr[...] = running
    cnt_ref[...] = running


def _merge_call(x, oa, of, ob, z, gate, mod3, mod_stride, w):
    B, L, _ = x.shape
    n_tok = B * L
    tm = ROW_TILE
    assert n_tok % tm == 0 and (L % tm == 0 or tm % L == 0)
    flat = lambda a: a.reshape(n_tok, a.shape[-1])
    rspec = lambda n: pl.BlockSpec((tm, n), lambda i: (i, 0))
    mspec = lambda k: pl.BlockSpec((1, 1, D_MODEL), lambda i: ((i * tm) // L * mod_stride + k, 0, 0))
    t = np.arange(MERGE_PART)
    ltri = jnp.asarray((t[:, None] > t[None, :]).astype(np.float32)).astype(BF16)
    consts = [w["onw"], w["wpa"], w["wpb"], w["wo"], w["norm2_w"], w["wr3"], w["br"], ltri]
    return pl.pallas_call(
        _merge_kernel,
        grid=(n_tok // tm,),
        in_specs=[rspec(D_MODEL), rspec(W_A), rspec(W_B), rspec(W_B), rspec(W_B), rspec(2 * D_MODEL),
                  mspec(2), mspec(3), mspec(4)] + [_const_spec(c.shape) for c in consts],
        out_specs=[rspec(D_MODEL), pl.BlockSpec((tm * SUB, LANES), lambda i: (i, 0)), rspec(LANES),
                   _const_spec((1, LANES))],
        out_shape=[jax.ShapeDtypeStruct((n_tok, D_MODEL), F32),
                   jax.ShapeDtypeStruct((n_tok * SUB, LANES), F32),
                   jax.ShapeDtypeStruct((n_tok, LANES), F32),
                   jax.ShapeDtypeStruct((1, LANES), F32)],
        scratch_shapes=[pltpu.VMEM((1, LANES), F32)],
        compiler_params=_cparams(("arbitrary",)),
        name="merge",
    )(flat(x), flat(oa), flat(of), flat(ob), flat(z), flat(gate), mod3, mod3, mod3, *consts)


SUB = 8
ROW_CHUNKS = D_MODEL // LANES


def _store_token_major(ref, x):
    for s in range(ROW_CHUNKS):
        ref[pl.ds(s, x.shape[0], stride=ROW_CHUNKS), :] = x[:, s * LANES:(s + 1) * LANES]


def _load_token_major(ref, slot, row0, n_tok):
    return jnp.concatenate([ref[slot, pl.ds(row0 * ROW_CHUNKS + s, n_tok, stride=ROW_CHUNKS), :]
                            for s in range(ROW_CHUNKS)], axis=1)


def _row_gather_start(idx_ref, n_rows, src_hbm, buf, sem, slot):
    for r in range(n_rows):
        src = pl.multiple_of(idx_ref[0, 0, r] * SUB, SUB)
        pltpu.make_async_copy(src_hbm.at[pl.ds(src, SUB), :], buf.at[slot, pl.ds(r * SUB, SUB), :],
                              sem.at[slot]).start(priority=r % 2)


def _row_gather_wait(n_rows, src_hbm, buf, sem, slot):
    for r in range(n_rows):
        pltpu.make_async_copy(src_hbm.at[pl.ds(0, SUB), :], buf.at[slot, pl.ds(r * SUB, SUB), :],
                              sem.at[slot]).wait()


def _gather_pipeline(i, n_steps, idx_ref, idx_next_ref, n_rows, src_hbm, buf, sem):
    slot = lax.rem(i, 2)

    @pl.when(i == 0)
    def _():
        _row_gather_start(idx_ref, n_rows, src_hbm, buf, sem, 0)

    @pl.when(i + 1 < n_steps)
    def _():
        _row_gather_start(idx_next_ref, n_rows, src_hbm, buf, sem, 1 - slot)

    @pl.when(i < n_steps)
    def _():
        _row_gather_wait(n_rows, src_hbm, buf, sem, slot)

    return slot


def _moe_kernel(te_ref, nt_ref, idx_ref, idxn_ref, h2_hbm, wg_ref, wu_ref, wd_ref, y_ref, xbuf, sem):
    i = pl.program_id(0)
    n_used = nt_ref[0]
    slot = _gather_pipeline(i, n_used, idx_ref, idxn_ref, MOE_TILE, h2_hbm, xbuf, sem)

    @pl.when(i < n_used)
    def _():
        x = _load_token_major(xbuf, slot, 0, MOE_TILE).astype(BF16)
        g = _dot(x, wg_ref[0])
        hid = g * jax.nn.sigmoid(g) * _dot(x, wu_ref[0])
        _store_token_major(y_ref, _dot(hid.astype(BF16), wd_ref[0]))

    @pl.when(i >= n_used)
    def _():
        y_ref[...] = jnp.zeros_like(y_ref)


def _moe_call(h2, src_tok, tile_expert, n_tiles_used, wg, wu, wd):
    p_pad = src_tok.shape[0]
    n_tiles = p_pad // MOE_TILE
    idx = src_tok.reshape(n_tiles, 1, MOE_TILE)
    smem_tile = lambda m: pl.BlockSpec((1, 1, MOE_TILE), m, memory_space=pltpu.SMEM)
    grid_spec = pltpu.PrefetchScalarGridSpec(
        num_scalar_prefetch=2,
        grid=(n_tiles,),
        in_specs=[smem_tile(lambda i, te, nt: (i, 0, 0)),
                  smem_tile(lambda i, te, nt: (jnp.minimum(i + 1, n_tiles - 1), 0, 0)),
                  pl.BlockSpec(memory_space=pl.ANY),
                  pl.BlockSpec((1, D_MODEL, D_FF_E), lambda i, te, nt: (te[i], 0, 0)),
                  pl.BlockSpec((1, D_MODEL, D_FF_E), lambda i, te, nt: (te[i], 0, 0)),
                  pl.BlockSpec((1, D_FF_E, D_MODEL), lambda i, te, nt: (te[i], 0, 0))],
        out_specs=pl.BlockSpec((MOE_TILE * SUB, LANES), lambda i, te, nt: (i, 0)),
        scratch_shapes=[pltpu.VMEM((2, MOE_TILE * SUB, LANES), F32), pltpu.SemaphoreType.DMA((2,))],
    )
    return pl.pallas_call(
        _moe_kernel,
        grid_spec=grid_spec,
        out_shape=jax.ShapeDtypeStruct((p_pad * SUB, LANES), F32),
        compiler_params=_cparams(("arbitrary",)),
        name="moe",
    )(tile_expert, n_tiles_used, idx, idx, h2, wg, wu, wd)


COMBINE_TILE = 512


def _combine_kernel(pos_ref, posn_ref, x1_ref, route_ref, g2_ref, y_hbm, o_ref, ybuf, sem):
    i = pl.program_id(0)
    t = COMBINE_TILE
    slot = _gather_pipeline(i, pl.num_programs(0), pos_ref, posn_ref, 2 * t, y_hbm, ybuf, sem)
    route = route_ref[...]
    w1 = route[:, 2:3]
    w2 = route[:, 3:4]
    ya = _load_token_major(ybuf, slot, 0, t)
    yb = _load_token_major(ybuf, slot, t, t)
    o_ref[...] = x1_ref[...] + g2_ref[0] * (w1 * ya + w2 * yb)


def _combine_call(x1, route, pos2, y, mod3, mod_stride, seq_len):
    n = x1.shape[0]
    t = COMBINE_TILE
    n_tiles = n // t
    pos_tiles = pos2.reshape(n_tiles, t, 2).transpose(0, 2, 1).reshape(n_tiles, 1, 2 * t)
    smem_tile = lambda m: pl.BlockSpec((1, 1, 2 * t), m, memory_space=pltpu.SMEM)
    return pl.pallas_call(
        _combine_kernel,
        grid=(n_tiles,),
        in_specs=[smem_tile(lambda i: (i, 0, 0)),
                  smem_tile(lambda i: (jnp.minimum(i + 1, n_tiles - 1), 0, 0)),
                  pl.BlockSpec((t, D_MODEL), lambda i: (i, 0)),
                  pl.BlockSpec((t, LANES), lambda i: (i, 0)),
                  pl.BlockSpec((1, 1, D_MODEL), lambda i: ((i * t) // seq_len * mod_stride + 5, 0, 0)),
                  pl.BlockSpec(memory_space=pl.ANY)],
        out_specs=pl.BlockSpec((t, D_MODEL), lambda i: (i, 0)),
        out_shape=jax.ShapeDtypeStruct((n, D_MODEL), F32),
        scratch_shapes=[pltpu.VMEM((2, 2 * t * SUB, LANES), F32), pltpu.SemaphoreType.DMA((2,))],
        compiler_params=_cparams(("arbitrary",)),
        name="combine",
    )(pos_tiles, pos_tiles, x1, route, mod3, y)


def _inverse_map_kernel(pos_ref, init_hbm, out_ref):
    pltpu.sync_copy(init_hbm, out_ref)

    def body(p, carry):
        out_ref[pos_ref[p]] = lax.shift_right_logical(p, 1)
        return carry

    lax.fori_loop(0, pos_ref.shape[0], body, 0, unroll=8)


def _inverse_map_call(pos, init):
    return pl.pallas_call(
        _inverse_map_kernel,
        in_specs=[pl.BlockSpec(memory_space=pltpu.SMEM), pl.BlockSpec(memory_space=pl.ANY)],
        out_specs=pl.BlockSpec(memory_space=pltpu.SMEM),
        out_shape=jax.ShapeDtypeStruct(init.shape, jnp.int32),
        name="inverse_map",
    )(pos, init)


def _moe_forward(h2, route, counts, wg, wu, wd):
    n = route.shape[0]
    eid = route[:, :2].astype(jnp.int32).reshape(-1)
    rank = route[:, 4:6].astype(jnp.int32).reshape(-1)
    counts = counts[0, :N_EXPERTS].astype(jnp.int32)
    padded = ((counts + MOE_TILE - 1) // MOE_TILE) * MOE_TILE
    seg_end = jnp.cumsum(padded)
    seg_start = seg_end - padded
    pos = seg_start[eid] + rank
    p_pad = 2 * n + N_EXPERTS * MOE_TILE
    src_tok = _inverse_map_call(pos, jnp.arange(p_pad, dtype=jnp.int32) % n)
    n_tiles = p_pad // MOE_TILE
    tile_start = jnp.arange(n_tiles, dtype=jnp.int32) * MOE_TILE
    tile_expert = jnp.minimum(jnp.sum((tile_start[:, None] >= seg_end[None, :]).astype(jnp.int32), axis=1),
                              N_EXPERTS - 1)
    n_used = (seg_end[-1] // MOE_TILE).astype(jnp.int32).reshape(1)
    y = _moe_call(h2, src_tok, tile_expert, n_used, wg, wu, wd)
    return y, pos.reshape(n, 2)


def _prep_weights(norm1_w, norm2_w, w_in, conv_w, a_log, dt_bias, q_norm_w, k_norm_w, o_norm_w,
                  w_proj_a, w_proj_b, w_out, w_router_g, b_router_g, w_router_e, b_router_e,
                  w_exp_gate, w_exp_up, w_exp_down):
    w = {}
    w["norm1_w"] = norm1_w.reshape(1, D_MODEL)
    w["norm2_w"] = norm2_w.reshape(1, D_MODEL)
    o = 0
    w["wa"] = w_in[:, o:o + 3 * W_A].astype(BF16); o += 3 * W_A
    w["wb"] = w_in[:, o:o + 3 * W_B].astype(BF16); o += 3 * W_B
    w["wz"] = w_in[:, o:o + W_B].astype(BF16); o += W_B
    small = w_in[:, o:o + 4 * H_B]; o += 4 * H_B
    w["wg"] = w_in[:, o:o + 2 * D_MODEL].astype(BF16)
    def group_lanes(t):
        r = t.shape[0]
        t = t.reshape(r, -1, N_HG, HP).transpose(0, 2, 1, 3).reshape(r, N_HG, -1)
        return jnp.pad(t, ((0, 0), (0, 0), (0, LANES - t.shape[-1]))).reshape(r, N_HG * LANES)

    w["we"] = group_lanes(small).astype(BF16)
    w["alog_l"] = group_lanes(a_log.reshape(1, 2 * H_B))
    w["dtb_l"] = group_lanes(dt_bias.reshape(1, 2 * H_B))
    w["qnw"] = jnp.tile(q_norm_w, H_A).reshape(1, W_A)
    w["knw"] = jnp.tile(k_norm_w, H_A).reshape(1, W_A)
    w["onw"] = jnp.tile(o_norm_w, H_B).reshape(1, W_B)
    blk = np.arange(256) // D_HA
    w["bd256"] = jnp.asarray((blk[:, None] == blk[None, :]).astype(np.float32)).astype(BF16)
    w["conv_w"] = conv_w
    w["wpa"] = w_proj_a.astype(BF16)
    w["wpb"] = w_proj_b.astype(BF16)
    w["wo"] = w_out.astype(BF16)
    wr = jnp.concatenate([w_router_g, w_router_e], axis=1)
    wr = jnp.pad(wr, ((0, 0), (0, LANES - wr.shape[1])))
    wr_hi = wr.astype(BF16)
    wr_lo = (wr - wr_hi.astype(F32)).astype(BF16)
    w["wr3"] = jnp.concatenate([wr_hi, wr_hi, wr_lo], axis=0)
    br = jnp.concatenate([b_router_g, b_router_e])
    w["br"] = jnp.pad(br, (0, LANES - br.shape[0])).reshape(1, LANES)
    w["wg_e"] = w_exp_gate.astype(BF16)
    w["wu_e"] = w_exp_up.astype(BF16)
    w["wd_e"] = w_exp_down.astype(BF16)
    return w


def _layer(x, mod3, mod_stride, w, ctx):
    B, L, _ = x.shape
    latent = ctx is not None
    qa, ka, va, qkvb, z, gate, ecol, *new_kv = _inproj_call(x, mod3, mod_stride, w, not latent)
    if latent:
        k_ctx, v_ctx, s0f, s0b, bias = ctx
        oa = _nbr_attn_call(qa, ka, va, k_ctx, v_ctx, bias)
    else:
        s0f = s0b = jnp.zeros((B, H_B, D_K, D_V), F32)
        oa = _ctx_attn_call(qa, ka, va)
    qd, kd, vd = _conv_call(qkvb, w["conv_w"], rope=latent)
    of, ob, s_f, s_b = _delta_call(qd, kd, vd, ecol, s0f, s0b)
    x1, h2, route, counts = _merge_call(x, oa, of, ob, z, gate, mod3, mod_stride, w)
    ys, pos2 = _moe_forward(h2, route, counts, w["wg_e"], w["wu_e"], w["wd_e"])
    y = _combine_call(x1, route, pos2, ys, mod3, mod_stride, L)
    return y.reshape(B, L, D_MODEL), new_kv, s_f, s_b


def kernel(x_prompt, x_sample, c, cache_attn_k, cache_attn_v, state_delta_fwd, state_delta_bwd, c_ctx,
           norm1_w, norm2_w, w_mod, b_mod, w_in, conv_w, a_log, dt_bias, q_norm_w, k_norm_w, rpb,
           o_norm_w, w_proj_a, w_proj_b, w_out, w_router_g, b_router_g, w_router_e, b_router_e,
           w_exp_gate, w_exp_up, w_exp_down):
    assert norm1_w.shape[0] == 1, "single-layer trunk"
    Bp, Lp, _ = x_prompt.shape
    Bs, Ls, _ = x_sample.shape
    l = 0
    w = _prep_weights(norm1_w[l], norm2_w[l], w_in[l], conv_w[l], a_log[l], dt_bias[l], q_norm_w[l],
                      k_norm_w[l], o_norm_w[l], w_proj_a[l], w_proj_b[l], w_out[l], w_router_g[l],
                      b_router_g[l], w_router_e[l], b_router_e[l], w_exp_gate[l], w_exp_up[l],
                      w_exp_down[l])
    cond = jnp.concatenate([c_ctx[None, :], c, jnp.zeros((8 - 1 - Bs, D_MODEL), F32)], axis=0)
    mod = _mod_call(cond, w_mod[l], b_mod[l])
    mod3 = mod.reshape(8 * 6, 1, D_MODEL)
    bias = _bias_table_call(rpb[l])
    y_p, (new_k, new_v), s_f, s_b = _layer(x_prompt, mod3[:6], 0, w, None)
    lc = cache_attn_k.shape[2]
    ctx = (cache_attn_k[:, l].reshape(Bs, lc, W_A).astype(BF16),
           cache_attn_v[:, l].reshape(Bs, lc, W_A).astype(BF16),
           state_delta_fwd[:, l], state_delta_bwd[:, l], bias)
    y_s, _, _, _ = _layer(x_sample, mod3[6:6 + 6 * Bs], 6, w, ctx)
    return (y_p, y_s, new_k, new_v, s_f[:, None], s_b[:, None])
```

```python
import functools
import math

import numpy as np
import jax
import jax.numpy as jnp
from jax import lax
from jax.experimental import pallas as pl
from jax.experimental.pallas import tpu as pltpu

F32 = jnp.float32
BF16 = jnp.bfloat16

D_MODEL = 1024
GRID_W = 64
H_A = 8
D_HA = 64
W_A = H_A * D_HA
WIN_H = 8
WIN_W = 16
H_B = 4
D_K = 128
D_V = 128
W_B = H_B * D_V
CONV_K = 5
CHUNK = 64
ROPE_BASE = 10000.0
N_GROUPS = 4
EXP_PER_GROUP = 8
N_EXPERTS = N_GROUPS * EXP_PER_GROUP
D_FF_E = 256
EPS = 1e-6
NEG_INF = -1e30

LANES = 128
HP = H_B
N_HG = H_B // HP
ROW_TILE = 512
INPROJ_TILE = 256
INPROJ_PART = 256
MERGE_PART = 256
MOE_TILE = 256
VMEM_LIMIT = 56 * 1024 * 1024

HIGHEST = lax.Precision.HIGHEST


def _cparams(sem):
    return pltpu.CompilerParams(dimension_semantics=sem, vmem_limit_bytes=VMEM_LIMIT)


def _dot(a, b):
    return jnp.dot(a, b, preferred_element_type=F32)


def _dot_nt(a, b):
    return lax.dot_general(a, b, (((1,), (1,)), ((), ())), preferred_element_type=F32)


def _dot_tn(a, b):
    return lax.dot_general(a, b, (((0,), (0,)), ((), ())), preferred_element_type=F32)


def _const_spec(shape):
    nd = len(shape)
    return pl.BlockSpec(shape, lambda *_: (0,) * nd)


def _mod_kernel(c_ref, w_ref, b_ref, o_ref):
    c = c_ref[...]
    s = c * jax.nn.sigmoid(c)
    o_ref[...] = jnp.dot(s, w_ref[...], preferred_element_type=F32, precision=HIGHEST) + b_ref[...]


def _mod_call(cond8, w_mod, b_mod):
    n_out = w_mod.shape[1]
    tn = 1024
    return pl.pallas_call(
        _mod_kernel,
        grid=(n_out // tn,),
        in_specs=[pl.BlockSpec((8, D_MODEL), lambda j: (0, 0)),
                  pl.BlockSpec((D_MODEL, tn), lambda j: (0, j)),
                  pl.BlockSpec((1, tn), lambda j: (0, j))],
        out_specs=pl.BlockSpec((8, tn), lambda j: (0, j)),
        out_shape=jax.ShapeDtypeStruct((8, n_out), F32),
        compiler_params=_cparams(("arbitrary",)),
        name="mod",
    )(cond8, w_mod, b_mod.reshape(1, n_out))


def _softplus(x):
    return jnp.maximum(x, 0.0) + jnp.log1p(jnp.exp(-jnp.abs(x)))


def _inproj_kernel(x_ref, sh_ref, sc_ref, nw_ref, wa_ref, wb_ref, wz_ref, wg_ref, we_ref,
                   qnw_ref, knw_ref, bd_ref, alog_ref, dtb_ref, tri_ref,
                   qa_ref, ka_ref, va_ref, qkvb_ref, z_ref, gate_ref, e_ref, *kv5_refs):
    tm = x_ref.shape[0]
    pm = tri_ref.shape[1]
    parts = [dict(rows=slice(p * pm, (p + 1) * pm)) for p in range(tm // pm)]
    for s in parts:
        x = x_ref[s["rows"], :]
        ms = jnp.mean(x * x, axis=-1, keepdims=True)
        h = x * lax.rsqrt(ms + EPS) * nw_ref[...]
        h = h * (1.0 + sc_ref[0]) + sh_ref[0]
        s["hb"] = h.astype(BF16)
    for s in parts:
        s["e"] = _dot(s["hb"], we_ref[...])
    for s in parts:
        s["a"] = _dot(s["hb"], wa_ref[...])
    for s in parts:
        qkvb_ref[s["rows"], :] = _dot(s["hb"], wb_ref[...]).astype(qkvb_ref.dtype)
    bd = bd_ref[...]

    def head_rms(t, w):
        sq = (t * t).astype(BF16)
        ss = jnp.concatenate([_dot(sq[:, i * 256:(i + 1) * 256], bd) for i in range(W_A // 256)], axis=-1)
        return t * lax.rsqrt(ss * (1.0 / D_HA) + EPS) * w

    for p, s in enumerate(parts):
        rows, a = s["rows"], s["a"]
        qa = head_rms(a[:, :W_A], qnw_ref[...]) * (D_HA ** -0.5)
        qa_ref[rows, :] = qa.astype(qa_ref.dtype)
        ka = head_rms(a[:, W_A:2 * W_A], knw_ref[...])
        va = a[:, 2 * W_A:]
        ka_ref[rows, :] = ka.astype(ka_ref.dtype)
        va_ref[rows, :] = va.astype(va_ref.dtype)
        if kv5_refs:
            seq_len = kv5_refs[0].shape[2]
            b_in, off = (p * pm) // seq_len, (p * pm) % seq_len
            for ref, t in zip(kv5_refs, (ka, va)):
                for hd in range(H_A):
                    ref[b_in, 0, off:off + pm, hd, :] = t[:, hd * D_HA:(hd + 1) * D_HA]
    for s in parts:
        z = _dot(s["hb"], wz_ref[...])
        z_ref[s["rows"], :] = (z * jax.nn.sigmoid(z)).astype(z_ref.dtype)
    lane = lax.broadcasted_iota(jnp.int32, (pm, N_HG * LANES), 1) % LANES
    tri = tri_ref[...]
    for s in parts:
        e = s["e"]
        g = -jnp.exp(alog_ref[...]) * _softplus(e + dtb_ref[...])
        act = jnp.where(lane < 2 * HP, g, jax.nn.sigmoid(e))
        a_hi = act.astype(BF16)
        rem = act - a_hi.astype(F32)
        a_mid = rem.astype(BF16)
        a_lo = (rem - a_mid.astype(F32)).astype(BF16)
        cum = _dot(tri, a_hi) + _dot(tri, a_mid) + _dot(tri, a_lo)
        e_ref[s["rows"], :] = jnp.where(lane < HP, cum[:pm], jnp.where(lane < 2 * HP, cum[pm:], act))
    for s in parts:
        gate_ref[s["rows"], :] = jax.nn.sigmoid(_dot(s["hb"], wg_ref[...])).astype(gate_ref.dtype)


def _inproj_call(x, mod3, mod_stride, w, new_cache):
    B, L, _ = x.shape
    n_tok = B * L
    tm, pm = INPROJ_TILE, INPROJ_PART
    assert n_tok % tm == 0 and (L % tm == 0 or (tm % L == 0 and mod_stride == 0)) and L % pm == 0
    row = lambda n, dt: jax.ShapeDtypeStruct((n_tok, n), dt)
    rspec = lambda n: pl.BlockSpec((tm, n), lambda i: (i, 0))
    mspec = lambda k: pl.BlockSpec((1, 1, D_MODEL), lambda i: ((i * tm) // L * mod_stride + k, 0, 0))
    t = np.arange(pm)
    same = (t[:, None] // CHUNK) == (t[None, :] // CHUNK)
    tri = np.concatenate([same & (t[:, None] >= t[None, :]), same & (t[:, None] <= t[None, :])], axis=0)
    consts = [w["norm1_w"], w["wa"], w["wb"], w["wz"], w["wg"], w["we"], w["qnw"], w["knw"],
              w["bd256"], w["alog_l"], w["dtb_l"], jnp.asarray(tri.astype(np.float32)).astype(BF16)]
    out_specs = [rspec(W_A), rspec(W_A), rspec(W_A), rspec(3 * W_B), rspec(W_B),
                 rspec(2 * D_MODEL), rspec(N_HG * LANES)]
    out_shape = [row(W_A, BF16), row(W_A, BF16), row(W_A, BF16), row(3 * W_B, BF16),
                 row(W_B, BF16), row(2 * D_MODEL, BF16), row(N_HG * LANES, F32)]
    if new_cache:
        if tm <= L:
            per_seq = L // tm
            cache_spec = pl.BlockSpec((1, 1, tm, H_A, D_HA), lambda i: (i // per_seq, 0, i % per_seq, 0, 0))
        else:
            cache_spec = pl.BlockSpec((tm // L, 1, L, H_A, D_HA), lambda i: (i, 0, 0, 0, 0))
        out_specs += [cache_spec] * 2
        out_shape += [jax.ShapeDtypeStruct((B, 1, L, H_A, D_HA), F32)] * 2
    outs = pl.pallas_call(
        _inproj_kernel,
        grid=(n_tok // tm,),
        in_specs=[rspec(D_MODEL), mspec(0), mspec(1)] + [_const_spec(c.shape) for c in consts],
        out_specs=out_specs,
        out_shape=out_shape,
        compiler_params=_cparams(("parallel",)),
        name="inproj",
    )(x.reshape(n_tok, D_MODEL), mod3, mod3, *consts)
    return [o.reshape(B, L, o.shape[-1]) for o in outs[:7]] + list(outs[7:])


def _pair_softmax_pv(s_parts, v_parts):
    m = s_parts[0].max(axis=-1, keepdims=True)
    for s in s_parts[1:]:
        m = jnp.maximum(m, s.max(axis=-1, keepdims=True))
    den = 0.0
    acc = 0.0
    for s, v in zip(s_parts, v_parts):
        p = jnp.exp(s - m)
        den = den + p.sum(axis=-1, keepdims=True)
        acc = acc + _dot(p.astype(BF16), v)
    return acc / den


def _stack_pair(qp):
    lane = lax.broadcasted_iota(jnp.int32, qp.shape, 1)
    zero = jnp.zeros_like(qp)
    return jnp.concatenate([jnp.where(lane < D_HA, qp, zero), jnp.where(lane >= D_HA, qp, zero)], axis=0)


def _unstack_pair(o2):
    t = o2.shape[0] // 2
    lane = lax.broadcasted_iota(jnp.int32, (t, LANES), 1)
    return jnp.where(lane < D_HA, o2[:t], o2[t:])


def _ctx_attn_kernel(q_ref, k_ref, v_ref, o_ref):
    for p in range(H_A // 2):
        sl = slice(p * LANES, (p + 1) * LANES)
        qs = _stack_pair(q_ref[0, :, sl])
        kp = k_ref[0, :, sl].astype(BF16)
        vp = v_ref[0, :, sl].astype(BF16)
        o2 = _pair_softmax_pv([_dot_nt(qs, kp)], [vp])
        o_ref[0, :, sl] = _unstack_pair(o2).astype(o_ref.dtype)


def _ctx_attn_call(q, k, v):
    B, L, _ = q.shape
    spec = pl.BlockSpec((1, L, W_A), lambda b: (b, 0, 0))
    return pl.pallas_call(
        _ctx_attn_kernel,
        grid=(B,),
        in_specs=[spec, spec, spec],
        out_specs=spec,
        out_shape=jax.ShapeDtypeStruct((B, L, W_A), BF16),
        compiler_params=_cparams(("parallel",)),
        name="ctx_attn",
    )(q, k, v)


N_DR = 2 * WIN_H - 1
N_DC = 2 * WIN_W - 1
NBR_ROWS = 8


def _bias_table_kernel(rpb_ref, o_ref):
    h = pl.program_id(0)
    c = lax.broadcasted_iota(jnp.int32, (GRID_W, 2 * GRID_W), 0)
    lane = lax.broadcasted_iota(jnp.int32, (GRID_W, 2 * GRID_W), 1)
    kc = lane % GRID_W
    ws = jnp.clip(c - WIN_W // 2, 0, GRID_W - WIN_W)
    in_win = (kc >= ws) & (kc < ws + WIN_W)
    dc = kc - c + (WIN_W - 1)
    for dr in range(N_DR - 1):
        base = (h * N_DR + dr) * N_DC
        val = jnp.zeros((GRID_W, 2 * GRID_W), F32)
        for d in range(N_DC):
            val = jnp.where(dc == d, jnp.where(lane < GRID_W, rpb_ref[base + d], rpb_ref[base + N_DC + d]), val)
        o_ref[0, dr] = jnp.where(in_win, val, NEG_INF)


def _bias_table_call(rpb):
    return pl.pallas_call(
        _bias_table_kernel,
        grid=(H_A,),
        in_specs=[pl.BlockSpec(memory_space=pltpu.SMEM)],
        out_specs=pl.BlockSpec((1, N_DR - 1, GRID_W, 2 * GRID_W), lambda h: (h, 0, 0, 0)),
        out_shape=jax.ShapeDtypeStruct((H_A, N_DR - 1, GRID_W, 2 * GRID_W), F32),
        compiler_params=_cparams(("arbitrary",)),
        name="bias_table",
    )(rpb.reshape(-1))


def _nbr_attn_kernel(q_ref, k_ref, v_ref, kc_ref, vc_ref, bias_ref, o_ref, *, rows):
    inst = []
    for rr in range(NBR_ROWS):
        r = pl.program_id(1) * NBR_ROWS + rr
        first = jnp.clip(r - WIN_H // 2, 0, rows - WIN_H)
        start = pl.multiple_of(first * GRID_W, GRID_W)
        shift = r - first
        for p in range(H_A // 2):
            inst.append((rr, p, start, shift))
    scores = []
    for rr, p, start, shift in inst:
        sl = slice(p * LANES, (p + 1) * LANES)
        qs = _stack_pair(q_ref[0, rr * GRID_W:(rr + 1) * GRID_W, sl])
        bias = jnp.concatenate(
            [jnp.concatenate([bias_ref[2 * p + hh, 2 * u - shift + WIN_H - 1] for u in range(WIN_H // 2)], axis=1)
             for hh in range(2)], axis=0)
        s_loc = _dot_nt(qs, k_ref[0, pl.ds(start, WIN_H * GRID_W), sl]) + bias
        s_ctx = _dot_nt(qs, kc_ref[0, :, sl])
        scores.append((s_loc, s_ctx))
    for (rr, p, start, shift), (s_loc, s_ctx) in zip(inst, scores):
        sl = slice(p * LANES, (p + 1) * LANES)
        o2 = _pair_softmax_pv([s_loc, s_ctx], [v_ref[0, pl.ds(start, WIN_H * GRID_W), sl], vc_ref[0, :, sl]])
        o_ref[0, rr * GRID_W:(rr + 1) * GRID_W, sl] = _unstack_pair(o2).astype(o_ref.dtype)


def _nbr_attn_call(q, k, v, k_ctx, v_ctx, bias):
    B, L, _ = q.shape
    rows = L // GRID_W
    lc = k_ctx.shape[1]
    full = pl.BlockSpec((1, L, W_A), lambda b, r: (b, 0, 0))
    ctx = pl.BlockSpec((1, lc, W_A), lambda b, r: (b, 0, 0))
    rowspec = pl.BlockSpec((1, NBR_ROWS * GRID_W, W_A), lambda b, r: (b, r, 0))
    return pl.pallas_call(
        functools.partial(_nbr_attn_kernel, rows=rows),
        grid=(B, rows // NBR_ROWS),
        in_specs=[rowspec, full, full, ctx, ctx, _const_spec(bias.shape)],
        out_specs=rowspec,
        out_shape=jax.ShapeDtypeStruct((B, L, W_A), BF16),
        compiler_params=_cparams(("parallel", "arbitrary")),
        name="nbr_attn",
    )(q, k, v, k_ctx, v_ctx, bias)


HALO = 64
CONV_BLK = 128


def _conv_kernel(*refs, rope):
    if rope:
        x_ref, prev_ref, next_ref, w_ref, sh_ref, cos_ref, sin_ref, q_ref, k_ref, v_ref, xe_ref = refs
    else:
        x_ref, prev_ref, next_ref, w_ref, sh_ref, q_ref, k_ref, v_ref, xe_ref = refs
    i = pl.program_id(1)
    n_i = pl.num_programs(1)
    tl = x_ref.shape[1]
    zero = jnp.zeros_like(prev_ref[0])
    xe_ref[0:HALO] = jnp.where(i > 0, prev_ref[0], zero)
    xe_ref[HALO:HALO + tl] = x_ref[0]
    xe_ref[HALO + tl:] = jnp.where(i < n_i - 1, next_ref[0], zero)
    half = CONV_K // 2
    blocks = []
    for r0 in range(0, tl, CONV_BLK):
        win = xe_ref[r0:r0 + CONV_BLK + 2 * HALO]
        shifted = _dot(sh_ref[...], win)
        yb = w_ref[half:half + 1, :] * win[HALO:HALO + CONV_BLK].astype(F32)
        for n, j in enumerate([j for j in range(CONV_K) if j != half]):
            yb = yb + w_ref[j:j + 1, :] * shifted[n * CONV_BLK:(n + 1) * CONV_BLK]
        blocks.append(yb)
    y = jnp.concatenate(blocks, axis=0)
    y = y * jax.nn.sigmoid(y)

    def l2n(t):
        return t * lax.rsqrt(jnp.sum(t * t, axis=-1, keepdims=True) + EPS)

    def rot(t):
        if not rope:
            return t
        lane = lax.broadcasted_iota(jnp.int32, t.shape, 1)
        swapped = jnp.where(lane % 2 == 0, pltpu.roll(t, LANES - 1, 1), pltpu.roll(t, 1, 1))
        return t * cos_ref[...] + swapped * sin_ref[...]

    for h in range(H_B):
        sl = slice(h * D_K, (h + 1) * D_K)
        q_ref[0, :, sl] = (rot(l2n(y[:, sl])) * (D_K ** -0.5)).astype(q_ref.dtype)
        k_ref[0, :, sl] = rot(l2n(y[:, W_B + h * D_K:W_B + (h + 1) * D_K])).astype(k_ref.dtype)
    v_ref[0] = y[:, 2 * W_B:].astype(v_ref.dtype)


def _rope_tables(L):
    pos = jnp.arange(L)
    rows = (pos // GRID_W).astype(F32)
    cols = (pos % GRID_W).astype(F32)
    n_pairs = D_K // 4
    inv = ROPE_BASE ** (-jnp.arange(n_pairs, dtype=F32) / n_pairs)
    ang = jnp.concatenate([rows[:, None] * inv, cols[:, None] * inv], axis=-1)
    cos = jnp.repeat(jnp.cos(ang), 2, axis=-1)
    sin = jnp.repeat(jnp.sin(ang), 2, axis=-1)
    sign = jnp.asarray(np.tile(np.array([-1.0, 1.0], np.float32), D_K // 2))
    return cos, sin * sign


def _conv_call(x, conv_w, rope):
    B, L, C = x.shape
    tl = min(ROW_TILE, L)
    nh = tl // HALO
    n_halo = L // HALO
    cur = pl.BlockSpec((1, tl, C), lambda b, i: (b, i, 0))
    prev = pl.BlockSpec((1, HALO, C), lambda b, i: (b, jnp.maximum(i * nh - 1, 0), 0))
    nxt = pl.BlockSpec((1, HALO, C), lambda b, i: (b, jnp.minimum((i + 1) * nh, n_halo - 1), 0))
    half = CONV_K // 2
    win_rows = CONV_BLK + 2 * HALO
    sh = np.zeros((CONV_K - 1, CONV_BLK, win_rows), np.float32)
    for n, j in enumerate([j for j in range(CONV_K) if j != half]):
        sh[n, np.arange(CONV_BLK), HALO + np.arange(CONV_BLK) + j - half] = 1.0
    sh = jnp.asarray(sh.reshape((CONV_K - 1) * CONV_BLK, win_rows)).astype(BF16)
    assert tl % CONV_BLK == 0 and L % HALO == 0
    ins = [x, x, x, conv_w, sh]
    specs = [cur, prev, nxt, _const_spec(conv_w.shape), _const_spec(sh.shape)]
    if rope:
        cos, sin = _rope_tables(L)
        ins += [cos, sin]
        specs += [pl.BlockSpec((tl, D_K), lambda b, i: (i, 0))] * 2
    out = pl.BlockSpec((1, tl, W_B), lambda b, i: (b, i, 0))
    return pl.pallas_call(
        functools.partial(_conv_kernel, rope=rope),
        grid=(B, L // tl),
        in_specs=specs,
        out_specs=[out, out, out],
        out_shape=[jax.ShapeDtypeStruct((B, L, W_B), BF16)] * 3,
        scratch_shapes=[pltpu.VMEM((tl + 2 * HALO, C), BF16)],
        compiler_params=_cparams(("parallel", "parallel")),
        name="conv",
    )(*ins)


DELTA_CT = 4
DELTA_CT_LONG = 8
C4 = H_B * CHUNK


def _blockdiag_rows(x, n_blk):
    w = x.shape[1] // n_blk
    lane_blk = lax.broadcasted_iota(jnp.int32, x.shape, 1) // w
    return jnp.concatenate([jnp.where(lane_blk == b, x, 0.0) for b in range(n_blk)], axis=0).astype(BF16)


def _widen(ec, col0, width):
    blk = lax.broadcasted_iota(jnp.int32, (ec.shape[0], H_B * width), 1) // width
    out = ec[:, col0 + H_B - 1:col0 + H_B]
    for h in range(H_B - 2, -1, -1):
        out = jnp.where(blk == h, ec[:, col0 + h:col0 + h + 1], out)
    return out


def _delta_prep(groups):
    c = CHUNK
    row = lax.broadcasted_iota(jnp.int32, (c, C4), 0)
    coll = lax.broadcasted_iota(jnp.int32, (c, C4), 1) % c
    eye = jnp.where(row == coll, 1.0, 0.0)
    st = []
    for q, k, v, ec, gr, backward in groups:
        gi = H_B if backward else 0
        bi = 3 * H_B if backward else 2 * H_B
        gc64, gc128 = _widen(ec, gi, c), _widen(ec, gi, D_K)
        beta64, beta128 = _widen(ec, bi, c), _widen(ec, bi, D_K)
        incl = (row <= coll) if backward else (row >= coll)
        strict = (row < coll) if backward else (row > coll)
        decay = jnp.where(incl, jnp.exp(jnp.where(incl, gc64 - gr, 0.0)), 0.0)
        kf = k.astype(F32)
        qf = q.astype(F32)
        eg = jnp.exp(gc128)
        g_last = gc128[0:1, :] if backward else gc128[c - 1:c, :]
        st.append(dict(
            kf=kf, qf=qf, decay=decay, strict=strict, beta64=beta64,
            qg=(qf * eg).astype(BF16),
            kg=(kf * jnp.exp(g_last - gc128)).astype(BF16),
            rhs_v=(v.astype(F32) * beta128).astype(BF16),
            rhs_k=(kf * (beta128 * eg)).astype(BF16),
            egl=jnp.exp(g_last)))
    lane2 = lax.broadcasted_iota(jnp.int32, (c, 2 * D_K), 1)
    for s in st:
        outs = []
        for p in range(H_B // 2):
            kp = s["kf"][:, p * 2 * D_K:(p + 1) * 2 * D_K]
            qp = s["qf"][:, p * 2 * D_K:(p + 1) * 2 * D_K]
            rmat = jnp.concatenate([jnp.where(lane2 < D_K, kp, 0.0), jnp.where(lane2 >= D_K, kp, 0.0)], axis=0)
            outs.append(_dot_nt(jnp.concatenate([kp, qp], axis=0).astype(BF16), rmat.astype(BF16)))
        kk = jnp.concatenate([o[:c] for o in outs], axis=1)
        qk = jnp.concatenate([o[c:] for o in outs], axis=1)
        s["a"] = jnp.where(s["strict"], kk * s["decay"], 0.0) * s["beta64"]
        s["intra"] = (qk * s["decay"]).astype(BF16)
        s["t"] = eye - jnp.where(row // 2 == coll // 2, s["a"], 0.0)
    b = 2
    while b < c:
        join = (row // (2 * b) == coll // (2 * b)) & (row // b != coll // b)
        for s in st:
            s["tb"] = s["t"].astype(BF16)
            s["tl"] = _dot(s["tb"], _blockdiag_rows(jnp.where(join, s["a"], 0.0), H_B))
        for s in st:
            s["t"] = s["t"] - _dot(s["tl"].astype(BF16), _blockdiag_rows(s["t"], H_B))
        b *= 2
    zero = jnp.zeros((c, 2 * D_K), BF16)
    for s in st:
        tb = s["t"].astype(BF16)
        us, ws = [], []
        for h in range(H_B):
            rhs = jnp.concatenate([s["rhs_v"][:, h * D_V:(h + 1) * D_V], s["rhs_k"][:, h * D_K:(h + 1) * D_K]], axis=1)
            uw = _dot(tb, jnp.concatenate([zero] * h + [rhs] + [zero] * (H_B - 1 - h), axis=0))
            us.append(uw[:, :D_V])
            ws.append(uw[:, D_V:])
        s["u"] = jnp.concatenate(us, axis=1)
        s["w"] = jnp.concatenate(ws, axis=1).astype(BF16)
    return st


def _delta_scan_step(steps, states):
    c = CHUNK
    zero = jnp.zeros((D_K, D_V), BF16)
    mids = []
    for s, S in zip(steps, states):
        w_s, q_s = [], []
        for p in range(H_B // 2):
            sl = slice(p * 2 * D_K, (p + 1) * 2 * D_K)
            lhs = jnp.concatenate([s["w"][:, sl], s["qg"][:, sl]], axis=0)
            sa, sb = S[2 * p].astype(BF16), S[2 * p + 1].astype(BF16)
            sbd = jnp.concatenate([jnp.concatenate([sa, zero], axis=1), jnp.concatenate([zero, sb], axis=1)], axis=0)
            out = _dot(lhs, sbd)
            w_s.append(out[:c])
            q_s.append(out[c:])
        v_new = s["u"] - jnp.concatenate(w_s, axis=1)
        mids.append((v_new, jnp.concatenate(q_s, axis=1)))
    outs, new_states = [], []
    for s, S, (v_new, q_s) in zip(steps, states, mids):
        vnb = v_new.astype(BF16)
        outs.append(q_s + _dot(s["intra"], _blockdiag_rows(v_new, H_B)))
        new_states.append([S[h] * s["egl"][:, h * D_V:(h + 1) * D_V]
                           + _dot_tn(s["kg"][:, h * D_K:(h + 1) * D_K], vnb[:, h * D_V:(h + 1) * D_V])
                           for h in range(H_B)])
    return outs, new_states


def _delta_kernel(qf_ref, kf_ref, vf_ref, ecf_ref, erf_ref, qb_ref, kb_ref, vb_ref, ecb_ref, erb_ref,
                  s0f_ref, s0b_ref, of_ref, ob_ref, sf_ref, sb_ref, s_scr):
    j = pl.program_id(1)
    c = CHUNK

    @pl.when(j == 0)
    def _():
        s_scr[0] = s0f_ref[0]
        s_scr[1] = s0b_ref[0]

    def group(refs, ci, backward):
        q_ref, k_ref, v_ref, ec_ref, er_ref = refs
        rows = slice(ci * c, (ci + 1) * c)
        d = 1 if backward else 0
        return (q_ref[0, rows, :], k_ref[0, rows, :], v_ref[0, rows, :], ec_ref[0, rows, :],
                er_ref[0, ci, d:d + 1, :], backward)

    fwd = (qf_ref, kf_ref, vf_ref, ecf_ref, erf_ref)
    bwd = (qb_ref, kb_ref, vb_ref, ecb_ref, erb_ref)
    ct = erf_ref.shape[1]
    prep = _delta_prep([group(fwd, ci, False) for ci in range(ct)]
                       + [group(bwd, ci, True) for ci in range(ct)])
    states = [[s_scr[d, h] for h in range(H_B)] for d in range(2)]
    for step in range(ct):
        cf, cb = step, ct - 1 - step
        outs, states = _delta_scan_step([prep[cf], prep[ct + cb]], states)
        of_ref[0, cf * c:(cf + 1) * c, :] = outs[0].astype(of_ref.dtype)
        ob_ref[0, cb * c:(cb + 1) * c, :] = outs[1].astype(ob_ref.dtype)
    for d in range(2):
        for h in range(H_B):
            s_scr[d, h] = states[d][h]

    @pl.when(j == pl.num_programs(1) - 1)
    def _():
        sf_ref[0] = s_scr[0]
        sb_ref[0] = s_scr[1]


def _delta_call(q, k, v, ecol, s0f, s0b):
    B, L, _ = q.shape
    n = L // CHUNK
    ct = DELTA_CT_LONG if n % DELTA_CT_LONG == 0 else DELTA_CT
    tl = ct * CHUNK
    nb = L // tl
    erow = ecol[..., :2 * H_B].reshape(B, n, CHUNK, 2, H_B).transpose(0, 1, 3, 4, 2).reshape(B, n, 2, C4)
    fmap = lambda b, j: (b, j, 0)
    bmap = lambda b, j: (b, nb - 1 - j, 0)
    seq = lambda m: pl.BlockSpec((1, tl, W_B), m)
    ecs = lambda m: pl.BlockSpec((1, tl, LANES), m)
    ers = lambda m: pl.BlockSpec((1, ct, 2, C4), lambda b, j: m(b, j) + (0,))
    st = pl.BlockSpec((1, H_B, D_K, D_V), lambda b, j: (b, 0, 0, 0))
    return pl.pallas_call(
        _delta_kernel,
        grid=(B, nb),
        in_specs=[seq(fmap), seq(fmap), seq(fmap), ecs(fmap), ers(fmap),
                  seq(bmap), seq(bmap), seq(bmap), ecs(bmap), ers(bmap), st, st],
        out_specs=[seq(fmap), seq(bmap), st, st],
        out_shape=[jax.ShapeDtypeStruct((B, L, W_B), BF16)] * 2
        + [jax.ShapeDtypeStruct((B, H_B, D_K, D_V), F32)] * 2,
        scratch_shapes=[pltpu.VMEM((2, H_B, D_K, D_V), F32)],
        compiler_params=_cparams(("parallel", "arbitrary")),
        name="delta",
    )(q, k, v, ecol, erow, q, k, v, ecol, erow, s0f, s0b)


def _merge_kernel(x_ref, oa_ref, of_ref, ob_ref, z_ref, gate_ref, g1_ref, sh2_ref, sc2_ref,
                  onw_ref, wpa_ref, wpb_ref, wo_ref, n2w_ref, wr_ref, br_ref, ltri_ref,
                  x1_ref, h2_ref, route_ref, cnt_ref, cnt_scr):
    tm = x_ref.shape[0]
    pm = ltri_ref.shape[0]
    parts = [dict(rows=slice(p * pm, (p + 1) * pm)) for p in range(tm // pm)]
    for s in parts:
        rows = s["rows"]
        o = of_ref[rows, :].astype(F32) + ob_ref[rows, :].astype(F32)
        heads = []
        for h in range(H_B):
            t = o[:, h * D_V:(h + 1) * D_V]
            heads.append(t * lax.rsqrt(jnp.mean(t * t, axis=-1, keepdims=True) + EPS))
        ob = jnp.concatenate(heads, axis=-1) * onw_ref[...] * z_ref[rows, :].astype(F32)
        s["pa"] = _dot(oa_ref[rows, :], wpa_ref[...])
        s["pb"] = _dot(ob.astype(BF16), wpb_ref[...])
    for s in parts:
        gate = gate_ref[s["rows"], :]
        merged = gate[:, :D_MODEL].astype(F32) * s["pa"] + gate[:, D_MODEL:].astype(F32) * s["pb"]
        s["mix"] = _dot(merged.astype(BF16), wo_ref[...])
    for s in parts:
        rows = s["rows"]
        x1 = x_ref[rows, :] + g1_ref[0] * s["mix"]
        x1_ref[rows, :] = x1
        ms = jnp.mean(x1 * x1, axis=-1, keepdims=True)
        h2 = x1 * lax.rsqrt(ms + EPS) * n2w_ref[...]
        h2 = h2 * (1.0 + sc2_ref[0]) + sh2_ref[0]
        hi = h2.astype(BF16)
        for c in range(ROW_CHUNKS):
            h2_ref[pl.ds(rows.start * ROW_CHUNKS + c, pm, stride=ROW_CHUNKS), :] = h2[:, c * LANES:(c + 1) * LANES]
        lo = (h2 - hi.astype(F32)).astype(BF16)
        s["logits"] = _dot(jnp.concatenate([hi, lo, hi], axis=-1), wr_ref[...]) + br_ref[...]
    lane = lax.broadcasted_iota(jnp.int32, (pm, LANES), 1)
    big = jnp.int32(LANES)

    def first_max(vals, mask):
        mv = jnp.where(mask, vals, NEG_INF)
        m = mv.max(axis=-1, keepdims=True)
        idx = jnp.where(mask & (mv == m), lane, big).min(axis=-1, keepdims=True)
        return m, idx

    for s in parts:
        logits = s["logits"]
        gmask = lane < N_GROUPS
        gm, grp = first_max(logits, gmask)
        p_grp = 1.0 / jnp.where(gmask, jnp.exp(logits - gm), 0.0).sum(axis=-1, keepdims=True)
        lo_lane = N_GROUPS + grp * EXP_PER_GROUP
        emask = (lane >= lo_lane) & (lane < lo_lane + EXP_PER_GROUP)
        em, _ = first_max(logits, emask)
        ex = jnp.where(emask, jnp.exp(logits - em), 0.0)
        probs = ex / ex.sum(axis=-1, keepdims=True)
        p1, i1 = first_max(probs, emask)
        p2, i2 = first_max(probs, emask & (lane != i1))
        tot = p1 + p2
        s["w1"] = p_grp * p1 / tot
        s["w2"] = p_grp * p2 / tot
        s["e1"] = i1 - N_GROUPS
        s["e2"] = i2 - N_GROUPS
        s["hit"] = jnp.where(lane == s["e1"], 1.0, 0.0) + jnp.where(lane == s["e2"], 1.0, 0.0)
        s["within"] = _dot(ltri_ref[...], s["hit"].astype(BF16))

    @pl.when(pl.program_id(0) == 0)
    def _():
        cnt_scr[...] = jnp.zeros_like(cnt_scr)

    running = cnt_scr[...]
    for s in parts:
        before = s["within"] + running
        running = running + s["hit"].sum(axis=0, keepdims=True)
        r1 = jnp.where(lane == s["e1"], before, 0.0).sum(axis=-1, keepdims=True)
        r2 = jnp.where(lane == s["e2"], before, 0.0).sum(axis=-1, keepdims=True)
        cols = [s["e1"].astype(F32), s["e2"].astype(F32), s["w1"], s["w2"], r1, r2]
        route = jnp.zeros((pm, LANES), F32)
        for i, col in enumerate(cols):
            route = jnp.where(lane == i, col, route)
        route_ref[s["rows"], :] = route
    cnt_scr[...] = running
    cnt_ref[...] = running


def _merge_call(x, oa, of, ob, z, gate, mod3, mod_stride, w):
    B, L, _ = x.shape
    n_tok = B * L
    tm = ROW_TILE
    assert n_tok % tm == 0 and (L % tm == 0 or tm % L == 0)
    flat = lambda a: a.reshape(n_tok, a.shape[-1])
    rspec = lambda n: pl.BlockSpec((tm, n), lambda i: (i, 0))
    mspec = lambda k: pl.BlockSpec((1, 1, D_MODEL), lambda i: ((i * tm) // L * mod_stride + k, 0, 0))
    t = np.arange(MERGE_PART)
    ltri = jnp.asarray((t[:, None] > t[None, :]).astype(np.float32)).astype(BF16)
    consts = [w["onw"], w["wpa"], w["wpb"], w["wo"], w["norm2_w"], w["wr3"], w["br"], ltri]
    return pl.pallas_call(
        _merge_kernel,
        grid=(n_tok // tm,),
        in_specs=[rspec(D_MODEL), rspec(W_A), rspec(W_B), rspec(W_B), rspec(W_B), rspec(2 * D_MODEL),
                  mspec(2), mspec(3), mspec(4)] + [_const_spec(c.shape) for c in consts],
        out_specs=[rspec(D_MODEL), pl.BlockSpec((tm * SUB, LANES), lambda i: (i, 0)), rspec(LANES),
                   _const_spec((1, LANES))],
        out_shape=[jax.ShapeDtypeStruct((n_tok, D_MODEL), F32),
                   jax.ShapeDtypeStruct((n_tok * SUB, LANES), F32),
                   jax.ShapeDtypeStruct((n_tok, LANES), F32),
                   jax.ShapeDtypeStruct((1, LANES), F32)],
        scratch_shapes=[pltpu.VMEM((1, LANES), F32)],
        compiler_params=_cparams(("arbitrary",)),
        name="merge",
    )(flat(x), flat(oa), flat(of), flat(ob), flat(z), flat(gate), mod3, mod3, mod3, *consts)


SUB = 8
ROW_CHUNKS = D_MODEL // LANES


def _store_token_major(ref, x):
    for s in range(ROW_CHUNKS):
        ref[pl.ds(s, x.shape[0], stride=ROW_CHUNKS), :] = x[:, s * LANES:(s + 1) * LANES]


def _load_token_major(ref, slot, row0, n_tok):
    return jnp.concatenate([ref[slot, pl.ds(row0 * ROW_CHUNKS + s, n_tok, stride=ROW_CHUNKS), :]
                            for s in range(ROW_CHUNKS)], axis=1)


def _row_gather_start(idx_ref, n_rows, src_hbm, buf, sem, slot):
    for r in range(n_rows):
        src = pl.multiple_of(idx_ref[0, 0, r] * SUB, SUB)
        pltpu.make_async_copy(src_hbm.at[pl.ds(src, SUB), :], buf.at[slot, pl.ds(r * SUB, SUB), :],
                              sem.at[slot]).start(priority=r % 2)


def _row_gather_wait(n_rows, src_hbm, buf, sem, slot):
    for r in range(n_rows):
        pltpu.make_async_copy(src_hbm.at[pl.ds(0, SUB), :], buf.at[slot, pl.ds(r * SUB, SUB), :],
                              sem.at[slot]).wait()


def _gather_pipeline(i, n_steps, idx_ref, idx_next_ref, n_rows, src_hbm, buf, sem):
    slot = lax.rem(i, 2)

    @pl.when(i == 0)
    def _():
        _row_gather_start(idx_ref, n_rows, src_hbm, buf, sem, 0)

    @pl.when(i + 1 < n_steps)
    def _():
        _row_gather_start(idx_next_ref, n_rows, src_hbm, buf, sem, 1 - slot)

    @pl.when(i < n_steps)
    def _():
        _row_gather_wait(n_rows, src_hbm, buf, sem, slot)

    return slot


def _moe_kernel(te_ref, nt_ref, idx_ref, idxn_ref, h2_hbm, wg_ref, wu_ref, wd_ref, y_ref, xbuf, sem):
    i = pl.program_id(0)
    n_used = nt_ref[0]
    slot = lax.rem(i, 2)
    n_kc = 4
    kw = D_MODEL // n_kc
    rows_kc = MOE_TILE // n_kc

    def start_rows(ref, r0, r1, s):
        for r in range(r0, r1):
            src = pl.multiple_of(ref[0, 0, r] * SUB, SUB)
            pltpu.make_async_copy(h2_hbm.at[pl.ds(src, SUB), :], xbuf.at[s, pl.ds(r * SUB, SUB), :],
                                  sem.at[s]).start(priority=r % 2)

    @pl.when(i == 0)
    def _():
        start_rows(idx_ref, 0, MOE_TILE, 0)

    @pl.when(i < n_used)
    def _():
        _row_gather_wait(MOE_TILE, h2_hbm, xbuf, sem, slot)
        x = _load_token_major(xbuf, slot, 0, MOE_TILE).astype(BF16)
        g = u = None
        for kc in range(n_kc):
            start_rows(idxn_ref, kc * rows_kc, (kc + 1) * rows_kc, 1 - slot)
            xs = x[:, kc * kw:(kc + 1) * kw]
            dg = _dot(xs, wg_ref[0, kc * kw:(kc + 1) * kw, :])
            du = _dot(xs, wu_ref[0, kc * kw:(kc + 1) * kw, :])
            g = dg if g is None else g + dg
            u = du if u is None else u + du
        hid = g * jax.nn.sigmoid(g) * u
        _store_token_major(y_ref, _dot(hid.astype(BF16), wd_ref[0]))

    @pl.when(i == n_used - 1)
    def _():
        _row_gather_wait(MOE_TILE, h2_hbm, xbuf, sem, 1 - slot)

    @pl.when(i >= n_used)
    def _():
        y_ref[...] = jnp.zeros_like(y_ref)


def _moe_call(h2, src_tok, tile_expert, n_tiles_used, wg, wu, wd):
    p_pad = src_tok.shape[0]
    n_tiles = p_pad // MOE_TILE
    idx = src_tok.reshape(n_tiles, 1, MOE_TILE)
    smem_tile = lambda m: pl.BlockSpec((1, 1, MOE_TILE), m, memory_space=pltpu.SMEM)
    grid_spec = pltpu.PrefetchScalarGridSpec(
        num_scalar_prefetch=2,
        grid=(n_tiles,),
        in_specs=[smem_tile(lambda i, te, nt: (i, 0, 0)),
                  smem_tile(lambda i, te, nt: (jnp.minimum(i + 1, n_tiles - 1), 0, 0)),
                  pl.BlockSpec(memory_space=pl.ANY),
                  pl.BlockSpec((1, D_MODEL, D_FF_E), lambda i, te, nt: (te[i], 0, 0)),
                  pl.BlockSpec((1, D_MODEL, D_FF_E), lambda i, te, nt: (te[i], 0, 0)),
                  pl.BlockSpec((1, D_FF_E, D_MODEL), lambda i, te, nt: (te[i], 0, 0))],
        out_specs=pl.BlockSpec((MOE_TILE * SUB, LANES), lambda i, te, nt: (i, 0)),
        scratch_shapes=[pltpu.VMEM((2, MOE_TILE * SUB, LANES), F32), pltpu.SemaphoreType.DMA((2,))],
    )
    return pl.pallas_call(
        _moe_kernel,
        grid_spec=grid_spec,
        out_shape=jax.ShapeDtypeStruct((p_pad * SUB, LANES), F32),
        compiler_params=_cparams(("arbitrary",)),
        name="moe",
    )(tile_expert, n_tiles_used, idx, idx, h2, wg, wu, wd)


COMBINE_TILE = 256


def _combine_kernel(pos_ref, posn_ref, x1_ref, route_ref, g2_ref, y_hbm, o_ref, ybuf, sem):
    i = pl.program_id(0)
    t = COMBINE_TILE
    slot = _gather_pipeline(i, pl.num_programs(0), pos_ref, posn_ref, 2 * t, y_hbm, ybuf, sem)
    route = route_ref[...]
    w1 = route[:, 2:3]
    w2 = route[:, 3:4]
    ya = _load_token_major(ybuf, slot, 0, t)
    yb = _load_token_major(ybuf, slot, t, t)
    o_ref[...] = x1_ref[...] + g2_ref[0] * (w1 * ya + w2 * yb)


def _combine_call(x1, route, pos2, y, mod3, mod_stride, seq_len):
    n = x1.shape[0]
    t = COMBINE_TILE
    n_tiles = n // t
    pos_tiles = pos2.reshape(n_tiles, t, 2).transpose(0, 2, 1).reshape(n_tiles, 1, 2 * t)
    smem_tile = lambda m: pl.BlockSpec((1, 1, 2 * t), m, memory_space=pltpu.SMEM)
    return pl.pallas_call(
        _combine_kernel,
        grid=(n_tiles,),
        in_specs=[smem_tile(lambda i: (i, 0, 0)),
                  smem_tile(lambda i: (jnp.minimum(i + 1, n_tiles - 1), 0, 0)),
                  pl.BlockSpec((t, D_MODEL), lambda i: (i, 0)),
                  pl.BlockSpec((t, LANES), lambda i: (i, 0)),
                  pl.BlockSpec((1, 1, D_MODEL), lambda i: ((i * t) // seq_len * mod_stride + 5, 0, 0)),
                  pl.BlockSpec(memory_space=pl.ANY)],
        out_specs=pl.BlockSpec((t, D_MODEL), lambda i: (i, 0)),
        out_shape=jax.ShapeDtypeStruct((n, D_MODEL), F32),
        scratch_shapes=[pltpu.VMEM((2, 2 * t * SUB, LANES), F32), pltpu.SemaphoreType.DMA((2,))],
        compiler_params=_cparams(("arbitrary",)),
        name="combine",
    )(pos_tiles, pos_tiles, x1, route, mod3, y)


def _inverse_map_kernel(pos_ref, init_hbm, out_ref):
    pltpu.sync_copy(init_hbm, out_ref)

    def body(p, carry):
        out_ref[pos_ref[p]] = lax.shift_right_logical(p, 1)
        return carry

    lax.fori_loop(0, pos_ref.shape[0], body, 0, unroll=8)


def _inverse_map_call(pos, init):
    return pl.pallas_call(
        _inverse_map_kernel,
        in_specs=[pl.BlockSpec(memory_space=pltpu.SMEM), pl.BlockSpec(memory_space=pl.ANY)],
        out_specs=pl.BlockSpec(memory_space=pltpu.SMEM),
        out_shape=jax.ShapeDtypeStruct(init.shape, jnp.int32),
        name="inverse_map",
    )(pos, init)


def _moe_forward(h2, route, counts, wg, wu, wd):
    n = route.shape[0]
    eid = route[:, :2].astype(jnp.int32).reshape(-1)
    rank = route[:, 4:6].astype(jnp.int32).reshape(-1)
    counts = counts[0, :N_EXPERTS].astype(jnp.int32)
    padded = ((counts + MOE_TILE - 1) // MOE_TILE) * MOE_TILE
    seg_end = jnp.cumsum(padded)
    seg_start = seg_end - padded
    pos = seg_start[eid] + rank
    p_pad = 2 * n + N_EXPERTS * MOE_TILE
    src_tok = _inverse_map_call(pos, jnp.arange(p_pad, dtype=jnp.int32) % n)
    n_tiles = p_pad // MOE_TILE
    tile_start = jnp.arange(n_tiles, dtype=jnp.int32) * MOE_TILE
    tile_expert = jnp.minimum(jnp.sum((tile_start[:, None] >= seg_end[None, :]).astype(jnp.int32), axis=1),
                              N_EXPERTS - 1)
    n_used = (seg_end[-1] // MOE_TILE).astype(jnp.int32).reshape(1)
    y = _moe_call(h2, src_tok, tile_expert, n_used, wg, wu, wd)
    return y, pos.reshape(n, 2)


def _prep_weights(norm1_w, norm2_w, w_in, conv_w, a_log, dt_bias, q_norm_w, k_norm_w, o_norm_w,
                  w_proj_a, w_proj_b, w_out, w_router_g, b_router_g, w_router_e, b_router_e,
                  w_exp_gate, w_exp_up, w_exp_down):
    w = {}
    w["norm1_w"] = norm1_w.reshape(1, D_MODEL)
    w["norm2_w"] = norm2_w.reshape(1, D_MODEL)
    o = 0
    w["wa"] = w_in[:, o:o + 3 * W_A].astype(BF16); o += 3 * W_A
    w["wb"] = w_in[:, o:o + 3 * W_B].astype(BF16); o += 3 * W_B
    w["wz"] = w_in[:, o:o + W_B].astype(BF16); o += W_B
    small = w_in[:, o:o + 4 * H_B]; o += 4 * H_B
    w["wg"] = w_in[:, o:o + 2 * D_MODEL].astype(BF16)
    def group_lanes(t):
        r = t.shape[0]
        t = t.reshape(r, -1, N_HG, HP).transpose(0, 2, 1, 3).reshape(r, N_HG, -1)
        return jnp.pad(t, ((0, 0), (0, 0), (0, LANES - t.shape[-1]))).reshape(r, N_HG * LANES)

    w["we"] = group_lanes(small).astype(BF16)
    w["alog_l"] = group_lanes(a_log.reshape(1, 2 * H_B))
    w["dtb_l"] = group_lanes(dt_bias.reshape(1, 2 * H_B))
    w["qnw"] = jnp.tile(q_norm_w, H_A).reshape(1, W_A)
    w["knw"] = jnp.tile(k_norm_w, H_A).reshape(1, W_A)
    w["onw"] = jnp.tile(o_norm_w, H_B).reshape(1, W_B)
    blk = np.arange(256) // D_HA
    w["bd256"] = jnp.asarray((blk[:, None] == blk[None, :]).astype(np.float32)).astype(BF16)
    w["conv_w"] = conv_w
    w["wpa"] = w_proj_a.astype(BF16)
    w["wpb"] = w_proj_b.astype(BF16)
    w["wo"] = w_out.astype(BF16)
    wr = jnp.concatenate([w_router_g, w_router_e], axis=1)
    wr = jnp.pad(wr, ((0, 0), (0, LANES - wr.shape[1])))
    wr_hi = wr.astype(BF16)
    wr_lo = (wr - wr_hi.astype(F32)).astype(BF16)
    w["wr3"] = jnp.concatenate([wr_hi, wr_hi, wr_lo], axis=0)
    br = jnp.concatenate([b_router_g, b_router_e])
    w["br"] = jnp.pad(br, (0, LANES - br.shape[0])).reshape(1, LANES)
    w["wg_e"] = w_exp_gate.astype(BF16)
    w["wu_e"] = w_exp_up.astype(BF16)
    w["wd_e"] = w_exp_down.astype(BF16)
    return w


def _layer(x, mod3, mod_stride, w, ctx):
    B, L, _ = x.shape
    latent = ctx is not None
    qa, ka, va, qkvb, z, gate, ecol, *new_kv = _inproj_call(x, mod3, mod_stride, w, not latent)
    if latent:
        k_ctx, v_ctx, s0f, s0b, bias = ctx
        oa = _nbr_attn_call(qa, ka, va, k_ctx, v_ctx, bias)
    else:
        s0f = s0b = jnp.zeros((B, H_B, D_K, D_V), F32)
        oa = _ctx_attn_call(qa, ka, va)
    qd, kd, vd = _conv_call(qkvb, w["conv_w"], rope=latent)
    of, ob, s_f, s_b = _delta_call(qd, kd, vd, ecol, s0f, s0b)
    x1, h2, route, counts = _merge_call(x, oa, of, ob, z, gate, mod3, mod_stride, w)
    ys, pos2 = _moe_forward(h2, route, counts, w["wg_e"], w["wu_e"], w["wd_e"])
    y = _combine_call(x1, route, pos2, ys, mod3, mod_stride, L)
    return y.reshape(B, L, D_MODEL), new_kv, s_f, s_b


def kernel(x_prompt, x_sample, c, cache_attn_k, cache_attn_v, state_delta_fwd, state_delta_bwd, c_ctx,
           norm1_w, norm2_w, w_mod, b_mod, w_in, conv_w, a_log, dt_bias, q_norm_w, k_norm_w, rpb,
           o_norm_w, w_proj_a, w_proj_b, w_out, w_router_g, b_router_g, w_router_e, b_router_e,
           w_exp_gate, w_exp_up, w_exp_down):
    assert norm1_w.shape[0] == 1, "single-layer trunk"
    Bp, Lp, _ = x_prompt.shape
    Bs, Ls, _ = x_sample.shape
    l = 0
    w = _prep_weights(norm1_w[l], norm2_w[l], w_in[l], conv_w[l], a_log[l], dt_bias[l], q_norm_w[l],
                      k_norm_w[l], o_norm_w[l], w_proj_a[l], w_proj_b[l], w_out[l], w_router_g[l],
                      b_router_g[l], w_router_e[l], b_router_e[l], w_exp_gate[l], w_exp_up[l],
                      w_exp_down[l])
    cond = jnp.concatenate([c_ctx[None, :], c, jnp.zeros((8 - 1 - Bs, D_MODEL), F32)], axis=0)
    mod = _mod_call(cond, w_mod[l], b_mod[l])
    mod3 = mod.reshape(8 * 6, 1, D_MODEL)
    bias = _bias_table_call(rpb[l])
    y_p, (new_k, new_v), s_f, s_b = _layer(x_prompt, mod3[:6], 0, w, None)
    lc = cache_attn_k.shape[2]
    ctx = (cache_attn_k[:, l].reshape(Bs, lc, W_A).astype(BF16),
           cache_attn_v[:, l].reshape(Bs, lc, W_A).astype(BF16),
           state_delta_fwd[:, l], state_delta_bwd[:, l], bias)
    y_s, _, _, _ = _layer(x_sample, mod3[6:6 + 6 * Bs], 6, w, ctx)
    return (y_p, y_s, new_k, new_v, s_f[:, None], s_b[:, None])
```
